```python
import jax, jax.numpy as jnp
from jax import lax
import numpy as np

D_MODEL = 2048
BATCH = 4
SEQ = 2048
DEPTH = 2

N_MIXERS = 2
N_SUBLAYERS = 3
D_FF = 5632
POOL_GROUPS = 4
POOL_WINDOWS = (2, 4, 8, 16)
POOL_GROUP_DIM = D_MODEL // POOL_GROUPS
N_HEADS = 16
N_KV_HEADS = 4
KV_GROUP = N_HEADS // N_KV_HEADS
HEAD_DIM = D_MODEL // N_HEADS
ROT_DIM = HEAD_DIM // 4
ROPE_THETA = 500000.0
IDX_HEADS = 16
IDX_DIM = 64
IDX_ROT_DIM = IDX_DIM // 4
TOPK_MAX = 256
Q_BLOCK = 128
EPS = 1e-6
NEG = -1e30
Q_W = N_HEADS * HEAD_DIM
KV_W = N_KV_HEADS * HEAD_DIM
QI_W = IDX_HEADS * IDX_DIM
DSA_IN = Q_W + 2 * KV_W + QI_W + IDX_DIM + IDX_HEADS
DSA_SPLITS = (Q_W, Q_W + KV_W, Q_W + 2 * KV_W, Q_W + 2 * KV_W + QI_W,
              Q_W + 2 * KV_W + QI_W + IDX_DIM)
N_POOL_LAYERS = (DEPTH + 1) // 2
N_DSA_LAYERS = DEPTH // 2

kernel_name = "hybrid_pool_dsa_macaron_adaln"


def rmsnorm(h, g):
    hf = h.astype(jnp.float32)
    hf = hf * lax.rsqrt(jnp.mean(hf * hf, axis=-1, keepdims=True) + EPS)
    return (hf * g.astype(jnp.float32)).astype(h.dtype)


def modulate(h, shift, scale):
    return h * (1 + scale[:, None, :]) + shift[:, None, :]


def partial_rope(x, positions, rot_dim):
    half = rot_dim // 2
    inv_freq = ROPE_THETA ** (-jnp.arange(half, dtype=jnp.float32) / half)
    ang = positions.astype(jnp.float32)[..., None] * inv_freq
    cos = jnp.cos(ang)[:, :, None, :]
    sin = jnp.sin(ang)[:, :, None, :]
    xr = x[..., :rot_dim].astype(jnp.float32)
    x1, x2 = xr[..., :half], xr[..., half:]
    rot = jnp.concatenate([x1 * cos - x2 * sin, x2 * cos + x1 * sin], axis=-1)
    return jnp.concatenate([rot.astype(x.dtype), x[..., rot_dim:]], axis=-1)


def swiglu_ffn(h, w_gu, w_d):
    g, u = jnp.split(h @ w_gu, 2, axis=-1)
    return (jax.nn.silu(g) * u) @ w_d


def pool_mixer(h, w_in, w_grp, ls, w_out):
    B, S, _ = h.shape
    u = (h @ w_in).reshape(B, S, POOL_GROUPS, POOL_GROUP_DIM)
    cs = jnp.cumsum(u.astype(jnp.float32), axis=1)
    t = jnp.arange(S)
    outs = []
    for g, w in enumerate(POOL_WINDOWS):
        c_g = cs[:, :, g]
        c_lag = jnp.pad(c_g, ((0, 0), (w, 0), (0, 0)))[:, :S]
        cnt = jnp.minimum(t + 1, w).astype(jnp.float32)[None, :, None]
        outs.append(((c_g - c_lag) / cnt - u[:, :, g].astype(jnp.float32)).astype(h.dtype))
    p = jnp.stack(outs, axis=2)
    v = jnp.einsum('bsgc,gcd->bsgd', p, w_grp).reshape(B, S, D_MODEL)
    return (v * ls) @ w_out


def dsa_mixer(h, positions, w_in, w_out):
    B, S, _ = h.shape
    q, k, v, qi, ki, wi = jnp.split(h @ w_in, DSA_SPLITS, axis=-1)
    q = partial_rope(q.reshape(B, S, N_HEADS, HEAD_DIM), positions, ROT_DIM)
    k = partial_rope(k.reshape(B, S, N_KV_HEADS, HEAD_DIM), positions, ROT_DIM)
    v = v.reshape(B, S, N_KV_HEADS, HEAD_DIM)
    qi = partial_rope(qi.reshape(B, S, IDX_HEADS, IDX_DIM), positions, IDX_ROT_DIM)
    ki = partial_rope(ki.reshape(B, S, 1, IDX_DIM), positions, IDX_ROT_DIM)[:, :, 0]
    ki32 = ki.astype(jnp.float32)
    top_k = min(TOPK_MAX, S // 4)
    nb = S // Q_BLOCK
    key_pos = jnp.arange(S)
    bidx = jnp.arange(B)[:, None, None]

    def to_blocks(a):
        return jnp.moveaxis(a.reshape(B, nb, Q_BLOCK, *a.shape[2:]), 1, 0)

    def block(args):
        blk, q_b, qi_b, wi_b = args
        t_pos = blk * Q_BLOCK + jnp.arange(Q_BLOCK)
        causal = key_pos[None, :] <= t_pos[:, None]
        rel = jax.nn.relu(jnp.einsum('bthd,bsd->bths', qi_b.astype(jnp.float32), ki32))
        score = jnp.einsum('bths,bth->bts', rel, wi_b.astype(jnp.float32))
        score = jnp.where(causal[None], score, NEG)
        _, sel = lax.top_k(score, top_k)
        valid = sel <= t_pos[None, :, None]
        k_sel = k[bidx, sel]
        v_sel = v[bidx, sel]
        qg = q_b.reshape(B, Q_BLOCK, N_KV_HEADS, KV_GROUP, HEAD_DIM)
        logits = jnp.einsum('btgrd,btkgd->btgrk', qg.astype(jnp.float32),
                            k_sel.astype(jnp.float32)) * (HEAD_DIM ** -0.5)
        logits = jnp.where(valid[:, :, None, None, :], logits, NEG)
        prob = jax.nn.softmax(logits, axis=-1)
        o = jnp.einsum('btgrk,btkgd->btgrd', prob.astype(v.dtype), v_sel)
        return o.reshape(B, Q_BLOCK, Q_W)

    out = lax.map(block, (jnp.arange(nb), to_blocks(q), to_blocks(qi), to_blocks(wi)))
    out = jnp.moveaxis(out, 0, 1).reshape(B, S, Q_W)
    return out @ w_out


def setup_inputs(seed: int = 0) -> dict:
    key = jax.random.key(seed)
    ks = jax.random.split(key, 16)
    f32 = jnp.float32
    nrm = lambda k, shape, s: jax.random.normal(k, shape, f32) * s
    x = nrm(ks[0], (BATCH, SEQ, D_MODEL), 1.0)
    c = nrm(ks[1], (BATCH, D_MODEL), 1.0)
    offsets = jax.random.randint(ks[2], (BATCH, 1), 0, 1024, dtype=jnp.int32)
    positions = offsets + jnp.arange(SEQ, dtype=jnp.int32)[None, :]
    ada_w = nrm(ks[3], (DEPTH, D_MODEL, N_SUBLAYERS * 3 * D_MODEL), 0.02)
    ada_b = nrm(ks[4], (DEPTH, N_SUBLAYERS * 3 * D_MODEL), 0.02)
    norm_g = 1.0 + nrm(ks[5], (DEPTH, N_SUBLAYERS, D_MODEL), 0.02)
    final_g = 1.0 + nrm(ks[6], (D_MODEL,), 0.02)
    ffn_wgu = nrm(ks[7], (DEPTH, 2, D_MODEL, 2 * D_FF), D_MODEL ** -0.5)
    ffn_wd = nrm(ks[8], (DEPTH, 2, D_FF, D_MODEL), D_FF ** -0.5)
    pool_w_in = nrm(ks[9], (N_POOL_LAYERS, D_MODEL, D_MODEL), D_MODEL ** -0.5)
    pool_w_grp = nrm(ks[10], (N_POOL_LAYERS, POOL_GROUPS, POOL_GROUP_DIM, POOL_GROUP_DIM), POOL_GROUP_DIM ** -0.5)
    pool_scale = 1.0 + nrm(ks[11], (N_POOL_LAYERS, D_MODEL), 0.1)
    pool_w_out = nrm(ks[12], (N_POOL_LAYERS, D_MODEL, D_MODEL), D_MODEL ** -0.5)
    dsa_w_in = nrm(ks[13], (N_DSA_LAYERS, D_MODEL, DSA_IN), D_MODEL ** -0.5)
    dsa_w_out = nrm(ks[14], (N_DSA_LAYERS, Q_W, D_MODEL), Q_W ** -0.5)
    return {"x": x, "c": c, "positions": positions, "ada_w": ada_w, "ada_b": ada_b,
            "norm_g": norm_g, "final_g": final_g, "ffn_wgu": ffn_wgu, "ffn_wd": ffn_wd,
            "pool_w_in": pool_w_in, "pool_w_grp": pool_w_grp, "pool_scale": pool_scale,
            "pool_w_out": pool_w_out, "dsa_w_in": dsa_w_in, "dsa_w_out": dsa_w_out}


def reference(x, c, positions, ada_w, ada_b, norm_g, final_g, ffn_wgu, ffn_wd,
              pool_w_in, pool_w_grp, pool_scale, pool_w_out, dsa_w_in, dsa_w_out):
    B = x.shape[0]
    ada = jnp.einsum('bd,lde->ble', jax.nn.silu(c), ada_w) + ada_b[None]
    ada = ada.reshape(B, DEPTH, N_SUBLAYERS, 3, D_MODEL)
    h = x
    for i in range(DEPTH):
        m = ada[:, i]
        y = modulate(rmsnorm(h, norm_g[i, 0]), m[:, 0, 0], m[:, 0, 1])
        h = h + 0.5 * m[:, 0, 2][:, None, :] * swiglu_ffn(y, ffn_wgu[i, 0], ffn_wd[i, 0])
        y = modulate(rmsnorm(h, norm_g[i, 1]), m[:, 1, 0], m[:, 1, 1])
        j = i // N_MIXERS
        if i % N_MIXERS == 0:
            y = pool_mixer(y, pool_w_in[j], pool_w_grp[j], pool_scale[j], pool_w_out[j])
        else:
            y = dsa_mixer(y, positions, dsa_w_in[j], dsa_w_out[j])
        h = h + m[:, 1, 2][:, None, :] * y
        y = modulate(rmsnorm(h, norm_g[i, 2]), m[:, 2, 0], m[:, 2, 1])
        h = h + 0.5 * m[:, 2, 2][:, None, :] * swiglu_ffn(y, ffn_wgu[i, 1], ffn_wd[i, 1])
    return rmsnorm(h, final_g)
```

```python
import functools

import jax
import jax.numpy as jnp
from jax import lax
from jax.experimental import pallas as pl
from jax.experimental.pallas import tpu as pltpu

F32 = jnp.float32
BF16 = jnp.bfloat16

N_SUBLAYERS = 3
POOL_WINDOWS = (2, 4, 8, 16)
N_HEADS = 16
N_KV_HEADS = 4
KV_GROUP = N_HEADS // N_KV_HEADS
HEAD_DIM = 128
ROT_DIM = HEAD_DIM // 4
ROPE_THETA = 500000.0
IDX_HEADS = 16
IDX_DIM = 64
IDX_ROT_DIM = IDX_DIM // 4
TOPK_MAX = 256
EPS = 1e-6
NEG = -1e30
Q_W = N_HEADS * HEAD_DIM
KV_W = N_KV_HEADS * HEAD_DIM
QI_W = IDX_HEADS * IDX_DIM
DSA_MAIN_W = Q_W + 2 * KV_W + QI_W
DSA_SMALL_W = IDX_DIM + IDX_HEADS

LANES = 128
SUBLANES = 8
VMEM_LIMIT_BYTES = 56 * 1024 * 1024
INT32_MIN = -(2 ** 31)


def _cparams(semantics):
    return pltpu.CompilerParams(dimension_semantics=semantics,
                                vmem_limit_bytes=VMEM_LIMIT_BYTES)


def _rmsnorm(x, g):
    ms = jnp.mean(x * x, axis=-1, keepdims=True)
    return x * lax.rsqrt(ms + EPS) * g


def _norm_mod(x, g, shift, scale):
    return _rmsnorm(x, g) * (1.0 + scale) + shift


def _silu(x):
    return x * jax.nn.sigmoid(x)


def _dot(a, b):
    return jnp.dot(a, b, preferred_element_type=F32)


def _dot_t(a, b):
    return lax.dot_general(a, b, (((1,), (1,)), ((), ())), preferred_element_type=F32)


def _ada_kernel(c_ref, w_ref, b_ref, o_ref):
    sc = _silu(c_ref[...]).astype(BF16)
    o_ref[...] = _dot(sc, w_ref[...].astype(BF16)) + b_ref[...]


def _ada(c8, ada_w, ada_b, tn):
    depth, d, n = ada_w.shape
    rows = c8.shape[0]
    return pl.pallas_call(
        _ada_kernel,
        grid=(depth, n // tn),
        in_specs=[
            pl.BlockSpec((rows, d), lambda l, j: (0, 0)),
            pl.BlockSpec((None, d, tn), lambda l, j: (l, 0, j)),
            pl.BlockSpec((None, 1, tn), lambda l, j: (l, 0, j)),
        ],
        out_specs=pl.BlockSpec((None, rows, tn), lambda l, j: (l, 0, j)),
        out_shape=jax.ShapeDtypeStruct((depth, rows, n), F32),
        compiler_params=_cparams(("arbitrary", "arbitrary")),
        name="ada",
    )(c8, ada_w, ada_b.reshape(depth, 1, n))


def _for_row_slabs(n_rows, slab, body):
    def step(r, carry):
        body(pl.ds(pl.multiple_of(r * slab, slab), slab))
        return carry
    lax.fori_loop(0, n_rows // slab, step, 0)


def _ffn_kernel(sub, final, slab, *refs):
    if final:
        x_ref, g_ref, m_ref, wg_ref, wu_ref, wd_ref, fg_ref, o_ref, y_ref = refs
    else:
        x_ref, g_ref, m_ref, wg_ref, wu_ref, wd_ref, o_ref, y_ref = refs
    k = pl.program_id(1)
    nk = pl.num_programs(1)
    tm = x_ref.shape[0]

    @pl.when(k == 0)
    def _():
        shift = m_ref[3 * sub + 0:3 * sub + 1, :]
        scale = m_ref[3 * sub + 1:3 * sub + 2, :]

        def norm_rows(rows):
            y_ref[rows, :] = _norm_mod(x_ref[rows, :], g_ref[...], shift, scale).astype(BF16)
            o_ref[rows, :] = jnp.zeros((slab, o_ref.shape[1]), F32)

        _for_row_slabs(tm, slab, norm_rows)

    wg = wg_ref[...].astype(BF16)
    wu = wu_ref[...].astype(BF16)
    wd = wd_ref[...].astype(BF16)

    def ffn_rows(rows):
        y = y_ref[rows, :]
        a = (_silu(_dot(y, wg)) * _dot(y, wu)).astype(BF16)
        o_ref[rows, :] += _dot(a, wd)

    _for_row_slabs(tm, slab, ffn_rows)

    @pl.when(k == nk - 1)
    def _():
        gate = 0.5 * m_ref[3 * sub + 2:3 * sub + 3, :]

        def finish_rows(rows):
            res = x_ref[rows, :] + gate * o_ref[rows, :]
            if final:
                res = _rmsnorm(res, fg_ref[...])
            o_ref[rows, :] = res

        _for_row_slabs(tm, slab, finish_rows)


def _ffn(h, ada4, norm_g, ffn_wgu, ffn_wd, layer, which, sub, seq, tm, tf, final_g=None):
    t, d = h.shape
    d_ff = ffn_wd.shape[2]
    nk = d_ff // tf
    tpb = seq // tm
    final = final_g is not None
    in_specs = [
        pl.BlockSpec((tm, d), lambda i, k: (i, 0), pipeline_mode=pl.Buffered(1)),
        pl.BlockSpec((1, d), lambda i, k: (0, 0)),
        pl.BlockSpec((None, None, 3 * N_SUBLAYERS, d), lambda i, k: (layer, i // tpb, 0, 0)),
        pl.BlockSpec((None, None, d, tf), lambda i, k: (layer, which, 0, k)),
        pl.BlockSpec((None, None, d, tf), lambda i, k: (layer, which, 0, nk + k)),
        pl.BlockSpec((None, None, tf, d), lambda i, k: (layer, which, k, 0)),
    ]
    args = [h, norm_g[layer, sub][None, :], ada4, ffn_wgu, ffn_wgu, ffn_wd]
    if final:
        in_specs.append(pl.BlockSpec((1, d), lambda i, k: (0, 0)))
        args.append(final_g[None, :])
    return pl.pallas_call(
        functools.partial(_ffn_kernel, sub, final, min(256, tm)),
        grid=(t // tm, nk),
        in_specs=in_specs,
        out_specs=pl.BlockSpec((tm, d), lambda i, k: (i, 0)),
        out_shape=jax.ShapeDtypeStruct((t, d), F32),
        scratch_shapes=[pltpu.VMEM((tm, d), BF16)],
        compiler_params=_cparams(("arbitrary", "arbitrary")),
        name=f"ffn_l{layer}_{which}",
    )(*args)


def _norm_linear_kernel(sub, x_ref, g_ref, m_ref, w_ref, o_ref, y_ref):
    @pl.when(pl.program_id(1) == 0)
    def _():
        shift = m_ref[3 * sub + 0:3 * sub + 1, :]
        scale = m_ref[3 * sub + 1:3 * sub + 2, :]
        y_ref[...] = _norm_mod(x_ref[...], g_ref[...], shift, scale).astype(BF16)

    o_ref[...] = _dot(y_ref[...], w_ref[...].astype(BF16))


def _norm_linear(h, ada4, g_row, w, layer, sub, seq, tm, tn):
    t, d = h.shape
    n = w.shape[1]
    tpb = seq // tm
    return pl.pallas_call(
        functools.partial(_norm_linear_kernel, sub),
        grid=(t // tm, n // tn),
        in_specs=[
            pl.BlockSpec((tm, d), lambda i, j: (i, 0)),
            pl.BlockSpec((1, d), lambda i, j: (0, 0)),
            pl.BlockSpec((None, None, 3 * N_SUBLAYERS, d), lambda i, j: (layer, i // tpb, 0, 0)),
            pl.BlockSpec((d, tn), lambda i, j: (0, j)),
        ],
        out_specs=pl.BlockSpec((tm, tn), lambda i, j: (i, j)),
        out_shape=jax.ShapeDtypeStruct((t, n), F32),
        scratch_shapes=[pltpu.VMEM((tm, d), BF16)],
        compiler_params=_cparams(("arbitrary", "arbitrary")),
        name=f"norm_linear_l{layer}",
    )(h, g_row, ada4, w)


def _pool_kernel(u_ref, w_ref, ls_ref, o_ref):
    g = pl.program_id(1)
    u = u_ref[...]
    t = lax.broadcasted_iota(jnp.int32, u.shape, 0)

    def shifted(v, d):
        return jnp.where(t >= d, pltpu.roll(v, d, 0), 0.0)

    for gi, win in enumerate(POOL_WINDOWS):
        @pl.when(g == gi)
        def _(win=win):
            acc = u
            d = 1
            while d < win:
                acc = acc + shifted(acc, d)
                d *= 2
            cnt = jnp.minimum(t + 1, win).astype(F32)
            p = (acc / cnt - u).astype(BF16)
            v = _dot(p, w_ref[...].astype(BF16)) * ls_ref[...]
            o_ref[...] = v.astype(BF16)


def _pool_core(u, w_grp, ls_row, batch, seq):
    t, d = u.shape
    groups, c, _ = w_grp.shape
    return pl.pallas_call(
        _pool_kernel,
        grid=(batch, groups),
        in_specs=[
            pl.BlockSpec((seq, c), lambda b, g: (b, g)),
            pl.BlockSpec((None, c, c), lambda b, g: (g, 0, 0)),
            pl.BlockSpec((1, c), lambda b, g: (0, g)),
        ],
        out_specs=pl.BlockSpec((seq, c), lambda b, g: (b, g)),
        out_shape=jax.ShapeDtypeStruct((t, d), BF16),
        compiler_params=_cparams(("arbitrary", "arbitrary")),
        name="pool_core",
    )(u, w_grp, ls_row)


def _linear_residual_kernel(sub, a_ref, w_ref, h_ref, m_ref, o_ref):
    gate = m_ref[3 * sub + 2:3 * sub + 3, :]
    o_ref[...] = h_ref[...] + gate * _dot(a_ref[...], w_ref[...].astype(BF16))


def _linear_residual(a, w, h, ada4, layer, sub, seq, tm, tn):
    t, kdim = a.shape
    n = w.shape[1]
    tpb = seq // tm
    return pl.pallas_call(
        functools.partial(_linear_residual_kernel, sub),
        grid=(t // tm, n // tn),
        in_specs=[
            pl.BlockSpec((tm, kdim), lambda i, j: (i, 0)),
            pl.BlockSpec((kdim, tn), lambda i, j: (0, j)),
            pl.BlockSpec((tm, tn), lambda i, j: (i, j)),
            pl.BlockSpec((None, None, 3 * N_SUBLAYERS, tn), lambda i, j: (layer, i // tpb, 0, j)),
        ],
        out_specs=pl.BlockSpec((tm, tn), lambda i, j: (i, j)),
        out_shape=jax.ShapeDtypeStruct((t, n), F32),
        compiler_params=_cparams(("arbitrary", "arbitrary")),
        name=f"linear_residual_l{layer}",
    )(a, w, h, ada4)


def _rope_table_kernel(pos_ref, invf_ref, o_ref):
    pos = pos_ref[...].astype(F32)
    lane = lax.broadcasted_iota(jnp.int32, (pos.shape[0], LANES), 1)

    def tables(invf, lane_in_head, rot_dim):
        half = rot_dim // 2
        ang = pos * invf
        cos = jnp.cos(ang)
        sin = jnp.sin(ang)
        c = jnp.where(lane_in_head < rot_dim, cos, 1.0)
        sa = jnp.where(lane_in_head < half, -sin, 0.0)
        sb = jnp.where(lane_in_head < rot_dim, jnp.where(lane_in_head >= half, sin, 0.0), 0.0)
        return c, sa, sb

    big = tables(invf_ref[:, 0:LANES], lane, ROT_DIM)
    small = tables(invf_ref[:, LANES:2 * LANES], lane & (IDX_DIM - 1), IDX_ROT_DIM)
    for n, tab in enumerate(big + small):
        o_ref[:, n * LANES:(n + 1) * LANES] = tab


def _rope_tables(positions, tm):
    t = positions.size
    lane = jnp.arange(LANES)

    def inv_freq(rot_dim):
        half = rot_dim // 2
        return ROPE_THETA ** (-jnp.arange(half, dtype=F32) / half)

    invf = jnp.concatenate([inv_freq(ROT_DIM)[lane % (ROT_DIM // 2)],
                            inv_freq(IDX_ROT_DIM)[lane % (IDX_ROT_DIM // 2)]])[None, :]
    return pl.pallas_call(
        _rope_table_kernel,
        grid=(t // tm,),
        in_specs=[
            pl.BlockSpec((tm, 1), lambda i: (i, 0)),
            pl.BlockSpec((1, 2 * LANES), lambda i: (0, 0)),
        ],
        out_specs=pl.BlockSpec((tm, 6 * LANES), lambda i: (i, 0)),
        out_shape=jax.ShapeDtypeStruct((t, 6 * LANES), F32),
        compiler_params=_cparams(("arbitrary",)),
        name="rope_tables",
    )(positions.reshape(t, 1), invf)


def _rope(x, c, sa, sb, half):
    return x * c + pltpu.roll(x, LANES - half, 1) * sa + pltpu.roll(x, half, 1) * sb


def _dsa_proj_kernel(sub, tn, x_ref, g_ref, m_ref, w_ref, ws_ref, tab_ref, o_ref, os_ref, y_ref):
    j = pl.program_id(1)
    n_rope_big = (Q_W + KV_W) // tn
    n_plain = KV_W // tn

    def tab(n):
        return tab_ref[:, n * LANES:(n + 1) * LANES]

    @pl.when(j == 0)
    def _():
        shift = m_ref[3 * sub + 0:3 * sub + 1, :]
        scale = m_ref[3 * sub + 1:3 * sub + 2, :]
        y = _norm_mod(x_ref[...], g_ref[...], shift, scale).astype(BF16)
        y_ref[...] = y
        small = _dot(y, ws_ref[...].astype(BF16))
        lane = lax.broadcasted_iota(jnp.int32, small.shape, 1)
        is_ki = lane < IDX_DIM
        c = jnp.where(is_ki, tab(3), 1.0)
        sa = jnp.where(is_ki, tab(4), 0.0)
        sb = jnp.where(is_ki, tab(5), 0.0)
        os_ref[...] = _rope(small, c, sa, sb, IDX_ROT_DIM // 2)

    acc = _dot(y_ref[...], w_ref[...].astype(BF16))

    def store_rope(first, half):
        for s in range(tn // LANES):
            xs = acc[:, s * LANES:(s + 1) * LANES]
            o_ref[:, s * LANES:(s + 1) * LANES] = _rope(
                xs, tab(first), tab(first + 1), tab(first + 2), half).astype(BF16)

    @pl.when(j < n_rope_big)
    def _():
        store_rope(0, ROT_DIM // 2)

    @pl.when(jnp.logical_and(j >= n_rope_big, j < n_rope_big + n_plain))
    def _():
        o_ref[...] = acc.astype(BF16)

    @pl.when(j >= n_rope_big + n_plain)
    def _():
        store_rope(3, IDX_ROT_DIM // 2)


def _dsa_proj(h, ada4, g_row, w_in, w_small, tabs, layer, sub, seq, tm, tn):
    t, d = h.shape
    tpb = seq // tm
    return pl.pallas_call(
        functools.partial(_dsa_proj_kernel, sub, tn),
        grid=(t // tm, DSA_MAIN_W // tn),
        in_specs=[
            pl.BlockSpec((tm, d), lambda i, j: (i, 0)),
            pl.BlockSpec((1, d), lambda i, j: (0, 0)),
            pl.BlockSpec((None, None, 3 * N_SUBLAYERS, d), lambda i, j: (layer, i // tpb, 0, 0)),
            pl.BlockSpec((d, tn), lambda i, j: (0, j)),
            pl.BlockSpec((d, LANES), lambda i, j: (0, 0)),
            pl.BlockSpec((tm, 6 * LANES), lambda i, j: (i, 0)),
        ],
        out_specs=[
            pl.BlockSpec((tm, tn), lambda i, j: (i, j)),
            pl.BlockSpec((tm, LANES), lambda i, j: (i, 0)),
        ],
        out_shape=[
            jax.ShapeDtypeStruct((t, DSA_MAIN_W), BF16),
            jax.ShapeDtypeStruct((t, LANES), F32),
        ],
        scratch_shapes=[pltpu.VMEM((tm, d), BF16)],
        compiler_params=_cparams(("arbitrary", "arbitrary")),
        name=f"dsa_proj_l{layer}",
    )(h, g_row, ada4, w_in, w_small, tabs)


def _dsa_attn_kernel(tq, tk, top_k, q_ref, qi_ref, sq_ref, k_ref, v_ref, skv_ref, o_ref,
                     key_ref, bias_ref):
    i = pl.program_id(1)
    n_chunks = ((i + 1) * tq + tk - 1) // tk
    t_pos = i * tq + lax.broadcasted_iota(jnp.int32, (tq, tk), 0)
    k_iota = lax.broadcasted_iota(jnp.int32, (tq, tk), 1)

    wi = sq_ref[:, IDX_DIM:IDX_DIM + IDX_HEADS]

    def score_chunk(c, carry):
        off = pl.multiple_of(c * tk, tk)
        ki = skv_ref[pl.ds(off, tk), :][:, 0:IDX_DIM].astype(BF16)
        acc = jnp.zeros((tq, tk), F32)
        for hh in range(IDX_HEADS):
            r = _dot_t(qi_ref[:, hh * IDX_DIM:(hh + 1) * IDX_DIM], ki)
            acc = acc + wi[:, hh:hh + 1] * jnp.maximum(r, 0.0)
        acc = jnp.where(k_iota + off <= t_pos, acc, NEG)
        bits = pltpu.bitcast(acc, jnp.int32)
        key_ref[:, pl.ds(off, tk)] = bits ^ ((bits >> 31) & 0x7FFFFFFF)
        return carry

    lax.fori_loop(0, n_chunks, score_chunk, 0)

    def count_ge(thr):
        def body(c, cnt):
            off = pl.multiple_of(c * tk, tk)
            ge = jnp.where(key_ref[:, pl.ds(off, tk)] >= thr, 1.0, 0.0)
            return cnt + jnp.sum(ge, axis=1, keepdims=True)
        return lax.fori_loop(0, n_chunks, body, jnp.zeros((tq, 1), F32))

    k_f = float(top_k)
    thr0 = jnp.where(count_ge(jnp.zeros((tq, 1), jnp.int32)) >= k_f, 0, INT32_MIN)

    def bit_step(n, thr):
        cand = thr + lax.shift_left(jnp.int32(1), 30 - n)
        return jnp.where(count_ge(cand) >= k_f, cand, thr)

    thr = lax.fori_loop(0, 31, bit_step, thr0.astype(jnp.int32))

    def bias_chunk(c, carry):
        off = pl.multiple_of(c * tk, tk)
        sel = key_ref[:, pl.ds(off, tk)] >= thr
        causal = k_iota + off <= t_pos
        bias_ref[:, pl.ds(off, tk)] = jnp.where(sel, jnp.where(causal, 0.0, NEG), NEG)
        return carry

    lax.fori_loop(0, n_chunks, bias_chunk, 0)

    scale = HEAD_DIM ** -0.5
    rows = KV_GROUP * tq
    for g in range(N_KV_HEADS):
        qg = jnp.concatenate(
            [q_ref[:, (g * KV_GROUP + r) * HEAD_DIM:(g * KV_GROUP + r + 1) * HEAD_DIM]
             for r in range(KV_GROUP)], axis=0)

        def attn_chunk(c, carry, g=g, qg=qg):
            m, l, acc = carry
            off = pl.multiple_of(c * tk, tk)
            kc = k_ref[pl.ds(off, tk), g * HEAD_DIM:(g + 1) * HEAD_DIM]
            vc = v_ref[pl.ds(off, tk), g * HEAD_DIM:(g + 1) * HEAD_DIM]
            s = _dot_t(qg, kc) * scale
            b = bias_ref[:, pl.ds(off, tk)]
            s = (s.reshape(KV_GROUP, tq, tk) + b[None]).reshape(rows, tk)
            m_new = jnp.maximum(m, jnp.max(s, axis=1, keepdims=True))
            alpha = jnp.exp(m - m_new)
            p = jnp.exp(s - m_new)
            l = alpha * l + jnp.sum(p, axis=1, keepdims=True)
            acc = alpha * acc + _dot(p.astype(BF16), vc)
            return m_new, l, acc

        m0 = jnp.full((rows, 1), NEG, F32)
        l0 = jnp.zeros((rows, 1), F32)
        a0 = jnp.zeros((rows, HEAD_DIM), F32)
        _, l, acc = lax.fori_loop(0, n_chunks, attn_chunk, (m0, l0, a0))
        out = acc / l
        for r in range(KV_GROUP):
            hd = g * KV_GROUP + r
            o_ref[:, hd * HEAD_DIM:(hd + 1) * HEAD_DIM] = out[r * tq:(r + 1) * tq].astype(BF16)


def _dsa_attn(main, small, batch, seq, tq, tk, top_k):
    t = main.shape[0]
    nq = seq // tq
    return pl.pallas_call(
        functools.partial(_dsa_attn_kernel, tq, tk, top_k),
        grid=(batch, nq),
        in_specs=[
            pl.BlockSpec((tq, Q_W), lambda b, i: (b * nq + i, 0)),
            pl.BlockSpec((tq, QI_W), lambda b, i: (b * nq + i, (Q_W + 2 * KV_W) // QI_W)),
            pl.BlockSpec((tq, LANES), lambda b, i: (b * nq + i, 0)),
            pl.BlockSpec((seq, KV_W), lambda b, i: (b, Q_W // KV_W)),
            pl.BlockSpec((seq, KV_W), lambda b, i: (b, Q_W // KV_W + 1)),
            pl.BlockSpec((seq, LANES), lambda b, i: (b, 0)),
        ],
        out_specs=pl.BlockSpec((tq, Q_W), lambda b, i: (b * nq + i, 0)),
        out_shape=jax.ShapeDtypeStruct((t, Q_W), BF16),
        scratch_shapes=[pltpu.VMEM((tq, seq), jnp.int32), pltpu.VMEM((tq, seq), F32)],
        compiler_params=_cparams(("arbitrary", "arbitrary")),
        name="dsa_attn",
    )(main, main, small, main, main, small)


def _tiles(seq, d_ff):
    tm = min(1024, seq)
    tf = 256 if d_ff % 256 == 0 else d_ff
    tq = min(128, seq)
    tk = min(256, seq)
    return tm, tf, tq, tk


def kernel(x, c, positions, ada_w, ada_b, norm_g, final_g, ffn_wgu, ffn_wd,
           pool_w_in, pool_w_grp, pool_scale, pool_w_out, dsa_w_in, dsa_w_out):
    batch, seq, d = x.shape
    depth = ada_w.shape[0]
    d_ff = ffn_wd.shape[2]
    t = batch * seq
    tm, tf, tq, tk = _tiles(seq, d_ff)
    tn = min(512, d)
    top_k = min(TOPK_MAX, seq // 4)
    assert seq % tm == 0 and seq % tq == 0 and seq % tk == 0 and tk >= top_k
    assert dsa_w_in.shape[2] == DSA_MAIN_W + DSA_SMALL_W and DSA_MAIN_W % tn == 0

    rows = -(-batch // SUBLANES) * SUBLANES
    c8 = jnp.pad(c, ((0, rows - batch), (0, 0)))
    ada_n = ada_w.shape[2]
    ada = _ada(c8, ada_w, ada_b, 1024 if ada_n % 1024 == 0 else d)
    ada4 = ada.reshape(depth, rows, 3 * N_SUBLAYERS, d)

    tabs = _rope_tables(positions, tm) if depth > 1 else None

    h = x.reshape(t, d)
    for layer in range(depth):
        j = layer // 2
        h = _ffn(h, ada4, norm_g, ffn_wgu, ffn_wd, layer, 0, 0, seq, tm, tf)
        g_row = norm_g[layer, 1][None, :]
        if layer % 2 == 0:
            u = _norm_linear(h, ada4, g_row, pool_w_in[j], layer, 1, seq, tm, tn)
            a = _pool_core(u, pool_w_grp[j], pool_scale[j][None, :], batch, seq)
            h = _linear_residual(a, pool_w_out[j], h, ada4, layer, 1, seq, tm, tn)
        else:
            w_small = jnp.pad(dsa_w_in[j][:, DSA_MAIN_W:], ((0, 0), (0, LANES - DSA_SMALL_W)))
            main, small = _dsa_proj(h, ada4, g_row, dsa_w_in[j], w_small, tabs,
                                    layer, 1, seq, tm, tn)
            o = _dsa_attn(main, small, batch, seq, tq, tk, top_k)
            h = _linear_residual(o, dsa_w_out[j], h, ada4, layer, 1, seq, tm, tn)
        last = layer == depth - 1
        h = _ffn(h, ada4, norm_g, ffn_wgu, ffn_wd, layer, 1, 2, seq, tm, tf,
                 final_g=final_g if last else None)
    return h.reshape(batch, seq, d)
```

```python
import functools

import jax
import jax.numpy as jnp
from jax import lax
from jax.experimental import pallas as pl
from jax.experimental.pallas import tpu as pltpu

F32 = jnp.float32
BF16 = jnp.bfloat16

N_SUBLAYERS = 3
POOL_WINDOWS = (2, 4, 8, 16)
N_HEADS = 16
N_KV_HEADS = 4
KV_GROUP = N_HEADS // N_KV_HEADS
HEAD_DIM = 128
ROT_DIM = HEAD_DIM // 4
ROPE_THETA = 500000.0
IDX_HEADS = 16
IDX_DIM = 64
IDX_ROT_DIM = IDX_DIM // 4
TOPK_MAX = 256
EPS = 1e-6
NEG = -1e30
Q_W = N_HEADS * HEAD_DIM
KV_W = N_KV_HEADS * HEAD_DIM
QI_W = IDX_HEADS * IDX_DIM
DSA_MAIN_W = Q_W + 2 * KV_W + QI_W
DSA_SMALL_W = IDX_DIM + IDX_HEADS

LANES = 128
SUBLANES = 8
VMEM_LIMIT_BYTES = 56 * 1024 * 1024
INT32_MIN = -(2 ** 31)


def _cparams(semantics):
    return pltpu.CompilerParams(dimension_semantics=semantics,
                                vmem_limit_bytes=VMEM_LIMIT_BYTES)


def _rmsnorm(x, g):
    ms = jnp.mean(x * x, axis=-1, keepdims=True)
    return x * lax.rsqrt(ms + EPS) * g


def _norm_mod(x, g, shift, scale):
    return _rmsnorm(x, g) * (1.0 + scale) + shift


def _silu(x):
    return x * jax.nn.sigmoid(x)


def _dot(a, b):
    return jnp.dot(a, b, preferred_element_type=F32)


def _dot_t(a, b):
    return lax.dot_general(a, b, (((1,), (1,)), ((), ())), preferred_element_type=F32)


def _ada_kernel(c_ref, w_ref, b_ref, o_ref):
    sc = _silu(c_ref[...]).astype(BF16)
    o_ref[...] = _dot(sc, w_ref[...].astype(BF16)) + b_ref[...]


def _ada(c8, ada_w, ada_b, tn):
    depth, d, n = ada_w.shape
    rows = c8.shape[0]
    return pl.pallas_call(
        _ada_kernel,
        grid=(depth, n // tn),
        in_specs=[
            pl.BlockSpec((rows, d), lambda l, j: (0, 0)),
            pl.BlockSpec((None, d, tn), lambda l, j: (l, 0, j)),
            pl.BlockSpec((None, 1, tn), lambda l, j: (l, 0, j)),
        ],
        out_specs=pl.BlockSpec((None, rows, tn), lambda l, j: (l, 0, j)),
        out_shape=jax.ShapeDtypeStruct((depth, rows, n), F32),
        compiler_params=_cparams(("arbitrary", "arbitrary")),
        name="ada",
    )(c8, ada_w, ada_b.reshape(depth, 1, n))


def _for_row_slabs(n_rows, slab, body):
    def step(r, carry):
        body(pl.ds(pl.multiple_of(r * slab, slab), slab))
        return carry
    lax.fori_loop(0, n_rows // slab, step, 0)


def _ffn_kernel(sub, final, slab, *refs):
    if final:
        x_ref, g_ref, m_ref, wg_ref, wu_ref, wd_ref, fg_ref, o_ref, y_ref = refs
    else:
        x_ref, g_ref, m_ref, wg_ref, wu_ref, wd_ref, o_ref, y_ref = refs
    k = pl.program_id(1)
    nk = pl.num_programs(1)
    tm = x_ref.shape[0]

    @pl.when(k == 0)
    def _():
        shift = m_ref[3 * sub + 0:3 * sub + 1, :]
        scale = m_ref[3 * sub + 1:3 * sub + 2, :]

        def norm_rows(rows):
            y_ref[rows, :] = _norm_mod(x_ref[rows, :], g_ref[...], shift, scale).astype(BF16)
            o_ref[rows, :] = jnp.zeros((slab, o_ref.shape[1]), F32)

        _for_row_slabs(tm, slab, norm_rows)

    y = y_ref[...]
    g = _dot(y, wg_ref[...].astype(BF16))
    u = _dot(y, wu_ref[...].astype(BF16))
    a = (_silu(g) * u).astype(BF16)
    d_out = o_ref.shape[1]
    nc = min(512, d_out)
    for n in range(d_out // nc):
        cols = slice(n * nc, (n + 1) * nc)
        o_ref[:, cols] += _dot(a, wd_ref[:, cols].astype(BF16))

    @pl.when(k == nk - 1)
    def _():
        gate = 0.5 * m_ref[3 * sub + 2:3 * sub + 3, :]

        def finish_rows(rows):
            res = x_ref[rows, :] + gate * o_ref[rows, :]
            if final:
                res = _rmsnorm(res, fg_ref[...])
            o_ref[rows, :] = res

        _for_row_slabs(tm, slab, finish_rows)


def _ffn(h, ada4, norm_g, ffn_wgu, ffn_wd, layer, which, sub, seq, tm, tf, final_g=None):
    t, d = h.shape
    d_ff = ffn_wd.shape[2]
    nk = d_ff // tf
    tpb = seq // tm
    final = final_g is not None
    in_specs = [
        pl.BlockSpec((tm, d), lambda i, k: (i, 0), pipeline_mode=pl.Buffered(1)),
        pl.BlockSpec((1, d), lambda i, k: (0, 0)),
        pl.BlockSpec((None, None, 3 * N_SUBLAYERS, d), lambda i, k: (layer, i // tpb, 0, 0)),
        pl.BlockSpec((None, None, d, tf), lambda i, k: (layer, which, 0, k)),
        pl.BlockSpec((None, None, d, tf), lambda i, k: (layer, which, 0, nk + k)),
        pl.BlockSpec((None, None, tf, d), lambda i, k: (layer, which, k, 0)),
    ]
    args = [h, norm_g[layer, sub][None, :], ada4, ffn_wgu, ffn_wgu, ffn_wd]
    if final:
        in_specs.append(pl.BlockSpec((1, d), lambda i, k: (0, 0)))
        args.append(final_g[None, :])
    return pl.pallas_call(
        functools.partial(_ffn_kernel, sub, final, min(256, tm)),
        grid=(t // tm, nk),
        in_specs=in_specs,
        out_specs=pl.BlockSpec((tm, d), lambda i, k: (i, 0)),
        out_shape=jax.ShapeDtypeStruct((t, d), F32),
        scratch_shapes=[pltpu.VMEM((tm, d), BF16)],
        compiler_params=_cparams(("arbitrary", "arbitrary")),
        name=f"ffn_l{layer}_{which}",
    )(*args)


def _norm_linear_kernel(sub, x_ref, g_ref, m_ref, w_ref, o_ref, y_ref):
    @pl.when(pl.program_id(1) == 0)
    def _():
        shift = m_ref[3 * sub + 0:3 * sub + 1, :]
        scale = m_ref[3 * sub + 1:3 * sub + 2, :]
        y_ref[...] = _norm_mod(x_ref[...], g_ref[...], shift, scale).astype(BF16)

    o_ref[...] = _dot(y_ref[...], w_ref[...].astype(BF16))


def _norm_linear(h, ada4, g_row, w, widx, layer, sub, seq, tm, tn):
    t, d = h.shape
    n = w.shape[2]
    tpb = seq // tm
    return pl.pallas_call(
        functools.partial(_norm_linear_kernel, sub),
        grid=(t // tm, n // tn),
        in_specs=[
            pl.BlockSpec((tm, d), lambda i, j: (i, 0)),
            pl.BlockSpec((1, d), lambda i, j: (0, 0)),
            pl.BlockSpec((None, None, 3 * N_SUBLAYERS, d), lambda i, j: (layer, i // tpb, 0, 0)),
            pl.BlockSpec((None, d, tn), lambda i, j: (widx, 0, j)),
        ],
        out_specs=pl.BlockSpec((tm, tn), lambda i, j: (i, j)),
        out_shape=jax.ShapeDtypeStruct((t, n), F32),
        scratch_shapes=[pltpu.VMEM((tm, d), BF16)],
        compiler_params=_cparams(("arbitrary", "arbitrary")),
        name=f"norm_linear_l{layer}",
    )(h, g_row, ada4, w)


def _pool_kernel(u_ref, w_ref, ls_ref, o_ref):
    g = pl.program_id(1)
    u = u_ref[...]
    t = lax.broadcasted_iota(jnp.int32, u.shape, 0)

    def shifted(v, d):
        return jnp.where(t >= d, pltpu.roll(v, d, 0), 0.0)

    for gi, win in enumerate(POOL_WINDOWS):
        @pl.when(g == gi)
        def _(win=win):
            acc = u
            d = 1
            while d < win:
                acc = acc + shifted(acc, d)
                d *= 2
            cnt = jnp.minimum(t + 1, win).astype(F32)
            p = (acc / cnt - u).astype(BF16)
            v = _dot(p, w_ref[...].astype(BF16)) * ls_ref[...]
            o_ref[...] = v.astype(BF16)


def _pool_core(u, w_grp, ls, widx, batch, seq):
    t, d = u.shape
    _, groups, c, _ = w_grp.shape
    return pl.pallas_call(
        _pool_kernel,
        grid=(batch, groups),
        in_specs=[
            pl.BlockSpec((seq, c), lambda b, g: (b, g)),
            pl.BlockSpec((None, None, c, c), lambda b, g: (widx, g, 0, 0)),
            pl.BlockSpec((None, 1, c), lambda b, g: (widx, 0, g)),
        ],
        out_specs=pl.BlockSpec((seq, c), lambda b, g: (b, g)),
        out_shape=jax.ShapeDtypeStruct((t, d), BF16),
        compiler_params=_cparams(("arbitrary", "arbitrary")),
        name="pool_core",
    )(u, w_grp, ls.reshape(ls.shape[0], 1, d))


def _linear_residual_kernel(sub, a_ref, w_ref, h_ref, m_ref, o_ref):
    gate = m_ref[3 * sub + 2:3 * sub + 3, :]
    o_ref[...] = h_ref[...] + gate * _dot(a_ref[...], w_ref[...].astype(BF16))


def _linear_residual(a, w, widx, h, ada4, layer, sub, seq, tm, tn):
    t, kdim = a.shape
    n = w.shape[2]
    tpb = seq // tm
    return pl.pallas_call(
        functools.partial(_linear_residual_kernel, sub),
        grid=(t // tm, n // tn),
        in_specs=[
            pl.BlockSpec((tm, kdim), lambda i, j: (i, 0)),
            pl.BlockSpec((None, kdim, tn), lambda i, j: (widx, 0, j)),
            pl.BlockSpec((tm, tn), lambda i, j: (i, j)),
            pl.BlockSpec((None, None, 3 * N_SUBLAYERS, tn), lambda i, j: (layer, i // tpb, 0, j)),
        ],
        out_specs=pl.BlockSpec((tm, tn), lambda i, j: (i, j)),
        out_shape=jax.ShapeDtypeStruct((t, n), F32),
        compiler_params=_cparams(("arbitrary", "arbitrary")),
        name=f"linear_residual_l{layer}",
    )(a, w, h, ada4)


def _rope_table_kernel(pos_ref, invf_ref, o_ref):
    pos = pos_ref[...].astype(F32)
    lane = lax.broadcasted_iota(jnp.int32, (pos.shape[0], LANES), 1)

    def tables(invf, lane_in_head, rot_dim):
        half = rot_dim // 2
        ang = pos * invf
        cos = jnp.cos(ang)
        sin = jnp.sin(ang)
        c = jnp.where(lane_in_head < rot_dim, cos, 1.0)
        sa = jnp.where(lane_in_head < half, -sin, 0.0)
        sb = jnp.where(lane_in_head < rot_dim, jnp.where(lane_in_head >= half, sin, 0.0), 0.0)
        return c, sa, sb

    big = tables(invf_ref[:, 0:LANES], lane, ROT_DIM)
    small = tables(invf_ref[:, LANES:2 * LANES], lane & (IDX_DIM - 1), IDX_ROT_DIM)
    for n, tab in enumerate(big + small):
        o_ref[:, n * LANES:(n + 1) * LANES] = tab


def _rope_tables(positions, tm):
    t = positions.size
    lane = jnp.arange(LANES)

    def inv_freq(rot_dim):
        half = rot_dim // 2
        return ROPE_THETA ** (-jnp.arange(half, dtype=F32) / half)

    invf = jnp.concatenate([inv_freq(ROT_DIM)[lane % (ROT_DIM // 2)],
                            inv_freq(IDX_ROT_DIM)[lane % (IDX_ROT_DIM // 2)]])[None, :]
    return pl.pallas_call(
        _rope_table_kernel,
        grid=(t // tm,),
        in_specs=[
            pl.BlockSpec((tm, 1), lambda i: (i, 0)),
            pl.BlockSpec((1, 2 * LANES), lambda i: (0, 0)),
        ],
        out_specs=pl.BlockSpec((tm, 6 * LANES), lambda i: (i, 0)),
        out_shape=jax.ShapeDtypeStruct((t, 6 * LANES), F32),
        compiler_params=_cparams(("arbitrary",)),
        name="rope_tables",
    )(positions.reshape(t, 1), invf)


def _rope(x, c, sa, sb, half):
    return x * c + pltpu.roll(x, LANES - half, 1) * sa + pltpu.roll(x, half, 1) * sb


def _dsa_proj_kernel(sub, tn, x_ref, g_ref, m_ref, w_ref, ws_ref, tab_ref, o_ref, os_ref, y_ref):
    j = pl.program_id(1)
    n_rope_big = (Q_W + KV_W) // tn
    n_plain = KV_W // tn

    def tab(n):
        return tab_ref[:, n * LANES:(n + 1) * LANES]

    @pl.when(j == 0)
    def _():
        shift = m_ref[3 * sub + 0:3 * sub + 1, :]
        scale = m_ref[3 * sub + 1:3 * sub + 2, :]
        y = _norm_mod(x_ref[...], g_ref[...], shift, scale).astype(BF16)
        y_ref[...] = y
        small = _dot(y, ws_ref[...].astype(BF16))
        lane = lax.broadcasted_iota(jnp.int32, small.shape, 1)
        is_ki = lane < IDX_DIM
        c = jnp.where(is_ki, tab(3), 1.0)
        sa = jnp.where(is_ki, tab(4), 0.0)
        sb = jnp.where(is_ki, tab(5), 0.0)
        os_ref[...] = _rope(small, c, sa, sb, IDX_ROT_DIM // 2)

    acc = _dot(y_ref[...], w_ref[...].astype(BF16))

    def store_rope(first, half):
        for s in range(tn // LANES):
            xs = acc[:, s * LANES:(s + 1) * LANES]
            o_ref[:, s * LANES:(s + 1) * LANES] = _rope(
                xs, tab(first), tab(first + 1), tab(first + 2), half).astype(BF16)

    @pl.when(j < n_rope_big)
    def _():
        store_rope(0, ROT_DIM // 2)

    @pl.when(jnp.logical_and(j >= n_rope_big, j < n_rope_big + n_plain))
    def _():
        o_ref[...] = acc.astype(BF16)

    @pl.when(j >= n_rope_big + n_plain)
    def _():
        store_rope(3, IDX_ROT_DIM // 2)


def _dsa_proj(h, ada4, g_row, w_in, widx, w_small, tabs, layer, sub, seq, tm, tn):
    t, d = h.shape
    tpb = seq // tm
    return pl.pallas_call(
        functools.partial(_dsa_proj_kernel, sub, tn),
        grid=(t // tm, DSA_MAIN_W // tn),
        in_specs=[
            pl.BlockSpec((tm, d), lambda i, j: (i, 0)),
            pl.BlockSpec((1, d), lambda i, j: (0, 0)),
            pl.BlockSpec((None, None, 3 * N_SUBLAYERS, d), lambda i, j: (layer, i // tpb, 0, 0)),
            pl.BlockSpec((None, d, tn), lambda i, j: (widx, 0, j)),
            pl.BlockSpec((d, LANES), lambda i, j: (0, 0)),
            pl.BlockSpec((tm, 6 * LANES), lambda i, j: (i, 0)),
        ],
        out_specs=[
            pl.BlockSpec((tm, tn), lambda i, j: (i, j)),
            pl.BlockSpec((tm, LANES), lambda i, j: (i, 0)),
        ],
        out_shape=[
            jax.ShapeDtypeStruct((t, DSA_MAIN_W), BF16),
            jax.ShapeDtypeStruct((t, LANES), F32),
        ],
        scratch_shapes=[pltpu.VMEM((tm, d), BF16)],
        compiler_params=_cparams(("arbitrary", "arbitrary")),
        name=f"dsa_proj_l{layer}",
    )(h, g_row, ada4, w_in, w_small, tabs)


def _dsa_attn_kernel(tq, tk, top_k, q_ref, qi_ref, sq_ref, k_ref, v_ref, skv_ref, o_ref,
                     score_ref, bias_ref):
    i = pl.program_id(1)
    n_chunks = ((i + 1) * tq + tk - 1) // tk
    t_pos = i * tq + lax.broadcasted_iota(jnp.int32, (tq, tk), 0)
    k_iota = lax.broadcasted_iota(jnp.int32, (tq, tk), 1)

    wi = sq_ref[:, IDX_DIM:IDX_DIM + IDX_HEADS]

    def score_chunk(c, carry):
        off = pl.multiple_of(c * tk, tk)
        ki = skv_ref[pl.ds(off, tk), :][:, 0:IDX_DIM].astype(BF16)
        acc = jnp.zeros((tq, tk), F32)
        for hh in range(IDX_HEADS):
            r = _dot_t(qi_ref[:, hh * IDX_DIM:(hh + 1) * IDX_DIM], ki)
            acc = acc + wi[:, hh:hh + 1] * jnp.maximum(r, 0.0)
        score_ref[:, pl.ds(off, tk)] = jnp.where(k_iota + off <= t_pos, acc, NEG)
        return carry

    lax.fori_loop(0, n_chunks, score_chunk, 0)

    def key_to_f32(key):
        return pltpu.bitcast(key ^ ((key >> 31) & 0x7FFFFFFF), F32)

    def count_ge(thr_key):
        thr = jnp.broadcast_to(key_to_f32(thr_key), (tq, LANES))

        def body(c, cnt):
            off = pl.multiple_of(c * tk, tk)
            for jj in range(tk // LANES):
                s = score_ref[:, pl.ds(off + jj * LANES, LANES)]
                cnt = cnt + jnp.where(s >= thr, 1.0, 0.0)
            return cnt

        cnt = lax.fori_loop(0, n_chunks, body, jnp.zeros((tq, LANES), F32))
        return jnp.sum(cnt, axis=1, keepdims=True)

    k_f = float(top_k)
    thr0 = jnp.where(count_ge(jnp.zeros((tq, 1), jnp.int32)) >= k_f, 0, INT32_MIN)

    def bit_step(n, thr_key):
        cand = thr_key + lax.shift_left(jnp.int32(1), 30 - n)
        return jnp.where(count_ge(cand) >= k_f, cand, thr_key)

    thr = key_to_f32(lax.fori_loop(0, 31, bit_step, thr0.astype(jnp.int32)))

    def bias_chunk(c, carry):
        off = pl.multiple_of(c * tk, tk)
        sel = score_ref[:, pl.ds(off, tk)] >= thr
        causal = k_iota + off <= t_pos
        bias_ref[:, pl.ds(off, tk)] = jnp.where(sel, jnp.where(causal, 0.0, NEG), NEG)
        return carry

    lax.fori_loop(0, n_chunks, bias_chunk, 0)

    scale = HEAD_DIM ** -0.5
    rows = KV_GROUP * tq
    for g in range(N_KV_HEADS):
        qg = jnp.concatenate(
            [q_ref[:, (g * KV_GROUP + r) * HEAD_DIM:(g * KV_GROUP + r + 1) * HEAD_DIM]
             for r in range(KV_GROUP)], axis=0)

        def attn_chunk(c, carry, g=g, qg=qg):
            m, l, acc = carry
            off = pl.multiple_of(c * tk, tk)
            kc = k_ref[pl.ds(off, tk), g * HEAD_DIM:(g + 1) * HEAD_DIM]
            vc = v_ref[pl.ds(off, tk), g * HEAD_DIM:(g + 1) * HEAD_DIM]
            s = _dot_t(qg, kc) * scale
            b = bias_ref[:, pl.ds(off, tk)]
            s = (s.reshape(KV_GROUP, tq, tk) + b[None]).reshape(rows, tk)
            m_new = jnp.maximum(m, jnp.max(s, axis=1, keepdims=True))
            alpha = jnp.exp(m - m_new)
            p = jnp.exp(s - m_new)
            l = alpha * l + jnp.sum(p, axis=1, keepdims=True)
            acc = alpha * acc + _dot(p.astype(BF16), vc)
            return m_new, l, acc

        m0 = jnp.full((rows, 1), NEG, F32)
        l0 = jnp.zeros((rows, 1), F32)
        a0 = jnp.zeros((rows, HEAD_DIM), F32)
        _, l, acc = lax.fori_loop(0, n_chunks, attn_chunk, (m0, l0, a0))
        out = acc / l
        for r in range(KV_GROUP):
            hd = g * KV_GROUP + r
            o_ref[:, hd * HEAD_DIM:(hd + 1) * HEAD_DIM] = out[r * tq:(r + 1) * tq].astype(BF16)


def _dsa_attn(main, small, batch, seq, tq, tk, top_k):
    t = main.shape[0]
    nq = seq // tq
    return pl.pallas_call(
        functools.partial(_dsa_attn_kernel, tq, tk, top_k),
        grid=(batch, nq),
        in_specs=[
            pl.BlockSpec((tq, Q_W), lambda b, i: (b * nq + i, 0)),
            pl.BlockSpec((tq, QI_W), lambda b, i: (b * nq + i, (Q_W + 2 * KV_W) // QI_W)),
            pl.BlockSpec((tq, LANES), lambda b, i: (b * nq + i, 0)),
            pl.BlockSpec((seq, KV_W), lambda b, i: (b, Q_W // KV_W)),
            pl.BlockSpec((seq, KV_W), lambda b, i: (b, Q_W // KV_W + 1)),
            pl.BlockSpec((seq, LANES), lambda b, i: (b, 0)),
        ],
        out_specs=pl.BlockSpec((tq, Q_W), lambda b, i: (b * nq + i, 0)),
        out_shape=jax.ShapeDtypeStruct((t, Q_W), BF16),
        scratch_shapes=[pltpu.VMEM((tq, seq), F32), pltpu.VMEM((tq, seq), F32)],
        compiler_params=_cparams(("arbitrary", "arbitrary")),
        name="dsa_attn",
    )(main, main, small, main, main, small)


def _tiles(seq, d_ff):
    tm = min(1024, seq)
    tf = 256 if d_ff % 256 == 0 else d_ff
    tq = min(128, seq)
    tk = min(256, seq)
    return tm, tf, tq, tk


def kernel(x, c, positions, ada_w, ada_b, norm_g, final_g, ffn_wgu, ffn_wd,
           pool_w_in, pool_w_grp, pool_scale, pool_w_out, dsa_w_in, dsa_w_out):
    batch, seq, d = x.shape
    depth = ada_w.shape[0]
    d_ff = ffn_wd.shape[2]
    t = batch * seq
    tm, tf, tq, tk = _tiles(seq, d_ff)
    tn = min(512, d)
    top_k = min(TOPK_MAX, seq // 4)
    assert seq % tm == 0 and seq % tq == 0 and seq % tk == 0 and tk >= top_k
    assert dsa_w_in.shape[2] == DSA_MAIN_W + DSA_SMALL_W and DSA_MAIN_W % tn == 0

    rows = -(-batch // SUBLANES) * SUBLANES
    c8 = jnp.pad(c, ((0, rows - batch), (0, 0)))
    ada_n = ada_w.shape[2]
    ada = _ada(c8, ada_w, ada_b, 1024 if ada_n % 1024 == 0 else d)
    ada4 = ada.reshape(depth, rows, 3 * N_SUBLAYERS, d)

    tabs = _rope_tables(positions, tm) if depth > 1 else None

    h = x.reshape(t, d)
    for layer in range(depth):
        j = layer // 2
        h = _ffn(h, ada4, norm_g, ffn_wgu, ffn_wd, layer, 0, 0, seq, tm, tf)
        g_row = norm_g[layer, 1][None, :]
        if layer % 2 == 0:
            u = _norm_linear(h, ada4, g_row, pool_w_in, j, layer, 1, seq, tm, tn)
            a = _pool_core(u, pool_w_grp, pool_scale, j, batch, seq)
            h = _linear_residual(a, pool_w_out, j, h, ada4, layer, 1, seq, tm, tn)
        else:
            w_small = jnp.pad(dsa_w_in[j, :, DSA_MAIN_W:], ((0, 0), (0, LANES - DSA_SMALL_W)))
            main, small = _dsa_proj(h, ada4, g_row, dsa_w_in, j, w_small, tabs,
                                    layer, 1, seq, tm, tn)
            o = _dsa_attn(main, small, batch, seq, tq, tk, top_k)
            h = _linear_residual(o, dsa_w_out, j, h, ada4, layer, 1, seq, tm, tn)
        last = layer == depth - 1
        h = _ffn(h, ada4, norm_g, ffn_wgu, ffn_wd, layer, 1, 2, seq, tm, tf,
                 final_g=final_g if last else None)
    return h.reshape(batch, seq, d)
```

```python
import functools

import jax
import jax.numpy as jnp
from jax import lax
from jax.experimental import pallas as pl
from jax.experimental.pallas import tpu as pltpu

F32 = jnp.float32
BF16 = jnp.bfloat16

N_SUBLAYERS = 3
POOL_WINDOWS = (2, 4, 8, 16)
N_HEADS = 16
N_KV_HEADS = 4
KV_GROUP = N_HEADS // N_KV_HEADS
HEAD_DIM = 128
ROT_DIM = HEAD_DIM // 4
ROPE_THETA = 500000.0
IDX_HEADS = 16
IDX_DIM = 64
IDX_ROT_DIM = IDX_DIM // 4
TOPK_MAX = 256
EPS = 1e-6
NEG = -1e30
Q_W = N_HEADS * HEAD_DIM
KV_W = N_KV_HEADS * HEAD_DIM
QI_W = IDX_HEADS * IDX_DIM
DSA_MAIN_W = Q_W + 2 * KV_W + QI_W
DSA_SMALL_W = IDX_DIM + IDX_HEADS
Q_PRESCALE = HEAD_DIM ** -0.5 * 1.4426950408889634

LANES = 128
SUBLANES = 8
VMEM_LIMIT_BYTES = 56 * 1024 * 1024
INT32_MIN = -(2 ** 31)


def _cparams(semantics):
    return pltpu.CompilerParams(dimension_semantics=semantics,
                                vmem_limit_bytes=VMEM_LIMIT_BYTES)


def _rmsnorm(x, g):
    ms = jnp.mean(x * x, axis=-1, keepdims=True)
    return x * lax.rsqrt(ms + EPS) * g


NORM_SLAB = 16
NORM_UNROLL = 4


def _for_row_slabs(n_rows, slab, body, unroll=1):
    def step(r, carry):
        body(pl.ds(pl.multiple_of(r * slab, slab), slab))
        return carry
    lax.fori_loop(0, n_rows // slab, step, 0, unroll=unroll)


def _write_norm_mod(x_ref, g_ref, m_ref, sub, y_ref, also=None):
    shift = m_ref[3 * sub + 0:3 * sub + 1, :]
    gs = g_ref[...] * (1.0 + m_ref[3 * sub + 1:3 * sub + 2, :])
    slab = min(NORM_SLAB, x_ref.shape[0])

    def norm_rows(rows):
        x = x_ref[rows, :]
        r = lax.rsqrt(jnp.mean(x * x, axis=-1, keepdims=True) + EPS)
        y_ref[rows, :] = ((x * r) * gs + shift).astype(BF16)
        if also is not None:
            also(rows)

    _for_row_slabs(x_ref.shape[0], slab, norm_rows, unroll=NORM_UNROLL)


def _silu(x):
    return x * jax.nn.sigmoid(x)


def _dot(a, b):
    return jnp.dot(a, b, preferred_element_type=F32)


def _dot_t(a, b):
    return lax.dot_general(a, b, (((1,), (1,)), ((), ())), preferred_element_type=F32)


def _ada_kernel(c_ref, w_ref, b_ref, o_ref):
    sc = _silu(c_ref[...]).astype(BF16)
    o_ref[...] = _dot(sc, w_ref[...].astype(BF16)) + b_ref[...]


def _ada(c8, ada_w, ada_b, tn):
    depth, d, n = ada_w.shape
    rows = c8.shape[0]
    return pl.pallas_call(
        _ada_kernel,
        grid=(depth, n // tn),
        in_specs=[
            pl.BlockSpec((rows, d), lambda l, j: (0, 0)),
            pl.BlockSpec((None, d, tn), lambda l, j: (l, 0, j)),
            pl.BlockSpec((None, 1, tn), lambda l, j: (l, 0, j)),
        ],
        out_specs=pl.BlockSpec((None, rows, tn), lambda l, j: (l, 0, j)),
        out_shape=jax.ShapeDtypeStruct((depth, rows, n), F32),
        compiler_params=_cparams(("arbitrary", "arbitrary")),
        name="ada",
    )(c8, ada_w, ada_b.reshape(depth, 1, n))


def _ffn_kernel(sub, final, slab, *refs):
    if final:
        x_ref, g_ref, m_ref, wg_ref, wu_ref, wd_ref, fg_ref, o_ref, y_ref = refs
    else:
        x_ref, g_ref, m_ref, wg_ref, wu_ref, wd_ref, o_ref, y_ref = refs
    k = pl.program_id(1)
    nk = pl.num_programs(1)
    tm = x_ref.shape[0]

    @pl.when(k == 0)
    def _():
        def zero_rows(rows):
            o_ref[rows, :] = jnp.zeros((rows.size, o_ref.shape[1]), F32)

        _write_norm_mod(x_ref, g_ref, m_ref, sub, y_ref, also=zero_rows)

    y = y_ref[...]
    g = _dot(y, wg_ref[...].astype(BF16))
    u = _dot(y, wu_ref[...].astype(BF16))
    a = (_silu(g) * u).astype(BF16)
    d_out = o_ref.shape[1]
    nc = min(512, d_out)
    for n in range(d_out // nc):
        cols = slice(n * nc, (n + 1) * nc)
        o_ref[:, cols] += _dot(a, wd_ref[:, cols].astype(BF16))

    @pl.when(k == nk - 1)
    def _():
        gate = 0.5 * m_ref[3 * sub + 2:3 * sub + 3, :]

        def finish_rows(rows):
            res = x_ref[rows, :] + gate * o_ref[rows, :]
            if final:
                res = _rmsnorm(res, fg_ref[...])
            o_ref[rows, :] = res

        _for_row_slabs(tm, slab, finish_rows)


def _ffn(h, ada4, norm_g, ffn_wgu, ffn_wd, layer, which, sub, seq, tm, tf, final_g=None):
    t, d = h.shape
    d_ff = ffn_wd.shape[2]
    nk = d_ff // tf
    tpb = seq // tm
    final = final_g is not None
    in_specs = [
        pl.BlockSpec((tm, d), lambda i, k: (i, 0), pipeline_mode=pl.Buffered(1)),
        pl.BlockSpec((1, d), lambda i, k: (0, 0)),
        pl.BlockSpec((None, None, 3 * N_SUBLAYERS, d), lambda i, k: (layer, i // tpb, 0, 0)),
        pl.BlockSpec((None, None, d, tf), lambda i, k: (layer, which, 0, k)),
        pl.BlockSpec((None, None, d, tf), lambda i, k: (layer, which, 0, nk + k)),
        pl.BlockSpec((None, None, tf, d), lambda i, k: (layer, which, k, 0)),
    ]
    args = [h, norm_g[layer, sub][None, :], ada4, ffn_wgu, ffn_wgu, ffn_wd]
    if final:
        in_specs.append(pl.BlockSpec((1, d), lambda i, k: (0, 0)))
        args.append(final_g[None, :])
    return pl.pallas_call(
        functools.partial(_ffn_kernel, sub, final, min(256, tm)),
        grid=(t // tm, nk),
        in_specs=in_specs,
        out_specs=pl.BlockSpec((tm, d), lambda i, k: (i, 0)),
        out_shape=jax.ShapeDtypeStruct((t, d), F32),
        scratch_shapes=[pltpu.VMEM((tm, d), BF16)],
        compiler_params=_cparams(("arbitrary", "arbitrary")),
        name=f"ffn_l{layer}_{which}",
    )(*args)


def _keep_bf16_weight(w_ref, wbf_ref):
    j = pl.program_id(1)

    @pl.when(pl.program_id(0) == 0)
    def _():
        wbf_ref[j] = w_ref[...].astype(BF16)

    return wbf_ref[j]


def _resident_weight_spec(widx, kdim, tn, nj):
    return pl.BlockSpec((None, kdim, tn), lambda i, j: (widx, 0, jnp.where(i == 0, j, nj - 1)))


def _norm_linear_kernel(sub, x_ref, g_ref, m_ref, w_ref, o_ref, y_ref, wbf_ref):
    @pl.when(pl.program_id(1) == 0)
    def _():
        _write_norm_mod(x_ref, g_ref, m_ref, sub, y_ref)

    o_ref[...] = _dot(y_ref[...], _keep_bf16_weight(w_ref, wbf_ref))


def _norm_linear(h, ada4, g_row, w, widx, layer, sub, seq, tm, tn):
    t, d = h.shape
    n = w.shape[2]
    tpb = seq // tm
    return pl.pallas_call(
        functools.partial(_norm_linear_kernel, sub),
        grid=(t // tm, n // tn),
        in_specs=[
            pl.BlockSpec((tm, d), lambda i, j: (i, 0)),
            pl.BlockSpec((1, d), lambda i, j: (0, 0)),
            pl.BlockSpec((None, None, 3 * N_SUBLAYERS, d), lambda i, j: (layer, i // tpb, 0, 0)),
            _resident_weight_spec(widx, d, tn, n // tn),
        ],
        out_specs=pl.BlockSpec((tm, tn), lambda i, j: (i, j)),
        out_shape=jax.ShapeDtypeStruct((t, n), F32),
        scratch_shapes=[pltpu.VMEM((tm, d), BF16), pltpu.VMEM((n // tn, d, tn), BF16)],
        compiler_params=_cparams(("arbitrary", "arbitrary")),
        name=f"norm_linear_l{layer}",
    )(h, g_row, ada4, w)


def _pool_kernel(u_ref, w_ref, ls_ref, o_ref):
    g = pl.program_id(1)
    u = u_ref[...]
    t = lax.broadcasted_iota(jnp.int32, u.shape, 0)

    def shifted(v, d):
        return jnp.where(t >= d, pltpu.roll(v, d, 0), 0.0)

    for gi, win in enumerate(POOL_WINDOWS):
        @pl.when(g == gi)
        def _(win=win):
            acc = u
            d = 1
            while d < win:
                acc = acc + shifted(acc, d)
                d *= 2
            cnt = jnp.minimum(t + 1, win).astype(F32)
            p = (acc / cnt - u).astype(BF16)
            v = _dot(p, w_ref[...].astype(BF16)) * ls_ref[...]
            o_ref[...] = v.astype(BF16)


def _pool_core(u, w_grp, ls, widx, batch, seq):
    t, d = u.shape
    _, groups, c, _ = w_grp.shape
    return pl.pallas_call(
        _pool_kernel,
        grid=(batch, groups),
        in_specs=[
            pl.BlockSpec((seq, c), lambda b, g: (b, g)),
            pl.BlockSpec((None, None, c, c), lambda b, g: (widx, g, 0, 0)),
            pl.BlockSpec((None, 1, c), lambda b, g: (widx, 0, g)),
        ],
        out_specs=pl.BlockSpec((seq, c), lambda b, g: (b, g)),
        out_shape=jax.ShapeDtypeStruct((t, d), BF16),
        compiler_params=_cparams(("arbitrary", "arbitrary")),
        name="pool_core",
    )(u, w_grp, ls.reshape(ls.shape[0], 1, d))


def _linear_residual_kernel(sub, a_ref, w_ref, h_ref, m_ref, o_ref, wbf_ref):
    gate = m_ref[3 * sub + 2:3 * sub + 3, :]
    o_ref[...] = h_ref[...] + gate * _dot(a_ref[...], _keep_bf16_weight(w_ref, wbf_ref))


def _linear_residual(a, w, widx, h, ada4, layer, sub, seq, tm, tn):
    t, kdim = a.shape
    n = w.shape[2]
    tpb = seq // tm
    return pl.pallas_call(
        functools.partial(_linear_residual_kernel, sub),
        grid=(t // tm, n // tn),
        in_specs=[
            pl.BlockSpec((tm, kdim), lambda i, j: (i, 0)),
            _resident_weight_spec(widx, kdim, tn, n // tn),
            pl.BlockSpec((tm, tn), lambda i, j: (i, j)),
            pl.BlockSpec((None, None, 3 * N_SUBLAYERS, tn), lambda i, j: (layer, i // tpb, 0, j)),
        ],
        out_specs=pl.BlockSpec((tm, tn), lambda i, j: (i, j)),
        out_shape=jax.ShapeDtypeStruct((t, n), F32),
        scratch_shapes=[pltpu.VMEM((n // tn, kdim, tn), BF16)],
        compiler_params=_cparams(("arbitrary", "arbitrary")),
        name=f"linear_residual_l{layer}",
    )(a, w, h, ada4)


def _rope_table_kernel(pos_ref, invf_ref, o_ref):
    pos = pos_ref[...].astype(F32)
    lane = lax.broadcasted_iota(jnp.int32, (pos.shape[0], LANES), 1)

    def tables(invf, lane_in_head, rot_dim):
        half = rot_dim // 2
        ang = pos * invf
        cos = jnp.cos(ang)
        sin = jnp.sin(ang)
        c = jnp.where(lane_in_head < rot_dim, cos, 1.0)
        sa = jnp.where(lane_in_head < half, -sin, 0.0)
        sb = jnp.where(lane_in_head < rot_dim, jnp.where(lane_in_head >= half, sin, 0.0), 0.0)
        return c, sa, sb

    big = tables(invf_ref[:, 0:LANES], lane, ROT_DIM)
    small = tables(invf_ref[:, LANES:2 * LANES], lane & (IDX_DIM - 1), IDX_ROT_DIM)
    for n, tab in enumerate(big + small):
        o_ref[:, n * LANES:(n + 1) * LANES] = tab


def _rope_tables(positions, tm):
    t = positions.size
    lane = jnp.arange(LANES)

    def inv_freq(rot_dim):
        half = rot_dim // 2
        return ROPE_THETA ** (-jnp.arange(half, dtype=F32) / half)

    invf = jnp.concatenate([inv_freq(ROT_DIM)[lane % (ROT_DIM // 2)],
                            inv_freq(IDX_ROT_DIM)[lane % (IDX_ROT_DIM // 2)]])[None, :]
    return pl.pallas_call(
        _rope_table_kernel,
        grid=(t // tm,),
        in_specs=[
            pl.BlockSpec((tm, 1), lambda i: (i, 0)),
            pl.BlockSpec((1, 2 * LANES), lambda i: (0, 0)),
        ],
        out_specs=pl.BlockSpec((tm, 6 * LANES), lambda i: (i, 0)),
        out_shape=jax.ShapeDtypeStruct((t, 6 * LANES), F32),
        compiler_params=_cparams(("arbitrary",)),
        name="rope_tables",
    )(positions.reshape(t, 1), invf)


def _rope(x, c, sa, sb, half):
    return x * c + pltpu.roll(x, LANES - half, 1) * sa + pltpu.roll(x, half, 1) * sb


def _dsa_proj_kernel(sub, tn, x_ref, g_ref, m_ref, w_ref, ws_ref, tab_ref, o_ref, ok_ref, ow_ref,
                     y_ref, wbf_ref):
    j = pl.program_id(1)
    n_rope_big = (Q_W + KV_W) // tn
    n_plain = KV_W // tn

    def tab(n):
        return tab_ref[:, n * LANES:(n + 1) * LANES]

    @pl.when(j == 0)
    def _():
        _write_norm_mod(x_ref, g_ref, m_ref, sub, y_ref)
        small = _dot(y_ref[...], ws_ref[...])
        for s in range(2):
            ok_ref[:, s * LANES:(s + 1) * LANES] = _rope(
                small[:, s * LANES:(s + 1) * LANES], tab(3), tab(4), tab(5),
                IDX_ROT_DIM // 2).astype(BF16)
        ow_ref[...] = small[:, 2 * LANES:3 * LANES]

    acc = _dot(y_ref[...], _keep_bf16_weight(w_ref, wbf_ref))

    def store_rope(first, half, mult=None):
        for s in range(tn // LANES):
            xs = acc[:, s * LANES:(s + 1) * LANES]
            r = _rope(xs, tab(first), tab(first + 1), tab(first + 2), half)
            if mult is not None:
                r = r * mult
            o_ref[:, s * LANES:(s + 1) * LANES] = r.astype(BF16)

    @pl.when(j < Q_W // tn)
    def _():
        store_rope(0, ROT_DIM // 2, Q_PRESCALE)

    @pl.when(jnp.logical_and(j >= Q_W // tn, j < n_rope_big))
    def _():
        store_rope(0, ROT_DIM // 2)

    @pl.when(jnp.logical_and(j >= n_rope_big, j < n_rope_big + n_plain))
    def _():
        o_ref[...] = acc.astype(BF16)

    @pl.when(j >= n_rope_big + n_plain)
    def _():
        store_rope(3, IDX_ROT_DIM // 2)


def _dsa_proj(h, ada4, g_row, w_in, widx, w_small, tabs, layer, sub, seq, tm, tn):
    t, d = h.shape
    tpb = seq // tm
    return pl.pallas_call(
        functools.partial(_dsa_proj_kernel, sub, tn),
        grid=(t // tm, DSA_MAIN_W // tn),
        in_specs=[
            pl.BlockSpec((tm, d), lambda i, j: (i, 0), pipeline_mode=pl.Buffered(1)),
            pl.BlockSpec((1, d), lambda i, j: (0, 0)),
            pl.BlockSpec((None, None, 3 * N_SUBLAYERS, d), lambda i, j: (layer, i // tpb, 0, 0)),
            _resident_weight_spec(widx, d, tn, DSA_MAIN_W // tn),
            pl.BlockSpec((d, 3 * LANES), lambda i, j: (0, 0)),
            pl.BlockSpec((tm, 6 * LANES), lambda i, j: (i, 0)),
        ],
        out_specs=[
            pl.BlockSpec((tm, tn), lambda i, j: (i, j)),
            pl.BlockSpec((tm, 2 * LANES), lambda i, j: (i, 0)),
            pl.BlockSpec((tm, LANES), lambda i, j: (i, 0)),
        ],
        out_shape=[
            jax.ShapeDtypeStruct((t, DSA_MAIN_W), BF16),
            jax.ShapeDtypeStruct((t, 2 * LANES), BF16),
            jax.ShapeDtypeStruct((t, LANES), F32),
        ],
        scratch_shapes=[pltpu.VMEM((tm, d), BF16),
                        pltpu.VMEM((DSA_MAIN_W // tn, d, tn), BF16)],
        compiler_params=_cparams(("arbitrary", "arbitrary")),
        name=f"dsa_proj_l{layer}",
    )(h, g_row, ada4, w_in, w_small, tabs)


def _dsa_attn_kernel(tq, tk, top_k, q_ref, qi_ref, wi_ref, k_ref, v_ref, kk_ref, o_ref,
                     score_ref, bias_ref, thr_ref, wib_ref, qg_ref, m_ref, acc_ref):
    i = pl.program_id(1)
    n_chunks = ((i + 1) * tq + tk - 1) // tk
    max_chunks = score_ref.shape[1] // tk
    t_pos = i * tq + lax.broadcasted_iota(jnp.int32, (tq, tk), 0)
    k_iota = lax.broadcasted_iota(jnp.int32, (tq, tk), 1)
    n_cb = tk // LANES

    for hh in range(IDX_HEADS):
        wib_ref[hh] = jnp.broadcast_to(wi_ref[:, hh:hh + 1], (tq, LANES))
    for g in range(N_KV_HEADS):
        for r in range(KV_GROUP):
            hd = g * KV_GROUP + r
            qg_ref[g, r * tq:(r + 1) * tq, :] = q_ref[:, hd * HEAD_DIM:(hd + 1) * HEAD_DIM]

    def score_chunk(c, carry):
        off = pl.multiple_of(c * tk, tk)
        k_even = kk_ref[pl.ds(off, tk), 0:LANES]
        k_odd = kk_ref[pl.ds(off, tk), LANES:2 * LANES]
        acc = [jnp.zeros((tq, LANES), F32) for _ in range(n_cb)]
        for pair in range(IDX_HEADS // 2):
            qp = qi_ref[:, pair * LANES:(pair + 1) * LANES]
            for par, kop in enumerate((k_even, k_odd)):
                r = _dot_t(qp, kop)
                w = wib_ref[2 * pair + par]
                for cb in range(n_cb):
                    acc[cb] = acc[cb] + w * jnp.maximum(r[:, cb * LANES:(cb + 1) * LANES], 0.0)
        sc = jnp.concatenate(acc, axis=1)
        score_ref[:, pl.ds(off, tk)] = jnp.where(k_iota + off <= t_pos, sc, NEG)
        return carry

    lax.fori_loop(0, n_chunks, score_chunk, 0)

    def key_to_f32(key):
        return pltpu.bitcast(key ^ ((key >> 31) & 0x7FFFFFFF), F32)

    k_f = float(top_k)

    def select(n_static):
        def bit_step(n, thr_key):
            cand = thr_key + lax.shift_left(jnp.int32(1), 31 - n)
            thr = key_to_f32(cand)
            cnt = jnp.zeros((tq, LANES), F32)
            for cb in range(n_static * n_cb):
                s = score_ref[:, cb * LANES:(cb + 1) * LANES]
                cnt = cnt + jnp.where(s >= thr, 1.0, 0.0)
            total = jnp.sum(cnt, axis=1, keepdims=True)
            return jnp.where(total >= k_f, cand, thr_key)

        thr_key = lax.fori_loop(0, 32, bit_step, jnp.full((tq, LANES), INT32_MIN, jnp.int32),
                                unroll=4)
        thr_ref[...] = key_to_f32(thr_key)

    for n_static in range(1, max_chunks + 1):
        pl.when(n_chunks == n_static)(functools.partial(select, n_static))

    def bias_chunk(c, carry):
        off = pl.multiple_of(c * tk, tk)
        thr = thr_ref[...]
        causal = k_iota + off <= t_pos
        for cb in range(n_cb):
            cols = pl.ds(off + cb * LANES, LANES)
            sel = score_ref[:, cols] >= thr
            bias_ref[:, cols] = jnp.where(
                sel, jnp.where(causal[:, cb * LANES:(cb + 1) * LANES], 0.0, NEG), NEG)
        return carry

    lax.fori_loop(0, n_chunks, bias_chunk, 0)

    def count_rows(pred):
        def body(c, cnt):
            off = pl.multiple_of(c * tk, tk)
            thr = thr_ref[...]
            for cb in range(n_cb):
                s = score_ref[:, pl.ds(off + cb * LANES, LANES)]
                cnt = cnt + jnp.where(pred(s, thr), 1.0, 0.0)
            return cnt
        cnt = lax.fori_loop(0, n_chunks, body, jnp.zeros((tq, LANES), F32))
        return jnp.sum(cnt, axis=1, keepdims=True)

    n_ge = count_rows(lambda s, thr: s >= thr)

    @pl.when(jnp.max(n_ge) > k_f)
    def _():
        n_take = k_f - count_rows(lambda s, thr: s > thr)
        before = jnp.where(
            lax.broadcasted_iota(jnp.int32, (tk, tk), 0) < lax.broadcasted_iota(jnp.int32, (tk, tk), 1),
            1.0, 0.0).astype(BF16)

        def tie_chunk(c, seen):
            off = pl.multiple_of(c * tk, tk)
            sc = score_ref[:, pl.ds(off, tk)]
            thr = jnp.concatenate([thr_ref[...]] * n_cb, axis=1)
            eq = jnp.where(sc == thr, 1.0, 0.0)
            rank = _dot(eq.astype(BF16), before) + seen
            keep = jnp.where(sc > thr, 1.0, jnp.where(rank < n_take, eq, 0.0))
            causal = k_iota + off <= t_pos
            bias_ref[:, pl.ds(off, tk)] = jnp.where(causal, jnp.where(keep > 0.0, 0.0, NEG), NEG)
            return seen + jnp.sum(eq, axis=1, keepdims=True)

        lax.fori_loop(0, n_chunks, tie_chunk, jnp.zeros((tq, 1), F32))

    rows = KV_GROUP * tq
    m_ref[...] = jnp.full(m_ref.shape, NEG, F32)
    acc_ref[...] = jnp.zeros(acc_ref.shape, F32)
    ones = jnp.ones((tk, LANES), BF16)

    def attn_chunk(c, carry):
        off = pl.multiple_of(c * tk, tk)
        b = bias_ref[:, pl.ds(off, tk)]
        for g in range(N_KV_HEADS):
            kc = k_ref[pl.ds(off, tk), g * HEAD_DIM:(g + 1) * HEAD_DIM]
            vc = v_ref[pl.ds(off, tk), g * HEAD_DIM:(g + 1) * HEAD_DIM]
            s = _dot_t(qg_ref[g], kc)
            s = (s.reshape(KV_GROUP, tq, tk) + b[None]).reshape(rows, tk)
            m_old = m_ref[g]
            m_new = jnp.maximum(m_old, jnp.max(s, axis=1, keepdims=True))
            alpha = jnp.exp2(m_old - m_new)
            p = jnp.exp2(s - jnp.concatenate([m_new] * n_cb, axis=1))
            pv = _dot(p.astype(BF16), jnp.concatenate([vc, ones], axis=1))
            acc_ref[g] = jnp.concatenate([alpha, alpha], axis=1) * acc_ref[g] + pv
            m_ref[g] = m_new
        return carry

    lax.fori_loop(0, n_chunks, attn_chunk, 0)

    for g in range(N_KV_HEADS):
        out = acc_ref[g, :, 0:HEAD_DIM] / acc_ref[g, :, HEAD_DIM:2 * HEAD_DIM]
        for r in range(KV_GROUP):
            hd = g * KV_GROUP + r
            o_ref[:, hd * HEAD_DIM:(hd + 1) * HEAD_DIM] = out[r * tq:(r + 1) * tq].astype(BF16)


def _dsa_attn(main, kk, wi, batch, seq, tq, tk, top_k):
    t = main.shape[0]
    nq = seq // tq
    rows = KV_GROUP * tq
    return pl.pallas_call(
        functools.partial(_dsa_attn_kernel, tq, tk, top_k),
        grid=(batch, nq),
        in_specs=[
            pl.BlockSpec((tq, Q_W), lambda b, i: (b * nq + i, 0)),
            pl.BlockSpec((tq, QI_W), lambda b, i: (b * nq + i, (Q_W + 2 * KV_W) // QI_W)),
            pl.BlockSpec((tq, LANES), lambda b, i: (b * nq + i, 0)),
            pl.BlockSpec((seq, KV_W), lambda b, i: (b, Q_W // KV_W)),
            pl.BlockSpec((seq, KV_W), lambda b, i: (b, Q_W // KV_W + 1)),
            pl.BlockSpec((seq, 2 * LANES), lambda b, i: (b, 0)),
        ],
        out_specs=pl.BlockSpec((tq, Q_W), lambda b, i: (b * nq + i, 0)),
        out_shape=jax.ShapeDtypeStruct((t, Q_W), BF16),
        scratch_shapes=[
            pltpu.VMEM((tq, seq), F32),
            pltpu.VMEM((tq, seq), F32),
            pltpu.VMEM((tq, LANES), F32),
            pltpu.VMEM((IDX_HEADS, tq, LANES), F32),
            pltpu.VMEM((N_KV_HEADS, rows, HEAD_DIM), BF16),
            pltpu.VMEM((N_KV_HEADS, rows, LANES), F32),
            pltpu.VMEM((N_KV_HEADS, rows, 2 * HEAD_DIM), F32),
        ],
        compiler_params=_cparams(("arbitrary", "arbitrary")),
        name="dsa_attn",
    )(main, main, wi, main, main, kk)


def _tiles(seq, d_ff):
    tm = min(1024, seq)
    tf = 256 if d_ff % 256 == 0 else d_ff
    tq = min(256, seq)
    tk = min(256, seq)
    return tm, tf, tq, tk


def kernel(x, c, positions, ada_w, ada_b, norm_g, final_g, ffn_wgu, ffn_wd,
           pool_w_in, pool_w_grp, pool_scale, pool_w_out, dsa_w_in, dsa_w_out):
    batch, seq, d = x.shape
    depth = ada_w.shape[0]
    d_ff = ffn_wd.shape[2]
    t = batch * seq
    tm, tf, tq, tk = _tiles(seq, d_ff)
    tn = min(512, d)
    top_k = min(TOPK_MAX, seq // 4)
    assert seq % tm == 0 and seq % tq == 0 and seq % tk == 0 and tk >= top_k
    assert dsa_w_in.shape[2] == DSA_MAIN_W + DSA_SMALL_W and DSA_MAIN_W % tn == 0

    rows = -(-batch // SUBLANES) * SUBLANES
    c8 = jnp.pad(c, ((0, rows - batch), (0, 0)))
    ada_n = ada_w.shape[2]
    ada = _ada(c8, ada_w, ada_b, 1024 if ada_n % 1024 == 0 else d)
    ada4 = ada.reshape(depth, rows, 3 * N_SUBLAYERS, d)

    tabs = _rope_tables(positions, tm) if depth > 1 else None

    h = x.reshape(t, d)
    for layer in range(depth):
        j = layer // 2
        h = _ffn(h, ada4, norm_g, ffn_wgu, ffn_wd, layer, 0, 0, seq, tm, tf)
        g_row = norm_g[layer, 1][None, :]
        if layer % 2 == 0:
            u = _norm_linear(h, ada4, g_row, pool_w_in, j, layer, 1, seq, tm, tn)
            a = _pool_core(u, pool_w_grp, pool_scale, j, batch, seq)
            h = _linear_residual(a, pool_w_out, j, h, ada4, layer, 1, seq, tm, tn)
        else:
            w_ki = dsa_w_in[j, :, DSA_MAIN_W:DSA_MAIN_W + IDX_DIM]
            w_wi = dsa_w_in[j, :, DSA_MAIN_W + IDX_DIM:]
            zeros = jnp.zeros((d, LANES - IDX_DIM), F32)
            w_small = jnp.concatenate(
                [w_ki, zeros, zeros, w_ki, w_wi, jnp.zeros((d, LANES - IDX_HEADS), F32)],
                axis=1).astype(BF16)
            main, kk, wi = _dsa_proj(h, ada4, g_row, dsa_w_in, j, w_small, tabs,
                                     layer, 1, seq, tm, tn)
            o = _dsa_attn(main, kk, wi, batch, seq, tq, tk, top_k)
            h = _linear_residual(o, dsa_w_out, j, h, ada4, layer, 1, seq, tm, tn)
        last = layer == depth - 1
        h = _ffn(h, ada4, norm_g, ffn_wgu, ffn_wd, layer, 1, 2, seq, tm, tf,
                 final_g=final_g if last else None)
    return h.reshape(batch, seq, d)
```

```python
import functools

import jax
import jax.numpy as jnp
from jax import lax
from jax.experimental import pallas as pl
from jax.experimental.pallas import tpu as pltpu

F32 = jnp.float32
BF16 = jnp.bfloat16

N_SUBLAYERS = 3
POOL_WINDOWS = (2, 4, 8, 16)
N_HEADS = 16
N_KV_HEADS = 4
KV_GROUP = N_HEADS // N_KV_HEADS
HEAD_DIM = 128
ROT_DIM = HEAD_DIM // 4
ROPE_THETA = 500000.0
IDX_HEADS = 16
IDX_DIM = 64
IDX_ROT_DIM = IDX_DIM // 4
TOPK_MAX = 256
EPS = 1e-6
NEG = -1e30
Q_W = N_HEADS * HEAD_DIM
KV_W = N_KV_HEADS * HEAD_DIM
QI_W = IDX_HEADS * IDX_DIM
DSA_MAIN_W = Q_W + 2 * KV_W + QI_W
DSA_SMALL_W = IDX_DIM + IDX_HEADS
Q_PRESCALE = HEAD_DIM ** -0.5 * 1.4426950408889634

LANES = 128
SUBLANES = 8
VMEM_LIMIT_BYTES = 56 * 1024 * 1024
INT32_MIN = -(2 ** 31)


def _cparams(semantics):
    return pltpu.CompilerParams(dimension_semantics=semantics,
                                vmem_limit_bytes=VMEM_LIMIT_BYTES)


def _rmsnorm(x, g):
    ms = jnp.mean(x * x, axis=-1, keepdims=True)
    return x * lax.rsqrt(ms + EPS) * g


NORM_SLAB = 16
NORM_UNROLL = 4
FFN_ROW_SLAB = 128


def _for_row_slabs(n_rows, slab, body, unroll=1):
    def step(r, carry):
        body(pl.ds(pl.multiple_of(r * slab, slab), slab))
        return carry
    lax.fori_loop(0, n_rows // slab, step, 0, unroll=unroll)


def _write_norm_mod(x_ref, g_ref, m_ref, sub, y_ref, also=None):
    shift = m_ref[3 * sub + 0:3 * sub + 1, :]
    gs = g_ref[...] * (1.0 + m_ref[3 * sub + 1:3 * sub + 2, :])
    slab = min(NORM_SLAB, x_ref.shape[0])

    def norm_rows(rows):
        x = x_ref[rows, :]
        r = lax.rsqrt(jnp.mean(x * x, axis=-1, keepdims=True) + EPS)
        y_ref[rows, :] = ((x * r) * gs + shift).astype(BF16)
        if also is not None:
            also(rows)

    _for_row_slabs(x_ref.shape[0], slab, norm_rows, unroll=NORM_UNROLL)


def _silu(x):
    return x * jax.nn.sigmoid(x)


def _dot(a, b):
    return jnp.dot(a, b, preferred_element_type=F32)


def _dot_t(a, b):
    return lax.dot_general(a, b, (((1,), (1,)), ((), ())), preferred_element_type=F32)


def _ada_kernel(c_ref, w_ref, b_ref, o_ref):
    sc = _silu(c_ref[...]).astype(BF16)
    o_ref[...] = _dot(sc, w_ref[...].astype(BF16)) + b_ref[...]


def _ada(c8, ada_w, ada_b, tn):
    depth, d, n = ada_w.shape
    rows = c8.shape[0]
    return pl.pallas_call(
        _ada_kernel,
        grid=(depth, n // tn),
        in_specs=[
            pl.BlockSpec((rows, d), lambda l, j: (0, 0)),
            pl.BlockSpec((None, d, tn), lambda l, j: (l, 0, j)),
            pl.BlockSpec((None, 1, tn), lambda l, j: (l, 0, j)),
        ],
        out_specs=pl.BlockSpec((None, rows, tn), lambda l, j: (l, 0, j)),
        out_shape=jax.ShapeDtypeStruct((depth, rows, n), F32),
        compiler_params=_cparams(("arbitrary", "arbitrary")),
        name="ada",
    )(c8, ada_w, ada_b.reshape(depth, 1, n))


def _ffn_kernel(sub, final, rs, *refs):
    if final:
        (x_hbm, g_ref, m_ref, wg_ref, wu_ref, wd_ref, fg_ref, o_hbm,
         acc_ref, y_ref, xbuf, obuf, in_sem, out_sem) = refs
    else:
        (x_hbm, g_ref, m_ref, wg_ref, wu_ref, wd_ref, o_hbm,
         acc_ref, y_ref, xbuf, obuf, in_sem, out_sem) = refs
    k = pl.program_id(1)
    nk = pl.num_programs(1)
    tm, d_out = acc_ref.shape
    n_slabs = tm // rs
    row0 = pl.program_id(0) * tm

    def x_copy(s, slot):
        return pltpu.make_async_copy(x_hbm.at[pl.ds(row0 + s * rs, rs), :], xbuf.at[slot],
                                     in_sem.at[slot])

    def o_copy(s, slot):
        return pltpu.make_async_copy(obuf.at[slot], o_hbm.at[pl.ds(row0 + s * rs, rs), :],
                                     out_sem.at[slot])

    @pl.when(k == 0)
    def _():
        x_copy(0, 0).start()
        for s in range(n_slabs):
            slot = s % 2
            if s + 1 < n_slabs:
                x_copy(s + 1, 1 - slot).start()
            x_copy(s, slot).wait()
            _write_norm_mod(xbuf.at[slot], g_ref, m_ref, sub, y_ref.at[pl.ds(s * rs, rs), :])
        acc_ref[...] = jnp.zeros(acc_ref.shape, F32)

    y = y_ref[...]
    g = _dot(y, wg_ref[...].astype(BF16))
    u = _dot(y, wu_ref[...].astype(BF16))
    a = (_silu(g) * u).astype(BF16)
    nc = min(256, d_out)
    for n in range(d_out // nc):
        cols = slice(n * nc, (n + 1) * nc)
        acc_ref[:, cols] += _dot(a, wd_ref[:, cols].astype(BF16))

    @pl.when(k == nk - 1)
    def _():
        gate = 0.5 * m_ref[3 * sub + 2:3 * sub + 3, :]
        sub_rows = min(NORM_SLAB, rs)
        x_copy(0, 0).start()
        for s in range(n_slabs):
            slot = s % 2
            if s + 1 < n_slabs:
                x_copy(s + 1, 1 - slot).start()
            x_copy(s, slot).wait()
            if s >= 2:
                o_copy(s - 2, slot).wait()

            def finish_rows(r, carry, s=s, slot=slot):
                rows = pl.ds(pl.multiple_of(r * sub_rows, sub_rows), sub_rows)
                acc_rows = pl.ds(pl.multiple_of(s * rs + r * sub_rows, sub_rows), sub_rows)
                res = xbuf[slot, rows, :] + gate * acc_ref[acc_rows, :]
                if final:
                    res = _rmsnorm(res, fg_ref[...])
                obuf[slot, rows, :] = res
                return carry

            lax.fori_loop(0, rs // sub_rows, finish_rows, 0, unroll=NORM_UNROLL)
            o_copy(s, slot).start()
        for s in range(max(n_slabs - 2, 0), n_slabs):
            o_copy(s, s % 2).wait()


def _ffn(h, ada4, norm_g, ffn_wgu, ffn_wd, layer, which, sub, seq, tm, tf, final_g=None):
    t, d = h.shape
    d_ff = ffn_wd.shape[2]
    nk = d_ff // tf
    tpb = seq // tm
    rs = min(FFN_ROW_SLAB, tm)
    final = final_g is not None
    in_specs = [
        pl.BlockSpec(memory_space=pl.ANY),
        pl.BlockSpec((1, d), lambda i, k: (0, 0)),
        pl.BlockSpec((None, None, 3 * N_SUBLAYERS, d), lambda i, k: (layer, i // tpb, 0, 0)),
        pl.BlockSpec((None, None, d, tf), lambda i, k: (layer, which, 0, k)),
        pl.BlockSpec((None, None, d, tf), lambda i, k: (layer, which, 0, nk + k)),
        pl.BlockSpec((None, None, tf, d), lambda i, k: (layer, which, k, 0)),
    ]
    args = [h, norm_g[layer, sub][None, :], ada4, ffn_wgu, ffn_wgu, ffn_wd]
    if final:
        in_specs.append(pl.BlockSpec((1, d), lambda i, k: (0, 0)))
        args.append(final_g[None, :])
    return pl.pallas_call(
        functools.partial(_ffn_kernel, sub, final, rs),
        grid=(t // tm, nk),
        in_specs=in_specs,
        out_specs=pl.BlockSpec(memory_space=pl.ANY),
        out_shape=jax.ShapeDtypeStruct((t, d), F32),
        scratch_shapes=[
            pltpu.VMEM((tm, d), F32),
            pltpu.VMEM((tm, d), BF16),
            pltpu.VMEM((2, rs, d), F32),
            pltpu.VMEM((2, rs, d), F32),
            pltpu.SemaphoreType.DMA((2,)),
            pltpu.SemaphoreType.DMA((2,)),
        ],
        compiler_params=_cparams(("arbitrary", "arbitrary")),
        name=f"ffn_l{layer}_{which}",
    )(*args)


def _keep_bf16_weight(w_ref, wbf_ref):
    j = pl.program_id(1)

    @pl.when(pl.program_id(0) == 0)
    def _():
        wbf_ref[j] = w_ref[...].astype(BF16)

    return wbf_ref[j]


def _resident_weight_spec(widx, kdim, tn, nj):
    return pl.BlockSpec((None, kdim, tn), lambda i, j: (widx, 0, jnp.where(i == 0, j, nj - 1)))


def _norm_linear_kernel(sub, x_ref, g_ref, m_ref, w_ref, o_ref, y_ref, wbf_ref):
    @pl.when(pl.program_id(1) == 0)
    def _():
        _write_norm_mod(x_ref, g_ref, m_ref, sub, y_ref)

    o_ref[...] = _dot(y_ref[...], _keep_bf16_weight(w_ref, wbf_ref))


def _norm_linear(h, ada4, g_row, w, widx, layer, sub, seq, tm, tn):
    t, d = h.shape
    n = w.shape[2]
    tpb = seq // tm
    return pl.pallas_call(
        functools.partial(_norm_linear_kernel, sub),
        grid=(t // tm, n // tn),
        in_specs=[
            pl.BlockSpec((tm, d), lambda i, j: (i, 0)),
            pl.BlockSpec((1, d), lambda i, j: (0, 0)),
            pl.BlockSpec((None, None, 3 * N_SUBLAYERS, d), lambda i, j: (layer, i // tpb, 0, 0)),
            _resident_weight_spec(widx, d, tn, n // tn),
        ],
        out_specs=pl.BlockSpec((tm, tn), lambda i, j: (i, j)),
        out_shape=jax.ShapeDtypeStruct((t, n), F32),
        scratch_shapes=[pltpu.VMEM((tm, d), BF16), pltpu.VMEM((n // tn, d, tn), BF16)],
        compiler_params=_cparams(("arbitrary", "arbitrary")),
        name=f"norm_linear_l{layer}",
    )(h, g_row, ada4, w)


def _pool_kernel(u_ref, w_ref, ls_ref, o_ref):
    g = pl.program_id(1)
    u = u_ref[...]
    t = lax.broadcasted_iota(jnp.int32, u.shape, 0)

    def shifted(v, d):
        return jnp.where(t >= d, pltpu.roll(v, d, 0), 0.0)

    for gi, win in enumerate(POOL_WINDOWS):
        @pl.when(g == gi)
        def _(win=win):
            acc = u
            d = 1
            while d < win:
                acc = acc + shifted(acc, d)
                d *= 2
            cnt = jnp.minimum(t + 1, win).astype(F32)
            p = (acc / cnt - u).astype(BF16)
            v = _dot(p, w_ref[...].astype(BF16)) * ls_ref[...]
            o_ref[...] = v.astype(BF16)


def _pool_core(u, w_grp, ls, widx, batch, seq):
    t, d = u.shape
    _, groups, c, _ = w_grp.shape
    return pl.pallas_call(
        _pool_kernel,
        grid=(batch, groups),
        in_specs=[
            pl.BlockSpec((seq, c), lambda b, g: (b, g)),
            pl.BlockSpec((None, None, c, c), lambda b, g: (widx, g, 0, 0)),
            pl.BlockSpec((None, 1, c), lambda b, g: (widx, 0, g)),
        ],
        out_specs=pl.BlockSpec((seq, c), lambda b, g: (b, g)),
        out_shape=jax.ShapeDtypeStruct((t, d), BF16),
        compiler_params=_cparams(("arbitrary", "arbitrary")),
        name="pool_core",
    )(u, w_grp, ls.reshape(ls.shape[0], 1, d))


def _linear_residual_kernel(sub, a_ref, w_ref, h_ref, m_ref, o_ref, wbf_ref):
    gate = m_ref[3 * sub + 2:3 * sub + 3, :]
    o_ref[...] = h_ref[...] + gate * _dot(a_ref[...], _keep_bf16_weight(w_ref, wbf_ref))


def _linear_residual(a, w, widx, h, ada4, layer, sub, seq, tm, tn):
    t, kdim = a.shape
    n = w.shape[2]
    tpb = seq // tm
    return pl.pallas_call(
        functools.partial(_linear_residual_kernel, sub),
        grid=(t // tm, n // tn),
        in_specs=[
            pl.BlockSpec((tm, kdim), lambda i, j: (i, 0)),
            _resident_weight_spec(widx, kdim, tn, n // tn),
            pl.BlockSpec((tm, tn), lambda i, j: (i, j)),
            pl.BlockSpec((None, None, 3 * N_SUBLAYERS, tn), lambda i, j: (layer, i // tpb, 0, j)),
        ],
        out_specs=pl.BlockSpec((tm, tn), lambda i, j: (i, j)),
        out_shape=jax.ShapeDtypeStruct((t, n), F32),
        scratch_shapes=[pltpu.VMEM((n // tn, kdim, tn), BF16)],
        compiler_params=_cparams(("arbitrary", "arbitrary")),
        name=f"linear_residual_l{layer}",
    )(a, w, h, ada4)


def _rope_table_kernel(pos_ref, invf_ref, o_ref):
    pos = pos_ref[...].astype(F32)
    lane = lax.broadcasted_iota(jnp.int32, (pos.shape[0], LANES), 1)

    def tables(invf, lane_in_head, rot_dim):
        half = rot_dim // 2
        ang = pos * invf
        cos = jnp.cos(ang)
        sin = jnp.sin(ang)
        c = jnp.where(lane_in_head < rot_dim, cos, 1.0)
        sa = jnp.where(lane_in_head < half, -sin, 0.0)
        sb = jnp.where(lane_in_head < rot_dim, jnp.where(lane_in_head >= half, sin, 0.0), 0.0)
        return c, sa, sb

    big = tables(invf_ref[:, 0:LANES], lane, ROT_DIM)
    small = tables(invf_ref[:, LANES:2 * LANES], lane & (IDX_DIM - 1), IDX_ROT_DIM)
    for n, tab in enumerate(big + small):
        o_ref[:, n * LANES:(n + 1) * LANES] = tab


def _rope_tables(positions, tm):
    t = positions.size
    lane = jnp.arange(LANES)

    def inv_freq(rot_dim):
        half = rot_dim // 2
        return ROPE_THETA ** (-jnp.arange(half, dtype=F32) / half)

    invf = jnp.concatenate([inv_freq(ROT_DIM)[lane % (ROT_DIM // 2)],
                            inv_freq(IDX_ROT_DIM)[lane % (IDX_ROT_DIM // 2)]])[None, :]
    return pl.pallas_call(
        _rope_table_kernel,
        grid=(t // tm,),
        in_specs=[
            pl.BlockSpec((tm, 1), lambda i: (i, 0)),
            pl.BlockSpec((1, 2 * LANES), lambda i: (0, 0)),
        ],
        out_specs=pl.BlockSpec((tm, 6 * LANES), lambda i: (i, 0)),
        out_shape=jax.ShapeDtypeStruct((t, 6 * LANES), F32),
        compiler_params=_cparams(("arbitrary",)),
        name="rope_tables",
    )(positions.reshape(t, 1), invf)


def _rope(x, c, sa, sb, half):
    return x * c + pltpu.roll(x, LANES - half, 1) * sa + pltpu.roll(x, half, 1) * sb


def _dsa_proj_kernel(sub, tn, x_ref, g_ref, m_ref, w_ref, ws_ref, tab_ref, o_ref, ok_ref, ow_ref,
                     y_ref, wbf_ref):
    j = pl.program_id(1)
    n_rope_big = (Q_W + KV_W) // tn
    n_plain = KV_W // tn

    def tab(n):
        return tab_ref[:, n * LANES:(n + 1) * LANES]

    @pl.when(j == 0)
    def _():
        _write_norm_mod(x_ref, g_ref, m_ref, sub, y_ref)
        small = _dot(y_ref[...], ws_ref[...])
        for s in range(2):
            ok_ref[:, s * LANES:(s + 1) * LANES] = _rope(
                small[:, s * LANES:(s + 1) * LANES], tab(3), tab(4), tab(5),
                IDX_ROT_DIM // 2).astype(BF16)
        ow_ref[...] = small[:, 2 * LANES:3 * LANES]

    acc = _dot(y_ref[...], _keep_bf16_weight(w_ref, wbf_ref))

    def store_rope(first, half, mult=None):
        for s in range(tn // LANES):
            xs = acc[:, s * LANES:(s + 1) * LANES]
            r = _rope(xs, tab(first), tab(first + 1), tab(first + 2), half)
            if mult is not None:
                r = r * mult
            o_ref[:, s * LANES:(s + 1) * LANES] = r.astype(BF16)

    @pl.when(j < Q_W // tn)
    def _():
        store_rope(0, ROT_DIM // 2, Q_PRESCALE)

    @pl.when(jnp.logical_and(j >= Q_W // tn, j < n_rope_big))
    def _():
        store_rope(0, ROT_DIM // 2)

    @pl.when(jnp.logical_and(j >= n_rope_big, j < n_rope_big + n_plain))
    def _():
        o_ref[...] = acc.astype(BF16)

    @pl.when(j >= n_rope_big + n_plain)
    def _():
        store_rope(3, IDX_ROT_DIM // 2)


def _dsa_proj(h, ada4, g_row, w_in, widx, w_small, tabs, layer, sub, seq, tm, tn):
    t, d = h.shape
    tpb = seq // tm
    return pl.pallas_call(
        functools.partial(_dsa_proj_kernel, sub, tn),
        grid=(t // tm, DSA_MAIN_W // tn),
        in_specs=[
            pl.BlockSpec((tm, d), lambda i, j: (i, 0), pipeline_mode=pl.Buffered(1)),
            pl.BlockSpec((1, d), lambda i, j: (0, 0)),
            pl.BlockSpec((None, None, 3 * N_SUBLAYERS, d), lambda i, j: (layer, i // tpb, 0, 0)),
            _resident_weight_spec(widx, d, tn, DSA_MAIN_W // tn),
            pl.BlockSpec((d, 3 * LANES), lambda i, j: (0, 0)),
            pl.BlockSpec((tm, 6 * LANES), lambda i, j: (i, 0)),
        ],
        out_specs=[
            pl.BlockSpec((tm, tn), lambda i, j: (i, j)),
            pl.BlockSpec((tm, 2 * LANES), lambda i, j: (i, 0)),
            pl.BlockSpec((tm, LANES), lambda i, j: (i, 0)),
        ],
        out_shape=[
            jax.ShapeDtypeStruct((t, DSA_MAIN_W), BF16),
            jax.ShapeDtypeStruct((t, 2 * LANES), BF16),
            jax.ShapeDtypeStruct((t, LANES), F32),
        ],
        scratch_shapes=[pltpu.VMEM((tm, d), BF16),
                        pltpu.VMEM((DSA_MAIN_W // tn, d, tn), BF16)],
        compiler_params=_cparams(("arbitrary", "arbitrary")),
        name=f"dsa_proj_l{layer}",
    )(h, g_row, ada4, w_in, w_small, tabs)


def _dsa_attn_kernel(tq, tk, top_k, q_ref, qi_ref, wi_ref, k_ref, v_ref, kk_ref, o_ref,
                     score_ref, bias_ref, thr_ref, wib_ref, qg_ref, m_ref, acc_ref):
    i = pl.program_id(1)
    n_chunks = ((i + 1) * tq + tk - 1) // tk
    max_chunks = score_ref.shape[1] // tk
    t_pos = i * tq + lax.broadcasted_iota(jnp.int32, (tq, tk), 0)
    k_iota = lax.broadcasted_iota(jnp.int32, (tq, tk), 1)
    n_cb = tk // LANES

    for hh in range(IDX_HEADS):
        wib_ref[hh] = jnp.broadcast_to(wi_ref[:, hh:hh + 1], (tq, LANES))
    for g in range(N_KV_HEADS):
        for r in range(KV_GROUP):
            hd = g * KV_GROUP + r
            qg_ref[g, r * tq:(r + 1) * tq, :] = q_ref[:, hd * HEAD_DIM:(hd + 1) * HEAD_DIM]

    def score_chunk(c, carry):
        off = pl.multiple_of(c * tk, tk)
        k_even = kk_ref[pl.ds(off, tk), 0:LANES]
        k_odd = kk_ref[pl.ds(off, tk), LANES:2 * LANES]
        acc = [jnp.zeros((tq, LANES), F32) for _ in range(n_cb)]
        for pair in range(IDX_HEADS // 2):
            qp = qi_ref[:, pair * LANES:(pair + 1) * LANES]
            for par, kop in enumerate((k_even, k_odd)):
                r = _dot_t(qp, kop)
                w = wib_ref[2 * pair + par]
                for cb in range(n_cb):
                    acc[cb] = acc[cb] + w * jnp.maximum(r[:, cb * LANES:(cb + 1) * LANES], 0.0)
        sc = jnp.concatenate(acc, axis=1)
        score_ref[:, pl.ds(off, tk)] = jnp.where(k_iota + off <= t_pos, sc, NEG)
        return carry

    lax.fori_loop(0, n_chunks, score_chunk, 0)

    def key_to_f32(key):
        return pltpu.bitcast(key ^ ((key >> 31) & 0x7FFFFFFF), F32)

    k_f = float(top_k)

    def select(n_static):
        def bit_step(n, thr_key):
            cand = thr_key + lax.shift_left(jnp.int32(1), 31 - n)
            thr = key_to_f32(cand)
            cnt = jnp.zeros((tq, LANES), F32)
            for cb in range(n_static * n_cb):
                s = score_ref[:, cb * LANES:(cb + 1) * LANES]
                cnt = cnt + jnp.where(s >= thr, 1.0, 0.0)
            total = jnp.sum(cnt, axis=1, keepdims=True)
            return jnp.where(total >= k_f, cand, thr_key)

        thr_key = lax.fori_loop(0, 32, bit_step, jnp.full((tq, LANES), INT32_MIN, jnp.int32),
                                unroll=4)
        thr_ref[...] = key_to_f32(thr_key)

    for n_static in range(1, max_chunks + 1):
        pl.when(n_chunks == n_static)(functools.partial(select, n_static))

    def bias_chunk(c, carry):
        off = pl.multiple_of(c * tk, tk)
        thr = thr_ref[...]
        causal = k_iota + off <= t_pos
        for cb in range(n_cb):
            cols = pl.ds(off + cb * LANES, LANES)
            sel = score_ref[:, cols] >= thr
            bias_ref[:, cols] = jnp.where(
                sel, jnp.where(causal[:, cb * LANES:(cb + 1) * LANES], 0.0, NEG), NEG)
        return carry

    lax.fori_loop(0, n_chunks, bias_chunk, 0)

    def count_rows(pred):
        def body(c, cnt):
            off = pl.multiple_of(c * tk, tk)
            thr = thr_ref[...]
            for cb in range(n_cb):
                s = score_ref[:, pl.ds(off + cb * LANES, LANES)]
                cnt = cnt + jnp.where(pred(s, thr), 1.0, 0.0)
            return cnt
        cnt = lax.fori_loop(0, n_chunks, body, jnp.zeros((tq, LANES), F32))
        return jnp.sum(cnt, axis=1, keepdims=True)

    n_ge = count_rows(lambda s, thr: s >= thr)

    @pl.when(jnp.max(n_ge) > k_f)
    def _():
        n_take = k_f - count_rows(lambda s, thr: s > thr)
        before = jnp.where(
            lax.broadcasted_iota(jnp.int32, (tk, tk), 0) < lax.broadcasted_iota(jnp.int32, (tk, tk), 1),
            1.0, 0.0).astype(BF16)

        def tie_chunk(c, seen):
            off = pl.multiple_of(c * tk, tk)
            sc = score_ref[:, pl.ds(off, tk)]
            thr = jnp.concatenate([thr_ref[...]] * n_cb, axis=1)
            eq = jnp.where(sc == thr, 1.0, 0.0)
            rank = _dot(eq.astype(BF16), before) + seen
            keep = jnp.where(sc > thr, 1.0, jnp.where(rank < n_take, eq, 0.0))
            causal = k_iota + off <= t_pos
            bias_ref[:, pl.ds(off, tk)] = jnp.where(causal, jnp.where(keep > 0.0, 0.0, NEG), NEG)
            return seen + jnp.sum(eq, axis=1, keepdims=True)

        lax.fori_loop(0, n_chunks, tie_chunk, jnp.zeros((tq, 1), F32))

    rows = KV_GROUP * tq
    m_ref[...] = jnp.full(m_ref.shape, NEG, F32)
    acc_ref[...] = jnp.zeros(acc_ref.shape, F32)
    ones = jnp.ones((tk, LANES), BF16)

    def attn_chunk(c, carry):
        off = pl.multiple_of(c * tk, tk)
        b = bias_ref[:, pl.ds(off, tk)]
        for g in range(N_KV_HEADS):
            kc = k_ref[pl.ds(off, tk), g * HEAD_DIM:(g + 1) * HEAD_DIM]
            vc = v_ref[pl.ds(off, tk), g * HEAD_DIM:(g + 1) * HEAD_DIM]
            s = _dot_t(qg_ref[g], kc)
            s = (s.reshape(KV_GROUP, tq, tk) + b[None]).reshape(rows, tk)
            m_old = m_ref[g]
            m_new = jnp.maximum(m_old, jnp.max(s, axis=1, keepdims=True))
            alpha = jnp.exp2(m_old - m_new)
            p = jnp.exp2(s - jnp.concatenate([m_new] * n_cb, axis=1))
            pv = _dot(p.astype(BF16), jnp.concatenate([vc, ones], axis=1))
            acc_ref[g] = jnp.concatenate([alpha, alpha], axis=1) * acc_ref[g] + pv
            m_ref[g] = m_new
        return carry

    lax.fori_loop(0, n_chunks, attn_chunk, 0)

    for g in range(N_KV_HEADS):
        out = acc_ref[g, :, 0:HEAD_DIM] / acc_ref[g, :, HEAD_DIM:2 * HEAD_DIM]
        for r in range(KV_GROUP):
            hd = g * KV_GROUP + r
            o_ref[:, hd * HEAD_DIM:(hd + 1) * HEAD_DIM] = out[r * tq:(r + 1) * tq].astype(BF16)


def _dsa_attn(main, kk, wi, batch, seq, tq, tk, top_k):
    t = main.shape[0]
    nq = seq // tq
    rows = KV_GROUP * tq
    return pl.pallas_call(
        functools.partial(_dsa_attn_kernel, tq, tk, top_k),
        grid=(batch, nq),
        in_specs=[
            pl.BlockSpec((tq, Q_W), lambda b, i: (b * nq + i, 0)),
            pl.BlockSpec((tq, QI_W), lambda b, i: (b * nq + i, (Q_W + 2 * KV_W) // QI_W)),
            pl.BlockSpec((tq, LANES), lambda b, i: (b * nq + i, 0)),
            pl.BlockSpec((seq, KV_W), lambda b, i: (b, Q_W // KV_W)),
            pl.BlockSpec((seq, KV_W), lambda b, i: (b, Q_W // KV_W + 1)),
            pl.BlockSpec((seq, 2 * LANES), lambda b, i: (b, 0)),
        ],
        out_specs=pl.BlockSpec((tq, Q_W), lambda b, i: (b * nq + i, 0)),
        out_shape=jax.ShapeDtypeStruct((t, Q_W), BF16),
        scratch_shapes=[
            pltpu.VMEM((tq, seq), F32),
            pltpu.VMEM((tq, seq), F32),
            pltpu.VMEM((tq, LANES), F32),
            pltpu.VMEM((IDX_HEADS, tq, LANES), F32),
            pltpu.VMEM((N_KV_HEADS, rows, HEAD_DIM), BF16),
            pltpu.VMEM((N_KV_HEADS, rows, LANES), F32),
            pltpu.VMEM((N_KV_HEADS, rows, 2 * HEAD_DIM), F32),
        ],
        compiler_params=_cparams(("arbitrary", "arbitrary")),
        name="dsa_attn",
    )(main, main, wi, main, main, kk)


def _tiles(seq, d_ff):
    tm = min(1024, seq)
    tm_ffn = min(2048, seq)
    tf = 256 if d_ff % 256 == 0 else d_ff
    tq = min(256, seq)
    tk = min(256, seq)
    return tm, tm_ffn, tf, tq, tk


def kernel(x, c, positions, ada_w, ada_b, norm_g, final_g, ffn_wgu, ffn_wd,
           pool_w_in, pool_w_grp, pool_scale, pool_w_out, dsa_w_in, dsa_w_out):
    batch, seq, d = x.shape
    depth = ada_w.shape[0]
    d_ff = ffn_wd.shape[2]
    t = batch * seq
    tm, tm_ffn, tf, tq, tk = _tiles(seq, d_ff)
    tn = min(512, d)
    top_k = min(TOPK_MAX, seq // 4)
    assert seq % tm == 0 and seq % tm_ffn == 0 and seq % tq == 0 and seq % tk == 0 and tk >= top_k
    assert dsa_w_in.shape[2] == DSA_MAIN_W + DSA_SMALL_W and DSA_MAIN_W % tn == 0

    rows = -(-batch // SUBLANES) * SUBLANES
    c8 = jnp.pad(c, ((0, rows - batch), (0, 0)))
    ada_n = ada_w.shape[2]
    ada = _ada(c8, ada_w, ada_b, 1024 if ada_n % 1024 == 0 else d)
    ada4 = ada.reshape(depth, rows, 3 * N_SUBLAYERS, d)

    tabs = _rope_tables(positions, tm) if depth > 1 else None

    h = x.reshape(t, d)
    for layer in range(depth):
        j = layer // 2
        h = _ffn(h, ada4, norm_g, ffn_wgu, ffn_wd, layer, 0, 0, seq, tm_ffn, tf)
        g_row = norm_g[layer, 1][None, :]
        if layer % 2 == 0:
            u = _norm_linear(h, ada4, g_row, pool_w_in, j, layer, 1, seq, tm, tn)
            a = _pool_core(u, pool_w_grp, pool_scale, j, batch, seq)
            h = _linear_residual(a, pool_w_out, j, h, ada4, layer, 1, seq, tm, tn)
        else:
            w_ki = dsa_w_in[j, :, DSA_MAIN_W:DSA_MAIN_W + IDX_DIM]
            w_wi = dsa_w_in[j, :, DSA_MAIN_W + IDX_DIM:]
            zeros = jnp.zeros((d, LANES - IDX_DIM), F32)
            w_small = jnp.concatenate(
                [w_ki, zeros, zeros, w_ki, w_wi, jnp.zeros((d, LANES - IDX_HEADS), F32)],
                axis=1).astype(BF16)
            main, kk, wi = _dsa_proj(h, ada4, g_row, dsa_w_in, j, w_small, tabs,
                                     layer, 1, seq, tm, tn)
            o = _dsa_attn(main, kk, wi, batch, seq, tq, tk, top_k)
            h = _linear_residual(o, dsa_w_out, j, h, ada4, layer, 1, seq, tm, tn)
        last = layer == depth - 1
        h = _ffn(h, ada4, norm_g, ffn_wgu, ffn_wd, layer, 1, 2, seq, tm_ffn, tf,
                 final_g=final_g if last else None)
    return h.reshape(batch, seq, d)
```

```python
import functools

import jax
import jax.numpy as jnp
from jax import lax
from jax.experimental import pallas as pl
from jax.experimental.pallas import tpu as pltpu

F32 = jnp.float32
BF16 = jnp.bfloat16

N_SUBLAYERS = 3
POOL_WINDOWS = (2, 4, 8, 16)
N_HEADS = 16
N_KV_HEADS = 4
KV_GROUP = N_HEADS // N_KV_HEADS
HEAD_DIM = 128
ROT_DIM = HEAD_DIM // 4
ROPE_THETA = 500000.0
IDX_HEADS = 16
IDX_DIM = 64
IDX_ROT_DIM = IDX_DIM // 4
TOPK_MAX = 256
EPS = 1e-6
NEG = -1e30
Q_W = N_HEADS * HEAD_DIM
KV_W = N_KV_HEADS * HEAD_DIM
QI_W = IDX_HEADS * IDX_DIM
DSA_MAIN_W = Q_W + 2 * KV_W + QI_W
DSA_SMALL_W = IDX_DIM + IDX_HEADS
Q_PRESCALE = HEAD_DIM ** -0.5 * 1.4426950408889634

LANES = 128
SUBLANES = 8
VMEM_LIMIT_BYTES = 56 * 1024 * 1024
INT32_MIN = -(2 ** 31)


def _cparams(semantics):
    return pltpu.CompilerParams(dimension_semantics=semantics,
                                vmem_limit_bytes=VMEM_LIMIT_BYTES)


def _rmsnorm(x, g):
    ms = jnp.mean(x * x, axis=-1, keepdims=True)
    return x * lax.rsqrt(ms + EPS) * g


NORM_SLAB = 16
NORM_UNROLL = 4
FFN_ROW_SLAB = 128


def _for_row_slabs(n_rows, slab, body, unroll=1):
    def step(r, carry):
        body(pl.ds(pl.multiple_of(r * slab, slab), slab))
        return carry
    lax.fori_loop(0, n_rows // slab, step, 0, unroll=unroll)


def _write_norm_mod(x_ref, g_ref, m_ref, sub, y_ref, also=None):
    shift = m_ref[3 * sub + 0:3 * sub + 1, :]
    gs = g_ref[...] * (1.0 + m_ref[3 * sub + 1:3 * sub + 2, :])
    slab = min(NORM_SLAB, x_ref.shape[0])

    def norm_rows(rows):
        x = x_ref[rows, :]
        r = lax.rsqrt(jnp.mean(x * x, axis=-1, keepdims=True) + EPS)
        y_ref[rows, :] = ((x * r) * gs + shift).astype(BF16)
        if also is not None:
            also(rows)

    _for_row_slabs(x_ref.shape[0], slab, norm_rows, unroll=NORM_UNROLL)


def _silu(x):
    return x * jax.nn.sigmoid(x)


def _dot(a, b):
    return jnp.dot(a, b, preferred_element_type=F32)


def _dot_t(a, b):
    return lax.dot_general(a, b, (((1,), (1,)), ((), ())), preferred_element_type=F32)


def _ada_kernel(c_ref, w_ref, b_ref, o_ref):
    sc = _silu(c_ref[...]).astype(BF16)
    o_ref[...] = _dot(sc, w_ref[...].astype(BF16)) + b_ref[...]


def _ada(c8, ada_w, ada_b, tn):
    depth, d, n = ada_w.shape
    rows = c8.shape[0]
    return pl.pallas_call(
        _ada_kernel,
        grid=(depth, n // tn),
        in_specs=[
            pl.BlockSpec((rows, d), lambda l, j: (0, 0)),
            pl.BlockSpec((None, d, tn), lambda l, j: (l, 0, j)),
            pl.BlockSpec((None, 1, tn), lambda l, j: (l, 0, j)),
        ],
        out_specs=pl.BlockSpec((None, rows, tn), lambda l, j: (l, 0, j)),
        out_shape=jax.ShapeDtypeStruct((depth, rows, n), F32),
        compiler_params=_cparams(("arbitrary", "arbitrary")),
        name="ada",
    )(c8, ada_w, ada_b.reshape(depth, 1, n))


def _ffn_kernel(sub, final, rs, *refs):
    if final:
        (x_hbm, g_ref, m_ref, wg_ref, wu_ref, wd_ref, fg_ref, o_hbm,
         acc_ref, y_ref, xbuf, obuf, in_sem, out_sem) = refs
    else:
        (x_hbm, g_ref, m_ref, wg_ref, wu_ref, wd_ref, o_hbm,
         acc_ref, y_ref, xbuf, obuf, in_sem, out_sem) = refs
    k = pl.program_id(1)
    nk = pl.num_programs(1)
    tm, d_out = acc_ref.shape
    n_slabs = tm // rs
    row0 = pl.program_id(0) * tm

    def x_copy(s, slot):
        return pltpu.make_async_copy(x_hbm.at[pl.ds(row0 + s * rs, rs), :], xbuf.at[slot],
                                     in_sem.at[slot])

    def o_copy(s, slot):
        return pltpu.make_async_copy(obuf.at[slot], o_hbm.at[pl.ds(row0 + s * rs, rs), :],
                                     out_sem.at[slot])

    @pl.when(k == 0)
    def _():
        x_copy(0, 0).start()
        for s in range(n_slabs):
            slot = s % 2
            if s + 1 < n_slabs:
                x_copy(s + 1, 1 - slot).start()
            x_copy(s, slot).wait()
            _write_norm_mod(xbuf.at[slot], g_ref, m_ref, sub, y_ref.at[pl.ds(s * rs, rs), :])
        acc_ref[...] = jnp.zeros(acc_ref.shape, F32)

    y = y_ref[...]
    g = _dot(y, wg_ref[...].astype(BF16))
    u = _dot(y, wu_ref[...].astype(BF16))
    a = (_silu(g) * u).astype(BF16)
    nc = min(512, d_out)
    for n in range(d_out // nc):
        cols = slice(n * nc, (n + 1) * nc)
        acc_ref[:, cols] += _dot(a, wd_ref[:, cols].astype(BF16))

    @pl.when(k == nk - 1)
    def _():
        gate = 0.5 * m_ref[3 * sub + 2:3 * sub + 3, :]
        sub_rows = min(NORM_SLAB, rs)
        x_copy(0, 0).start()
        for s in range(n_slabs):
            slot = s % 2
            if s + 1 < n_slabs:
                x_copy(s + 1, 1 - slot).start()
            x_copy(s, slot).wait()
            if s >= 2:
                o_copy(s - 2, slot).wait()

            def finish_rows(r, carry, s=s, slot=slot):
                rows = pl.ds(pl.multiple_of(r * sub_rows, sub_rows), sub_rows)
                acc_rows = pl.ds(pl.multiple_of(s * rs + r * sub_rows, sub_rows), sub_rows)
                res = xbuf[slot, rows, :] + gate * acc_ref[acc_rows, :]
                if final:
                    res = _rmsnorm(res, fg_ref[...])
                obuf[slot, rows, :] = res
                return carry

            lax.fori_loop(0, rs // sub_rows, finish_rows, 0, unroll=NORM_UNROLL)
            o_copy(s, slot).start()
        for s in range(max(n_slabs - 2, 0), n_slabs):
            o_copy(s, s % 2).wait()


def _ffn(h, ada4, norm_g, ffn_wgu, ffn_wd, layer, which, sub, seq, tm, tf, final_g=None):
    t, d = h.shape
    d_ff = ffn_wd.shape[2]
    nk = d_ff // tf
    tpb = seq // tm
    rs = min(FFN_ROW_SLAB, tm)
    final = final_g is not None
    in_specs = [
        pl.BlockSpec(memory_space=pl.ANY),
        pl.BlockSpec((1, d), lambda i, k: (0, 0)),
        pl.BlockSpec((None, None, 3 * N_SUBLAYERS, d), lambda i, k: (layer, i // tpb, 0, 0)),
        pl.BlockSpec((None, None, d, tf), lambda i, k: (layer, which, 0, k)),
        pl.BlockSpec((None, None, d, tf), lambda i, k: (layer, which, 0, nk + k)),
        pl.BlockSpec((None, None, tf, d), lambda i, k: (layer, which, k, 0)),
    ]
    args = [h, norm_g[layer, sub][None, :], ada4, ffn_wgu, ffn_wgu, ffn_wd]
    if final:
        in_specs.append(pl.BlockSpec((1, d), lambda i, k: (0, 0)))
        args.append(final_g[None, :])
    return pl.pallas_call(
        functools.partial(_ffn_kernel, sub, final, rs),
        grid=(t // tm, nk),
        in_specs=in_specs,
        out_specs=pl.BlockSpec(memory_space=pl.ANY),
        out_shape=jax.ShapeDtypeStruct((t, d), F32),
        scratch_shapes=[
            pltpu.VMEM((tm, d), F32),
            pltpu.VMEM((tm, d), BF16),
            pltpu.VMEM((2, rs, d), F32),
            pltpu.VMEM((2, rs, d), F32),
            pltpu.SemaphoreType.DMA((2,)),
            pltpu.SemaphoreType.DMA((2,)),
        ],
        compiler_params=_cparams(("arbitrary", "arbitrary")),
        name=f"ffn_l{layer}_{which}",
    )(*args)


def _keep_bf16_weight(w_ref, wbf_ref):
    j = pl.program_id(1)

    @pl.when(pl.program_id(0) == 0)
    def _():
        wbf_ref[j] = w_ref[...].astype(BF16)

    return wbf_ref[j]


def _resident_weight_spec(widx, kdim, tn, nj):
    return pl.BlockSpec((None, kdim, tn), lambda i, j: (widx, 0, jnp.where(i == 0, j, nj - 1)))


def _norm_linear_kernel(sub, x_ref, g_ref, m_ref, w_ref, o_ref, y_ref, wbf_ref):
    @pl.when(pl.program_id(1) == 0)
    def _():
        _write_norm_mod(x_ref, g_ref, m_ref, sub, y_ref)

    o_ref[...] = _dot(y_ref[...], _keep_bf16_weight(w_ref, wbf_ref))


def _norm_linear(h, ada4, g_row, w, widx, layer, sub, seq, tm, tn):
    t, d = h.shape
    n = w.shape[2]
    tpb = seq // tm
    return pl.pallas_call(
        functools.partial(_norm_linear_kernel, sub),
        grid=(t // tm, n // tn),
        in_specs=[
            pl.BlockSpec((tm, d), lambda i, j: (i, 0)),
            pl.BlockSpec((1, d), lambda i, j: (0, 0)),
            pl.BlockSpec((None, None, 3 * N_SUBLAYERS, d), lambda i, j: (layer, i // tpb, 0, 0)),
            _resident_weight_spec(widx, d, tn, n // tn),
        ],
        out_specs=pl.BlockSpec((tm, tn), lambda i, j: (i, j)),
        out_shape=jax.ShapeDtypeStruct((t, n), F32),
        scratch_shapes=[pltpu.VMEM((tm, d), BF16), pltpu.VMEM((n // tn, d, tn), BF16)],
        compiler_params=_cparams(("arbitrary", "arbitrary")),
        name=f"norm_linear_l{layer}",
    )(h, g_row, ada4, w)


def _pool_kernel(u_ref, w_ref, ls_ref, o_ref):
    g = pl.program_id(1)
    u = u_ref[...]
    t = lax.broadcasted_iota(jnp.int32, u.shape, 0)

    def shifted(v, d):
        return jnp.where(t >= d, pltpu.roll(v, d, 0), 0.0)

    for gi, win in enumerate(POOL_WINDOWS):
        @pl.when(g == gi)
        def _(win=win):
            acc = u
            d = 1
            while d < win:
                acc = acc + shifted(acc, d)
                d *= 2
            cnt = jnp.minimum(t + 1, win).astype(F32)
            p = (acc / cnt - u).astype(BF16)
            v = _dot(p, w_ref[...].astype(BF16)) * ls_ref[...]
            o_ref[...] = v.astype(BF16)


def _pool_core(u, w_grp, ls, widx, batch, seq):
    t, d = u.shape
    _, groups, c, _ = w_grp.shape
    return pl.pallas_call(
        _pool_kernel,
        grid=(batch, groups),
        in_specs=[
            pl.BlockSpec((seq, c), lambda b, g: (b, g)),
            pl.BlockSpec((None, None, c, c), lambda b, g: (widx, g, 0, 0)),
            pl.BlockSpec((None, 1, c), lambda b, g: (widx, 0, g)),
        ],
        out_specs=pl.BlockSpec((seq, c), lambda b, g: (b, g)),
        out_shape=jax.ShapeDtypeStruct((t, d), BF16),
        compiler_params=_cparams(("arbitrary", "arbitrary")),
        name="pool_core",
    )(u, w_grp, ls.reshape(ls.shape[0], 1, d))


def _linear_residual_kernel(sub, a_ref, w_ref, h_ref, m_ref, o_ref, wbf_ref):
    gate = m_ref[3 * sub + 2:3 * sub + 3, :]
    o_ref[...] = h_ref[...] + gate * _dot(a_ref[...], _keep_bf16_weight(w_ref, wbf_ref))


def _linear_residual(a, w, widx, h, ada4, layer, sub, seq, tm, tn):
    t, kdim = a.shape
    n = w.shape[2]
    tpb = seq // tm
    return pl.pallas_call(
        functools.partial(_linear_residual_kernel, sub),
        grid=(t // tm, n // tn),
        in_specs=[
            pl.BlockSpec((tm, kdim), lambda i, j: (i, 0)),
            _resident_weight_spec(widx, kdim, tn, n // tn),
            pl.BlockSpec((tm, tn), lambda i, j: (i, j)),
            pl.BlockSpec((None, None, 3 * N_SUBLAYERS, tn), lambda i, j: (layer, i // tpb, 0, j)),
        ],
        out_specs=pl.BlockSpec((tm, tn), lambda i, j: (i, j)),
        out_shape=jax.ShapeDtypeStruct((t, n), F32),
        scratch_shapes=[pltpu.VMEM((n // tn, kdim, tn), BF16)],
        compiler_params=_cparams(("arbitrary", "arbitrary")),
        name=f"linear_residual_l{layer}",
    )(a, w, h, ada4)


ROPE_SET_Q, ROPE_SET_K, ROPE_SET_NONE, ROPE_SET_IDX = 0, 1, 2, 3
HALF_LANES = LANES // 2


def _rope_table_kernel(pos_ref, invf_ref, o_ref):
    pos = pos_ref[...].astype(F32)
    lane = lax.broadcasted_iota(jnp.int32, (pos.shape[0], LANES), 1)
    first = lane < HALF_LANES

    def tables(invf, is_rot):
        ang = pos * invf
        c = jnp.where(is_rot, jnp.cos(ang), 1.0)
        s = jnp.where(is_rot, jnp.where(first, -jnp.sin(ang), jnp.sin(ang)), 0.0)
        return c, s

    cb, sb = tables(invf_ref[:, 0:LANES], (lane & (HALF_LANES - 1)) < ROT_DIM // 2)
    ci, si = tables(invf_ref[:, LANES:2 * LANES], (lane & (HALF_LANES // 2 - 1)) < IDX_ROT_DIM // 2)
    o_ref[2 * ROPE_SET_Q] = cb * Q_PRESCALE
    o_ref[2 * ROPE_SET_Q + 1] = sb * Q_PRESCALE
    o_ref[2 * ROPE_SET_K] = cb
    o_ref[2 * ROPE_SET_K + 1] = sb
    o_ref[2 * ROPE_SET_NONE] = jnp.ones_like(cb)
    o_ref[2 * ROPE_SET_NONE + 1] = jnp.zeros_like(cb)
    o_ref[2 * ROPE_SET_IDX] = ci
    o_ref[2 * ROPE_SET_IDX + 1] = si


def _rope_tables(positions, tm):
    t = positions.size
    lane = jnp.arange(LANES)

    def inv_freq(rot_dim):
        half = rot_dim // 2
        return ROPE_THETA ** (-jnp.arange(half, dtype=F32) / half)

    invf = jnp.concatenate([inv_freq(ROT_DIM)[lane % (ROT_DIM // 2)],
                            inv_freq(IDX_ROT_DIM)[lane % (IDX_ROT_DIM // 2)]])[None, :]
    return pl.pallas_call(
        _rope_table_kernel,
        grid=(t // tm,),
        in_specs=[
            pl.BlockSpec((tm, 1), lambda i: (i, 0)),
            pl.BlockSpec((1, 2 * LANES), lambda i: (0, 0)),
        ],
        out_specs=pl.BlockSpec((8, tm, LANES), lambda i: (0, i, 0)),
        out_shape=jax.ShapeDtypeStruct((8, t, LANES), F32),
        compiler_params=_cparams(("arbitrary",)),
        name="rope_tables",
    )(positions.reshape(t, 1), invf)


def _rope(x, c, s):
    return x * c + pltpu.roll(x, HALF_LANES, 1) * s


def _permute_dsa_weights(w):
    d = w.shape[0]

    def big_heads(cols, n_heads):
        x = cols.reshape(d, n_heads, HEAD_DIM)
        h, p = ROT_DIM // 2, (HEAD_DIM - ROT_DIM) // 2
        return jnp.concatenate([x[..., 0:h], x[..., ROT_DIM:ROT_DIM + p], x[..., h:ROT_DIM],
                                x[..., ROT_DIM + p:]], axis=-1).reshape(d, n_heads * HEAD_DIM)

    def lo_hi(x):
        h, p = IDX_ROT_DIM // 2, (IDX_DIM - IDX_ROT_DIM) // 2
        lo = jnp.concatenate([x[..., 0:h], x[..., IDX_ROT_DIM:IDX_ROT_DIM + p]], axis=-1)
        hi = jnp.concatenate([x[..., h:IDX_ROT_DIM], x[..., IDX_ROT_DIM + p:]], axis=-1)
        return lo, hi

    o_k, o_v, o_qi = Q_W, Q_W + KV_W, Q_W + 2 * KV_W
    lo, hi = lo_hi(w[:, o_qi:DSA_MAIN_W].reshape(d, IDX_HEADS // 2, 2, IDX_DIM))
    qi = jnp.concatenate([lo[:, :, 0], lo[:, :, 1], hi[:, :, 0], hi[:, :, 1]],
                         axis=-1).reshape(d, QI_W)
    main = jnp.concatenate([big_heads(w[:, 0:o_k], N_HEADS), big_heads(w[:, o_k:o_v], N_KV_HEADS),
                            w[:, o_v:o_qi], qi], axis=1)
    klo, khi = lo_hi(w[:, DSA_MAIN_W:DSA_MAIN_W + IDX_DIM])
    z = jnp.zeros_like(klo)
    wi = jnp.pad(w[:, DSA_MAIN_W + IDX_DIM:], ((0, 0), (0, LANES - IDX_HEADS)))
    small = jnp.concatenate([klo, z, khi, z, z, klo, z, khi, wi], axis=1)
    return main.astype(BF16), small.astype(BF16)


def _dsa_proj_kernel(sub, tn, x_ref, g_ref, m_ref, w_ref, ws_ref, tab_ref, o_ref, ok_ref, ow_ref,
                     y_ref, wbf_ref):
    j = pl.program_id(1)
    first_k, first_v, first_qi = Q_W // tn, (Q_W + KV_W) // tn, (Q_W + 2 * KV_W) // tn

    @pl.when(j == 0)
    def _():
        _write_norm_mod(x_ref, g_ref, m_ref, sub, y_ref)
        small = _dot(y_ref[...], ws_ref[...])
        for s in range(2):
            ok_ref[:, s * LANES:(s + 1) * LANES] = _rope(
                small[:, s * LANES:(s + 1) * LANES], tab_ref[2 * ROPE_SET_IDX],
                tab_ref[2 * ROPE_SET_IDX + 1]).astype(BF16)
        ow_ref[...] = small[:, 2 * LANES:3 * LANES]

    @pl.when(pl.program_id(0) == 0)
    def _():
        wbf_ref[j] = w_ref[...]

    acc = _dot(y_ref[...], wbf_ref[j])
    rope_set = jnp.where(j < first_k, ROPE_SET_Q,
                         jnp.where(j < first_v, ROPE_SET_K,
                                   jnp.where(j < first_qi, ROPE_SET_NONE, ROPE_SET_IDX)))
    c = tab_ref[2 * rope_set]
    s = tab_ref[2 * rope_set + 1]
    for sl in range(tn // LANES):
        cols = slice(sl * LANES, (sl + 1) * LANES)
        o_ref[:, cols] = _rope(acc[:, cols], c, s).astype(BF16)


def _dsa_proj(h, ada4, g_row, w_main, w_small, tabs, layer, sub, seq, tm, tn):
    t, d = h.shape
    tpb = seq // tm
    return pl.pallas_call(
        functools.partial(_dsa_proj_kernel, sub, tn),
        grid=(t // tm, DSA_MAIN_W // tn),
        in_specs=[
            pl.BlockSpec((tm, d), lambda i, j: (i, 0), pipeline_mode=pl.Buffered(1)),
            pl.BlockSpec((1, d), lambda i, j: (0, 0)),
            pl.BlockSpec((None, None, 3 * N_SUBLAYERS, d), lambda i, j: (layer, i // tpb, 0, 0)),
            _resident_weight_spec(0, d, tn, DSA_MAIN_W // tn),
            pl.BlockSpec((d, 3 * LANES), lambda i, j: (0, 0)),
            pl.BlockSpec((8, tm, LANES), lambda i, j: (0, i, 0)),
        ],
        out_specs=[
            pl.BlockSpec((tm, tn), lambda i, j: (i, j)),
            pl.BlockSpec((tm, 2 * LANES), lambda i, j: (i, 0)),
            pl.BlockSpec((tm, LANES), lambda i, j: (i, 0)),
        ],
        out_shape=[
            jax.ShapeDtypeStruct((t, DSA_MAIN_W), BF16),
            jax.ShapeDtypeStruct((t, 2 * LANES), BF16),
            jax.ShapeDtypeStruct((t, LANES), F32),
        ],
        scratch_shapes=[pltpu.VMEM((tm, d), BF16),
                        pltpu.VMEM((DSA_MAIN_W // tn, d, tn), BF16)],
        compiler_params=_cparams(("arbitrary", "arbitrary")),
        name=f"dsa_proj_l{layer}",
    )(h, g_row, ada4, w_main[None], w_small, tabs)


def _dsa_attn_kernel(tq, tk, top_k, q_ref, qi_ref, wi_ref, k_ref, v_ref, kk_ref, o_ref,
                     score_ref, bias_ref, thr_ref, wib_ref, qg_ref, m_ref, acc_ref):
    i = pl.program_id(1)
    n_chunks = ((i + 1) * tq + tk - 1) // tk
    max_chunks = score_ref.shape[1] // tk
    t_pos = i * tq + lax.broadcasted_iota(jnp.int32, (tq, tk), 0)
    k_iota = lax.broadcasted_iota(jnp.int32, (tq, tk), 1)
    n_cb = tk // LANES

    for hh in range(IDX_HEADS):
        wib_ref[hh] = jnp.broadcast_to(wi_ref[:, hh:hh + 1], (tq, LANES))
    for g in range(N_KV_HEADS):
        for r in range(KV_GROUP):
            hd = g * KV_GROUP + r
            qg_ref[g, r * tq:(r + 1) * tq, :] = q_ref[:, hd * HEAD_DIM:(hd + 1) * HEAD_DIM]

    def score_chunk(c, carry):
        off = pl.multiple_of(c * tk, tk)
        k_even = kk_ref[pl.ds(off, tk), 0:LANES]
        k_odd = kk_ref[pl.ds(off, tk), LANES:2 * LANES]
        acc = [jnp.zeros((tq, LANES), F32) for _ in range(n_cb)]
        for pair in range(IDX_HEADS // 2):
            qp = qi_ref[:, pair * LANES:(pair + 1) * LANES]
            for par, kop in enumerate((k_even, k_odd)):
                r = _dot_t(qp, kop)
                w = wib_ref[2 * pair + par]
                for cb in range(n_cb):
                    acc[cb] = acc[cb] + w * jnp.maximum(r[:, cb * LANES:(cb + 1) * LANES], 0.0)
        sc = jnp.concatenate(acc, axis=1)
        score_ref[:, pl.ds(off, tk)] = jnp.where(k_iota + off <= t_pos, sc, NEG)
        return carry

    lax.fori_loop(0, n_chunks, score_chunk, 0)

    def key_to_f32(key):
        return pltpu.bitcast(key ^ ((key >> 31) & 0x7FFFFFFF), F32)

    k_f = float(top_k)

    def select(n_static):
        def bit_step(n, thr_key):
            cand = thr_key + lax.shift_left(jnp.int32(1), 31 - n)
            thr = key_to_f32(cand)
            cnt = jnp.zeros((tq, LANES), F32)
            for cb in range(n_static * n_cb):
                s = score_ref[:, cb * LANES:(cb + 1) * LANES]
                cnt = cnt + jnp.where(s >= thr, 1.0, 0.0)
            total = jnp.sum(cnt, axis=1, keepdims=True)
            return jnp.where(total >= k_f, cand, thr_key)

        thr_key = lax.fori_loop(0, 32, bit_step, jnp.full((tq, LANES), INT32_MIN, jnp.int32),
                                unroll=4)
        thr_ref[...] = key_to_f32(thr_key)

    for n_static in range(1, max_chunks + 1):
        pl.when(n_chunks == n_static)(functools.partial(select, n_static))

    def bias_chunk(c, carry):
        off = pl.multiple_of(c * tk, tk)
        thr = thr_ref[...]
        causal = k_iota + off <= t_pos
        for cb in range(n_cb):
            cols = pl.ds(off + cb * LANES, LANES)
            sel = score_ref[:, cols] >= thr
            bias_ref[:, cols] = jnp.where(
                sel, jnp.where(causal[:, cb * LANES:(cb + 1) * LANES], 0.0, NEG), NEG)
        return carry

    lax.fori_loop(0, n_chunks, bias_chunk, 0)

    def count_rows(pred):
        def body(c, cnt):
            off = pl.multiple_of(c * tk, tk)
            thr = thr_ref[...]
            for cb in range(n_cb):
                s = score_ref[:, pl.ds(off + cb * LANES, LANES)]
                cnt = cnt + jnp.where(pred(s, thr), 1.0, 0.0)
            return cnt
        cnt = lax.fori_loop(0, n_chunks, body, jnp.zeros((tq, LANES), F32))
        return jnp.sum(cnt, axis=1, keepdims=True)

    n_ge = count_rows(lambda s, thr: s >= thr)

    @pl.when(jnp.max(n_ge) > k_f)
    def _():
        n_take = k_f - count_rows(lambda s, thr: s > thr)
        before = jnp.where(
            lax.broadcasted_iota(jnp.int32, (tk, tk), 0) < lax.broadcasted_iota(jnp.int32, (tk, tk), 1),
            1.0, 0.0).astype(BF16)

        def tie_chunk(c, seen):
            off = pl.multiple_of(c * tk, tk)
            sc = score_ref[:, pl.ds(off, tk)]
            thr = jnp.concatenate([thr_ref[...]] * n_cb, axis=1)
            eq = jnp.where(sc == thr, 1.0, 0.0)
            rank = _dot(eq.astype(BF16), before) + seen
            keep = jnp.where(sc > thr, 1.0, jnp.where(rank < n_take, eq, 0.0))
            causal = k_iota + off <= t_pos
            bias_ref[:, pl.ds(off, tk)] = jnp.where(causal, jnp.where(keep > 0.0, 0.0, NEG), NEG)
            return seen + jnp.sum(eq, axis=1, keepdims=True)

        lax.fori_loop(0, n_chunks, tie_chunk, jnp.zeros((tq, 1), F32))

    rows = KV_GROUP * tq
    m_ref[...] = jnp.full(m_ref.shape, NEG, F32)
    acc_ref[...] = jnp.zeros(acc_ref.shape, F32)
    ones = jnp.ones((tk, LANES), BF16)

    def attn_chunk(c, carry):
        off = pl.multiple_of(c * tk, tk)
        b = bias_ref[:, pl.ds(off, tk)]
        for g in range(N_KV_HEADS):
            kc = k_ref[pl.ds(off, tk), g * HEAD_DIM:(g + 1) * HEAD_DIM]
            vc = v_ref[pl.ds(off, tk), g * HEAD_DIM:(g + 1) * HEAD_DIM]
            s = _dot_t(qg_ref[g], kc)
            s = (s.reshape(KV_GROUP, tq, tk) + b[None]).reshape(rows, tk)
            m_old = m_ref[g]
            m_new = jnp.maximum(m_old, jnp.max(s, axis=1, keepdims=True))
            alpha = jnp.exp2(m_old - m_new)
            p = jnp.exp2(s - jnp.concatenate([m_new] * n_cb, axis=1))
            pv = _dot(p.astype(BF16), jnp.concatenate([vc, ones], axis=1))
            acc_ref[g] = jnp.concatenate([alpha, alpha], axis=1) * acc_ref[g] + pv
            m_ref[g] = m_new
        return carry

    lax.fori_loop(0, n_chunks, attn_chunk, 0)

    for g in range(N_KV_HEADS):
        out = acc_ref[g, :, 0:HEAD_DIM] / acc_ref[g, :, HEAD_DIM:2 * HEAD_DIM]
        for r in range(KV_GROUP):
            hd = g * KV_GROUP + r
            o_ref[:, hd * HEAD_DIM:(hd + 1) * HEAD_DIM] = out[r * tq:(r + 1) * tq].astype(BF16)


def _dsa_attn(main, kk, wi, batch, seq, tq, tk, top_k):
    t = main.shape[0]
    nq = seq // tq
    rows = KV_GROUP * tq
    return pl.pallas_call(
        functools.partial(_dsa_attn_kernel, tq, tk, top_k),
        grid=(batch, nq),
        in_specs=[
            pl.BlockSpec((tq, Q_W), lambda b, i: (b * nq + i, 0)),
            pl.BlockSpec((tq, QI_W), lambda b, i: (b * nq + i, (Q_W + 2 * KV_W) // QI_W)),
            pl.BlockSpec((tq, LANES), lambda b, i: (b * nq + i, 0)),
            pl.BlockSpec((seq, KV_W), lambda b, i: (b, Q_W // KV_W)),
            pl.BlockSpec((seq, KV_W), lambda b, i: (b, Q_W // KV_W + 1)),
            pl.BlockSpec((seq, 2 * LANES), lambda b, i: (b, 0)),
        ],
        out_specs=pl.BlockSpec((tq, Q_W), lambda b, i: (b * nq + i, 0)),
        out_shape=jax.ShapeDtypeStruct((t, Q_W), BF16),
        scratch_shapes=[
            pltpu.VMEM((tq, seq), F32),
            pltpu.VMEM((tq, seq), F32),
            pltpu.VMEM((tq, LANES), F32),
            pltpu.VMEM((IDX_HEADS, tq, LANES), F32),
            pltpu.VMEM((N_KV_HEADS, rows, HEAD_DIM), BF16),
            pltpu.VMEM((N_KV_HEADS, rows, LANES), F32),
            pltpu.VMEM((N_KV_HEADS, rows, 2 * HEAD_DIM), F32),
        ],
        compiler_params=_cparams(("arbitrary", "arbitrary")),
        name="dsa_attn",
    )(main, main, wi, main, main, kk)


def _tiles(seq, d_ff):
    tm = min(1024, seq)
    tm_ffn = min(1024, seq)
    tf = 256 if d_ff % 256 == 0 else d_ff
    tq = min(256, seq)
    tk = min(256, seq)
    return tm, tm_ffn, tf, tq, tk


def kernel(x, c, positions, ada_w, ada_b, norm_g, final_g, ffn_wgu, ffn_wd,
           pool_w_in, pool_w_grp, pool_scale, pool_w_out, dsa_w_in, dsa_w_out):
    batch, seq, d = x.shape
    depth = ada_w.shape[0]
    d_ff = ffn_wd.shape[2]
    t = batch * seq
    tm, tm_ffn, tf, tq, tk = _tiles(seq, d_ff)
    tn = min(512, d)
    top_k = min(TOPK_MAX, seq // 4)
    assert seq % tm == 0 and seq % tm_ffn == 0 and seq % tq == 0 and seq % tk == 0 and tk >= top_k
    assert dsa_w_in.shape[2] == DSA_MAIN_W + DSA_SMALL_W and DSA_MAIN_W % tn == 0

    rows = -(-batch // SUBLANES) * SUBLANES
    c8 = jnp.pad(c, ((0, rows - batch), (0, 0)))
    ada_n = ada_w.shape[2]
    ada = _ada(c8, ada_w, ada_b, 1024 if ada_n % 1024 == 0 else d)
    ada4 = ada.reshape(depth, rows, 3 * N_SUBLAYERS, d)

    tabs = _rope_tables(positions, tm) if depth > 1 else None

    h = x.reshape(t, d)
    for layer in range(depth):
        j = layer // 2
        h = _ffn(h, ada4, norm_g, ffn_wgu, ffn_wd, layer, 0, 0, seq, tm_ffn, tf)
        g_row = norm_g[layer, 1][None, :]
        if layer % 2 == 0:
            u = _norm_linear(h, ada4, g_row, pool_w_in, j, layer, 1, seq, tm, tn)
            a = _pool_core(u, pool_w_grp, pool_scale, j, batch, seq)
            h = _linear_residual(a, pool_w_out, j, h, ada4, layer, 1, seq, tm, tn)
        else:
            w_main, w_small = _permute_dsa_weights(dsa_w_in[j])
            main, kk, wi = _dsa_proj(h, ada4, g_row, w_main, w_small, tabs,
                                     layer, 1, seq, tm, tn)
            o = _dsa_attn(main, kk, wi, batch, seq, tq, tk, top_k)
            h = _linear_residual(o, dsa_w_out, j, h, ada4, layer, 1, seq, tm, tn)
        last = layer == depth - 1
        h = _ffn(h, ada4, norm_g, ffn_wgu, ffn_wd, layer, 1, 2, seq, tm_ffn, tf,
                 final_g=final_g if last else None)
    return h.reshape(batch, seq, d)
```

```python
import functools

import jax
import jax.numpy as jnp
from jax import lax
from jax.experimental import pallas as pl
from jax.experimental.pallas import tpu as pltpu

F32 = jnp.float32
BF16 = jnp.bfloat16

N_SUBLAYERS = 3
POOL_WINDOWS = (2, 4, 8, 16)
N_HEADS = 16
N_KV_HEADS = 4
KV_GROUP = N_HEADS // N_KV_HEADS
HEAD_DIM = 128
ROT_DIM = HEAD_DIM // 4
ROPE_THETA = 500000.0
IDX_HEADS = 16
IDX_DIM = 64
IDX_ROT_DIM = IDX_DIM // 4
TOPK_MAX = 256
EPS = 1e-6
NEG = -1e30
Q_W = N_HEADS * HEAD_DIM
KV_W = N_KV_HEADS * HEAD_DIM
QI_W = IDX_HEADS * IDX_DIM
DSA_MAIN_W = Q_W + 2 * KV_W + QI_W
DSA_SMALL_W = IDX_DIM + IDX_HEADS
Q_PRESCALE = HEAD_DIM ** -0.5 * 1.4426950408889634

LANES = 128
SUBLANES = 8
VMEM_LIMIT_BYTES = 56 * 1024 * 1024
INT32_MIN = -(2 ** 31)


def _cparams(semantics):
    return pltpu.CompilerParams(dimension_semantics=semantics,
                                vmem_limit_bytes=VMEM_LIMIT_BYTES)


def _rmsnorm(x, g):
    ms = jnp.mean(x * x, axis=-1, keepdims=True)
    return x * lax.rsqrt(ms + EPS) * g


NORM_SLAB = 16
NORM_UNROLL = 4
FFN_ROW_SLAB = 128
FFN_SLAB_SLOTS = 4
FFN_DOWN_COLS = 256


def _for_row_slabs(n_rows, slab, body, unroll=1):
    def step(r, carry):
        body(pl.ds(pl.multiple_of(r * slab, slab), slab))
        return carry
    lax.fori_loop(0, n_rows // slab, step, 0, unroll=unroll)


def _write_norm_mod(x_ref, g_ref, m_ref, sub, y_ref, also=None):
    shift = m_ref[3 * sub + 0:3 * sub + 1, :]
    gs = g_ref[...] * (1.0 + m_ref[3 * sub + 1:3 * sub + 2, :])
    slab = min(NORM_SLAB, x_ref.shape[0])

    def norm_rows(rows):
        x = x_ref[rows, :]
        r = lax.rsqrt(jnp.mean(x * x, axis=-1, keepdims=True) + EPS)
        y_ref[rows, :] = ((x * r) * gs + shift).astype(BF16)
        if also is not None:
            also(rows)

    _for_row_slabs(x_ref.shape[0], slab, norm_rows, unroll=NORM_UNROLL)


def _silu(x):
    return x * jax.nn.sigmoid(x)


def _dot(a, b):
    return jnp.dot(a, b, preferred_element_type=F32)


def _dot_t(a, b):
    return lax.dot_general(a, b, (((1,), (1,)), ((), ())), preferred_element_type=F32)


def _ada_kernel(c_ref, w_ref, b_ref, o_ref):
    sc = _silu(c_ref[...]).astype(BF16)
    o_ref[...] = _dot(sc, w_ref[...].astype(BF16)) + b_ref[...]


def _ada(c8, ada_w, ada_b, tn):
    depth, d, n = ada_w.shape
    rows = c8.shape[0]
    return pl.pallas_call(
        _ada_kernel,
        grid=(depth, n // tn),
        in_specs=[
            pl.BlockSpec((rows, d), lambda l, j: (0, 0)),
            pl.BlockSpec((None, d, tn), lambda l, j: (l, 0, j)),
            pl.BlockSpec((None, 1, tn), lambda l, j: (l, 0, j)),
        ],
        out_specs=pl.BlockSpec((None, rows, tn), lambda l, j: (l, 0, j)),
        out_shape=jax.ShapeDtypeStruct((depth, rows, n), F32),
        compiler_params=_cparams(("arbitrary", "arbitrary")),
        name="ada",
    )(c8, ada_w, ada_b.reshape(depth, 1, n))


def _ffn_kernel(sub, final, rs, *refs):
    if final:
        (x_hbm, g_ref, m_ref, wg_ref, wu_ref, wd_ref, fg_ref, o_hbm,
         acc_ref, y_ref, xbuf, in_sem, out_sem) = refs
    else:
        (x_hbm, g_ref, m_ref, wg_ref, wu_ref, wd_ref, o_hbm,
         acc_ref, y_ref, xbuf, in_sem, out_sem) = refs
    k = pl.program_id(1)
    nk = pl.num_programs(1)
    tm, d_out = acc_ref.shape
    n_slabs = tm // rs
    n_slots = xbuf.shape[0]
    row0 = pl.program_id(0) * tm

    def x_copy(s):
        slot = s % n_slots
        return pltpu.make_async_copy(x_hbm.at[pl.ds(row0 + s * rs, rs), :], xbuf.at[slot],
                                     in_sem.at[slot])

    def o_copy(s):
        return pltpu.make_async_copy(acc_ref.at[pl.ds(s * rs, rs), :],
                                     o_hbm.at[pl.ds(row0 + s * rs, rs), :], out_sem.at[s])

    def for_x_slabs(body):
        for s in range(min(n_slots, n_slabs)):
            x_copy(s).start()
        for s in range(n_slabs):
            x_copy(s).wait()
            body(s, s % n_slots)
            if s + n_slots < n_slabs:
                x_copy(s + n_slots).start()

    @pl.when(k == 0)
    def _():
        def norm_slab(s, slot):
            _write_norm_mod(xbuf.at[slot], g_ref, m_ref, sub, y_ref.at[pl.ds(s * rs, rs), :])

        for_x_slabs(norm_slab)
        acc_ref[...] = jnp.zeros(acc_ref.shape, F32)

    y = y_ref[...]
    g = _dot(y, wg_ref[...].astype(BF16))
    u = _dot(y, wu_ref[...].astype(BF16))
    a = (_silu(g) * u).astype(BF16)
    nc = min(FFN_DOWN_COLS, d_out)
    for n in range(d_out // nc):
        cols = slice(n * nc, (n + 1) * nc)
        acc_ref[:, cols] += _dot(a, wd_ref[:, cols].astype(BF16))

    @pl.when(k == nk - 1)
    def _():
        gate = 0.5 * m_ref[3 * sub + 2:3 * sub + 3, :]
        sub_rows = min(NORM_SLAB, rs)

        def finish_slab(s, slot):
            def finish_rows(r, carry):
                rows = pl.ds(pl.multiple_of(r * sub_rows, sub_rows), sub_rows)
                acc_rows = pl.ds(pl.multiple_of(s * rs + r * sub_rows, sub_rows), sub_rows)
                res = xbuf[slot, rows, :] + gate * acc_ref[acc_rows, :]
                if final:
                    res = _rmsnorm(res, fg_ref[...])
                acc_ref[acc_rows, :] = res
                return carry

            lax.fori_loop(0, rs // sub_rows, finish_rows, 0, unroll=NORM_UNROLL)
            o_copy(s).start()

        for_x_slabs(finish_slab)
        for s in range(n_slabs):
            o_copy(s).wait()


def _ffn(h, ada4, norm_g, ffn_wgu, ffn_wd, layer, which, sub, seq, tm, tf, final_g=None):
    t, d = h.shape
    d_ff = ffn_wd.shape[2]
    nk = d_ff // tf
    tpb = seq // tm
    rs = min(FFN_ROW_SLAB, tm)
    final = final_g is not None
    in_specs = [
        pl.BlockSpec(memory_space=pl.ANY),
        pl.BlockSpec((1, d), lambda i, k: (0, 0)),
        pl.BlockSpec((None, None, 3 * N_SUBLAYERS, d), lambda i, k: (layer, i // tpb, 0, 0)),
        pl.BlockSpec((None, None, d, tf), lambda i, k: (layer, which, 0, k)),
        pl.BlockSpec((None, None, d, tf), lambda i, k: (layer, which, 0, nk + k)),
        pl.BlockSpec((None, None, tf, d), lambda i, k: (layer, which, k, 0)),
    ]
    args = [h, norm_g[layer, sub][None, :], ada4, ffn_wgu, ffn_wgu, ffn_wd]
    if final:
        in_specs.append(pl.BlockSpec((1, d), lambda i, k: (0, 0)))
        args.append(final_g[None, :])
    return pl.pallas_call(
        functools.partial(_ffn_kernel, sub, final, rs),
        grid=(t // tm, nk),
        in_specs=in_specs,
        out_specs=pl.BlockSpec(memory_space=pl.ANY),
        out_shape=jax.ShapeDtypeStruct((t, d), F32),
        scratch_shapes=[
            pltpu.VMEM((tm, d), F32),
            pltpu.VMEM((tm, d), BF16),
            pltpu.VMEM((FFN_SLAB_SLOTS, rs, d), F32),
            pltpu.SemaphoreType.DMA((FFN_SLAB_SLOTS,)),
            pltpu.SemaphoreType.DMA((tm // rs,)),
        ],
        compiler_params=_cparams(("arbitrary", "arbitrary")),
        name=f"ffn_l{layer}_{which}",
    )(*args)


def _keep_bf16_weight(w_ref, wbf_ref):
    j = pl.program_id(1)

    @pl.when(pl.program_id(0) == 0)
    def _():
        wbf_ref[j] = w_ref[...].astype(BF16)

    return wbf_ref[j]


def _resident_weight_spec(widx, kdim, tn, nj):
    return pl.BlockSpec((None, kdim, tn), lambda i, j: (widx, 0, jnp.where(i == 0, j, nj - 1)))


def _norm_linear_kernel(sub, x_ref, g_ref, m_ref, w_ref, o_ref, y_ref, wbf_ref):
    @pl.when(pl.program_id(1) == 0)
    def _():
        _write_norm_mod(x_ref, g_ref, m_ref, sub, y_ref)

    o_ref[...] = _dot(y_ref[...], _keep_bf16_weight(w_ref, wbf_ref))


def _norm_linear(h, ada4, g_row, w, widx, layer, sub, seq, tm, tn):
    t, d = h.shape
    n = w.shape[2]
    tpb = seq // tm
    return pl.pallas_call(
        functools.partial(_norm_linear_kernel, sub),
        grid=(t // tm, n // tn),
        in_specs=[
            pl.BlockSpec((tm, d), lambda i, j: (i, 0)),
            pl.BlockSpec((1, d), lambda i, j: (0, 0)),
            pl.BlockSpec((None, None, 3 * N_SUBLAYERS, d), lambda i, j: (layer, i // tpb, 0, 0)),
            _resident_weight_spec(widx, d, tn, n // tn),
        ],
        out_specs=pl.BlockSpec((tm, tn), lambda i, j: (i, j)),
        out_shape=jax.ShapeDtypeStruct((t, n), F32),
        scratch_shapes=[pltpu.VMEM((tm, d), BF16), pltpu.VMEM((n // tn, d, tn), BF16)],
        compiler_params=_cparams(("arbitrary", "arbitrary")),
        name=f"norm_linear_l{layer}",
    )(h, g_row, ada4, w)


def _pool_kernel(u_ref, w_ref, ls_ref, o_ref):
    g = pl.program_id(1)
    u = u_ref[...]
    t = lax.broadcasted_iota(jnp.int32, u.shape, 0)

    def shifted(v, d):
        return jnp.where(t >= d, pltpu.roll(v, d, 0), 0.0)

    for gi, win in enumerate(POOL_WINDOWS):
        @pl.when(g == gi)
        def _(win=win):
            acc = u
            d = 1
            while d < win:
                acc = acc + shifted(acc, d)
                d *= 2
            cnt = jnp.minimum(t + 1, win).astype(F32)
            p = (acc / cnt - u).astype(BF16)
            v = _dot(p, w_ref[...].astype(BF16)) * ls_ref[...]
            o_ref[...] = v.astype(BF16)


def _pool_core(u, w_grp, ls, widx, batch, seq):
    t, d = u.shape
    _, groups, c, _ = w_grp.shape
    return pl.pallas_call(
        _pool_kernel,
        grid=(batch, groups),
        in_specs=[
            pl.BlockSpec((seq, c), lambda b, g: (b, g)),
            pl.BlockSpec((None, None, c, c), lambda b, g: (widx, g, 0, 0)),
            pl.BlockSpec((None, 1, c), lambda b, g: (widx, 0, g)),
        ],
        out_specs=pl.BlockSpec((seq, c), lambda b, g: (b, g)),
        out_shape=jax.ShapeDtypeStruct((t, d), BF16),
        compiler_params=_cparams(("arbitrary", "arbitrary")),
        name="pool_core",
    )(u, w_grp, ls.reshape(ls.shape[0], 1, d))


def _linear_residual_kernel(sub, a_ref, w_ref, h_ref, m_ref, o_ref, wbf_ref):
    gate = m_ref[3 * sub + 2:3 * sub + 3, :]
    o_ref[...] = h_ref[...] + gate * _dot(a_ref[...], _keep_bf16_weight(w_ref, wbf_ref))


def _linear_residual(a, w, widx, h, ada4, layer, sub, seq, tm, tn):
    t, kdim = a.shape
    n = w.shape[2]
    tpb = seq // tm
    return pl.pallas_call(
        functools.partial(_linear_residual_kernel, sub),
        grid=(t // tm, n // tn),
        in_specs=[
            pl.BlockSpec((tm, kdim), lambda i, j: (i, 0)),
            _resident_weight_spec(widx, kdim, tn, n // tn),
            pl.BlockSpec((tm, tn), lambda i, j: (i, j)),
            pl.BlockSpec((None, None, 3 * N_SUBLAYERS, tn), lambda i, j: (layer, i // tpb, 0, j)),
        ],
        out_specs=pl.BlockSpec((tm, tn), lambda i, j: (i, j)),
        out_shape=jax.ShapeDtypeStruct((t, n), F32),
        scratch_shapes=[pltpu.VMEM((n // tn, kdim, tn), BF16)],
        compiler_params=_cparams(("arbitrary", "arbitrary")),
        name=f"linear_residual_l{layer}",
    )(a, w, h, ada4)


ROPE_SET_Q, ROPE_SET_K, ROPE_SET_NONE, ROPE_SET_IDX = 0, 1, 2, 3
HALF_LANES = LANES // 2


def _rope_table_kernel(pos_ref, invf_ref, o_ref):
    pos = pos_ref[...].astype(F32)
    lane = lax.broadcasted_iota(jnp.int32, (pos.shape[0], LANES), 1)
    first = lane < HALF_LANES

    def tables(invf, is_rot):
        ang = pos * invf
        c = jnp.where(is_rot, jnp.cos(ang), 1.0)
        s = jnp.where(is_rot, jnp.where(first, -jnp.sin(ang), jnp.sin(ang)), 0.0)
        return c, s

    cb, sb = tables(invf_ref[:, 0:LANES], (lane & (HALF_LANES - 1)) < ROT_DIM // 2)
    ci, si = tables(invf_ref[:, LANES:2 * LANES], (lane & (HALF_LANES // 2 - 1)) < IDX_ROT_DIM // 2)
    o_ref[2 * ROPE_SET_Q] = cb * Q_PRESCALE
    o_ref[2 * ROPE_SET_Q + 1] = sb * Q_PRESCALE
    o_ref[2 * ROPE_SET_K] = cb
    o_ref[2 * ROPE_SET_K + 1] = sb
    o_ref[2 * ROPE_SET_NONE] = jnp.ones_like(cb)
    o_ref[2 * ROPE_SET_NONE + 1] = jnp.zeros_like(cb)
    o_ref[2 * ROPE_SET_IDX] = ci
    o_ref[2 * ROPE_SET_IDX + 1] = si


def _rope_tables(positions, tm):
    t = positions.size
    lane = jnp.arange(LANES)

    def inv_freq(rot_dim):
        half = rot_dim // 2
        return ROPE_THETA ** (-jnp.arange(half, dtype=F32) / half)

    invf = jnp.concatenate([inv_freq(ROT_DIM)[lane % (ROT_DIM // 2)],
                            inv_freq(IDX_ROT_DIM)[lane % (IDX_ROT_DIM // 2)]])[None, :]
    return pl.pallas_call(
        _rope_table_kernel,
        grid=(t // tm,),
        in_specs=[
            pl.BlockSpec((tm, 1), lambda i: (i, 0)),
            pl.BlockSpec((1, 2 * LANES), lambda i: (0, 0)),
        ],
        out_specs=pl.BlockSpec((8, tm, LANES), lambda i: (0, i, 0)),
        out_shape=jax.ShapeDtypeStruct((8, t, LANES), F32),
        compiler_params=_cparams(("arbitrary",)),
        name="rope_tables",
    )(positions.reshape(t, 1), invf)


def _rope(x, c, s):
    return x * c + pltpu.roll(x, HALF_LANES, 1) * s


def _permute_dsa_weights(w):
    d = w.shape[0]

    def big_heads(cols, n_heads):
        x = cols.reshape(d, n_heads, HEAD_DIM)
        h, p = ROT_DIM // 2, (HEAD_DIM - ROT_DIM) // 2
        return jnp.concatenate([x[..., 0:h], x[..., ROT_DIM:ROT_DIM + p], x[..., h:ROT_DIM],
                                x[..., ROT_DIM + p:]], axis=-1).reshape(d, n_heads * HEAD_DIM)

    def lo_hi(x):
        h, p = IDX_ROT_DIM // 2, (IDX_DIM - IDX_ROT_DIM) // 2
        lo = jnp.concatenate([x[..., 0:h], x[..., IDX_ROT_DIM:IDX_ROT_DIM + p]], axis=-1)
        hi = jnp.concatenate([x[..., h:IDX_ROT_DIM], x[..., IDX_ROT_DIM + p:]], axis=-1)
        return lo, hi

    o_k, o_v, o_qi = Q_W, Q_W + KV_W, Q_W + 2 * KV_W
    lo, hi = lo_hi(w[:, o_qi:DSA_MAIN_W].reshape(d, IDX_HEADS // 2, 2, IDX_DIM))
    qi = jnp.concatenate([lo[:, :, 0], lo[:, :, 1], hi[:, :, 0], hi[:, :, 1]],
                         axis=-1).reshape(d, QI_W)
    main = jnp.concatenate([big_heads(w[:, 0:o_k], N_HEADS), big_heads(w[:, o_k:o_v], N_KV_HEADS),
                            w[:, o_v:o_qi], qi], axis=1)
    klo, khi = lo_hi(w[:, DSA_MAIN_W:DSA_MAIN_W + IDX_DIM])
    z = jnp.zeros_like(klo)
    wi = jnp.pad(w[:, DSA_MAIN_W + IDX_DIM:], ((0, 0), (0, LANES - IDX_HEADS)))
    small = jnp.concatenate([klo, z, khi, z, z, klo, z, khi, wi], axis=1)
    return main.astype(BF16), small.astype(BF16)


def _dsa_proj_kernel(sub, tn, x_ref, g_ref, m_ref, w_ref, ws_ref, tab_ref, o_ref, ok_ref, ow_ref,
                     y_ref, wbf_ref):
    j = pl.program_id(1)
    first_k, first_v, first_qi = Q_W // tn, (Q_W + KV_W) // tn, (Q_W + 2 * KV_W) // tn

    @pl.when(j == 0)
    def _():
        _write_norm_mod(x_ref, g_ref, m_ref, sub, y_ref)
        small = _dot(y_ref[...], ws_ref[...])
        for s in range(2):
            ok_ref[:, s * LANES:(s + 1) * LANES] = _rope(
                small[:, s * LANES:(s + 1) * LANES], tab_ref[2 * ROPE_SET_IDX],
                tab_ref[2 * ROPE_SET_IDX + 1]).astype(BF16)
        ow_ref[...] = small[:, 2 * LANES:3 * LANES]

    @pl.when(pl.program_id(0) == 0)
    def _():
        wbf_ref[j] = w_ref[...]

    acc = _dot(y_ref[...], wbf_ref[j])
    rope_set = jnp.where(j < first_k, ROPE_SET_Q,
                         jnp.where(j < first_v, ROPE_SET_K,
                                   jnp.where(j < first_qi, ROPE_SET_NONE, ROPE_SET_IDX)))
    c = tab_ref[2 * rope_set]
    s = tab_ref[2 * rope_set + 1]
    for sl in range(tn // LANES):
        cols = slice(sl * LANES, (sl + 1) * LANES)
        o_ref[:, cols] = _rope(acc[:, cols], c, s).astype(BF16)


def _dsa_proj(h, ada4, g_row, w_main, w_small, tabs, layer, sub, seq, tm, tn):
    t, d = h.shape
    tpb = seq // tm
    return pl.pallas_call(
        functools.partial(_dsa_proj_kernel, sub, tn),
        grid=(t // tm, DSA_MAIN_W // tn),
        in_specs=[
            pl.BlockSpec((tm, d), lambda i, j: (i, 0), pipeline_mode=pl.Buffered(1)),
            pl.BlockSpec((1, d), lambda i, j: (0, 0)),
            pl.BlockSpec((None, None, 3 * N_SUBLAYERS, d), lambda i, j: (layer, i // tpb, 0, 0)),
            _resident_weight_spec(0, d, tn, DSA_MAIN_W // tn),
            pl.BlockSpec((d, 3 * LANES), lambda i, j: (0, 0)),
            pl.BlockSpec((8, tm, LANES), lambda i, j: (0, i, 0)),
        ],
        out_specs=[
            pl.BlockSpec((tm, tn), lambda i, j: (i, j)),
            pl.BlockSpec((tm, 2 * LANES), lambda i, j: (i, 0)),
            pl.BlockSpec((tm, LANES), lambda i, j: (i, 0)),
        ],
        out_shape=[
            jax.ShapeDtypeStruct((t, DSA_MAIN_W), BF16),
            jax.ShapeDtypeStruct((t, 2 * LANES), BF16),
            jax.ShapeDtypeStruct((t, LANES), F32),
        ],
        scratch_shapes=[pltpu.VMEM((tm, d), BF16),
                        pltpu.VMEM((DSA_MAIN_W // tn, d, tn), BF16)],
        compiler_params=_cparams(("arbitrary", "arbitrary")),
        name=f"dsa_proj_l{layer}",
    )(h, g_row, ada4, w_main[None], w_small, tabs)


def _dsa_attn_kernel(tq, tk, top_k, q_ref, qi_ref, wi_ref, k_ref, v_ref, kk_ref, o_ref,
                     score_ref, bias_ref, thr_ref, wib_ref, qg_ref, m_ref, acc_ref):
    i = pl.program_id(1)
    n_chunks = ((i + 1) * tq + tk - 1) // tk
    max_chunks = score_ref.shape[1] // tk
    t_pos = i * tq + lax.broadcasted_iota(jnp.int32, (tq, tk), 0)
    k_iota = lax.broadcasted_iota(jnp.int32, (tq, tk), 1)
    n_cb = tk // LANES

    for hh in range(IDX_HEADS):
        wib_ref[hh] = jnp.broadcast_to(wi_ref[:, hh:hh + 1], (tq, LANES))
    for g in range(N_KV_HEADS):
        for r in range(KV_GROUP):
            hd = g * KV_GROUP + r
            qg_ref[g, r * tq:(r + 1) * tq, :] = q_ref[:, hd * HEAD_DIM:(hd + 1) * HEAD_DIM]

    def score_chunk(c, carry):
        off = pl.multiple_of(c * tk, tk)
        k_even = kk_ref[pl.ds(off, tk), 0:LANES]
        k_odd = kk_ref[pl.ds(off, tk), LANES:2 * LANES]
        acc = [jnp.zeros((tq, LANES), F32) for _ in range(n_cb)]
        for pair in range(IDX_HEADS // 2):
            qp = qi_ref[:, pair * LANES:(pair + 1) * LANES]
            for par, kop in enumerate((k_even, k_odd)):
                r = _dot_t(qp, kop)
                w = wib_ref[2 * pair + par]
                for cb in range(n_cb):
                    acc[cb] = acc[cb] + w * jnp.maximum(r[:, cb * LANES:(cb + 1) * LANES], 0.0)
        sc = jnp.concatenate(acc, axis=1)
        score_ref[:, pl.ds(off, tk)] = jnp.where(k_iota + off <= t_pos, sc, NEG)
        return carry

    lax.fori_loop(0, n_chunks, score_chunk, 0)

    def key_to_f32(key):
        return pltpu.bitcast(key ^ ((key >> 31) & 0x7FFFFFFF), F32)

    k_f = float(top_k)

    def select(n_static):
        def bit_step(n, thr_key):
            cand = thr_key + lax.shift_left(jnp.int32(1), 31 - n)
            thr = key_to_f32(cand)
            cnt = jnp.zeros((tq, LANES), F32)
            for cb in range(n_static * n_cb):
                s = score_ref[:, cb * LANES:(cb + 1) * LANES]
                cnt = cnt + jnp.where(s >= thr, 1.0, 0.0)
            total = jnp.sum(cnt, axis=1, keepdims=True)
            return jnp.where(total >= k_f, cand, thr_key)

        thr_key = lax.fori_loop(0, 32, bit_step, jnp.full((tq, LANES), INT32_MIN, jnp.int32),
                                unroll=4)
        thr_ref[...] = key_to_f32(thr_key)

    for n_static in range(1, max_chunks + 1):
        pl.when(n_chunks == n_static)(functools.partial(select, n_static))

    def bias_chunk(c, carry):
        off = pl.multiple_of(c * tk, tk)
        thr = thr_ref[...]
        causal = k_iota + off <= t_pos
        for cb in range(n_cb):
            cols = pl.ds(off + cb * LANES, LANES)
            sel = score_ref[:, cols] >= thr
            bias_ref[:, cols] = jnp.where(
                sel, jnp.where(causal[:, cb * LANES:(cb + 1) * LANES], 0.0, NEG), NEG)
        return carry

    lax.fori_loop(0, n_chunks, bias_chunk, 0)

    def count_rows(pred):
        def body(c, cnt):
            off = pl.multiple_of(c * tk, tk)
            thr = thr_ref[...]
            for cb in range(n_cb):
                s = score_ref[:, pl.ds(off + cb * LANES, LANES)]
                cnt = cnt + jnp.where(pred(s, thr), 1.0, 0.0)
            return cnt
        cnt = lax.fori_loop(0, n_chunks, body, jnp.zeros((tq, LANES), F32))
        return jnp.sum(cnt, axis=1, keepdims=True)

    n_ge = count_rows(lambda s, thr: s >= thr)

    @pl.when(jnp.max(n_ge) > k_f)
    def _():
        n_take = k_f - count_rows(lambda s, thr: s > thr)
        before = jnp.where(
            lax.broadcasted_iota(jnp.int32, (tk, tk), 0) < lax.broadcasted_iota(jnp.int32, (tk, tk), 1),
            1.0, 0.0).astype(BF16)

        def tie_chunk(c, seen):
            off = pl.multiple_of(c * tk, tk)
            sc = score_ref[:, pl.ds(off, tk)]
            thr = jnp.concatenate([thr_ref[...]] * n_cb, axis=1)
            eq = jnp.where(sc == thr, 1.0, 0.0)
            rank = _dot(eq.astype(BF16), before) + seen
            keep = jnp.where(sc > thr, 1.0, jnp.where(rank < n_take, eq, 0.0))
            causal = k_iota + off <= t_pos
            bias_ref[:, pl.ds(off, tk)] = jnp.where(causal, jnp.where(keep > 0.0, 0.0, NEG), NEG)
            return seen + jnp.sum(eq, axis=1, keepdims=True)

        lax.fori_loop(0, n_chunks, tie_chunk, jnp.zeros((tq, 1), F32))

    rows = KV_GROUP * tq
    m_ref[...] = jnp.full(m_ref.shape, NEG, F32)
    acc_ref[...] = jnp.zeros(acc_ref.shape, F32)
    ones = jnp.ones((tk, LANES), BF16)

    def attn_chunk(c, carry):
        off = pl.multiple_of(c * tk, tk)
        b = bias_ref[:, pl.ds(off, tk)]
        for g in range(N_KV_HEADS):
            kc = k_ref[pl.ds(off, tk), g * HEAD_DIM:(g + 1) * HEAD_DIM]
            vc = v_ref[pl.ds(off, tk), g * HEAD_DIM:(g + 1) * HEAD_DIM]
            s = _dot_t(qg_ref[g], kc)
            s = (s.reshape(KV_GROUP, tq, tk) + b[None]).reshape(rows, tk)
            m_old = m_ref[g]
            m_new = jnp.maximum(m_old, jnp.max(s, axis=1, keepdims=True))
            alpha = jnp.exp2(m_old - m_new)
            p = jnp.exp2(s - jnp.concatenate([m_new] * n_cb, axis=1))
            pv = _dot(p.astype(BF16), jnp.concatenate([vc, ones], axis=1))
            acc_ref[g] = jnp.concatenate([alpha, alpha], axis=1) * acc_ref[g] + pv
            m_ref[g] = m_new
        return carry

    lax.fori_loop(0, n_chunks, attn_chunk, 0)

    for g in range(N_KV_HEADS):
        out = acc_ref[g, :, 0:HEAD_DIM] / acc_ref[g, :, HEAD_DIM:2 * HEAD_DIM]
        for r in range(KV_GROUP):
            hd = g * KV_GROUP + r
            o_ref[:, hd * HEAD_DIM:(hd + 1) * HEAD_DIM] = out[r * tq:(r + 1) * tq].astype(BF16)


def _dsa_attn(main, kk, wi, batch, seq, tq, tk, top_k):
    t = main.shape[0]
    nq = seq // tq
    rows = KV_GROUP * tq
    return pl.pallas_call(
        functools.partial(_dsa_attn_kernel, tq, tk, top_k),
        grid=(batch, nq),
        in_specs=[
            pl.BlockSpec((tq, Q_W), lambda b, i: (b * nq + i, 0)),
            pl.BlockSpec((tq, QI_W), lambda b, i: (b * nq + i, (Q_W + 2 * KV_W) // QI_W)),
            pl.BlockSpec((tq, LANES), lambda b, i: (b * nq + i, 0)),
            pl.BlockSpec((seq, KV_W), lambda b, i: (b, Q_W // KV_W)),
            pl.BlockSpec((seq, KV_W), lambda b, i: (b, Q_W // KV_W + 1)),
            pl.BlockSpec((seq, 2 * LANES), lambda b, i: (b, 0)),
        ],
        out_specs=pl.BlockSpec((tq, Q_W), lambda b, i: (b * nq + i, 0)),
        out_shape=jax.ShapeDtypeStruct((t, Q_W), BF16),
        scratch_shapes=[
            pltpu.VMEM((tq, seq), F32),
            pltpu.VMEM((tq, seq), F32),
            pltpu.VMEM((tq, LANES), F32),
            pltpu.VMEM((IDX_HEADS, tq, LANES), F32),
            pltpu.VMEM((N_KV_HEADS, rows, HEAD_DIM), BF16),
            pltpu.VMEM((N_KV_HEADS, rows, LANES), F32),
            pltpu.VMEM((N_KV_HEADS, rows, 2 * HEAD_DIM), F32),
        ],
        compiler_params=_cparams(("arbitrary", "arbitrary")),
        name="dsa_attn",
    )(main, main, wi, main, main, kk)


def _tiles(seq, d_ff):
    tm = min(1024, seq)
    tm_ffn = min(2048, seq)
    tf = 256 if d_ff % 256 == 0 else d_ff
    tq = min(256, seq)
    tk = min(256, seq)
    return tm, tm_ffn, tf, tq, tk


def kernel(x, c, positions, ada_w, ada_b, norm_g, final_g, ffn_wgu, ffn_wd,
           pool_w_in, pool_w_grp, pool_scale, pool_w_out, dsa_w_in, dsa_w_out):
    batch, seq, d = x.shape
    depth = ada_w.shape[0]
    d_ff = ffn_wd.shape[2]
    t = batch * seq
    tm, tm_ffn, tf, tq, tk = _tiles(seq, d_ff)
    tn = min(512, d)
    top_k = min(TOPK_MAX, seq // 4)
    assert seq % tm == 0 and seq % tm_ffn == 0 and seq % tq == 0 and seq % tk == 0 and tk >= top_k
    assert dsa_w_in.shape[2] == DSA_MAIN_W + DSA_SMALL_W and DSA_MAIN_W % tn == 0

    rows = -(-batch // SUBLANES) * SUBLANES
    c8 = jnp.pad(c, ((0, rows - batch), (0, 0)))
    ada_n = ada_w.shape[2]
    ada = _ada(c8, ada_w, ada_b, 1024 if ada_n % 1024 == 0 else d)
    ada4 = ada.reshape(depth, rows, 3 * N_SUBLAYERS, d)

    tabs = _rope_tables(positions, tm) if depth > 1 else None

    h = x.reshape(t, d)
    for layer in range(depth):
        j = layer // 2
        h = _ffn(h, ada4, norm_g, ffn_wgu, ffn_wd, layer, 0, 0, seq, tm_ffn, tf)
        g_row = norm_g[layer, 1][None, :]
        if layer % 2 == 0:
            u = _norm_linear(h, ada4, g_row, pool_w_in, j, layer, 1, seq, tm, tn)
            a = _pool_core(u, pool_w_grp, pool_scale, j, batch, seq)
            h = _linear_residual(a, pool_w_out, j, h, ada4, layer, 1, seq, tm, tn)
        else:
            w_main, w_small = _permute_dsa_weights(dsa_w_in[j])
            main, kk, wi = _dsa_proj(h, ada4, g_row, w_main, w_small, tabs,
                                     layer, 1, seq, tm, tn)
            o = _dsa_attn(main, kk, wi, batch, seq, tq, tk, top_k)
            h = _linear_residual(o, dsa_w_out, j, h, ada4, layer, 1, seq, tm, tn)
        last = layer == depth - 1
        h = _ffn(h, ada4, norm_g, ffn_wgu, ffn_wd, layer, 1, 2, seq, tm_ffn, tf,
                 final_g=final_g if last else None)
    return h.reshape(batch, seq, d)
```

```python
import functools

import jax
import jax.numpy as jnp
import numpy as np
from jax import lax
from jax.experimental import pallas as pl
from jax.experimental.pallas import tpu as pltpu

F32 = jnp.float32
BF16 = jnp.bfloat16

N_SUBLAYERS = 3
POOL_WINDOWS = (2, 4, 8, 16)
N_HEADS = 16
N_KV_HEADS = 4
KV_GROUP = N_HEADS // N_KV_HEADS
HEAD_DIM = 128
ROT_DIM = HEAD_DIM // 4
ROPE_THETA = 500000.0
IDX_HEADS = 16
IDX_DIM = 64
IDX_ROT_DIM = IDX_DIM // 4
TOPK_MAX = 256
EPS = 1e-6
NEG = -1e30
Q_W = N_HEADS * HEAD_DIM
KV_W = N_KV_HEADS * HEAD_DIM
QI_W = IDX_HEADS * IDX_DIM
DSA_MAIN_W = Q_W + 2 * KV_W + QI_W
DSA_SMALL_W = IDX_DIM + IDX_HEADS
Q_PRESCALE = HEAD_DIM ** -0.5 * 1.4426950408889634

LANES = 128
SUBLANES = 8
VMEM_LIMIT_BYTES = 56 * 1024 * 1024
INT32_MIN = -(2 ** 31)


def _cparams(semantics):
    return pltpu.CompilerParams(dimension_semantics=semantics,
                                vmem_limit_bytes=VMEM_LIMIT_BYTES)


def _rmsnorm(x, g):
    ms = jnp.mean(x * x, axis=-1, keepdims=True)
    return x * lax.rsqrt(ms + EPS) * g


NORM_SLAB = 16
NORM_UNROLL = 4
FFN_ROW_SLAB = 128
FFN_SLAB_SLOTS = 4
FFN_DOWN_COLS = 256


def _for_row_slabs(n_rows, slab, body, unroll=1):
    def step(r, carry):
        body(pl.ds(pl.multiple_of(r * slab, slab), slab))
        return carry
    lax.fori_loop(0, n_rows // slab, step, 0, unroll=unroll)


def _write_norm_mod(x_ref, g_ref, m_ref, sub, y_ref, also=None):
    shift = m_ref[3 * sub + 0:3 * sub + 1, :]
    gs = g_ref[...] * (1.0 + m_ref[3 * sub + 1:3 * sub + 2, :])
    slab = min(NORM_SLAB, x_ref.shape[0])

    def norm_rows(rows):
        x = x_ref[rows, :]
        r = lax.rsqrt(jnp.mean(x * x, axis=-1, keepdims=True) + EPS)
        y_ref[rows, :] = ((x * r) * gs + shift).astype(BF16)
        if also is not None:
            also(rows)

    _for_row_slabs(x_ref.shape[0], slab, norm_rows, unroll=NORM_UNROLL)


def _silu(x):
    return x * jax.nn.sigmoid(x)


def _dot(a, b):
    return jnp.dot(a, b, preferred_element_type=F32)


def _dot_t(a, b):
    return lax.dot_general(a, b, (((1,), (1,)), ((), ())), preferred_element_type=F32)


def _ada_kernel(c_ref, w_ref, b_ref, o_ref):
    sc = _silu(c_ref[...]).astype(BF16)
    o_ref[...] = _dot(sc, w_ref[...].astype(BF16)) + b_ref[...]


def _ada(c8, ada_w, ada_b, tn):
    depth, d, n = ada_w.shape
    rows = c8.shape[0]
    return pl.pallas_call(
        _ada_kernel,
        grid=(depth, n // tn),
        in_specs=[
            pl.BlockSpec((rows, d), lambda l, j: (0, 0)),
            pl.BlockSpec((None, d, tn), lambda l, j: (l, 0, j)),
            pl.BlockSpec((None, 1, tn), lambda l, j: (l, 0, j)),
        ],
        out_specs=pl.BlockSpec((None, rows, tn), lambda l, j: (l, 0, j)),
        out_shape=jax.ShapeDtypeStruct((depth, rows, n), F32),
        compiler_params=_cparams(("arbitrary", "arbitrary")),
        name="ada",
    )(c8, ada_w, ada_b.reshape(depth, 1, n))


def _ffn_kernel(sub, final, rs, *refs):
    if final:
        (x_hbm, g_ref, m_ref, wg_ref, wu_ref, wd_ref, fg_ref, o_hbm,
         acc_ref, y_ref, xbuf, in_sem, out_sem) = refs
    else:
        (x_hbm, g_ref, m_ref, wg_ref, wu_ref, wd_ref, o_hbm,
         acc_ref, y_ref, xbuf, in_sem, out_sem) = refs
    k = pl.program_id(1)
    nk = pl.num_programs(1)
    tm, d_out = acc_ref.shape
    n_slabs = tm // rs
    n_slots = xbuf.shape[0]
    row0 = pl.program_id(0) * tm

    def x_copy(s):
        slot = s % n_slots
        return pltpu.make_async_copy(x_hbm.at[pl.ds(row0 + s * rs, rs), :], xbuf.at[slot],
                                     in_sem.at[slot])

    def o_copy(s):
        return pltpu.make_async_copy(acc_ref.at[pl.ds(s * rs, rs), :],
                                     o_hbm.at[pl.ds(row0 + s * rs, rs), :], out_sem.at[s])

    def for_x_slabs(body):
        for s in range(min(n_slots, n_slabs)):
            x_copy(s).start()
        for s in range(n_slabs):
            x_copy(s).wait()
            body(s, s % n_slots)
            if s + n_slots < n_slabs:
                x_copy(s + n_slots).start()

    def wait_results():
        for s in range(n_slabs):
            o_copy(s).wait()

    @pl.when(k == 0)
    def _():
        def norm_slab(s, slot):
            _write_norm_mod(xbuf.at[slot], g_ref, m_ref, sub, y_ref.at[pl.ds(s * rs, rs), :])

        for_x_slabs(norm_slab)
        pl.when(pl.program_id(0) > 0)(wait_results)
        acc_ref[...] = jnp.zeros(acc_ref.shape, F32)

    y = y_ref[...]
    g = _dot(y, wg_ref[...].astype(BF16))
    u = _dot(y, wu_ref[...].astype(BF16))
    a = (_silu(g) * u).astype(BF16)
    nc = min(FFN_DOWN_COLS, d_out)
    for n in range(d_out // nc):
        cols = slice(n * nc, (n + 1) * nc)
        acc_ref[:, cols] += _dot(a, wd_ref[:, cols].astype(BF16))

    @pl.when(k == nk - 1)
    def _():
        gate = 0.5 * m_ref[3 * sub + 2:3 * sub + 3, :]
        sub_rows = min(NORM_SLAB, rs)

        def finish_slab(s, slot):
            def finish_rows(r, carry):
                rows = pl.ds(pl.multiple_of(r * sub_rows, sub_rows), sub_rows)
                acc_rows = pl.ds(pl.multiple_of(s * rs + r * sub_rows, sub_rows), sub_rows)
                res = xbuf[slot, rows, :] + gate * acc_ref[acc_rows, :]
                if final:
                    res = _rmsnorm(res, fg_ref[...])
                acc_ref[acc_rows, :] = res
                return carry

            lax.fori_loop(0, rs // sub_rows, finish_rows, 0, unroll=NORM_UNROLL)
            o_copy(s).start()

        for_x_slabs(finish_slab)
        pl.when(pl.program_id(0) == pl.num_programs(0) - 1)(wait_results)


def _ffn(h, ada4, norm_g, ffn_wgu, ffn_wd, layer, which, sub, seq, tm, tf, final_g=None):
    t, d = h.shape
    d_ff = ffn_wd.shape[2]
    nk = d_ff // tf
    tpb = seq // tm
    rs = min(FFN_ROW_SLAB, tm)
    final = final_g is not None
    in_specs = [
        pl.BlockSpec(memory_space=pl.ANY),
        pl.BlockSpec((1, d), lambda i, k: (0, 0)),
        pl.BlockSpec((None, None, 3 * N_SUBLAYERS, d), lambda i, k: (layer, i // tpb, 0, 0)),
        pl.BlockSpec((None, None, d, tf), lambda i, k: (layer, which, 0, k)),
        pl.BlockSpec((None, None, d, tf), lambda i, k: (layer, which, 0, nk + k)),
        pl.BlockSpec((None, None, tf, d), lambda i, k: (layer, which, k, 0)),
    ]
    args = [h, norm_g[layer, sub][None, :], ada4, ffn_wgu, ffn_wgu, ffn_wd]
    if final:
        in_specs.append(pl.BlockSpec((1, d), lambda i, k: (0, 0)))
        args.append(final_g[None, :])
    return pl.pallas_call(
        functools.partial(_ffn_kernel, sub, final, rs),
        grid=(t // tm, nk),
        in_specs=in_specs,
        out_specs=pl.BlockSpec(memory_space=pl.ANY),
        out_shape=jax.ShapeDtypeStruct((t, d), F32),
        scratch_shapes=[
            pltpu.VMEM((tm, d), F32),
            pltpu.VMEM((tm, d), BF16),
            pltpu.VMEM((FFN_SLAB_SLOTS, rs, d), F32),
            pltpu.SemaphoreType.DMA((FFN_SLAB_SLOTS,)),
            pltpu.SemaphoreType.DMA((tm // rs,)),
        ],
        compiler_params=_cparams(("arbitrary", "arbitrary")),
        name=f"ffn_l{layer}_{which}",
    )(*args)


def _keep_bf16_weight(w_ref, wbf_ref):
    j = pl.program_id(1)

    @pl.when(pl.program_id(0) == 0)
    def _():
        wbf_ref[j] = w_ref[...].astype(BF16)

    return wbf_ref[j]


def _resident_weight_spec(widx, kdim, tn, nj):
    return pl.BlockSpec((None, kdim, tn), lambda i, j: (widx, 0, jnp.where(i == 0, j, nj - 1)))


def _norm_linear_kernel(sub, x_ref, g_ref, m_ref, w_ref, o_ref, y_ref, wbf_ref):
    @pl.when(pl.program_id(1) == 0)
    def _():
        _write_norm_mod(x_ref, g_ref, m_ref, sub, y_ref)

    o_ref[...] = _dot(y_ref[...], _keep_bf16_weight(w_ref, wbf_ref))


def _norm_linear(h, ada4, g_row, w, widx, layer, sub, seq, tm, tn):
    t, d = h.shape
    n = w.shape[2]
    tpb = seq // tm
    return pl.pallas_call(
        functools.partial(_norm_linear_kernel, sub),
        grid=(t // tm, n // tn),
        in_specs=[
            pl.BlockSpec((tm, d), lambda i, j: (i, 0)),
            pl.BlockSpec((1, d), lambda i, j: (0, 0)),
            pl.BlockSpec((None, None, 3 * N_SUBLAYERS, d), lambda i, j: (layer, i // tpb, 0, 0)),
            _resident_weight_spec(widx, d, tn, n // tn),
        ],
        out_specs=pl.BlockSpec((tm, tn), lambda i, j: (i, j)),
        out_shape=jax.ShapeDtypeStruct((t, n), F32),
        scratch_shapes=[pltpu.VMEM((tm, d), BF16), pltpu.VMEM((n // tn, d, tn), BF16)],
        compiler_params=_cparams(("arbitrary", "arbitrary")),
        name=f"norm_linear_l{layer}",
    )(h, g_row, ada4, w)


def _pool_kernel(u_ref, w_ref, ls_ref, o_ref):
    g = pl.program_id(1)
    u = u_ref[...]
    t = lax.broadcasted_iota(jnp.int32, u.shape, 0)

    def shifted(v, d):
        return jnp.where(t >= d, pltpu.roll(v, d, 0), 0.0)

    for gi, win in enumerate(POOL_WINDOWS):
        @pl.when(g == gi)
        def _(win=win):
            acc = u
            d = 1
            while d < win:
                acc = acc + shifted(acc, d)
                d *= 2
            cnt = jnp.minimum(t + 1, win).astype(F32)
            p = (acc / cnt - u).astype(BF16)
            v = _dot(p, w_ref[...].astype(BF16)) * ls_ref[...]
            o_ref[...] = v.astype(BF16)


def _pool_core(u, w_grp, ls, widx, batch, seq):
    t, d = u.shape
    _, groups, c, _ = w_grp.shape
    return pl.pallas_call(
        _pool_kernel,
        grid=(batch, groups),
        in_specs=[
            pl.BlockSpec((seq, c), lambda b, g: (b, g)),
            pl.BlockSpec((None, None, c, c), lambda b, g: (widx, g, 0, 0)),
            pl.BlockSpec((None, 1, c), lambda b, g: (widx, 0, g)),
        ],
        out_specs=pl.BlockSpec((seq, c), lambda b, g: (b, g)),
        out_shape=jax.ShapeDtypeStruct((t, d), BF16),
        compiler_params=_cparams(("arbitrary", "arbitrary")),
        name="pool_core",
    )(u, w_grp, ls.reshape(ls.shape[0], 1, d))


def _linear_residual_kernel(sub, a_ref, w_ref, h_ref, m_ref, o_ref, wbf_ref):
    gate = m_ref[3 * sub + 2:3 * sub + 3, :]
    o_ref[...] = h_ref[...] + gate * _dot(a_ref[...], _keep_bf16_weight(w_ref, wbf_ref))


def _linear_residual(a, w, widx, h, ada4, layer, sub, seq, tm, tn):
    t, kdim = a.shape
    n = w.shape[2]
    tpb = seq // tm
    return pl.pallas_call(
        functools.partial(_linear_residual_kernel, sub),
        grid=(t // tm, n // tn),
        in_specs=[
            pl.BlockSpec((tm, kdim), lambda i, j: (i, 0)),
            _resident_weight_spec(widx, kdim, tn, n // tn),
            pl.BlockSpec((tm, tn), lambda i, j: (i, j)),
            pl.BlockSpec((None, None, 3 * N_SUBLAYERS, tn), lambda i, j: (layer, i // tpb, 0, j)),
        ],
        out_specs=pl.BlockSpec((tm, tn), lambda i, j: (i, j)),
        out_shape=jax.ShapeDtypeStruct((t, n), F32),
        scratch_shapes=[pltpu.VMEM((n // tn, kdim, tn), BF16)],
        compiler_params=_cparams(("arbitrary", "arbitrary")),
        name=f"linear_residual_l{layer}",
    )(a, w, h, ada4)


ROPE_SET_Q, ROPE_SET_K, ROPE_SET_NONE, ROPE_SET_IDX = 0, 1, 2, 3
HALF_LANES = LANES // 2


def _rope_table_kernel(pos_ref, invf_ref, o_ref):
    pos = pos_ref[...].astype(F32)
    lane = lax.broadcasted_iota(jnp.int32, (pos.shape[0], LANES), 1)
    first = lane < HALF_LANES

    def tables(invf, is_rot):
        ang = pos * invf
        c = jnp.where(is_rot, jnp.cos(ang), 1.0)
        s = jnp.where(is_rot, jnp.where(first, -jnp.sin(ang), jnp.sin(ang)), 0.0)
        return c, s

    cb, sb = tables(invf_ref[:, 0:LANES], (lane & (HALF_LANES - 1)) < ROT_DIM // 2)
    ci, si = tables(invf_ref[:, LANES:2 * LANES], (lane & (HALF_LANES // 2 - 1)) < IDX_ROT_DIM // 2)
    o_ref[2 * ROPE_SET_Q] = cb * Q_PRESCALE
    o_ref[2 * ROPE_SET_Q + 1] = sb * Q_PRESCALE
    o_ref[2 * ROPE_SET_K] = cb
    o_ref[2 * ROPE_SET_K + 1] = sb
    o_ref[2 * ROPE_SET_NONE] = jnp.ones_like(cb)
    o_ref[2 * ROPE_SET_NONE + 1] = jnp.zeros_like(cb)
    o_ref[2 * ROPE_SET_IDX] = ci
    o_ref[2 * ROPE_SET_IDX + 1] = si


def _rope_tables(positions, tm):
    t = positions.size
    lane = jnp.arange(LANES)

    def inv_freq(rot_dim):
        half = rot_dim // 2
        return ROPE_THETA ** (-jnp.arange(half, dtype=F32) / half)

    invf = jnp.concatenate([inv_freq(ROT_DIM)[lane % (ROT_DIM // 2)],
                            inv_freq(IDX_ROT_DIM)[lane % (IDX_ROT_DIM // 2)]])[None, :]
    return pl.pallas_call(
        _rope_table_kernel,
        grid=(t // tm,),
        in_specs=[
            pl.BlockSpec((tm, 1), lambda i: (i, 0)),
            pl.BlockSpec((1, 2 * LANES), lambda i: (0, 0)),
        ],
        out_specs=pl.BlockSpec((8, tm, LANES), lambda i: (0, i, 0)),
        out_shape=jax.ShapeDtypeStruct((8, t, LANES), F32),
        compiler_params=_cparams(("arbitrary",)),
        name="rope_tables",
    )(positions.reshape(t, 1), invf)


def _rope(x, c, s):
    return x * c + pltpu.roll(x, HALF_LANES, 1) * s


PERM_BIG, PERM_NONE, PERM_IDX = 0, 1, 2


def _lane_permutations():
    h, p = ROT_DIM // 2, (HEAD_DIM - ROT_DIM) // 2
    big = np.concatenate([np.arange(0, h), np.arange(ROT_DIM, ROT_DIM + p),
                          np.arange(h, ROT_DIM), np.arange(ROT_DIM + p, HEAD_DIM)])
    lo, hi = _idx_lo_hi_dims()
    idx = np.concatenate([lo, IDX_DIM + lo, hi, IDX_DIM + hi])
    mats = np.zeros((3, LANES, LANES), np.float32)
    for kind, src in ((PERM_BIG, big), (PERM_NONE, np.arange(LANES)), (PERM_IDX, idx)):
        mats[kind, src, np.arange(LANES)] = 1.0
    return jnp.asarray(mats, BF16)


def _idx_lo_hi_dims():
    h, p = IDX_ROT_DIM // 2, (IDX_DIM - IDX_ROT_DIM) // 2
    lo = np.concatenate([np.arange(0, h), np.arange(IDX_ROT_DIM, IDX_ROT_DIM + p)])
    hi = np.concatenate([np.arange(h, IDX_ROT_DIM), np.arange(IDX_ROT_DIM + p, IDX_DIM)])
    return lo, hi


def _dsa_small_weights(w_tail):
    h, p = IDX_ROT_DIM // 2, (IDX_DIM - IDX_ROT_DIM) // 2
    klo = jnp.concatenate([w_tail[:, 0:h], w_tail[:, IDX_ROT_DIM:IDX_ROT_DIM + p]], axis=1)
    khi = jnp.concatenate([w_tail[:, h:IDX_ROT_DIM], w_tail[:, IDX_ROT_DIM + p:IDX_DIM]], axis=1)
    z = jnp.zeros_like(klo)
    wi = jnp.pad(w_tail[:, IDX_DIM:], ((0, 0), (0, LANES - IDX_HEADS)))
    return jnp.concatenate([klo, z, khi, z, z, klo, z, khi, wi], axis=1).astype(BF16)


def _dsa_proj_kernel(sub, tn, x_ref, g_ref, m_ref, w_ref, ws_ref, perm_ref, tab_ref,
                     o_ref, ok_ref, ow_ref, y_ref, wbf_ref):
    j = pl.program_id(1)
    first_k, first_v, first_qi = Q_W // tn, (Q_W + KV_W) // tn, (Q_W + 2 * KV_W) // tn

    @pl.when(j == 0)
    def _():
        _write_norm_mod(x_ref, g_ref, m_ref, sub, y_ref)
        small = _dot(y_ref[...], ws_ref[...])
        for s in range(2):
            ok_ref[:, s * LANES:(s + 1) * LANES] = _rope(
                small[:, s * LANES:(s + 1) * LANES], tab_ref[2 * ROPE_SET_IDX],
                tab_ref[2 * ROPE_SET_IDX + 1]).astype(BF16)
        ow_ref[...] = small[:, 2 * LANES:3 * LANES]

    @pl.when(pl.program_id(0) == 0)
    def _():
        kind = jnp.where(j < first_v, PERM_BIG, jnp.where(j < first_qi, PERM_NONE, PERM_IDX))
        perm = perm_ref[kind]
        for sl in range(tn // LANES):
            cols = slice(sl * LANES, (sl + 1) * LANES)
            wbf_ref[j, :, cols] = _dot(w_ref[:, cols].astype(BF16), perm).astype(BF16)

    acc = _dot(y_ref[...], wbf_ref[j])
    rope_set = jnp.where(j < first_k, ROPE_SET_Q,
                         jnp.where(j < first_v, ROPE_SET_K,
                                   jnp.where(j < first_qi, ROPE_SET_NONE, ROPE_SET_IDX)))
    c = tab_ref[2 * rope_set]
    s = tab_ref[2 * rope_set + 1]
    for sl in range(tn // LANES):
        cols = slice(sl * LANES, (sl + 1) * LANES)
        o_ref[:, cols] = _rope(acc[:, cols], c, s).astype(BF16)


def _dsa_proj(h, ada4, g_row, w_in, widx, w_small, tabs, layer, sub, seq, tm, tn):
    t, d = h.shape
    tpb = seq // tm
    return pl.pallas_call(
        functools.partial(_dsa_proj_kernel, sub, tn),
        grid=(t // tm, DSA_MAIN_W // tn),
        in_specs=[
            pl.BlockSpec((tm, d), lambda i, j: (i, 0), pipeline_mode=pl.Buffered(1)),
            pl.BlockSpec((1, d), lambda i, j: (0, 0)),
            pl.BlockSpec((None, None, 3 * N_SUBLAYERS, d), lambda i, j: (layer, i // tpb, 0, 0)),
            _resident_weight_spec(widx, d, tn, DSA_MAIN_W // tn),
            pl.BlockSpec((d, 3 * LANES), lambda i, j: (0, 0)),
            pl.BlockSpec((3, LANES, LANES), lambda i, j: (0, 0, 0)),
            pl.BlockSpec((8, tm, LANES), lambda i, j: (0, i, 0)),
        ],
        out_specs=[
            pl.BlockSpec((tm, tn), lambda i, j: (i, j)),
            pl.BlockSpec((tm, 2 * LANES), lambda i, j: (i, 0)),
            pl.BlockSpec((tm, LANES), lambda i, j: (i, 0)),
        ],
        out_shape=[
            jax.ShapeDtypeStruct((t, DSA_MAIN_W), BF16),
            jax.ShapeDtypeStruct((t, 2 * LANES), BF16),
            jax.ShapeDtypeStruct((t, LANES), F32),
        ],
        scratch_shapes=[pltpu.VMEM((tm, d), BF16),
                        pltpu.VMEM((DSA_MAIN_W // tn, d, tn), BF16)],
        compiler_params=_cparams(("arbitrary", "arbitrary")),
        name=f"dsa_proj_l{layer}",
    )(h, g_row, ada4, w_in, w_small, _lane_permutations(), tabs)


def _dsa_attn_kernel(tq, tk, top_k, q_ref, qi_ref, wi_ref, k_ref, v_ref, kk_ref, o_ref,
                     score_ref, bias_ref, thr_ref, wib_ref, qg_ref, m_ref, acc_ref):
    i = pl.program_id(1)
    n_chunks = ((i + 1) * tq + tk - 1) // tk
    max_chunks = score_ref.shape[1] // tk
    t_pos = i * tq + lax.broadcasted_iota(jnp.int32, (tq, tk), 0)
    k_iota = lax.broadcasted_iota(jnp.int32, (tq, tk), 1)
    n_cb = tk // LANES

    for hh in range(IDX_HEADS):
        wib_ref[hh] = jnp.broadcast_to(wi_ref[:, hh:hh + 1], (tq, LANES))
    for g in range(N_KV_HEADS):
        for r in range(KV_GROUP):
            hd = g * KV_GROUP + r
            qg_ref[g, r * tq:(r + 1) * tq, :] = q_ref[:, hd * HEAD_DIM:(hd + 1) * HEAD_DIM]

    def score_chunk(c, carry):
        off = pl.multiple_of(c * tk, tk)
        k_even = kk_ref[pl.ds(off, tk), 0:LANES]
        k_odd = kk_ref[pl.ds(off, tk), LANES:2 * LANES]
        acc = [jnp.zeros((tq, LANES), F32) for _ in range(n_cb)]
        for pair in range(IDX_HEADS // 2):
            qp = qi_ref[:, pair * LANES:(pair + 1) * LANES]
            for par, kop in enumerate((k_even, k_odd)):
                r = _dot_t(qp, kop)
                w = wib_ref[2 * pair + par]
                for cb in range(n_cb):
                    acc[cb] = acc[cb] + w * jnp.maximum(r[:, cb * LANES:(cb + 1) * LANES], 0.0)
        sc = jnp.concatenate(acc, axis=1)
        score_ref[:, pl.ds(off, tk)] = jnp.where(k_iota + off <= t_pos, sc, NEG)
        return carry

    lax.fori_loop(0, n_chunks, score_chunk, 0)

    def key_to_f32(key):
        return pltpu.bitcast(key ^ ((key >> 31) & 0x7FFFFFFF), F32)

    k_f = float(top_k)

    def select(n_static):
        def bit_step(n, thr_key):
            cand = thr_key + lax.shift_left(jnp.int32(1), 31 - n)
            thr = key_to_f32(cand)
            cnt = jnp.zeros((tq, LANES), F32)
            for cb in range(n_static * n_cb):
                s = score_ref[:, cb * LANES:(cb + 1) * LANES]
                cnt = cnt + jnp.where(s >= thr, 1.0, 0.0)
            total = jnp.sum(cnt, axis=1, keepdims=True)
            return jnp.where(total >= k_f, cand, thr_key)

        thr_key = lax.fori_loop(0, 32, bit_step, jnp.full((tq, LANES), INT32_MIN, jnp.int32),
                                unroll=4)
        thr_ref[...] = key_to_f32(thr_key)

    for n_static in range(1, max_chunks + 1):
        pl.when(n_chunks == n_static)(functools.partial(select, n_static))

    def bias_chunk(c, carry):
        off = pl.multiple_of(c * tk, tk)
        thr = thr_ref[...]
        causal = k_iota + off <= t_pos
        for cb in range(n_cb):
            cols = pl.ds(off + cb * LANES, LANES)
            sel = score_ref[:, cols] >= thr
            bias_ref[:, cols] = jnp.where(
                sel, jnp.where(causal[:, cb * LANES:(cb + 1) * LANES], 0.0, NEG), NEG)
        return carry

    lax.fori_loop(0, n_chunks, bias_chunk, 0)

    def count_rows(pred):
        def body(c, cnt):
            off = pl.multiple_of(c * tk, tk)
            thr = thr_ref[...]
            for cb in range(n_cb):
                s = score_ref[:, pl.ds(off + cb * LANES, LANES)]
                cnt = cnt + jnp.where(pred(s, thr), 1.0, 0.0)
            return cnt
        cnt = lax.fori_loop(0, n_chunks, body, jnp.zeros((tq, LANES), F32))
        return jnp.sum(cnt, axis=1, keepdims=True)

    n_ge = count_rows(lambda s, thr: s >= thr)

    @pl.when(jnp.max(n_ge) > k_f)
    def _():
        n_take = k_f - count_rows(lambda s, thr: s > thr)
        before = jnp.where(
            lax.broadcasted_iota(jnp.int32, (tk, tk), 0) < lax.broadcasted_iota(jnp.int32, (tk, tk), 1),
            1.0, 0.0).astype(BF16)

        def tie_chunk(c, seen):
            off = pl.multiple_of(c * tk, tk)
            sc = score_ref[:, pl.ds(off, tk)]
            thr = jnp.concatenate([thr_ref[...]] * n_cb, axis=1)
            eq = jnp.where(sc == thr, 1.0, 0.0)
            rank = _dot(eq.astype(BF16), before) + seen
            keep = jnp.where(sc > thr, 1.0, jnp.where(rank < n_take, eq, 0.0))
            causal = k_iota + off <= t_pos
            bias_ref[:, pl.ds(off, tk)] = jnp.where(causal, jnp.where(keep > 0.0, 0.0, NEG), NEG)
            return seen + jnp.sum(eq, axis=1, keepdims=True)

        lax.fori_loop(0, n_chunks, tie_chunk, jnp.zeros((tq, 1), F32))

    rows = KV_GROUP * tq
    m_ref[...] = jnp.full(m_ref.shape, NEG, F32)
    acc_ref[...] = jnp.zeros(acc_ref.shape, F32)
    ones = jnp.ones((tk, LANES), BF16)

    def attn_chunk(c, carry):
        off = pl.multiple_of(c * tk, tk)
        b = bias_ref[:, pl.ds(off, tk)]
        for g in range(N_KV_HEADS):
            kc = k_ref[pl.ds(off, tk), g * HEAD_DIM:(g + 1) * HEAD_DIM]
            vc = v_ref[pl.ds(off, tk), g * HEAD_DIM:(g + 1) * HEAD_DIM]
            s = _dot_t(qg_ref[g], kc)
            s = (s.reshape(KV_GROUP, tq, tk) + b[None]).reshape(rows, tk)
            m_old = m_ref[g]
            m_new = jnp.maximum(m_old, jnp.max(s, axis=1, keepdims=True))
            alpha = jnp.exp2(m_old - m_new)
            p = jnp.exp2(s - jnp.concatenate([m_new] * n_cb, axis=1))
            pv = _dot(p.astype(BF16), jnp.concatenate([vc, ones], axis=1))
            acc_ref[g] = jnp.concatenate([alpha, alpha], axis=1) * acc_ref[g] + pv
            m_ref[g] = m_new
        return carry

    lax.fori_loop(0, n_chunks, attn_chunk, 0)

    for g in range(N_KV_HEADS):
        out = acc_ref[g, :, 0:HEAD_DIM] / acc_ref[g, :, HEAD_DIM:2 * HEAD_DIM]
        for r in range(KV_GROUP):
            hd = g * KV_GROUP + r
            o_ref[:, hd * HEAD_DIM:(hd + 1) * HEAD_DIM] = out[r * tq:(r + 1) * tq].astype(BF16)


def _dsa_attn(main, kk, wi, batch, seq, tq, tk, top_k):
    t = main.shape[0]
    nq = seq // tq
    rows = KV_GROUP * tq
    return pl.pallas_call(
        functools.partial(_dsa_attn_kernel, tq, tk, top_k),
        grid=(batch, nq),
        in_specs=[
            pl.BlockSpec((tq, Q_W), lambda b, i: (b * nq + i, 0)),
            pl.BlockSpec((tq, QI_W), lambda b, i: (b * nq + i, (Q_W + 2 * KV_W) // QI_W)),
            pl.BlockSpec((tq, LANES), lambda b, i: (b * nq + i, 0)),
            pl.BlockSpec((seq, KV_W), lambda b, i: (b, Q_W // KV_W)),
            pl.BlockSpec((seq, KV_W), lambda b, i: (b, Q_W // KV_W + 1)),
            pl.BlockSpec((seq, 2 * LANES), lambda b, i: (b, 0)),
        ],
        out_specs=pl.BlockSpec((tq, Q_W), lambda b, i: (b * nq + i, 0)),
        out_shape=jax.ShapeDtypeStruct((t, Q_W), BF16),
        scratch_shapes=[
            pltpu.VMEM((tq, seq), F32),
            pltpu.VMEM((tq, seq), F32),
            pltpu.VMEM((tq, LANES), F32),
            pltpu.VMEM((IDX_HEADS, tq, LANES), F32),
            pltpu.VMEM((N_KV_HEADS, rows, HEAD_DIM), BF16),
            pltpu.VMEM((N_KV_HEADS, rows, LANES), F32),
            pltpu.VMEM((N_KV_HEADS, rows, 2 * HEAD_DIM), F32),
        ],
        compiler_params=_cparams(("arbitrary", "arbitrary")),
        name="dsa_attn",
    )(main, main, wi, main, main, kk)


def _tiles(seq, d_ff):
    tm = min(1024, seq)
    tm_ffn = min(2048, seq)
    tf = 256 if d_ff % 256 == 0 else d_ff
    tq = min(256, seq)
    tk = min(256, seq)
    return tm, tm_ffn, tf, tq, tk


def kernel(x, c, positions, ada_w, ada_b, norm_g, final_g, ffn_wgu, ffn_wd,
           pool_w_in, pool_w_grp, pool_scale, pool_w_out, dsa_w_in, dsa_w_out):
    batch, seq, d = x.shape
    depth = ada_w.shape[0]
    d_ff = ffn_wd.shape[2]
    t = batch * seq
    tm, tm_ffn, tf, tq, tk = _tiles(seq, d_ff)
    tn = min(512, d)
    top_k = min(TOPK_MAX, seq // 4)
    assert seq % tm == 0 and seq % tm_ffn == 0 and seq % tq == 0 and seq % tk == 0 and tk >= top_k
    assert dsa_w_in.shape[2] == DSA_MAIN_W + DSA_SMALL_W and DSA_MAIN_W % tn == 0

    rows = -(-batch // SUBLANES) * SUBLANES
    c8 = jnp.pad(c, ((0, rows - batch), (0, 0)))
    ada_n = ada_w.shape[2]
    ada = _ada(c8, ada_w, ada_b, 1024 if ada_n % 1024 == 0 else d)
    ada4 = ada.reshape(depth, rows, 3 * N_SUBLAYERS, d)

    tabs = _rope_tables(positions, tm) if depth > 1 else None

    h = x.reshape(t, d)
    for layer in range(depth):
        j = layer // 2
        h = _ffn(h, ada4, norm_g, ffn_wgu, ffn_wd, layer, 0, 0, seq, tm_ffn, tf)
        g_row = norm_g[layer, 1][None, :]
        if layer % 2 == 0:
            u = _norm_linear(h, ada4, g_row, pool_w_in, j, layer, 1, seq, tm, tn)
            a = _pool_core(u, pool_w_grp, pool_scale, j, batch, seq)
            h = _linear_residual(a, pool_w_out, j, h, ada4, layer, 1, seq, tm, tn)
        else:
            w_small = _dsa_small_weights(dsa_w_in[j, :, DSA_MAIN_W:])
            main, kk, wi = _dsa_proj(h, ada4, g_row, dsa_w_in, j, w_small, tabs,
                                     layer, 1, seq, tm, tn)
            o = _dsa_attn(main, kk, wi, batch, seq, tq, tk, top_k)
            h = _linear_residual(o, dsa_w_out, j, h, ada4, layer, 1, seq, tm, tn)
        last = layer == depth - 1
        h = _ffn(h, ada4, norm_g, ffn_wgu, ffn_wd, layer, 1, 2, seq, tm_ffn, tf,
                 final_g=final_g if last else None)
    return h.reshape(batch, seq, d)
```

```python
import functools

import jax
import jax.numpy as jnp
import numpy as np
from jax import lax
from jax.experimental import pallas as pl
from jax.experimental.pallas import tpu as pltpu

F32 = jnp.float32
BF16 = jnp.bfloat16

N_SUBLAYERS = 3
POOL_WINDOWS = (2, 4, 8, 16)
N_HEADS = 16
N_KV_HEADS = 4
KV_GROUP = N_HEADS // N_KV_HEADS
HEAD_DIM = 128
ROT_DIM = HEAD_DIM // 4
ROPE_THETA = 500000.0
IDX_HEADS = 16
IDX_DIM = 64
IDX_ROT_DIM = IDX_DIM // 4
TOPK_MAX = 256
EPS = 1e-6
NEG = -1e30
Q_W = N_HEADS * HEAD_DIM
KV_W = N_KV_HEADS * HEAD_DIM
QI_W = IDX_HEADS * IDX_DIM
DSA_MAIN_W = Q_W + 2 * KV_W + QI_W
DSA_SMALL_W = IDX_DIM + IDX_HEADS
Q_PRESCALE = HEAD_DIM ** -0.5 * 1.4426950408889634

LANES = 128
SUBLANES = 8
VMEM_LIMIT_BYTES = 56 * 1024 * 1024
INT32_MIN = -(2 ** 31)


def _cparams(semantics):
    return pltpu.CompilerParams(dimension_semantics=semantics,
                                vmem_limit_bytes=VMEM_LIMIT_BYTES)


def _rmsnorm(x, g):
    ms = jnp.mean(x * x, axis=-1, keepdims=True)
    return x * lax.rsqrt(ms + EPS) * g


def _rmsnorm_rows(ref, rows, g_ref):
    ref[rows, :] = _rmsnorm(ref[rows, :], g_ref[...])


NORM_SLAB = 16
NORM_UNROLL = 4
FFN_ROW_SLAB = 128
FFN_SLAB_SLOTS = 4
FFN_DOWN_COLS = 256


def _for_row_slabs(n_rows, slab, body, unroll=1):
    def step(r, carry):
        body(pl.ds(pl.multiple_of(r * slab, slab), slab))
        return carry
    lax.fori_loop(0, n_rows // slab, step, 0, unroll=unroll)


def _write_norm_mod(x_ref, g_ref, m_ref, sub, y_ref, copy_ref=None):
    shift = m_ref[3 * sub + 0:3 * sub + 1, :]
    gs = g_ref[...] * (1.0 + m_ref[3 * sub + 1:3 * sub + 2, :])
    slab = min(NORM_SLAB, x_ref.shape[0])

    def norm_rows(rows):
        x = x_ref[rows, :]
        r = lax.rsqrt(jnp.mean(x * x, axis=-1, keepdims=True) + EPS)
        y_ref[rows, :] = ((x * r) * gs + shift).astype(BF16)
        if copy_ref is not None:
            copy_ref[rows, :] = x

    _for_row_slabs(x_ref.shape[0], slab, norm_rows, unroll=NORM_UNROLL)


def _silu(x):
    return x * jax.nn.sigmoid(x)


def _dot(a, b):
    return jnp.dot(a, b, preferred_element_type=F32)


def _dot_t(a, b):
    return lax.dot_general(a, b, (((1,), (1,)), ((), ())), preferred_element_type=F32)


def _ada_kernel(c_ref, w_ref, b_ref, o_ref):
    sc = _silu(c_ref[...]).astype(BF16)
    o_ref[...] = _dot(sc, w_ref[...].astype(BF16)) + b_ref[...]


def _ada(c8, ada_w, ada_b, tn):
    depth, d, n = ada_w.shape
    rows = c8.shape[0]
    return pl.pallas_call(
        _ada_kernel,
        grid=(depth, n // tn),
        in_specs=[
            pl.BlockSpec((rows, d), lambda l, j: (0, 0)),
            pl.BlockSpec((None, d, tn), lambda l, j: (l, 0, j)),
            pl.BlockSpec((None, 1, tn), lambda l, j: (l, 0, j)),
        ],
        out_specs=pl.BlockSpec((None, rows, tn), lambda l, j: (l, 0, j)),
        out_shape=jax.ShapeDtypeStruct((depth, rows, n), F32),
        compiler_params=_cparams(("arbitrary", "arbitrary")),
        name="ada",
    )(c8, ada_w, ada_b.reshape(depth, 1, n))


def _ffn_kernel(sub, final, rs, *refs):
    if final:
        (x_hbm, g_ref, m_ref, wg_ref, wu_ref, wd_ref, fg_ref, o_hbm,
         acc_ref, y_ref, xbuf, in_sem, out_sem) = refs
    else:
        (x_hbm, g_ref, m_ref, wg_ref, wu_ref, wd_ref, o_hbm,
         acc_ref, y_ref, xbuf, in_sem, out_sem) = refs
    k = pl.program_id(1)
    nk = pl.num_programs(1)
    tm, d_out = acc_ref.shape
    n_slabs = tm // rs
    n_slots = xbuf.shape[0]
    row0 = pl.program_id(0) * tm

    def x_copy(s):
        slot = s % n_slots
        return pltpu.make_async_copy(x_hbm.at[pl.ds(row0 + s * rs, rs), :], xbuf.at[slot],
                                     in_sem.at[slot])

    def o_copy(s):
        return pltpu.make_async_copy(acc_ref.at[pl.ds(s * rs, rs), :],
                                     o_hbm.at[pl.ds(row0 + s * rs, rs), :], out_sem.at[s])

    @pl.when(k == 0)
    def _():
        for s in range(min(n_slots, n_slabs)):
            x_copy(s).start()
        for s in range(n_slabs):
            x_copy(s).wait()
            pl.when(pl.program_id(0) > 0)(o_copy(s).wait)
            rows = pl.ds(s * rs, rs)
            _write_norm_mod(xbuf.at[s % n_slots], g_ref, m_ref, sub, y_ref.at[rows, :],
                            copy_ref=acc_ref.at[rows, :])
            if s + n_slots < n_slabs:
                x_copy(s + n_slots).start()

    y = y_ref[...]
    g = _dot(y, wg_ref[...].astype(BF16))
    u = _dot(y, wu_ref[...].astype(BF16))
    a = (_silu(g) * u).astype(BF16)
    half_gate = 0.5 * m_ref[3 * sub + 2:3 * sub + 3, :]
    nc = min(FFN_DOWN_COLS, d_out)
    for n in range(d_out // nc):
        cols = slice(n * nc, (n + 1) * nc)
        acc_ref[:, cols] += half_gate[:, cols] * _dot(a, wd_ref[:, cols].astype(BF16))

    @pl.when(k == nk - 1)
    def _():
        for s in range(n_slabs):
            if final:
                _for_row_slabs(
                    rs, min(NORM_SLAB, rs),
                    lambda rows, s=s: _rmsnorm_rows(acc_ref.at[pl.ds(s * rs, rs), :], rows, fg_ref),
                    unroll=NORM_UNROLL)
            o_copy(s).start()

        @pl.when(pl.program_id(0) == pl.num_programs(0) - 1)
        def _():
            for s in range(n_slabs):
                o_copy(s).wait()


def _ffn(h, ada4, norm_g, ffn_wgu, ffn_wd, layer, which, sub, seq, tm, tf, final_g=None):
    t, d = h.shape
    d_ff = ffn_wd.shape[2]
    nk = d_ff // tf
    tpb = seq // tm
    rs = min(FFN_ROW_SLAB, tm)
    final = final_g is not None
    in_specs = [
        pl.BlockSpec(memory_space=pl.ANY),
        pl.BlockSpec((1, d), lambda i, k: (0, 0)),
        pl.BlockSpec((None, None, 3 * N_SUBLAYERS, d), lambda i, k: (layer, i // tpb, 0, 0)),
        pl.BlockSpec((None, None, d, tf), lambda i, k: (layer, which, 0, k)),
        pl.BlockSpec((None, None, d, tf), lambda i, k: (layer, which, 0, nk + k)),
        pl.BlockSpec((None, None, tf, d), lambda i, k: (layer, which, k, 0)),
    ]
    args = [h, norm_g[layer, sub][None, :], ada4, ffn_wgu, ffn_wgu, ffn_wd]
    if final:
        in_specs.append(pl.BlockSpec((1, d), lambda i, k: (0, 0)))
        args.append(final_g[None, :])
    return pl.pallas_call(
        functools.partial(_ffn_kernel, sub, final, rs),
        grid=(t // tm, nk),
        in_specs=in_specs,
        out_specs=pl.BlockSpec(memory_space=pl.ANY),
        out_shape=jax.ShapeDtypeStruct((t, d), F32),
        scratch_shapes=[
            pltpu.VMEM((tm, d), F32),
            pltpu.VMEM((tm, d), BF16),
            pltpu.VMEM((FFN_SLAB_SLOTS, rs, d), F32),
            pltpu.SemaphoreType.DMA((FFN_SLAB_SLOTS,)),
            pltpu.SemaphoreType.DMA((tm // rs,)),
        ],
        compiler_params=_cparams(("arbitrary", "arbitrary")),
        name=f"ffn_l{layer}_{which}",
    )(*args)


def _keep_bf16_weight(w_ref, wbf_ref):
    j = pl.program_id(1)

    @pl.when(pl.program_id(0) == 0)
    def _():
        wbf_ref[j] = w_ref[...].astype(BF16)

    return wbf_ref[j]


def _resident_weight_spec(widx, kdim, tn, nj):
    return pl.BlockSpec((None, kdim, tn), lambda i, j: (widx, 0, jnp.where(i == 0, j, nj - 1)))


def _norm_linear_kernel(sub, x_ref, g_ref, m_ref, w_ref, o_ref, y_ref, wbf_ref):
    @pl.when(pl.program_id(1) == 0)
    def _():
        _write_norm_mod(x_ref, g_ref, m_ref, sub, y_ref)

    o_ref[...] = _dot(y_ref[...], _keep_bf16_weight(w_ref, wbf_ref))


def _norm_linear(h, ada4, g_row, w, widx, layer, sub, seq, tm, tn):
    t, d = h.shape
    n = w.shape[2]
    tpb = seq // tm
    return pl.pallas_call(
        functools.partial(_norm_linear_kernel, sub),
        grid=(t // tm, n // tn),
        in_specs=[
            pl.BlockSpec((tm, d), lambda i, j: (i, 0)),
            pl.BlockSpec((1, d), lambda i, j: (0, 0)),
            pl.BlockSpec((None, None, 3 * N_SUBLAYERS, d), lambda i, j: (layer, i // tpb, 0, 0)),
            _resident_weight_spec(widx, d, tn, n // tn),
        ],
        out_specs=pl.BlockSpec((tm, tn), lambda i, j: (i, j)),
        out_shape=jax.ShapeDtypeStruct((t, n), F32),
        scratch_shapes=[pltpu.VMEM((tm, d), BF16), pltpu.VMEM((n // tn, d, tn), BF16)],
        compiler_params=_cparams(("arbitrary", "arbitrary")),
        name=f"norm_linear_l{layer}",
    )(h, g_row, ada4, w)


def _pool_kernel(u_ref, w_ref, ls_ref, o_ref):
    g = pl.program_id(1)
    u = u_ref[...]
    t = lax.broadcasted_iota(jnp.int32, u.shape, 0)

    def shifted(v, d):
        return jnp.where(t >= d, pltpu.roll(v, d, 0), 0.0)

    for gi, win in enumerate(POOL_WINDOWS):
        @pl.when(g == gi)
        def _(win=win):
            acc = u
            d = 1
            while d < win:
                acc = acc + shifted(acc, d)
                d *= 2
            cnt = jnp.minimum(t + 1, win).astype(F32)
            p = (acc / cnt - u).astype(BF16)
            v = _dot(p, w_ref[...].astype(BF16)) * ls_ref[...]
            o_ref[...] = v.astype(BF16)


def _pool_core(u, w_grp, ls, widx, batch, seq):
    t, d = u.shape
    _, groups, c, _ = w_grp.shape
    return pl.pallas_call(
        _pool_kernel,
        grid=(batch, groups),
        in_specs=[
            pl.BlockSpec((seq, c), lambda b, g: (b, g)),
            pl.BlockSpec((None, None, c, c), lambda b, g: (widx, g, 0, 0)),
            pl.BlockSpec((None, 1, c), lambda b, g: (widx, 0, g)),
        ],
        out_specs=pl.BlockSpec((seq, c), lambda b, g: (b, g)),
        out_shape=jax.ShapeDtypeStruct((t, d), BF16),
        compiler_params=_cparams(("arbitrary", "arbitrary")),
        name="pool_core",
    )(u, w_grp, ls.reshape(ls.shape[0], 1, d))


def _linear_residual_kernel(sub, a_ref, w_ref, h_ref, m_ref, o_ref, wbf_ref):
    gate = m_ref[3 * sub + 2:3 * sub + 3, :]
    o_ref[...] = h_ref[...] + gate * _dot(a_ref[...], _keep_bf16_weight(w_ref, wbf_ref))


def _linear_residual(a, w, widx, h, ada4, layer, sub, seq, tm, tn):
    t, kdim = a.shape
    n = w.shape[2]
    tpb = seq // tm
    return pl.pallas_call(
        functools.partial(_linear_residual_kernel, sub),
        grid=(t // tm, n // tn),
        in_specs=[
            pl.BlockSpec((tm, kdim), lambda i, j: (i, 0)),
            _resident_weight_spec(widx, kdim, tn, n // tn),
            pl.BlockSpec((tm, tn), lambda i, j: (i, j)),
            pl.BlockSpec((None, None, 3 * N_SUBLAYERS, tn), lambda i, j: (layer, i // tpb, 0, j)),
        ],
        out_specs=pl.BlockSpec((tm, tn), lambda i, j: (i, j)),
        out_shape=jax.ShapeDtypeStruct((t, n), F32),
        scratch_shapes=[pltpu.VMEM((n // tn, kdim, tn), BF16)],
        compiler_params=_cparams(("arbitrary", "arbitrary")),
        name=f"linear_residual_l{layer}",
    )(a, w, h, ada4)


ROPE_SET_Q, ROPE_SET_K, ROPE_SET_NONE, ROPE_SET_IDX = 0, 1, 2, 3
HALF_LANES = LANES // 2


def _rope_table_kernel(pos_ref, invf_ref, o_ref):
    pos = pos_ref[...].astype(F32)
    lane = lax.broadcasted_iota(jnp.int32, (pos.shape[0], LANES), 1)
    first = lane < HALF_LANES

    def tables(invf, is_rot):
        ang = pos * invf
        c = jnp.where(is_rot, jnp.cos(ang), 1.0)
        s = jnp.where(is_rot, jnp.where(first, -jnp.sin(ang), jnp.sin(ang)), 0.0)
        return c, s

    cb, sb = tables(invf_ref[:, 0:LANES], (lane & (HALF_LANES - 1)) < ROT_DIM // 2)
    ci, si = tables(invf_ref[:, LANES:2 * LANES], (lane & (HALF_LANES // 2 - 1)) < IDX_ROT_DIM // 2)
    o_ref[2 * ROPE_SET_Q] = cb * Q_PRESCALE
    o_ref[2 * ROPE_SET_Q + 1] = sb * Q_PRESCALE
    o_ref[2 * ROPE_SET_K] = cb
    o_ref[2 * ROPE_SET_K + 1] = sb
    o_ref[2 * ROPE_SET_NONE] = jnp.ones_like(cb)
    o_ref[2 * ROPE_SET_NONE + 1] = jnp.zeros_like(cb)
    o_ref[2 * ROPE_SET_IDX] = ci
    o_ref[2 * ROPE_SET_IDX + 1] = si


def _rope_tables(positions, tm):
    t = positions.size
    lane = jnp.arange(LANES)

    def inv_freq(rot_dim):
        half = rot_dim // 2
        return ROPE_THETA ** (-jnp.arange(half, dtype=F32) / half)

    invf = jnp.concatenate([inv_freq(ROT_DIM)[lane % (ROT_DIM // 2)],
                            inv_freq(IDX_ROT_DIM)[lane % (IDX_ROT_DIM // 2)]])[None, :]
    return pl.pallas_call(
        _rope_table_kernel,
        grid=(t // tm,),
        in_specs=[
            pl.BlockSpec((tm, 1), lambda i: (i, 0)),
            pl.BlockSpec((1, 2 * LANES), lambda i: (0, 0)),
        ],
        out_specs=pl.BlockSpec((8, tm, LANES), lambda i: (0, i, 0)),
        out_shape=jax.ShapeDtypeStruct((8, t, LANES), F32),
        compiler_params=_cparams(("arbitrary",)),
        name="rope_tables",
    )(positions.reshape(t, 1), invf)


def _rope(x, c, s):
    return x * c + pltpu.roll(x, HALF_LANES, 1) * s


PERM_BIG, PERM_NONE, PERM_IDX = 0, 1, 2


def _lane_permutations():
    h, p = ROT_DIM // 2, (HEAD_DIM - ROT_DIM) // 2
    big = np.concatenate([np.arange(0, h), np.arange(ROT_DIM, ROT_DIM + p),
                          np.arange(h, ROT_DIM), np.arange(ROT_DIM + p, HEAD_DIM)])
    lo, hi = _idx_lo_hi_dims()
    idx = np.concatenate([lo, IDX_DIM + lo, hi, IDX_DIM + hi])
    mats = np.zeros((3, LANES, LANES), np.float32)
    for kind, src in ((PERM_BIG, big), (PERM_NONE, np.arange(LANES)), (PERM_IDX, idx)):
        mats[kind, src, np.arange(LANES)] = 1.0
    return jnp.asarray(mats, BF16)


def _idx_lo_hi_dims():
    h, p = IDX_ROT_DIM // 2, (IDX_DIM - IDX_ROT_DIM) // 2
    lo = np.concatenate([np.arange(0, h), np.arange(IDX_ROT_DIM, IDX_ROT_DIM + p)])
    hi = np.concatenate([np.arange(h, IDX_ROT_DIM), np.arange(IDX_ROT_DIM + p, IDX_DIM)])
    return lo, hi


def _dsa_small_weights(w_tail):
    h, p = IDX_ROT_DIM // 2, (IDX_DIM - IDX_ROT_DIM) // 2
    klo = jnp.concatenate([w_tail[:, 0:h], w_tail[:, IDX_ROT_DIM:IDX_ROT_DIM + p]], axis=1)
    khi = jnp.concatenate([w_tail[:, h:IDX_ROT_DIM], w_tail[:, IDX_ROT_DIM + p:IDX_DIM]], axis=1)
    z = jnp.zeros_like(klo)
    wi = jnp.pad(w_tail[:, IDX_DIM:], ((0, 0), (0, LANES - IDX_HEADS)))
    return jnp.concatenate([klo, z, khi, z, z, klo, z, khi, wi], axis=1).astype(BF16)


def _dsa_proj_kernel(sub, tn, x_ref, g_ref, m_ref, w_ref, ws_ref, perm_ref, tab_ref,
                     o_ref, ok_ref, ow_ref, y_ref, wbf_ref):
    j = pl.program_id(1)
    first_k, first_v, first_qi = Q_W // tn, (Q_W + KV_W) // tn, (Q_W + 2 * KV_W) // tn

    @pl.when(j == 0)
    def _():
        _write_norm_mod(x_ref, g_ref, m_ref, sub, y_ref)
        small = _dot(y_ref[...], ws_ref[...])
        for s in range(2):
            ok_ref[:, s * LANES:(s + 1) * LANES] = _rope(
                small[:, s * LANES:(s + 1) * LANES], tab_ref[2 * ROPE_SET_IDX],
                tab_ref[2 * ROPE_SET_IDX + 1]).astype(BF16)
        ow_ref[...] = small[:, 2 * LANES:3 * LANES]

    @pl.when(pl.program_id(0) == 0)
    def _():
        kind = jnp.where(j < first_v, PERM_BIG, jnp.where(j < first_qi, PERM_NONE, PERM_IDX))
        perm = perm_ref[kind]
        for sl in range(tn // LANES):
            cols = slice(sl * LANES, (sl + 1) * LANES)
            wbf_ref[j, :, cols] = _dot(w_ref[:, cols].astype(BF16), perm).astype(BF16)

    acc = _dot(y_ref[...], wbf_ref[j])
    rope_set = jnp.where(j < first_k, ROPE_SET_Q,
                         jnp.where(j < first_v, ROPE_SET_K,
                                   jnp.where(j < first_qi, ROPE_SET_NONE, ROPE_SET_IDX)))
    c = tab_ref[2 * rope_set]
    s = tab_ref[2 * rope_set + 1]
    for sl in range(tn // LANES):
        cols = slice(sl * LANES, (sl + 1) * LANES)
        o_ref[:, cols] = _rope(acc[:, cols], c, s).astype(BF16)


def _dsa_proj(h, ada4, g_row, w_in, widx, w_small, tabs, layer, sub, seq, tm, tn):
    t, d = h.shape
    tpb = seq // tm
    return pl.pallas_call(
        functools.partial(_dsa_proj_kernel, sub, tn),
        grid=(t // tm, DSA_MAIN_W // tn),
        in_specs=[
            pl.BlockSpec((tm, d), lambda i, j: (i, 0), pipeline_mode=pl.Buffered(1)),
            pl.BlockSpec((1, d), lambda i, j: (0, 0)),
            pl.BlockSpec((None, None, 3 * N_SUBLAYERS, d), lambda i, j: (layer, i // tpb, 0, 0)),
            _resident_weight_spec(widx, d, tn, DSA_MAIN_W // tn),
            pl.BlockSpec((d, 3 * LANES), lambda i, j: (0, 0)),
            pl.BlockSpec((3, LANES, LANES), lambda i, j: (0, 0, 0)),
            pl.BlockSpec((8, tm, LANES), lambda i, j: (0, i, 0)),
        ],
        out_specs=[
            pl.BlockSpec((tm, tn), lambda i, j: (i, j)),
            pl.BlockSpec((tm, 2 * LANES), lambda i, j: (i, 0)),
            pl.BlockSpec((tm, LANES), lambda i, j: (i, 0)),
        ],
        out_shape=[
            jax.ShapeDtypeStruct((t, DSA_MAIN_W), BF16),
            jax.ShapeDtypeStruct((t, 2 * LANES), BF16),
            jax.ShapeDtypeStruct((t, LANES), F32),
        ],
        scratch_shapes=[pltpu.VMEM((tm, d), BF16),
                        pltpu.VMEM((DSA_MAIN_W // tn, d, tn), BF16)],
        compiler_params=_cparams(("arbitrary", "arbitrary")),
        name=f"dsa_proj_l{layer}",
    )(h, g_row, ada4, w_in, w_small, _lane_permutations(), tabs)


def _dsa_attn_kernel(tq, tk, top_k, q_ref, qi_ref, wi_ref, k_ref, v_ref, kk_ref, o_ref,
                     score_ref, bias_ref, thr_ref, wib_ref, qg_ref, m_ref, acc_ref):
    i = pl.program_id(1)
    n_chunks = ((i + 1) * tq + tk - 1) // tk
    max_chunks = score_ref.shape[1] // tk
    t_pos = i * tq + lax.broadcasted_iota(jnp.int32, (tq, tk), 0)
    k_iota = lax.broadcasted_iota(jnp.int32, (tq, tk), 1)
    n_cb = tk // LANES

    for hh in range(IDX_HEADS):
        wib_ref[hh] = jnp.broadcast_to(wi_ref[:, hh:hh + 1], (tq, LANES))
    for g in range(N_KV_HEADS):
        for r in range(KV_GROUP):
            hd = g * KV_GROUP + r
            qg_ref[g, r * tq:(r + 1) * tq, :] = q_ref[:, hd * HEAD_DIM:(hd + 1) * HEAD_DIM]

    def score_chunk(c, carry):
        off = pl.multiple_of(c * tk, tk)
        k_even = kk_ref[pl.ds(off, tk), 0:LANES]
        k_odd = kk_ref[pl.ds(off, tk), LANES:2 * LANES]
        acc = [jnp.zeros((tq, LANES), F32) for _ in range(n_cb)]
        for pair in range(IDX_HEADS // 2):
            qp = qi_ref[:, pair * LANES:(pair + 1) * LANES]
            for par, kop in enumerate((k_even, k_odd)):
                r = _dot_t(qp, kop)
                w = wib_ref[2 * pair + par]
                for cb in range(n_cb):
                    acc[cb] = acc[cb] + w * jnp.maximum(r[:, cb * LANES:(cb + 1) * LANES], 0.0)
        sc = jnp.concatenate(acc, axis=1)
        score_ref[:, pl.ds(off, tk)] = jnp.where(k_iota + off <= t_pos, sc, NEG)
        return carry

    lax.fori_loop(0, n_chunks, score_chunk, 0)

    def key_to_f32(key):
        return pltpu.bitcast(key ^ ((key >> 31) & 0x7FFFFFFF), F32)

    k_f = float(top_k)

    def select(n_static):
        def bit_step(n, thr_key):
            cand = thr_key + lax.shift_left(jnp.int32(1), 31 - n)
            thr = key_to_f32(cand)
            cnt = jnp.zeros((tq, LANES), F32)
            for cb in range(n_static * n_cb):
                s = score_ref[:, cb * LANES:(cb + 1) * LANES]
                cnt = cnt + jnp.where(s >= thr, 1.0, 0.0)
            total = jnp.sum(cnt, axis=1, keepdims=True)
            return jnp.where(total >= k_f, cand, thr_key)

        thr_key = lax.fori_loop(0, 32, bit_step, jnp.full((tq, LANES), INT32_MIN, jnp.int32),
                                unroll=4)
        thr_ref[...] = key_to_f32(thr_key)

    for n_static in range(1, max_chunks + 1):
        pl.when(n_chunks == n_static)(functools.partial(select, n_static))

    def bias_chunk(c, carry):
        off = pl.multiple_of(c * tk, tk)
        thr = thr_ref[...]
        causal = k_iota + off <= t_pos
        for cb in range(n_cb):
            cols = pl.ds(off + cb * LANES, LANES)
            sel = score_ref[:, cols] >= thr
            bias_ref[:, cols] = jnp.where(
                sel, jnp.where(causal[:, cb * LANES:(cb + 1) * LANES], 0.0, NEG), NEG)
        return carry

    lax.fori_loop(0, n_chunks, bias_chunk, 0)

    def count_rows(pred):
        def body(c, cnt):
            off = pl.multiple_of(c * tk, tk)
            thr = thr_ref[...]
            for cb in range(n_cb):
                s = score_ref[:, pl.ds(off + cb * LANES, LANES)]
                cnt = cnt + jnp.where(pred(s, thr), 1.0, 0.0)
            return cnt
        cnt = lax.fori_loop(0, n_chunks, body, jnp.zeros((tq, LANES), F32))
        return jnp.sum(cnt, axis=1, keepdims=True)

    n_ge = count_rows(lambda s, thr: s >= thr)

    @pl.when(jnp.max(n_ge) > k_f)
    def _():
        n_take = k_f - count_rows(lambda s, thr: s > thr)
        before = jnp.where(
            lax.broadcasted_iota(jnp.int32, (tk, tk), 0) < lax.broadcasted_iota(jnp.int32, (tk, tk), 1),
            1.0, 0.0).astype(BF16)

        def tie_chunk(c, seen):
            off = pl.multiple_of(c * tk, tk)
            sc = score_ref[:, pl.ds(off, tk)]
            thr = jnp.concatenate([thr_ref[...]] * n_cb, axis=1)
            eq = jnp.where(sc == thr, 1.0, 0.0)
            rank = _dot(eq.astype(BF16), before) + seen
            keep = jnp.where(sc > thr, 1.0, jnp.where(rank < n_take, eq, 0.0))
            causal = k_iota + off <= t_pos
            bias_ref[:, pl.ds(off, tk)] = jnp.where(causal, jnp.where(keep > 0.0, 0.0, NEG), NEG)
            return seen + jnp.sum(eq, axis=1, keepdims=True)

        lax.fori_loop(0, n_chunks, tie_chunk, jnp.zeros((tq, 1), F32))

    rows = KV_GROUP * tq
    m_ref[...] = jnp.full(m_ref.shape, NEG, F32)
    acc_ref[...] = jnp.zeros(acc_ref.shape, F32)
    ones = jnp.ones((tk, LANES), BF16)

    def attn_chunk(c, carry):
        off = pl.multiple_of(c * tk, tk)
        b = bias_ref[:, pl.ds(off, tk)]
        for g in range(N_KV_HEADS):
            kc = k_ref[pl.ds(off, tk), g * HEAD_DIM:(g + 1) * HEAD_DIM]
            vc = v_ref[pl.ds(off, tk), g * HEAD_DIM:(g + 1) * HEAD_DIM]
            s = _dot_t(qg_ref[g], kc)
            s = (s.reshape(KV_GROUP, tq, tk) + b[None]).reshape(rows, tk)
            m_old = m_ref[g]
            m_new = jnp.maximum(m_old, jnp.max(s, axis=1, keepdims=True))
            alpha = jnp.exp2(m_old - m_new)
            p = jnp.exp2(s - jnp.concatenate([m_new] * n_cb, axis=1))
            pv = _dot(p.astype(BF16), jnp.concatenate([vc, ones], axis=1))
            acc_ref[g] = jnp.concatenate([alpha, alpha], axis=1) * acc_ref[g] + pv
            m_ref[g] = m_new
        return carry

    lax.fori_loop(0, n_chunks, attn_chunk, 0)

    for g in range(N_KV_HEADS):
        out = acc_ref[g, :, 0:HEAD_DIM] / acc_ref[g, :, HEAD_DIM:2 * HEAD_DIM]
        for r in range(KV_GROUP):
            hd = g * KV_GROUP + r
            o_ref[:, hd * HEAD_DIM:(hd + 1) * HEAD_DIM] = out[r * tq:(r + 1) * tq].astype(BF16)


def _dsa_attn(main, kk, wi, batch, seq, tq, tk, top_k):
    t = main.shape[0]
    nq = seq // tq
    rows = KV_GROUP * tq
    return pl.pallas_call(
        functools.partial(_dsa_attn_kernel, tq, tk, top_k),
        grid=(batch, nq),
        in_specs=[
            pl.BlockSpec((tq, Q_W), lambda b, i: (b * nq + i, 0)),
            pl.BlockSpec((tq, QI_W), lambda b, i: (b * nq + i, (Q_W + 2 * KV_W) // QI_W)),
            pl.BlockSpec((tq, LANES), lambda b, i: (b * nq + i, 0)),
            pl.BlockSpec((seq, KV_W), lambda b, i: (b, Q_W // KV_W)),
            pl.BlockSpec((seq, KV_W), lambda b, i: (b, Q_W // KV_W + 1)),
            pl.BlockSpec((seq, 2 * LANES), lambda b, i: (b, 0)),
        ],
        out_specs=pl.BlockSpec((tq, Q_W), lambda b, i: (b * nq + i, 0)),
        out_shape=jax.ShapeDtypeStruct((t, Q_W), BF16),
        scratch_shapes=[
            pltpu.VMEM((tq, seq), F32),
            pltpu.VMEM((tq, seq), F32),
            pltpu.VMEM((tq, LANES), F32),
            pltpu.VMEM((IDX_HEADS, tq, LANES), F32),
            pltpu.VMEM((N_KV_HEADS, rows, HEAD_DIM), BF16),
            pltpu.VMEM((N_KV_HEADS, rows, LANES), F32),
            pltpu.VMEM((N_KV_HEADS, rows, 2 * HEAD_DIM), F32),
        ],
        compiler_params=_cparams(("arbitrary", "arbitrary")),
        name="dsa_attn",
    )(main, main, wi, main, main, kk)


def _tiles(seq, d_ff):
    tm = min(1024, seq)
    tm_ffn = min(2048, seq)
    tf = 256 if d_ff % 256 == 0 else d_ff
    tq = min(256, seq)
    tk = min(256, seq)
    return tm, tm_ffn, tf, tq, tk


def kernel(x, c, positions, ada_w, ada_b, norm_g, final_g, ffn_wgu, ffn_wd,
           pool_w_in, pool_w_grp, pool_scale, pool_w_out, dsa_w_in, dsa_w_out):
    batch, seq, d = x.shape
    depth = ada_w.shape[0]
    d_ff = ffn_wd.shape[2]
    t = batch * seq
    tm, tm_ffn, tf, tq, tk = _tiles(seq, d_ff)
    tn = min(512, d)
    top_k = min(TOPK_MAX, seq // 4)
    assert seq % tm == 0 and seq % tm_ffn == 0 and seq % tq == 0 and seq % tk == 0 and tk >= top_k
    assert dsa_w_in.shape[2] == DSA_MAIN_W + DSA_SMALL_W and DSA_MAIN_W % tn == 0

    rows = -(-batch // SUBLANES) * SUBLANES
    c8 = jnp.pad(c, ((0, rows - batch), (0, 0)))
    ada_n = ada_w.shape[2]
    ada = _ada(c8, ada_w, ada_b, 1024 if ada_n % 1024 == 0 else d)
    ada4 = ada.reshape(depth, rows, 3 * N_SUBLAYERS, d)

    tabs = _rope_tables(positions, tm) if depth > 1 else None

    h = x.reshape(t, d)
    for layer in range(depth):
        j = layer // 2
        h = _ffn(h, ada4, norm_g, ffn_wgu, ffn_wd, layer, 0, 0, seq, tm_ffn, tf)
        g_row = norm_g[layer, 1][None, :]
        if layer % 2 == 0:
            u = _norm_linear(h, ada4, g_row, pool_w_in, j, layer, 1, seq, tm, tn)
            a = _pool_core(u, pool_w_grp, pool_scale, j, batch, seq)
            h = _linear_residual(a, pool_w_out, j, h, ada4, layer, 1, seq, tm, tn)
        else:
            w_small = _dsa_small_weights(dsa_w_in[j, :, DSA_MAIN_W:])
            main, kk, wi = _dsa_proj(h, ada4, g_row, dsa_w_in, j, w_small, tabs,
                                     layer, 1, seq, tm, tn)
            o = _dsa_attn(main, kk, wi, batch, seq, tq, tk, top_k)
            h = _linear_residual(o, dsa_w_out, j, h, ada4, layer, 1, seq, tm, tn)
        last = layer == depth - 1
        h = _ffn(h, ada4, norm_g, ffn_wgu, ffn_wd, layer, 1, 2, seq, tm_ffn, tf,
                 final_g=final_g if last else None)
    return h.reshape(batch, seq, d)
```

```python
import functools

import jax
import jax.numpy as jnp
import numpy as np
from jax import lax
from jax.experimental import pallas as pl
from jax.experimental.pallas import tpu as pltpu

F32 = jnp.float32
BF16 = jnp.bfloat16

N_SUBLAYERS = 3
POOL_WINDOWS = (2, 4, 8, 16)
N_HEADS = 16
N_KV_HEADS = 4
KV_GROUP = N_HEADS // N_KV_HEADS
HEAD_DIM = 128
ROT_DIM = HEAD_DIM // 4
ROPE_THETA = 500000.0
IDX_HEADS = 16
IDX_DIM = 64
IDX_ROT_DIM = IDX_DIM // 4
TOPK_MAX = 256
EPS = 1e-6
NEG = -1e30
Q_W = N_HEADS * HEAD_DIM
KV_W = N_KV_HEADS * HEAD_DIM
QI_W = IDX_HEADS * IDX_DIM
DSA_MAIN_W = Q_W + 2 * KV_W + QI_W
DSA_SMALL_W = IDX_DIM + IDX_HEADS
Q_PRESCALE = HEAD_DIM ** -0.5 * 1.4426950408889634

LANES = 128
SUBLANES = 8
VMEM_LIMIT_BYTES = 56 * 1024 * 1024
INT32_MIN = -(2 ** 31)


def _cparams(semantics):
    return pltpu.CompilerParams(dimension_semantics=semantics,
                                vmem_limit_bytes=VMEM_LIMIT_BYTES)


def _rmsnorm(x, g):
    ms = jnp.mean(x * x, axis=-1, keepdims=True)
    return x * lax.rsqrt(ms + EPS) * g


def _rmsnorm_rows(ref, rows, g_ref):
    ref[rows, :] = _rmsnorm(ref[rows, :], g_ref[...])


NORM_SLAB = 16
NORM_UNROLL = 4
FFN_ROW_SLAB = 128
FFN_SLAB_SLOTS = 4
FFN_DOWN_COLS = 256


def _for_row_slabs(n_rows, slab, body, unroll=1):
    def step(r, carry):
        body(pl.ds(pl.multiple_of(r * slab, slab), slab))
        return carry
    lax.fori_loop(0, n_rows // slab, step, 0, unroll=unroll)


def _write_norm_mod(x_ref, g_ref, m_ref, sub, y_ref, copy_ref=None):
    shift = m_ref[3 * sub + 0:3 * sub + 1, :]
    gs = g_ref[...] * (1.0 + m_ref[3 * sub + 1:3 * sub + 2, :])
    slab = min(NORM_SLAB, x_ref.shape[0])

    def norm_rows(rows):
        x = x_ref[rows, :]
        r = lax.rsqrt(jnp.mean(x * x, axis=-1, keepdims=True) + EPS)
        y_ref[rows, :] = ((x * r) * gs + shift).astype(BF16)
        if copy_ref is not None:
            copy_ref[rows, :] = x

    _for_row_slabs(x_ref.shape[0], slab, norm_rows, unroll=NORM_UNROLL)


def _silu(x):
    return x * jax.nn.sigmoid(x)


def _dot(a, b):
    return jnp.dot(a, b, preferred_element_type=F32)


def _dot_t(a, b):
    return lax.dot_general(a, b, (((1,), (1,)), ((), ())), preferred_element_type=F32)


def _ada_kernel(c_ref, w_ref, b_ref, o_ref):
    sc = _silu(c_ref[...]).astype(BF16)
    o_ref[...] = _dot(sc, w_ref[...].astype(BF16)) + b_ref[...]


def _ada(c8, ada_w, ada_b, tn):
    depth, d, n = ada_w.shape
    rows = c8.shape[0]
    return pl.pallas_call(
        _ada_kernel,
        grid=(depth, n // tn),
        in_specs=[
            pl.BlockSpec((rows, d), lambda l, j: (0, 0)),
            pl.BlockSpec((None, d, tn), lambda l, j: (l, 0, j)),
            pl.BlockSpec((None, 1, tn), lambda l, j: (l, 0, j)),
        ],
        out_specs=pl.BlockSpec((None, rows, tn), lambda l, j: (l, 0, j)),
        out_shape=jax.ShapeDtypeStruct((depth, rows, n), F32),
        compiler_params=_cparams(("arbitrary", "arbitrary")),
        name="ada",
    )(c8, ada_w, ada_b.reshape(depth, 1, n))


def _ffn_kernel(sub, final, rs, *refs):
    if final:
        (x_hbm, g_ref, m_ref, wg_ref, wu_ref, wd_ref, fg_ref, o_hbm,
         acc_ref, y_ref, xbuf, in_sem, out_sem) = refs
    else:
        (x_hbm, g_ref, m_ref, wg_ref, wu_ref, wd_ref, o_hbm,
         acc_ref, y_ref, xbuf, in_sem, out_sem) = refs
    k = pl.program_id(1)
    nk = pl.num_programs(1)
    tm, d_out = acc_ref.shape
    n_slabs = tm // rs
    n_slots = xbuf.shape[0]
    n_first = min(n_slots, n_slabs)
    row0 = pl.program_id(0) * tm

    def x_copy(s, tile_row0=row0):
        slot = s % n_slots
        return pltpu.make_async_copy(x_hbm.at[pl.ds(tile_row0 + s * rs, rs), :], xbuf.at[slot],
                                     in_sem.at[slot])

    def o_copy(s):
        return pltpu.make_async_copy(acc_ref.at[pl.ds(s * rs, rs), :],
                                     o_hbm.at[pl.ds(row0 + s * rs, rs), :], out_sem.at[s])

    @pl.when(k == 0)
    def _():
        @pl.when(pl.program_id(0) == 0)
        def _():
            for s in range(n_first):
                x_copy(s).start()

        for s in range(n_slabs):
            x_copy(s).wait()
            pl.when(pl.program_id(0) > 0)(o_copy(s).wait)
            rows = pl.ds(s * rs, rs)
            _write_norm_mod(xbuf.at[s % n_slots], g_ref, m_ref, sub, y_ref.at[rows, :],
                            copy_ref=acc_ref.at[rows, :])
            if s + n_slots < n_slabs:
                x_copy(s + n_slots).start()

    @pl.when(jnp.logical_and(k == nk - 1, pl.program_id(0) + 1 < pl.num_programs(0)))
    def _():
        for s in range(n_first):
            x_copy(s, row0 + tm).start()

    y = y_ref[...]
    g = _dot(y, wg_ref[...].astype(BF16))
    u = _dot(y, wu_ref[...].astype(BF16))
    a = (_silu(g) * u).astype(BF16)
    half_gate = 0.5 * m_ref[3 * sub + 2:3 * sub + 3, :]
    nc = min(FFN_DOWN_COLS, d_out)
    for n in range(d_out // nc):
        cols = slice(n * nc, (n + 1) * nc)
        acc_ref[:, cols] += half_gate[:, cols] * _dot(a, wd_ref[:, cols].astype(BF16))

    @pl.when(k == nk - 1)
    def _():
        for s in range(n_slabs):
            if final:
                _for_row_slabs(
                    rs, min(NORM_SLAB, rs),
                    lambda rows, s=s: _rmsnorm_rows(acc_ref.at[pl.ds(s * rs, rs), :], rows, fg_ref),
                    unroll=2 * NORM_UNROLL)
            o_copy(s).start()

        @pl.when(pl.program_id(0) == pl.num_programs(0) - 1)
        def _():
            for s in range(n_slabs):
                o_copy(s).wait()


def _ffn(h, ada4, norm_g, ffn_wgu, ffn_wd, layer, which, sub, seq, tm, tf, final_g=None):
    t, d = h.shape
    d_ff = ffn_wd.shape[2]
    nk = d_ff // tf
    tpb = seq // tm
    rs = min(FFN_ROW_SLAB, tm)
    final = final_g is not None
    in_specs = [
        pl.BlockSpec(memory_space=pl.ANY),
        pl.BlockSpec((1, d), lambda i, k: (0, 0)),
        pl.BlockSpec((None, None, 3 * N_SUBLAYERS, d), lambda i, k: (layer, i // tpb, 0, 0)),
        pl.BlockSpec((None, None, d, tf), lambda i, k: (layer, which, 0, k)),
        pl.BlockSpec((None, None, d, tf), lambda i, k: (layer, which, 0, nk + k)),
        pl.BlockSpec((None, None, tf, d), lambda i, k: (layer, which, k, 0)),
    ]
    args = [h, norm_g[layer, sub][None, :], ada4, ffn_wgu, ffn_wgu, ffn_wd]
    if final:
        in_specs.append(pl.BlockSpec((1, d), lambda i, k: (0, 0)))
        args.append(final_g[None, :])
    return pl.pallas_call(
        functools.partial(_ffn_kernel, sub, final, rs),
        grid=(t // tm, nk),
        in_specs=in_specs,
        out_specs=pl.BlockSpec(memory_space=pl.ANY),
        out_shape=jax.ShapeDtypeStruct((t, d), F32),
        scratch_shapes=[
            pltpu.VMEM((tm, d), F32),
            pltpu.VMEM((tm, d), BF16),
            pltpu.VMEM((FFN_SLAB_SLOTS, rs, d), F32),
            pltpu.SemaphoreType.DMA((FFN_SLAB_SLOTS,)),
            pltpu.SemaphoreType.DMA((tm // rs,)),
        ],
        compiler_params=_cparams(("arbitrary", "arbitrary")),
        name=f"ffn_l{layer}_{which}",
    )(*args)


def _keep_bf16_weight(w_ref, wbf_ref):
    j = pl.program_id(1)

    @pl.when(pl.program_id(0) == 0)
    def _():
        wbf_ref[j] = w_ref[...].astype(BF16)

    return wbf_ref[j]


def _resident_weight_spec(widx, kdim, tn, nj):
    return pl.BlockSpec((None, kdim, tn), lambda i, j: (widx, 0, jnp.where(i == 0, j, nj - 1)))


def _norm_linear_kernel(sub, x_ref, g_ref, m_ref, w_ref, o_ref, y_ref, wbf_ref):
    @pl.when(pl.program_id(1) == 0)
    def _():
        _write_norm_mod(x_ref, g_ref, m_ref, sub, y_ref)

    o_ref[...] = _dot(y_ref[...], _keep_bf16_weight(w_ref, wbf_ref))


def _norm_linear(h, ada4, g_row, w, widx, layer, sub, seq, tm, tn):
    t, d = h.shape
    n = w.shape[2]
    tpb = seq // tm
    return pl.pallas_call(
        functools.partial(_norm_linear_kernel, sub),
        grid=(t // tm, n // tn),
        in_specs=[
            pl.BlockSpec((tm, d), lambda i, j: (i, 0)),
            pl.BlockSpec((1, d), lambda i, j: (0, 0)),
            pl.BlockSpec((None, None, 3 * N_SUBLAYERS, d), lambda i, j: (layer, i // tpb, 0, 0)),
            _resident_weight_spec(widx, d, tn, n // tn),
        ],
        out_specs=pl.BlockSpec((tm, tn), lambda i, j: (i, j)),
        out_shape=jax.ShapeDtypeStruct((t, n), F32),
        scratch_shapes=[pltpu.VMEM((tm, d), BF16), pltpu.VMEM((n // tn, d, tn), BF16)],
        compiler_params=_cparams(("arbitrary", "arbitrary")),
        name=f"norm_linear_l{layer}",
    )(h, g_row, ada4, w)


def _pool_kernel(u_ref, w_ref, ls_ref, o_ref):
    g = pl.program_id(1)
    u = u_ref[...]
    t = lax.broadcasted_iota(jnp.int32, u.shape, 0)

    def shifted(v, d):
        return jnp.where(t >= d, pltpu.roll(v, d, 0), 0.0)

    for gi, win in enumerate(POOL_WINDOWS):
        @pl.when(g == gi)
        def _(win=win):
            acc = u
            d = 1
            while d < win:
                acc = acc + shifted(acc, d)
                d *= 2
            cnt = jnp.minimum(t + 1, win).astype(F32)
            p = (acc / cnt - u).astype(BF16)
            v = _dot(p, w_ref[...].astype(BF16)) * ls_ref[...]
            o_ref[...] = v.astype(BF16)


def _pool_core(u, w_grp, ls, widx, batch, seq):
    t, d = u.shape
    _, groups, c, _ = w_grp.shape
    return pl.pallas_call(
        _pool_kernel,
        grid=(batch, groups),
        in_specs=[
            pl.BlockSpec((seq, c), lambda b, g: (b, g)),
            pl.BlockSpec((None, None, c, c), lambda b, g: (widx, g, 0, 0)),
            pl.BlockSpec((None, 1, c), lambda b, g: (widx, 0, g)),
        ],
        out_specs=pl.BlockSpec((seq, c), lambda b, g: (b, g)),
        out_shape=jax.ShapeDtypeStruct((t, d), BF16),
        compiler_params=_cparams(("arbitrary", "arbitrary")),
        name="pool_core",
    )(u, w_grp, ls.reshape(ls.shape[0], 1, d))


def _linear_residual_kernel(sub, a_ref, w_ref, h_ref, m_ref, o_ref, wbf_ref):
    gate = m_ref[3 * sub + 2:3 * sub + 3, :]
    o_ref[...] = h_ref[...] + gate * _dot(a_ref[...], _keep_bf16_weight(w_ref, wbf_ref))


def _linear_residual(a, w, widx, h, ada4, layer, sub, seq, tm, tn):
    t, kdim = a.shape
    n = w.shape[2]
    tpb = seq // tm
    return pl.pallas_call(
        functools.partial(_linear_residual_kernel, sub),
        grid=(t // tm, n // tn),
        in_specs=[
            pl.BlockSpec((tm, kdim), lambda i, j: (i, 0)),
            _resident_weight_spec(widx, kdim, tn, n // tn),
            pl.BlockSpec((tm, tn), lambda i, j: (i, j)),
            pl.BlockSpec((None, None, 3 * N_SUBLAYERS, tn), lambda i, j: (layer, i // tpb, 0, j)),
        ],
        out_specs=pl.BlockSpec((tm, tn), lambda i, j: (i, j)),
        out_shape=jax.ShapeDtypeStruct((t, n), F32),
        scratch_shapes=[pltpu.VMEM((n // tn, kdim, tn), BF16)],
        compiler_params=_cparams(("arbitrary", "arbitrary")),
        name=f"linear_residual_l{layer}",
    )(a, w, h, ada4)


ROPE_SET_Q, ROPE_SET_K, ROPE_SET_NONE, ROPE_SET_IDX = 0, 1, 2, 3
HALF_LANES = LANES // 2


def _rope_table_kernel(pos_ref, invf_ref, o_ref):
    pos = pos_ref[...].astype(F32)
    lane = lax.broadcasted_iota(jnp.int32, (pos.shape[0], LANES), 1)
    first = lane < HALF_LANES

    def tables(invf, is_rot):
        ang = pos * invf
        c = jnp.where(is_rot, jnp.cos(ang), 1.0)
        s = jnp.where(is_rot, jnp.where(first, -jnp.sin(ang), jnp.sin(ang)), 0.0)
        return c, s

    cb, sb = tables(invf_ref[:, 0:LANES], (lane & (HALF_LANES - 1)) < ROT_DIM // 2)
    ci, si = tables(invf_ref[:, LANES:2 * LANES], (lane & (HALF_LANES // 2 - 1)) < IDX_ROT_DIM // 2)
    o_ref[2 * ROPE_SET_Q] = cb * Q_PRESCALE
    o_ref[2 * ROPE_SET_Q + 1] = sb * Q_PRESCALE
    o_ref[2 * ROPE_SET_K] = cb
    o_ref[2 * ROPE_SET_K + 1] = sb
    o_ref[2 * ROPE_SET_NONE] = jnp.ones_like(cb)
    o_ref[2 * ROPE_SET_NONE + 1] = jnp.zeros_like(cb)
    o_ref[2 * ROPE_SET_IDX] = ci
    o_ref[2 * ROPE_SET_IDX + 1] = si


def _rope_tables(positions, tm):
    t = positions.size
    lane = jnp.arange(LANES)

    def inv_freq(rot_dim):
        half = rot_dim // 2
        return ROPE_THETA ** (-jnp.arange(half, dtype=F32) / half)

    invf = jnp.concatenate([inv_freq(ROT_DIM)[lane % (ROT_DIM // 2)],
                            inv_freq(IDX_ROT_DIM)[lane % (IDX_ROT_DIM // 2)]])[None, :]
    return pl.pallas_call(
        _rope_table_kernel,
        grid=(t // tm,),
        in_specs=[
            pl.BlockSpec((tm, 1), lambda i: (i, 0)),
            pl.BlockSpec((1, 2 * LANES), lambda i: (0, 0)),
        ],
        out_specs=pl.BlockSpec((8, tm, LANES), lambda i: (0, i, 0)),
        out_shape=jax.ShapeDtypeStruct((8, t, LANES), F32),
        compiler_params=_cparams(("arbitrary",)),
        name="rope_tables",
    )(positions.reshape(t, 1), invf)


def _rope(x, c, s):
    return x * c + pltpu.roll(x, HALF_LANES, 1) * s


PERM_BIG, PERM_NONE, PERM_IDX = 0, 1, 2


def _lane_permutations():
    h, p = ROT_DIM // 2, (HEAD_DIM - ROT_DIM) // 2
    big = np.concatenate([np.arange(0, h), np.arange(ROT_DIM, ROT_DIM + p),
                          np.arange(h, ROT_DIM), np.arange(ROT_DIM + p, HEAD_DIM)])
    lo, hi = _idx_lo_hi_dims()
    idx = np.concatenate([lo, IDX_DIM + lo, hi, IDX_DIM + hi])
    mats = np.zeros((3, LANES, LANES), np.float32)
    for kind, src in ((PERM_BIG, big), (PERM_NONE, np.arange(LANES)), (PERM_IDX, idx)):
        mats[kind, src, np.arange(LANES)] = 1.0
    return jnp.asarray(mats, BF16)


def _idx_lo_hi_dims():
    h, p = IDX_ROT_DIM // 2, (IDX_DIM - IDX_ROT_DIM) // 2
    lo = np.concatenate([np.arange(0, h), np.arange(IDX_ROT_DIM, IDX_ROT_DIM + p)])
    hi = np.concatenate([np.arange(h, IDX_ROT_DIM), np.arange(IDX_ROT_DIM + p, IDX_DIM)])
    return lo, hi


def _dsa_small_weights(w_tail):
    h, p = IDX_ROT_DIM // 2, (IDX_DIM - IDX_ROT_DIM) // 2
    klo = jnp.concatenate([w_tail[:, 0:h], w_tail[:, IDX_ROT_DIM:IDX_ROT_DIM + p]], axis=1)
    khi = jnp.concatenate([w_tail[:, h:IDX_ROT_DIM], w_tail[:, IDX_ROT_DIM + p:IDX_DIM]], axis=1)
    z = jnp.zeros_like(klo)
    wi = jnp.pad(w_tail[:, IDX_DIM:], ((0, 0), (0, LANES - IDX_HEADS)))
    return jnp.concatenate([klo, z, khi, z, z, klo, z, khi, wi], axis=1).astype(BF16)


def _dsa_proj_kernel(sub, tn, x_hbm, g_ref, m_ref, w_ref, ws_ref, perm_ref, tab_ref,
                     o_ref, ok_ref, ow_ref, y_ref, wbf_ref, x_ref, x_sem):
    i = pl.program_id(0)
    j = pl.program_id(1)
    tm = x_ref.shape[0]
    first_k, first_v, first_qi = Q_W // tn, (Q_W + KV_W) // tn, (Q_W + 2 * KV_W) // tn

    def x_copy(tile):
        return pltpu.make_async_copy(x_hbm.at[pl.ds(tile * tm, tm), :], x_ref, x_sem.at[0])

    @pl.when(j == 0)
    def _():
        pl.when(i == 0)(x_copy(i).start)
        x_copy(i).wait()
        _write_norm_mod(x_ref, g_ref, m_ref, sub, y_ref)
        pl.when(i + 1 < pl.num_programs(0))(x_copy(i + 1).start)
        small = _dot(y_ref[...], ws_ref[...])
        for s in range(2):
            ok_ref[:, s * LANES:(s + 1) * LANES] = _rope(
                small[:, s * LANES:(s + 1) * LANES], tab_ref[2 * ROPE_SET_IDX],
                tab_ref[2 * ROPE_SET_IDX + 1]).astype(BF16)
        ow_ref[...] = small[:, 2 * LANES:3 * LANES]

    @pl.when(pl.program_id(0) == 0)
    def _():
        kind = jnp.where(j < first_v, PERM_BIG, jnp.where(j < first_qi, PERM_NONE, PERM_IDX))
        perm = perm_ref[kind]
        for sl in range(tn // LANES):
            cols = slice(sl * LANES, (sl + 1) * LANES)
            wbf_ref[j, :, cols] = _dot(w_ref[:, cols].astype(BF16), perm).astype(BF16)

    acc = _dot(y_ref[...], wbf_ref[j])
    rope_set = jnp.where(j < first_k, ROPE_SET_Q,
                         jnp.where(j < first_v, ROPE_SET_K,
                                   jnp.where(j < first_qi, ROPE_SET_NONE, ROPE_SET_IDX)))
    c = tab_ref[2 * rope_set]
    s = tab_ref[2 * rope_set + 1]
    for sl in range(tn // LANES):
        cols = slice(sl * LANES, (sl + 1) * LANES)
        o_ref[:, cols] = _rope(acc[:, cols], c, s).astype(BF16)


def _dsa_proj(h, ada4, g_row, w_in, widx, w_small, tabs, layer, sub, seq, tm, tn):
    t, d = h.shape
    tpb = seq // tm
    return pl.pallas_call(
        functools.partial(_dsa_proj_kernel, sub, tn),
        grid=(t // tm, DSA_MAIN_W // tn),
        in_specs=[
            pl.BlockSpec(memory_space=pl.ANY),
            pl.BlockSpec((1, d), lambda i, j: (0, 0)),
            pl.BlockSpec((None, None, 3 * N_SUBLAYERS, d), lambda i, j: (layer, i // tpb, 0, 0)),
            _resident_weight_spec(widx, d, tn, DSA_MAIN_W // tn),
            pl.BlockSpec((d, 3 * LANES), lambda i, j: (0, 0)),
            pl.BlockSpec((3, LANES, LANES), lambda i, j: (0, 0, 0)),
            pl.BlockSpec((8, tm, LANES), lambda i, j: (0, i, 0)),
        ],
        out_specs=[
            pl.BlockSpec((tm, tn), lambda i, j: (i, j)),
            pl.BlockSpec((tm, 2 * LANES), lambda i, j: (i, 0)),
            pl.BlockSpec((tm, LANES), lambda i, j: (i, 0)),
        ],
        out_shape=[
            jax.ShapeDtypeStruct((t, DSA_MAIN_W), BF16),
            jax.ShapeDtypeStruct((t, 2 * LANES), BF16),
            jax.ShapeDtypeStruct((t, LANES), F32),
        ],
        scratch_shapes=[pltpu.VMEM((tm, d), BF16),
                        pltpu.VMEM((DSA_MAIN_W // tn, d, tn), BF16),
                        pltpu.VMEM((tm, d), F32),
                        pltpu.SemaphoreType.DMA((1,))],
        compiler_params=_cparams(("arbitrary", "arbitrary")),
        name=f"dsa_proj_l{layer}",
    )(h, g_row, ada4, w_in, w_small, _lane_permutations(), tabs)


def _dsa_attn_kernel(tq, tk, top_k, q_ref, qi_ref, wi_ref, k_ref, v_ref, kk_ref, o_ref,
                     score_ref, bias_ref, thr_ref, wib_ref, qg_ref, m_ref, acc_ref):
    i = pl.program_id(1)
    n_chunks = ((i + 1) * tq + tk - 1) // tk
    max_chunks = score_ref.shape[1] // tk
    t_pos = i * tq + lax.broadcasted_iota(jnp.int32, (tq, tk), 0)
    k_iota = lax.broadcasted_iota(jnp.int32, (tq, tk), 1)
    n_cb = tk // LANES

    for hh in range(IDX_HEADS):
        wib_ref[hh] = jnp.broadcast_to(wi_ref[:, hh:hh + 1], (tq, LANES))
    for g in range(N_KV_HEADS):
        for r in range(KV_GROUP):
            hd = g * KV_GROUP + r
            qg_ref[g, r * tq:(r + 1) * tq, :] = q_ref[:, hd * HEAD_DIM:(hd + 1) * HEAD_DIM]

    def score_chunk(c, carry):
        off = pl.multiple_of(c * tk, tk)
        k_even = kk_ref[pl.ds(off, tk), 0:LANES]
        k_odd = kk_ref[pl.ds(off, tk), LANES:2 * LANES]
        acc = [jnp.zeros((tq, LANES), F32) for _ in range(n_cb)]
        for pair in range(IDX_HEADS // 2):
            qp = qi_ref[:, pair * LANES:(pair + 1) * LANES]
            for par, kop in enumerate((k_even, k_odd)):
                r = _dot_t(qp, kop)
                w = wib_ref[2 * pair + par]
                for cb in range(n_cb):
                    acc[cb] = acc[cb] + w * jnp.maximum(r[:, cb * LANES:(cb + 1) * LANES], 0.0)
        sc = jnp.concatenate(acc, axis=1)
        score_ref[:, pl.ds(off, tk)] = jnp.where(k_iota + off <= t_pos, sc, NEG)
        return carry

    lax.fori_loop(0, n_chunks, score_chunk, 0)

    def key_to_f32(key):
        return pltpu.bitcast(key ^ ((key >> 31) & 0x7FFFFFFF), F32)

    k_f = float(top_k)

    def select(n_static):
        def bit_step(n, thr_key):
            cand = thr_key + lax.shift_left(jnp.int32(1), 31 - n)
            thr = key_to_f32(cand)
            cnt = jnp.zeros((tq, LANES), F32)
            for cb in range(n_static * n_cb):
                s = score_ref[:, cb * LANES:(cb + 1) * LANES]
                cnt = cnt + jnp.where(s >= thr, 1.0, 0.0)
            total = jnp.sum(cnt, axis=1, keepdims=True)
            return jnp.where(total >= k_f, cand, thr_key)

        thr_key = lax.fori_loop(0, 32, bit_step, jnp.full((tq, LANES), INT32_MIN, jnp.int32),
                                unroll=4)
        thr_ref[...] = key_to_f32(thr_key)

    for n_static in range(1, max_chunks + 1):
        pl.when(n_chunks == n_static)(functools.partial(select, n_static))

    def bias_chunk(c, carry):
        off = pl.multiple_of(c * tk, tk)
        thr = thr_ref[...]
        causal = k_iota + off <= t_pos
        for cb in range(n_cb):
            cols = pl.ds(off + cb * LANES, LANES)
            sel = score_ref[:, cols] >= thr
            bias_ref[:, cols] = jnp.where(
                sel, jnp.where(causal[:, cb * LANES:(cb + 1) * LANES], 0.0, NEG), NEG)
        return carry

    lax.fori_loop(0, n_chunks, bias_chunk, 0)

    def count_rows(pred):
        def body(c, cnt):
            off = pl.multiple_of(c * tk, tk)
            thr = thr_ref[...]
            for cb in range(n_cb):
                s = score_ref[:, pl.ds(off + cb * LANES, LANES)]
                cnt = cnt + jnp.where(pred(s, thr), 1.0, 0.0)
            return cnt
        cnt = lax.fori_loop(0, n_chunks, body, jnp.zeros((tq, LANES), F32))
        return jnp.sum(cnt, axis=1, keepdims=True)

    n_ge = count_rows(lambda s, thr: s >= thr)

    @pl.when(jnp.max(n_ge) > k_f)
    def _():
        n_take = k_f - count_rows(lambda s, thr: s > thr)
        before = jnp.where(
            lax.broadcasted_iota(jnp.int32, (tk, tk), 0) < lax.broadcasted_iota(jnp.int32, (tk, tk), 1),
            1.0, 0.0).astype(BF16)

        def tie_chunk(c, seen):
            off = pl.multiple_of(c * tk, tk)
            sc = score_ref[:, pl.ds(off, tk)]
            thr = jnp.concatenate([thr_ref[...]] * n_cb, axis=1)
            eq = jnp.where(sc == thr, 1.0, 0.0)
            rank = _dot(eq.astype(BF16), before) + seen
            keep = jnp.where(sc > thr, 1.0, jnp.where(rank < n_take, eq, 0.0))
            causal = k_iota + off <= t_pos
            bias_ref[:, pl.ds(off, tk)] = jnp.where(causal, jnp.where(keep > 0.0, 0.0, NEG), NEG)
            return seen + jnp.sum(eq, axis=1, keepdims=True)

        lax.fori_loop(0, n_chunks, tie_chunk, jnp.zeros((tq, 1), F32))

    rows = KV_GROUP * tq
    m_ref[...] = jnp.full(m_ref.shape, NEG, F32)
    acc_ref[...] = jnp.zeros(acc_ref.shape, F32)
    ones = jnp.ones((tk, LANES), BF16)

    def attn_chunk(c, carry):
        off = pl.multiple_of(c * tk, tk)
        b = bias_ref[:, pl.ds(off, tk)]
        for g in range(N_KV_HEADS):
            kc = k_ref[pl.ds(off, tk), g * HEAD_DIM:(g + 1) * HEAD_DIM]
            vc = v_ref[pl.ds(off, tk), g * HEAD_DIM:(g + 1) * HEAD_DIM]
            s = _dot_t(qg_ref[g], kc)
            s = (s.reshape(KV_GROUP, tq, tk) + b[None]).reshape(rows, tk)
            m_old = m_ref[g]
            m_new = jnp.maximum(m_old, jnp.max(s, axis=1, keepdims=True))
            alpha = jnp.exp2(m_old - m_new)
            p = jnp.exp2(s - jnp.concatenate([m_new] * n_cb, axis=1))
            pv = _dot(p.astype(BF16), jnp.concatenate([vc, ones], axis=1))
            acc_ref[g] = jnp.concatenate([alpha, alpha], axis=1) * acc_ref[g] + pv
            m_ref[g] = m_new
        return carry

    lax.fori_loop(0, n_chunks, attn_chunk, 0)

    for g in range(N_KV_HEADS):
        out = acc_ref[g, :, 0:HEAD_DIM] / acc_ref[g, :, HEAD_DIM:2 * HEAD_DIM]
        for r in range(KV_GROUP):
            hd = g * KV_GROUP + r
            o_ref[:, hd * HEAD_DIM:(hd + 1) * HEAD_DIM] = out[r * tq:(r + 1) * tq].astype(BF16)


def _dsa_attn(main, kk, wi, batch, seq, tq, tk, top_k):
    t = main.shape[0]
    nq = seq // tq
    rows = KV_GROUP * tq
    return pl.pallas_call(
        functools.partial(_dsa_attn_kernel, tq, tk, top_k),
        grid=(batch, nq),
        in_specs=[
            pl.BlockSpec((tq, Q_W), lambda b, i: (b * nq + i, 0)),
            pl.BlockSpec((tq, QI_W), lambda b, i: (b * nq + i, (Q_W + 2 * KV_W) // QI_W)),
            pl.BlockSpec((tq, LANES), lambda b, i: (b * nq + i, 0)),
            pl.BlockSpec((seq, KV_W), lambda b, i: (b, Q_W // KV_W)),
            pl.BlockSpec((seq, KV_W), lambda b, i: (b, Q_W // KV_W + 1)),
            pl.BlockSpec((seq, 2 * LANES), lambda b, i: (b, 0)),
        ],
        out_specs=pl.BlockSpec((tq, Q_W), lambda b, i: (b * nq + i, 0)),
        out_shape=jax.ShapeDtypeStruct((t, Q_W), BF16),
        scratch_shapes=[
            pltpu.VMEM((tq, seq), F32),
            pltpu.VMEM((tq, seq), F32),
            pltpu.VMEM((tq, LANES), F32),
            pltpu.VMEM((IDX_HEADS, tq, LANES), F32),
            pltpu.VMEM((N_KV_HEADS, rows, HEAD_DIM), BF16),
            pltpu.VMEM((N_KV_HEADS, rows, LANES), F32),
            pltpu.VMEM((N_KV_HEADS, rows, 2 * HEAD_DIM), F32),
        ],
        compiler_params=_cparams(("arbitrary", "arbitrary")),
        name="dsa_attn",
    )(main, main, wi, main, main, kk)


def _tiles(seq, d_ff):
    tm = min(1024, seq)
    tm_ffn = min(2048, seq)
    tf = 256 if d_ff % 256 == 0 else d_ff
    tq = min(256, seq)
    tk = min(256, seq)
    return tm, tm_ffn, tf, tq, tk


def kernel(x, c, positions, ada_w, ada_b, norm_g, final_g, ffn_wgu, ffn_wd,
           pool_w_in, pool_w_grp, pool_scale, pool_w_out, dsa_w_in, dsa_w_out):
    batch, seq, d = x.shape
    depth = ada_w.shape[0]
    d_ff = ffn_wd.shape[2]
    t = batch * seq
    tm, tm_ffn, tf, tq, tk = _tiles(seq, d_ff)
    tn = min(512, d)
    top_k = min(TOPK_MAX, seq // 4)
    assert seq % tm == 0 and seq % tm_ffn == 0 and seq % tq == 0 and seq % tk == 0 and tk >= top_k
    assert dsa_w_in.shape[2] == DSA_MAIN_W + DSA_SMALL_W and DSA_MAIN_W % tn == 0

    rows = -(-batch // SUBLANES) * SUBLANES
    c8 = jnp.pad(c, ((0, rows - batch), (0, 0)))
    ada_n = ada_w.shape[2]
    ada = _ada(c8, ada_w, ada_b, 1024 if ada_n % 1024 == 0 else d)
    ada4 = ada.reshape(depth, rows, 3 * N_SUBLAYERS, d)

    tabs = _rope_tables(positions, tm) if depth > 1 else None

    h = x.reshape(t, d)
    for layer in range(depth):
        j = layer // 2
        h = _ffn(h, ada4, norm_g, ffn_wgu, ffn_wd, layer, 0, 0, seq, tm_ffn, tf)
        g_row = norm_g[layer, 1][None, :]
        if layer % 2 == 0:
            u = _norm_linear(h, ada4, g_row, pool_w_in, j, layer, 1, seq, tm, tn)
            a = _pool_core(u, pool_w_grp, pool_scale, j, batch, seq)
            h = _linear_residual(a, pool_w_out, j, h, ada4, layer, 1, seq, tm, tn)
        else:
            w_small = _dsa_small_weights(dsa_w_in[j, :, DSA_MAIN_W:])
            main, kk, wi = _dsa_proj(h, ada4, g_row, dsa_w_in, j, w_small, tabs,
                                     layer, 1, seq, tm, tn)
            o = _dsa_attn(main, kk, wi, batch, seq, tq, tk, top_k)
            h = _linear_residual(o, dsa_w_out, j, h, ada4, layer, 1, seq, tm, tn)
        last = layer == depth - 1
        h = _ffn(h, ada4, norm_g, ffn_wgu, ffn_wd, layer, 1, 2, seq, tm_ffn, tf,
                 final_g=final_g if last else None)
    return h.reshape(batch, seq, d)
```

```python
import functools

import jax
import jax.numpy as jnp
import numpy as np
from jax import lax
from jax.experimental import pallas as pl
from jax.experimental.pallas import tpu as pltpu

F32 = jnp.float32
BF16 = jnp.bfloat16

N_SUBLAYERS = 3
POOL_WINDOWS = (2, 4, 8, 16)
N_HEADS = 16
N_KV_HEADS = 4
KV_GROUP = N_HEADS // N_KV_HEADS
HEAD_DIM = 128
ROT_DIM = HEAD_DIM // 4
ROPE_THETA = 500000.0
IDX_HEADS = 16
IDX_DIM = 64
IDX_ROT_DIM = IDX_DIM // 4
TOPK_MAX = 256
EPS = 1e-6
NEG = -1e30
Q_W = N_HEADS * HEAD_DIM
KV_W = N_KV_HEADS * HEAD_DIM
QI_W = IDX_HEADS * IDX_DIM
DSA_MAIN_W = Q_W + 2 * KV_W + QI_W
DSA_SMALL_W = IDX_DIM + IDX_HEADS
Q_PRESCALE = HEAD_DIM ** -0.5 * 1.4426950408889634

LANES = 128
SUBLANES = 8
VMEM_LIMIT_BYTES = 56 * 1024 * 1024
INT32_MIN = -(2 ** 31)


def _cparams(semantics):
    return pltpu.CompilerParams(dimension_semantics=semantics,
                                vmem_limit_bytes=VMEM_LIMIT_BYTES)


def _rmsnorm(x, g):
    ms = jnp.mean(x * x, axis=-1, keepdims=True)
    return x * lax.rsqrt(ms + EPS) * g


def _rmsnorm_rows(ref, rows, g_ref):
    ref[rows, :] = _rmsnorm(ref[rows, :], g_ref[...])


NORM_SLAB = 16
NORM_UNROLL = 8
FFN_ROW_SLAB = 128
FFN_SLAB_SLOTS = 4
FFN_DOWN_COLS = 256


def _for_row_slabs(n_rows, slab, body, unroll=1):
    def step(r, carry):
        body(pl.ds(pl.multiple_of(r * slab, slab), slab))
        return carry
    lax.fori_loop(0, n_rows // slab, step, 0, unroll=unroll)


def _write_norm_mod(x_ref, g_ref, m_ref, sub, y_ref, copy_ref=None):
    shift = m_ref[3 * sub + 0:3 * sub + 1, :]
    gs = g_ref[...] * (1.0 + m_ref[3 * sub + 1:3 * sub + 2, :])
    slab = min(NORM_SLAB, x_ref.shape[0])

    def norm_rows(rows):
        x = x_ref[rows, :]
        r = lax.rsqrt(jnp.mean(x * x, axis=-1, keepdims=True) + EPS)
        y_ref[rows, :] = ((x * r) * gs + shift).astype(BF16)
        if copy_ref is not None:
            copy_ref[rows, :] = x

    _for_row_slabs(x_ref.shape[0], slab, norm_rows, unroll=NORM_UNROLL)


def _silu(x):
    return x * jax.nn.sigmoid(x)


def _dot(a, b):
    return jnp.dot(a, b, preferred_element_type=F32)


def _dot_t(a, b):
    return lax.dot_general(a, b, (((1,), (1,)), ((), ())), preferred_element_type=F32)


def _ada_kernel(c_ref, w_ref, b_ref, o_ref):
    sc = _silu(c_ref[...]).astype(BF16)
    o_ref[...] = _dot(sc, w_ref[...].astype(BF16)) + b_ref[...]


def _ada(c8, ada_w, ada_b, tn):
    depth, d, n = ada_w.shape
    rows = c8.shape[0]
    return pl.pallas_call(
        _ada_kernel,
        grid=(depth, n // tn),
        in_specs=[
            pl.BlockSpec((rows, d), lambda l, j: (0, 0)),
            pl.BlockSpec((None, d, tn), lambda l, j: (l, 0, j)),
            pl.BlockSpec((None, 1, tn), lambda l, j: (l, 0, j)),
        ],
        out_specs=pl.BlockSpec((None, rows, tn), lambda l, j: (l, 0, j)),
        out_shape=jax.ShapeDtypeStruct((depth, rows, n), F32),
        compiler_params=_cparams(("arbitrary", "arbitrary")),
        name="ada",
    )(c8, ada_w, ada_b.reshape(depth, 1, n))


def _ffn_kernel(sub, final, rs, *refs):
    if final:
        (x_hbm, g_ref, m_ref, wg_ref, wu_ref, wd_ref, fg_ref, o_hbm,
         acc_ref, y_ref, xbuf, in_sem, out_sem) = refs
    else:
        (x_hbm, g_ref, m_ref, wg_ref, wu_ref, wd_ref, o_hbm,
         acc_ref, y_ref, xbuf, in_sem, out_sem) = refs
    k = pl.program_id(1)
    nk = pl.num_programs(1)
    tm, d_out = acc_ref.shape
    n_slabs = tm // rs
    n_slots = xbuf.shape[0]
    n_first = min(n_slots, n_slabs)
    row0 = pl.program_id(0) * tm

    def x_copy(s, tile_row0=row0):
        slot = s % n_slots
        return pltpu.make_async_copy(x_hbm.at[pl.ds(tile_row0 + s * rs, rs), :], xbuf.at[slot],
                                     in_sem.at[slot])

    def o_copy(s):
        return pltpu.make_async_copy(acc_ref.at[pl.ds(s * rs, rs), :],
                                     o_hbm.at[pl.ds(row0 + s * rs, rs), :], out_sem.at[s])

    @pl.when(k == 0)
    def _():
        @pl.when(pl.program_id(0) == 0)
        def _():
            for s in range(n_first):
                x_copy(s).start()

        for s in range(n_slabs):
            x_copy(s).wait()
            pl.when(pl.program_id(0) > 0)(o_copy(s).wait)
            rows = pl.ds(s * rs, rs)
            _write_norm_mod(xbuf.at[s % n_slots], g_ref, m_ref, sub, y_ref.at[rows, :],
                            copy_ref=acc_ref.at[rows, :])
            if s + n_slots < n_slabs:
                x_copy(s + n_slots).start()

    @pl.when(jnp.logical_and(k == nk - 1, pl.program_id(0) + 1 < pl.num_programs(0)))
    def _():
        for s in range(n_first):
            x_copy(s, row0 + tm).start()

    y = y_ref[...]
    g = _dot(y, wg_ref[...].astype(BF16))
    u = _dot(y, wu_ref[...].astype(BF16))
    a = (_silu(g) * u).astype(BF16)
    half_gate = 0.5 * m_ref[3 * sub + 2:3 * sub + 3, :]
    nc = min(FFN_DOWN_COLS, d_out)
    for n in range(d_out // nc):
        cols = slice(n * nc, (n + 1) * nc)
        acc_ref[:, cols] += half_gate[:, cols] * _dot(a, wd_ref[:, cols].astype(BF16))

    @pl.when(k == nk - 1)
    def _():
        for s in range(n_slabs):
            if final:
                _for_row_slabs(
                    rs, min(NORM_SLAB, rs),
                    lambda rows, s=s: _rmsnorm_rows(acc_ref.at[pl.ds(s * rs, rs), :], rows, fg_ref),
                    unroll=NORM_UNROLL)
            o_copy(s).start()

        @pl.when(pl.program_id(0) == pl.num_programs(0) - 1)
        def _():
            for s in range(n_slabs):
                o_copy(s).wait()


def _ffn(h, ada4, norm_g, ffn_wgu, ffn_wd, layer, which, sub, seq, tm, tf, final_g=None):
    t, d = h.shape
    d_ff = ffn_wd.shape[2]
    nk = d_ff // tf
    tpb = seq // tm
    rs = min(FFN_ROW_SLAB, tm)
    final = final_g is not None
    in_specs = [
        pl.BlockSpec(memory_space=pl.ANY),
        pl.BlockSpec((1, d), lambda i, k: (0, 0)),
        pl.BlockSpec((None, None, 3 * N_SUBLAYERS, d), lambda i, k: (layer, i // tpb, 0, 0)),
        pl.BlockSpec((None, None, d, tf), lambda i, k: (layer, which, 0, k)),
        pl.BlockSpec((None, None, d, tf), lambda i, k: (layer, which, 0, nk + k)),
        pl.BlockSpec((None, None, tf, d), lambda i, k: (layer, which, k, 0)),
    ]
    args = [h, norm_g[layer, sub][None, :], ada4, ffn_wgu, ffn_wgu, ffn_wd]
    if final:
        in_specs.append(pl.BlockSpec((1, d), lambda i, k: (0, 0)))
        args.append(final_g[None, :])
    return pl.pallas_call(
        functools.partial(_ffn_kernel, sub, final, rs),
        grid=(t // tm, nk),
        in_specs=in_specs,
        out_specs=pl.BlockSpec(memory_space=pl.ANY),
        out_shape=jax.ShapeDtypeStruct((t, d), F32),
        scratch_shapes=[
            pltpu.VMEM((tm, d), F32),
            pltpu.VMEM((tm, d), BF16),
            pltpu.VMEM((FFN_SLAB_SLOTS, rs, d), F32),
            pltpu.SemaphoreType.DMA((FFN_SLAB_SLOTS,)),
            pltpu.SemaphoreType.DMA((tm // rs,)),
        ],
        compiler_params=_cparams(("arbitrary", "arbitrary")),
        name=f"ffn_l{layer}_{which}",
    )(*args)


def _keep_bf16_weight(w_ref, wbf_ref):
    j = pl.program_id(1)

    @pl.when(pl.program_id(0) == 0)
    def _():
        wbf_ref[j] = w_ref[...].astype(BF16)

    return wbf_ref[j]


def _resident_weight_spec(widx, kdim, tn, nj):
    return pl.BlockSpec((None, kdim, tn), lambda i, j: (widx, 0, jnp.where(i == 0, j, nj - 1)))


def _norm_linear_kernel(sub, x_ref, g_ref, m_ref, w_ref, o_ref, y_ref, wbf_ref):
    @pl.when(pl.program_id(1) == 0)
    def _():
        _write_norm_mod(x_ref, g_ref, m_ref, sub, y_ref)

    o_ref[...] = _dot(y_ref[...], _keep_bf16_weight(w_ref, wbf_ref))


def _norm_linear(h, ada4, g_row, w, widx, layer, sub, seq, tm, tn):
    t, d = h.shape
    n = w.shape[2]
    tpb = seq // tm
    return pl.pallas_call(
        functools.partial(_norm_linear_kernel, sub),
        grid=(t // tm, n // tn),
        in_specs=[
            pl.BlockSpec((tm, d), lambda i, j: (i, 0)),
            pl.BlockSpec((1, d), lambda i, j: (0, 0)),
            pl.BlockSpec((None, None, 3 * N_SUBLAYERS, d), lambda i, j: (layer, i // tpb, 0, 0)),
            _resident_weight_spec(widx, d, tn, n // tn),
        ],
        out_specs=pl.BlockSpec((tm, tn), lambda i, j: (i, j)),
        out_shape=jax.ShapeDtypeStruct((t, n), F32),
        scratch_shapes=[pltpu.VMEM((tm, d), BF16), pltpu.VMEM((n // tn, d, tn), BF16)],
        compiler_params=_cparams(("arbitrary", "arbitrary")),
        name=f"norm_linear_l{layer}",
    )(h, g_row, ada4, w)


def _pool_kernel(u_ref, w_ref, ls_ref, o_ref):
    g = pl.program_id(1)
    u = u_ref[...]
    t = lax.broadcasted_iota(jnp.int32, u.shape, 0)

    def shifted(v, d):
        return jnp.where(t >= d, pltpu.roll(v, d, 0), 0.0)

    for gi, win in enumerate(POOL_WINDOWS):
        @pl.when(g == gi)
        def _(win=win):
            acc = u
            d = 1
            while d < win:
                acc = acc + shifted(acc, d)
                d *= 2
            cnt = jnp.minimum(t + 1, win).astype(F32)
            p = (acc / cnt - u).astype(BF16)
            v = _dot(p, w_ref[...].astype(BF16)) * ls_ref[...]
            o_ref[...] = v.astype(BF16)


def _pool_core(u, w_grp, ls, widx, batch, seq):
    t, d = u.shape
    _, groups, c, _ = w_grp.shape
    return pl.pallas_call(
        _pool_kernel,
        grid=(batch, groups),
        in_specs=[
            pl.BlockSpec((seq, c), lambda b, g: (b, g)),
            pl.BlockSpec((None, None, c, c), lambda b, g: (widx, g, 0, 0)),
            pl.BlockSpec((None, 1, c), lambda b, g: (widx, 0, g)),
        ],
        out_specs=pl.BlockSpec((seq, c), lambda b, g: (b, g)),
        out_shape=jax.ShapeDtypeStruct((t, d), BF16),
        compiler_params=_cparams(("arbitrary", "arbitrary")),
        name="pool_core",
    )(u, w_grp, ls.reshape(ls.shape[0], 1, d))


def _linear_residual_kernel(sub, a_ref, w_ref, h_ref, m_ref, o_ref, wbf_ref):
    gate = m_ref[3 * sub + 2:3 * sub + 3, :]
    o_ref[...] = h_ref[...] + gate * _dot(a_ref[...], _keep_bf16_weight(w_ref, wbf_ref))


def _linear_residual(a, w, widx, h, ada4, layer, sub, seq, tm, tn):
    t, kdim = a.shape
    n = w.shape[2]
    tpb = seq // tm
    return pl.pallas_call(
        functools.partial(_linear_residual_kernel, sub),
        grid=(t // tm, n // tn),
        in_specs=[
            pl.BlockSpec((tm, kdim), lambda i, j: (i, 0)),
            _resident_weight_spec(widx, kdim, tn, n // tn),
            pl.BlockSpec((tm, tn), lambda i, j: (i, j)),
            pl.BlockSpec((None, None, 3 * N_SUBLAYERS, tn), lambda i, j: (layer, i // tpb, 0, j)),
        ],
        out_specs=pl.BlockSpec((tm, tn), lambda i, j: (i, j)),
        out_shape=jax.ShapeDtypeStruct((t, n), F32),
        scratch_shapes=[pltpu.VMEM((n // tn, kdim, tn), BF16)],
        compiler_params=_cparams(("arbitrary", "arbitrary")),
        name=f"linear_residual_l{layer}",
    )(a, w, h, ada4)


ROPE_SET_Q, ROPE_SET_K, ROPE_SET_NONE, ROPE_SET_IDX = 0, 1, 2, 3
HALF_LANES = LANES // 2


def _rope_table_kernel(pos_ref, invf_ref, o_ref):
    pos = pos_ref[...].astype(F32)
    lane = lax.broadcasted_iota(jnp.int32, (pos.shape[0], LANES), 1)
    first = lane < HALF_LANES

    def tables(invf, is_rot):
        ang = pos * invf
        c = jnp.where(is_rot, jnp.cos(ang), 1.0)
        s = jnp.where(is_rot, jnp.where(first, -jnp.sin(ang), jnp.sin(ang)), 0.0)
        return c, s

    cb, sb = tables(invf_ref[:, 0:LANES], (lane & (HALF_LANES - 1)) < ROT_DIM // 2)
    ci, si = tables(invf_ref[:, LANES:2 * LANES], (lane & (HALF_LANES // 2 - 1)) < IDX_ROT_DIM // 2)
    o_ref[2 * ROPE_SET_Q] = cb * Q_PRESCALE
    o_ref[2 * ROPE_SET_Q + 1] = sb * Q_PRESCALE
    o_ref[2 * ROPE_SET_K] = cb
    o_ref[2 * ROPE_SET_K + 1] = sb
    o_ref[2 * ROPE_SET_NONE] = jnp.ones_like(cb)
    o_ref[2 * ROPE_SET_NONE + 1] = jnp.zeros_like(cb)
    o_ref[2 * ROPE_SET_IDX] = ci
    o_ref[2 * ROPE_SET_IDX + 1] = si


def _rope_tables(positions, tm):
    t = positions.size
    lane = jnp.arange(LANES)

    def inv_freq(rot_dim):
        half = rot_dim // 2
        return ROPE_THETA ** (-jnp.arange(half, dtype=F32) / half)

    invf = jnp.concatenate([inv_freq(ROT_DIM)[lane % (ROT_DIM // 2)],
                            inv_freq(IDX_ROT_DIM)[lane % (IDX_ROT_DIM // 2)]])[None, :]
    return pl.pallas_call(
        _rope_table_kernel,
        grid=(t // tm,),
        in_specs=[
            pl.BlockSpec((tm, 1), lambda i: (i, 0)),
            pl.BlockSpec((1, 2 * LANES), lambda i: (0, 0)),
        ],
        out_specs=pl.BlockSpec((8, tm, LANES), lambda i: (0, i, 0)),
        out_shape=jax.ShapeDtypeStruct((8, t, LANES), F32),
        compiler_params=_cparams(("arbitrary",)),
        name="rope_tables",
    )(positions.reshape(t, 1), invf)


def _rope(x, c, s):
    return x * c + pltpu.roll(x, HALF_LANES, 1) * s


PERM_BIG, PERM_NONE, PERM_IDX = 0, 1, 2


def _lane_permutations():
    h, p = ROT_DIM // 2, (HEAD_DIM - ROT_DIM) // 2
    big = np.concatenate([np.arange(0, h), np.arange(ROT_DIM, ROT_DIM + p),
                          np.arange(h, ROT_DIM), np.arange(ROT_DIM + p, HEAD_DIM)])
    lo, hi = _idx_lo_hi_dims()
    idx = np.concatenate([lo, IDX_DIM + lo, hi, IDX_DIM + hi])
    mats = np.zeros((3, LANES, LANES), np.float32)
    for kind, src in ((PERM_BIG, big), (PERM_NONE, np.arange(LANES)), (PERM_IDX, idx)):
        mats[kind, src, np.arange(LANES)] = 1.0
    return jnp.asarray(mats, BF16)


def _idx_lo_hi_dims():
    h, p = IDX_ROT_DIM // 2, (IDX_DIM - IDX_ROT_DIM) // 2
    lo = np.concatenate([np.arange(0, h), np.arange(IDX_ROT_DIM, IDX_ROT_DIM + p)])
    hi = np.concatenate([np.arange(h, IDX_ROT_DIM), np.arange(IDX_ROT_DIM + p, IDX_DIM)])
    return lo, hi


def _dsa_small_weights(w_tail):
    h, p = IDX_ROT_DIM // 2, (IDX_DIM - IDX_ROT_DIM) // 2
    klo = jnp.concatenate([w_tail[:, 0:h], w_tail[:, IDX_ROT_DIM:IDX_ROT_DIM + p]], axis=1)
    khi = jnp.concatenate([w_tail[:, h:IDX_ROT_DIM], w_tail[:, IDX_ROT_DIM + p:IDX_DIM]], axis=1)
    z = jnp.zeros_like(klo)
    wi = jnp.pad(w_tail[:, IDX_DIM:], ((0, 0), (0, LANES - IDX_HEADS)))
    return jnp.concatenate([klo, z, khi, z, z, klo, z, khi, wi], axis=1).astype(BF16)


def _dsa_proj_kernel(sub, tn, x_hbm, g_ref, m_ref, w_ref, ws_ref, perm_ref, tab_ref,
                     o_ref, ok_ref, ow_ref, y_ref, wbf_ref, x_ref, x_sem):
    i = pl.program_id(0)
    j = pl.program_id(1)
    tm = x_ref.shape[0]
    first_k, first_v, first_qi = Q_W // tn, (Q_W + KV_W) // tn, (Q_W + 2 * KV_W) // tn

    def x_copy(tile):
        return pltpu.make_async_copy(x_hbm.at[pl.ds(tile * tm, tm), :], x_ref, x_sem.at[0])

    @pl.when(j == 0)
    def _():
        pl.when(i == 0)(x_copy(i).start)
        x_copy(i).wait()
        _write_norm_mod(x_ref, g_ref, m_ref, sub, y_ref)
        pl.when(i + 1 < pl.num_programs(0))(x_copy(i + 1).start)
        small = _dot(y_ref[...], ws_ref[...])
        for s in range(2):
            ok_ref[:, s * LANES:(s + 1) * LANES] = _rope(
                small[:, s * LANES:(s + 1) * LANES], tab_ref[2 * ROPE_SET_IDX],
                tab_ref[2 * ROPE_SET_IDX + 1]).astype(BF16)
        ow_ref[...] = small[:, 2 * LANES:3 * LANES]

    @pl.when(pl.program_id(0) == 0)
    def _():
        kind = jnp.where(j < first_v, PERM_BIG, jnp.where(j < first_qi, PERM_NONE, PERM_IDX))
        perm = perm_ref[kind]
        for sl in range(tn // LANES):
            cols = slice(sl * LANES, (sl + 1) * LANES)
            wbf_ref[j, :, cols] = _dot(w_ref[:, cols].astype(BF16), perm).astype(BF16)

    acc = _dot(y_ref[...], wbf_ref[j])
    rope_set = jnp.where(j < first_k, ROPE_SET_Q,
                         jnp.where(j < first_v, ROPE_SET_K,
                                   jnp.where(j < first_qi, ROPE_SET_NONE, ROPE_SET_IDX)))
    c = tab_ref[2 * rope_set]
    s = tab_ref[2 * rope_set + 1]
    for sl in range(tn // LANES):
        cols = slice(sl * LANES, (sl + 1) * LANES)
        o_ref[:, cols] = _rope(acc[:, cols], c, s).astype(BF16)


def _dsa_proj(h, ada4, g_row, w_in, widx, w_small, tabs, layer, sub, seq, tm, tn):
    t, d = h.shape
    tpb = seq // tm
    return pl.pallas_call(
        functools.partial(_dsa_proj_kernel, sub, tn),
        grid=(t // tm, DSA_MAIN_W // tn),
        in_specs=[
            pl.BlockSpec(memory_space=pl.ANY),
            pl.BlockSpec((1, d), lambda i, j: (0, 0)),
            pl.BlockSpec((None, None, 3 * N_SUBLAYERS, d), lambda i, j: (layer, i // tpb, 0, 0)),
            _resident_weight_spec(widx, d, tn, DSA_MAIN_W // tn),
            pl.BlockSpec((d, 3 * LANES), lambda i, j: (0, 0)),
            pl.BlockSpec((3, LANES, LANES), lambda i, j: (0, 0, 0)),
            pl.BlockSpec((8, tm, LANES), lambda i, j: (0, i, 0)),
        ],
        out_specs=[
            pl.BlockSpec((tm, tn), lambda i, j: (i, j)),
            pl.BlockSpec((tm, 2 * LANES), lambda i, j: (i, 0)),
            pl.BlockSpec((tm, LANES), lambda i, j: (i, 0)),
        ],
        out_shape=[
            jax.ShapeDtypeStruct((t, DSA_MAIN_W), BF16),
            jax.ShapeDtypeStruct((t, 2 * LANES), BF16),
            jax.ShapeDtypeStruct((t, LANES), F32),
        ],
        scratch_shapes=[pltpu.VMEM((tm, d), BF16),
                        pltpu.VMEM((DSA_MAIN_W // tn, d, tn), BF16),
                        pltpu.VMEM((tm, d), F32),
                        pltpu.SemaphoreType.DMA((1,))],
        compiler_params=_cparams(("arbitrary", "arbitrary")),
        name=f"dsa_proj_l{layer}",
    )(h, g_row, ada4, w_in, w_small, _lane_permutations(), tabs)


def _dsa_attn_kernel(tq, tk, top_k, q_ref, qi_ref, wi_ref, k_ref, v_ref, kk_ref, o_ref,
                     score_ref, bias_ref, thr_ref, wib_ref, qg_ref, m_ref, acc_ref):
    i = pl.program_id(1)
    n_chunks = ((i + 1) * tq + tk - 1) // tk
    max_chunks = score_ref.shape[1] // tk
    t_pos = i * tq + lax.broadcasted_iota(jnp.int32, (tq, tk), 0)
    k_iota = lax.broadcasted_iota(jnp.int32, (tq, tk), 1)
    n_cb = tk // LANES

    for hh in range(IDX_HEADS):
        wib_ref[hh] = jnp.broadcast_to(wi_ref[:, hh:hh + 1], (tq, LANES))
    for g in range(N_KV_HEADS):
        for r in range(KV_GROUP):
            hd = g * KV_GROUP + r
            qg_ref[g, r * tq:(r + 1) * tq, :] = q_ref[:, hd * HEAD_DIM:(hd + 1) * HEAD_DIM]

    def score_chunk(c, carry):
        off = pl.multiple_of(c * tk, tk)
        k_even = kk_ref[pl.ds(off, tk), 0:LANES]
        k_odd = kk_ref[pl.ds(off, tk), LANES:2 * LANES]
        acc = [jnp.zeros((tq, LANES), F32) for _ in range(n_cb)]
        for pair in range(IDX_HEADS // 2):
            qp = qi_ref[:, pair * LANES:(pair + 1) * LANES]
            for par, kop in enumerate((k_even, k_odd)):
                r = _dot_t(qp, kop)
                w = wib_ref[2 * pair + par]
                for cb in range(n_cb):
                    acc[cb] = acc[cb] + w * jnp.maximum(r[:, cb * LANES:(cb + 1) * LANES], 0.0)
        sc = jnp.concatenate(acc, axis=1)
        score_ref[:, pl.ds(off, tk)] = jnp.where(k_iota + off <= t_pos, sc, NEG)
        return carry

    lax.fori_loop(0, n_chunks, score_chunk, 0)

    def key_to_f32(key):
        return pltpu.bitcast(key ^ ((key >> 31) & 0x7FFFFFFF), F32)

    k_f = float(top_k)

    def select(n_static):
        def bit_step(n, thr_key):
            cand = thr_key + lax.shift_left(jnp.int32(1), 31 - n)
            thr = key_to_f32(cand)
            cnt = jnp.zeros((tq, LANES), F32)
            for cb in range(n_static * n_cb):
                s = score_ref[:, cb * LANES:(cb + 1) * LANES]
                cnt = cnt + jnp.where(s >= thr, 1.0, 0.0)
            total = jnp.sum(cnt, axis=1, keepdims=True)
            return jnp.where(total >= k_f, cand, thr_key)

        thr_key = lax.fori_loop(0, 32, bit_step, jnp.full((tq, LANES), INT32_MIN, jnp.int32),
                                unroll=4)
        thr_ref[...] = key_to_f32(thr_key)

    for n_static in range(1, max_chunks + 1):
        pl.when(n_chunks == n_static)(functools.partial(select, n_static))

    def bias_chunk(c, carry):
        off = pl.multiple_of(c * tk, tk)
        thr = thr_ref[...]
        causal = k_iota + off <= t_pos
        for cb in range(n_cb):
            cols = pl.ds(off + cb * LANES, LANES)
            sel = score_ref[:, cols] >= thr
            bias_ref[:, cols] = jnp.where(
                sel, jnp.where(causal[:, cb * LANES:(cb + 1) * LANES], 0.0, NEG), NEG)
        return carry

    lax.fori_loop(0, n_chunks, bias_chunk, 0)

    def count_rows(pred):
        def body(c, cnt):
            off = pl.multiple_of(c * tk, tk)
            thr = thr_ref[...]
            for cb in range(n_cb):
                s = score_ref[:, pl.ds(off + cb * LANES, LANES)]
                cnt = cnt + jnp.where(pred(s, thr), 1.0, 0.0)
            return cnt
        cnt = lax.fori_loop(0, n_chunks, body, jnp.zeros((tq, LANES), F32))
        return jnp.sum(cnt, axis=1, keepdims=True)

    n_ge = count_rows(lambda s, thr: s >= thr)

    @pl.when(jnp.max(n_ge) > k_f)
    def _():
        n_take = k_f - count_rows(lambda s, thr: s > thr)
        before = jnp.where(
            lax.broadcasted_iota(jnp.int32, (tk, tk), 0) < lax.broadcasted_iota(jnp.int32, (tk, tk), 1),
            1.0, 0.0).astype(BF16)

        def tie_chunk(c, seen):
            off = pl.multiple_of(c * tk, tk)
            sc = score_ref[:, pl.ds(off, tk)]
            thr = jnp.concatenate([thr_ref[...]] * n_cb, axis=1)
            eq = jnp.where(sc == thr, 1.0, 0.0)
            rank = _dot(eq.astype(BF16), before) + seen
            keep = jnp.where(sc > thr, 1.0, jnp.where(rank < n_take, eq, 0.0))
            causal = k_iota + off <= t_pos
            bias_ref[:, pl.ds(off, tk)] = jnp.where(causal, jnp.where(keep > 0.0, 0.0, NEG), NEG)
            return seen + jnp.sum(eq, axis=1, keepdims=True)

        lax.fori_loop(0, n_chunks, tie_chunk, jnp.zeros((tq, 1), F32))

    rows = KV_GROUP * tq
    m_ref[...] = jnp.full(m_ref.shape, NEG, F32)
    acc_ref[...] = jnp.zeros(acc_ref.shape, F32)
    ones = jnp.ones((tk, LANES), BF16)

    def attn_chunk(c, carry):
        off = pl.multiple_of(c * tk, tk)
        b = bias_ref[:, pl.ds(off, tk)]
        for g in range(N_KV_HEADS):
            kc = k_ref[pl.ds(off, tk), g * HEAD_DIM:(g + 1) * HEAD_DIM]
            vc = v_ref[pl.ds(off, tk), g * HEAD_DIM:(g + 1) * HEAD_DIM]
            s = _dot_t(qg_ref[g], kc)
            s = (s.reshape(KV_GROUP, tq, tk) + b[None]).reshape(rows, tk)
            m_old = m_ref[g]
            m_new = jnp.maximum(m_old, jnp.max(s, axis=1, keepdims=True))
            alpha = jnp.exp2(m_old - m_new)
            p = jnp.exp2(s - jnp.concatenate([m_new] * n_cb, axis=1))
            pv = _dot(p.astype(BF16), jnp.concatenate([vc, ones], axis=1))
            acc_ref[g] = jnp.concatenate([alpha, alpha], axis=1) * acc_ref[g] + pv
            m_ref[g] = m_new
        return carry

    lax.fori_loop(0, n_chunks, attn_chunk, 0)

    for g in range(N_KV_HEADS):
        out = acc_ref[g, :, 0:HEAD_DIM] / acc_ref[g, :, HEAD_DIM:2 * HEAD_DIM]
        for r in range(KV_GROUP):
            hd = g * KV_GROUP + r
            o_ref[:, hd * HEAD_DIM:(hd + 1) * HEAD_DIM] = out[r * tq:(r + 1) * tq].astype(BF16)


def _dsa_attn(main, kk, wi, batch, seq, tq, tk, top_k):
    t = main.shape[0]
    nq = seq // tq
    rows = KV_GROUP * tq
    return pl.pallas_call(
        functools.partial(_dsa_attn_kernel, tq, tk, top_k),
        grid=(batch, nq),
        in_specs=[
            pl.BlockSpec((tq, Q_W), lambda b, i: (b * nq + i, 0)),
            pl.BlockSpec((tq, QI_W), lambda b, i: (b * nq + i, (Q_W + 2 * KV_W) // QI_W)),
            pl.BlockSpec((tq, LANES), lambda b, i: (b * nq + i, 0)),
            pl.BlockSpec((seq, KV_W), lambda b, i: (b, Q_W // KV_W)),
            pl.BlockSpec((seq, KV_W), lambda b, i: (b, Q_W // KV_W + 1)),
            pl.BlockSpec((seq, 2 * LANES), lambda b, i: (b, 0)),
        ],
        out_specs=pl.BlockSpec((tq, Q_W), lambda b, i: (b * nq + i, 0)),
        out_shape=jax.ShapeDtypeStruct((t, Q_W), BF16),
        scratch_shapes=[
            pltpu.VMEM((tq, seq), F32),
            pltpu.VMEM((tq, seq), F32),
            pltpu.VMEM((tq, LANES), F32),
            pltpu.VMEM((IDX_HEADS, tq, LANES), F32),
            pltpu.VMEM((N_KV_HEADS, rows, HEAD_DIM), BF16),
            pltpu.VMEM((N_KV_HEADS, rows, LANES), F32),
            pltpu.VMEM((N_KV_HEADS, rows, 2 * HEAD_DIM), F32),
        ],
        compiler_params=_cparams(("arbitrary", "arbitrary")),
        name="dsa_attn",
    )(main, main, wi, main, main, kk)


def _tiles(seq, d_ff):
    tm = min(1024, seq)
    tm_ffn = min(2048, seq)
    tf = 256 if d_ff % 256 == 0 else d_ff
    tq = min(256, seq)
    tk = min(256, seq)
    return tm, tm_ffn, tf, tq, tk


def kernel(x, c, positions, ada_w, ada_b, norm_g, final_g, ffn_wgu, ffn_wd,
           pool_w_in, pool_w_grp, pool_scale, pool_w_out, dsa_w_in, dsa_w_out):
    batch, seq, d = x.shape
    depth = ada_w.shape[0]
    d_ff = ffn_wd.shape[2]
    t = batch * seq
    tm, tm_ffn, tf, tq, tk = _tiles(seq, d_ff)
    tn = min(512, d)
    top_k = min(TOPK_MAX, seq // 4)
    assert seq % tm == 0 and seq % tm_ffn == 0 and seq % tq == 0 and seq % tk == 0 and tk >= top_k
    assert dsa_w_in.shape[2] == DSA_MAIN_W + DSA_SMALL_W and DSA_MAIN_W % tn == 0

    rows = -(-batch // SUBLANES) * SUBLANES
    c8 = jnp.pad(c, ((0, rows - batch), (0, 0)))
    ada_n = ada_w.shape[2]
    ada = _ada(c8, ada_w, ada_b, 1024 if ada_n % 1024 == 0 else d)
    ada4 = ada.reshape(depth, rows, 3 * N_SUBLAYERS, d)

    tabs = _rope_tables(positions, tm) if depth > 1 else None

    h = x.reshape(t, d)
    for layer in range(depth):
        j = layer // 2
        h = _ffn(h, ada4, norm_g, ffn_wgu, ffn_wd, layer, 0, 0, seq, tm_ffn, tf)
        g_row = norm_g[layer, 1][None, :]
        if layer % 2 == 0:
            u = _norm_linear(h, ada4, g_row, pool_w_in, j, layer, 1, seq, tm, tn)
            a = _pool_core(u, pool_w_grp, pool_scale, j, batch, seq)
            h = _linear_residual(a, pool_w_out, j, h, ada4, layer, 1, seq, tm, tn)
        else:
            w_small = _dsa_small_weights(dsa_w_in[j, :, DSA_MAIN_W:])
            main, kk, wi = _dsa_proj(h, ada4, g_row, dsa_w_in, j, w_small, tabs,
                                     layer, 1, seq, tm, tn)
            o = _dsa_attn(main, kk, wi, batch, seq, tq, tk, top_k)
            h = _linear_residual(o, dsa_w_out, j, h, ada4, layer, 1, seq, tm, tn)
        last = layer == depth - 1
        h = _ffn(h, ada4, norm_g, ffn_wgu, ffn_wd, layer, 1, 2, seq, tm_ffn, tf,
                 final_g=final_g if last else None)
    return h.reshape(batch, seq, d)
```

```python
import functools

import jax
import jax.numpy as jnp
import numpy as np
from jax import lax
from jax.experimental import pallas as pl
from jax.experimental.pallas import tpu as pltpu

F32 = jnp.float32
BF16 = jnp.bfloat16

N_SUBLAYERS = 3
POOL_WINDOWS = (2, 4, 8, 16)
N_HEADS = 16
N_KV_HEADS = 4
KV_GROUP = N_HEADS // N_KV_HEADS
HEAD_DIM = 128
ROT_DIM = HEAD_DIM // 4
ROPE_THETA = 500000.0
IDX_HEADS = 16
IDX_DIM = 64
IDX_ROT_DIM = IDX_DIM // 4
TOPK_MAX = 256
EPS = 1e-6
NEG = -1e30
Q_W = N_HEADS * HEAD_DIM
KV_W = N_KV_HEADS * HEAD_DIM
QI_W = IDX_HEADS * IDX_DIM
DSA_MAIN_W = Q_W + 2 * KV_W + QI_W
DSA_SMALL_W = IDX_DIM + IDX_HEADS
Q_PRESCALE = HEAD_DIM ** -0.5 * 1.4426950408889634

LANES = 128
SUBLANES = 8
VMEM_LIMIT_BYTES = 56 * 1024 * 1024
INT32_MIN = -(2 ** 31)


def _cparams(semantics):
    return pltpu.CompilerParams(dimension_semantics=semantics,
                                vmem_limit_bytes=VMEM_LIMIT_BYTES)


def _rmsnorm(x, g):
    ms = jnp.mean(x * x, axis=-1, keepdims=True)
    return x * lax.rsqrt(ms + EPS) * g


def _rmsnorm_rows(ref, rows, g_ref):
    ref[rows, :] = _rmsnorm(ref[rows, :], g_ref[...])


NORM_SLAB = 16
NORM_UNROLL = 8
FFN_ROW_SLAB = 128
FFN_SLAB_SLOTS = 4
FFN_DOWN_COLS = 256


def _for_row_slabs(n_rows, slab, body, unroll=1):
    def step(r, carry):
        body(pl.ds(pl.multiple_of(r * slab, slab), slab))
        return carry
    lax.fori_loop(0, n_rows // slab, step, 0, unroll=unroll)


def _write_norm_mod(x_ref, g_ref, m_ref, sub, y_ref, copy_ref=None):
    shift = m_ref[3 * sub + 0:3 * sub + 1, :]
    gs = g_ref[...] * (1.0 + m_ref[3 * sub + 1:3 * sub + 2, :])
    slab = min(NORM_SLAB, x_ref.shape[0])

    def norm_rows(rows):
        x = x_ref[rows, :]
        r = lax.rsqrt(jnp.mean(x * x, axis=-1, keepdims=True) + EPS)
        y_ref[rows, :] = ((x * r) * gs + shift).astype(BF16)
        if copy_ref is not None:
            copy_ref[rows, :] = x

    _for_row_slabs(x_ref.shape[0], slab, norm_rows, unroll=NORM_UNROLL)


def _silu(x):
    return x * jax.nn.sigmoid(x)


def _dot(a, b):
    return jnp.dot(a, b, preferred_element_type=F32)


def _dot_t(a, b):
    return lax.dot_general(a, b, (((1,), (1,)), ((), ())), preferred_element_type=F32)


def _ada_kernel(c_ref, w_ref, b_ref, o_ref):
    sc = _silu(c_ref[...]).astype(BF16)
    o_ref[...] = _dot(sc, w_ref[...].astype(BF16)) + b_ref[...]


def _ada(c8, ada_w, ada_b, tn):
    depth, d, n = ada_w.shape
    rows = c8.shape[0]
    return pl.pallas_call(
        _ada_kernel,
        grid=(depth, n // tn),
        in_specs=[
            pl.BlockSpec((rows, d), lambda l, j: (0, 0)),
            pl.BlockSpec((None, d, tn), lambda l, j: (l, 0, j)),
            pl.BlockSpec((None, 1, tn), lambda l, j: (l, 0, j)),
        ],
        out_specs=pl.BlockSpec((None, rows, tn), lambda l, j: (l, 0, j)),
        out_shape=jax.ShapeDtypeStruct((depth, rows, n), F32),
        compiler_params=_cparams(("arbitrary", "arbitrary")),
        name="ada",
    )(c8, ada_w, ada_b.reshape(depth, 1, n))


def _ffn_kernel(sub, final, rs, *refs):
    if final:
        (x_hbm, g_ref, m_ref, wg_ref, wu_ref, wd_ref, fg_ref, o_hbm,
         acc_ref, y_ref, xbuf, in_sem, out_sem) = refs
    else:
        (x_hbm, g_ref, m_ref, wg_ref, wu_ref, wd_ref, o_hbm,
         acc_ref, y_ref, xbuf, in_sem, out_sem) = refs
    k = pl.program_id(1)
    nk = pl.num_programs(1)
    tm, d_out = acc_ref.shape
    n_slabs = tm // rs
    n_slots = xbuf.shape[0]
    n_first = min(n_slots, n_slabs)
    row0 = pl.program_id(0) * tm

    def x_copy(s, tile_row0=row0):
        slot = s % n_slots
        return pltpu.make_async_copy(x_hbm.at[pl.ds(tile_row0 + s * rs, rs), :], xbuf.at[slot],
                                     in_sem.at[slot])

    def o_copy(s):
        return pltpu.make_async_copy(acc_ref.at[pl.ds(s * rs, rs), :],
                                     o_hbm.at[pl.ds(row0 + s * rs, rs), :], out_sem.at[s])

    @pl.when(k == 0)
    def _():
        @pl.when(pl.program_id(0) == 0)
        def _():
            for s in range(n_first):
                x_copy(s).start()

        for s in range(n_slabs):
            x_copy(s).wait()
            pl.when(pl.program_id(0) > 0)(o_copy(s).wait)
            rows = pl.ds(s * rs, rs)
            _write_norm_mod(xbuf.at[s % n_slots], g_ref, m_ref, sub, y_ref.at[rows, :],
                            copy_ref=acc_ref.at[rows, :])
            if s + n_slots < n_slabs:
                x_copy(s + n_slots).start()

    @pl.when(jnp.logical_and(k == nk - 1, pl.program_id(0) + 1 < pl.num_programs(0)))
    def _():
        for s in range(n_first):
            x_copy(s, row0 + tm).start()

    y = y_ref[...]
    g = _dot(y, wg_ref[...].astype(BF16))
    u = _dot(y, wu_ref[...].astype(BF16))
    a = (_silu(g) * u).astype(BF16)
    half_gate = 0.5 * m_ref[3 * sub + 2:3 * sub + 3, :]
    nc = min(FFN_DOWN_COLS, d_out)
    for n in range(d_out // nc):
        cols = slice(n * nc, (n + 1) * nc)
        acc_ref[:, cols] += half_gate[:, cols] * _dot(a, wd_ref[:, cols].astype(BF16))

    @pl.when(k == nk - 1)
    def _():
        for s in range(n_slabs):
            if final:
                _for_row_slabs(
                    rs, min(NORM_SLAB, rs),
                    lambda rows, s=s: _rmsnorm_rows(acc_ref.at[pl.ds(s * rs, rs), :], rows, fg_ref),
                    unroll=NORM_UNROLL)
            o_copy(s).start()

        @pl.when(pl.program_id(0) == pl.num_programs(0) - 1)
        def _():
            for s in range(n_slabs):
                o_copy(s).wait()


def _ffn(h, ada4, norm_g, ffn_wgu, ffn_wd, layer, which, sub, seq, tm, tf, final_g=None):
    t, d = h.shape
    d_ff = ffn_wd.shape[2]
    nk = d_ff // tf
    tpb = seq // tm
    rs = min(FFN_ROW_SLAB, tm)
    final = final_g is not None
    in_specs = [
        pl.BlockSpec(memory_space=pl.ANY),
        pl.BlockSpec((1, d), lambda i, k: (0, 0)),
        pl.BlockSpec((None, None, 3 * N_SUBLAYERS, d), lambda i, k: (layer, i // tpb, 0, 0)),
        pl.BlockSpec((None, None, d, tf), lambda i, k: (layer, which, 0, k)),
        pl.BlockSpec((None, None, d, tf), lambda i, k: (layer, which, 0, nk + k)),
        pl.BlockSpec((None, None, tf, d), lambda i, k: (layer, which, k, 0)),
    ]
    args = [h, norm_g[layer, sub][None, :], ada4, ffn_wgu, ffn_wgu, ffn_wd]
    if final:
        in_specs.append(pl.BlockSpec((1, d), lambda i, k: (0, 0)))
        args.append(final_g[None, :])
    return pl.pallas_call(
        functools.partial(_ffn_kernel, sub, final, rs),
        grid=(t // tm, nk),
        in_specs=in_specs,
        out_specs=pl.BlockSpec(memory_space=pl.ANY),
        out_shape=jax.ShapeDtypeStruct((t, d), F32),
        scratch_shapes=[
            pltpu.VMEM((tm, d), F32),
            pltpu.VMEM((tm, d), BF16),
            pltpu.VMEM((FFN_SLAB_SLOTS, rs, d), F32),
            pltpu.SemaphoreType.DMA((FFN_SLAB_SLOTS,)),
            pltpu.SemaphoreType.DMA((tm // rs,)),
        ],
        compiler_params=_cparams(("arbitrary", "arbitrary")),
        name=f"ffn_l{layer}_{which}",
    )(*args)


def _keep_bf16_weight(w_ref, wbf_ref):
    j = pl.program_id(1)

    @pl.when(pl.program_id(0) == 0)
    def _():
        wbf_ref[j] = w_ref[...].astype(BF16)

    return wbf_ref[j]


def _resident_weight_spec(widx, kdim, tn, nj):
    return pl.BlockSpec((None, kdim, tn), lambda i, j: (widx, 0, jnp.where(i == 0, j, nj - 1)))


def _norm_linear_kernel(sub, x_ref, g_ref, m_ref, w_ref, o_ref, y_ref, wbf_ref):
    @pl.when(pl.program_id(1) == 0)
    def _():
        _write_norm_mod(x_ref, g_ref, m_ref, sub, y_ref)

    o_ref[...] = _dot(y_ref[...], _keep_bf16_weight(w_ref, wbf_ref))


def _norm_linear(h, ada4, g_row, w, widx, layer, sub, seq, tm, tn):
    t, d = h.shape
    n = w.shape[2]
    tpb = seq // tm
    return pl.pallas_call(
        functools.partial(_norm_linear_kernel, sub),
        grid=(t // tm, n // tn),
        in_specs=[
            pl.BlockSpec((tm, d), lambda i, j: (i, 0)),
            pl.BlockSpec((1, d), lambda i, j: (0, 0)),
            pl.BlockSpec((None, None, 3 * N_SUBLAYERS, d), lambda i, j: (layer, i // tpb, 0, 0)),
            _resident_weight_spec(widx, d, tn, n // tn),
        ],
        out_specs=pl.BlockSpec((tm, tn), lambda i, j: (i, j)),
        out_shape=jax.ShapeDtypeStruct((t, n), F32),
        scratch_shapes=[pltpu.VMEM((tm, d), BF16), pltpu.VMEM((n // tn, d, tn), BF16)],
        compiler_params=_cparams(("arbitrary", "arbitrary")),
        name=f"norm_linear_l{layer}",
    )(h, g_row, ada4, w)


def _pool_kernel(u_ref, w_ref, ls_ref, o_ref):
    g = pl.program_id(1)
    u = u_ref[...]
    t = lax.broadcasted_iota(jnp.int32, u.shape, 0)

    def shifted(v, d):
        return jnp.where(t >= d, pltpu.roll(v, d, 0), 0.0)

    for gi, win in enumerate(POOL_WINDOWS):
        @pl.when(g == gi)
        def _(win=win):
            acc = u
            d = 1
            while d < win:
                acc = acc + shifted(acc, d)
                d *= 2
            cnt = jnp.minimum(t + 1, win).astype(F32)
            p = (acc / cnt - u).astype(BF16)
            v = _dot(p, w_ref[...].astype(BF16)) * ls_ref[...]
            o_ref[...] = v.astype(BF16)


def _pool_core(u, w_grp, ls, widx, batch, seq):
    t, d = u.shape
    _, groups, c, _ = w_grp.shape
    return pl.pallas_call(
        _pool_kernel,
        grid=(batch, groups),
        in_specs=[
            pl.BlockSpec((seq, c), lambda b, g: (b, g)),
            pl.BlockSpec((None, None, c, c), lambda b, g: (widx, g, 0, 0)),
            pl.BlockSpec((None, 1, c), lambda b, g: (widx, 0, g)),
        ],
        out_specs=pl.BlockSpec((seq, c), lambda b, g: (b, g)),
        out_shape=jax.ShapeDtypeStruct((t, d), BF16),
        compiler_params=_cparams(("arbitrary", "arbitrary")),
        name="pool_core",
    )(u, w_grp, ls.reshape(ls.shape[0], 1, d))


def _linear_residual_kernel(sub, a_ref, w_ref, h_ref, m_ref, o_ref, wbf_ref):
    gate = m_ref[3 * sub + 2:3 * sub + 3, :]
    o_ref[...] = h_ref[...] + gate * _dot(a_ref[...], _keep_bf16_weight(w_ref, wbf_ref))


def _linear_residual(a, w, widx, h, ada4, layer, sub, seq, tm, tn):
    t, kdim = a.shape
    n = w.shape[2]
    tpb = seq // tm
    return pl.pallas_call(
        functools.partial(_linear_residual_kernel, sub),
        grid=(t // tm, n // tn),
        in_specs=[
            pl.BlockSpec((tm, kdim), lambda i, j: (i, 0)),
            _resident_weight_spec(widx, kdim, tn, n // tn),
            pl.BlockSpec((tm, tn), lambda i, j: (i, j)),
            pl.BlockSpec((None, None, 3 * N_SUBLAYERS, tn), lambda i, j: (layer, i // tpb, 0, j)),
        ],
        out_specs=pl.BlockSpec((tm, tn), lambda i, j: (i, j)),
        out_shape=jax.ShapeDtypeStruct((t, n), F32),
        scratch_shapes=[pltpu.VMEM((n // tn, kdim, tn), BF16)],
        compiler_params=_cparams(("arbitrary", "arbitrary")),
        name=f"linear_residual_l{layer}",
    )(a, w, h, ada4)


ROPE_SET_Q, ROPE_SET_K, ROPE_SET_NONE, ROPE_SET_IDX = 0, 1, 2, 3
HALF_LANES = LANES // 2


def _rope_table_kernel(pos_ref, invf_ref, o_ref):
    pos = pos_ref[...].astype(F32)
    lane = lax.broadcasted_iota(jnp.int32, (pos.shape[0], LANES), 1)
    first = lane < HALF_LANES

    def tables(invf, is_rot):
        ang = pos * invf
        c = jnp.where(is_rot, jnp.cos(ang), 1.0)
        s = jnp.where(is_rot, jnp.where(first, -jnp.sin(ang), jnp.sin(ang)), 0.0)
        return c, s

    cb, sb = tables(invf_ref[:, 0:LANES], (lane & (HALF_LANES - 1)) < ROT_DIM // 2)
    ci, si = tables(invf_ref[:, LANES:2 * LANES], (lane & (HALF_LANES // 2 - 1)) < IDX_ROT_DIM // 2)
    o_ref[2 * ROPE_SET_Q] = cb * Q_PRESCALE
    o_ref[2 * ROPE_SET_Q + 1] = sb * Q_PRESCALE
    o_ref[2 * ROPE_SET_K] = cb
    o_ref[2 * ROPE_SET_K + 1] = sb
    o_ref[2 * ROPE_SET_NONE] = jnp.ones_like(cb)
    o_ref[2 * ROPE_SET_NONE + 1] = jnp.zeros_like(cb)
    o_ref[2 * ROPE_SET_IDX] = ci
    o_ref[2 * ROPE_SET_IDX + 1] = si


def _rope_tables(positions, tm):
    t = positions.size
    lane = jnp.arange(LANES)

    def inv_freq(rot_dim):
        half = rot_dim // 2
        return ROPE_THETA ** (-jnp.arange(half, dtype=F32) / half)

    invf = jnp.concatenate([inv_freq(ROT_DIM)[lane % (ROT_DIM // 2)],
                            inv_freq(IDX_ROT_DIM)[lane % (IDX_ROT_DIM // 2)]])[None, :]
    return pl.pallas_call(
        _rope_table_kernel,
        grid=(t // tm,),
        in_specs=[
            pl.BlockSpec((tm, 1), lambda i: (i, 0)),
            pl.BlockSpec((1, 2 * LANES), lambda i: (0, 0)),
        ],
        out_specs=pl.BlockSpec((8, tm, LANES), lambda i: (0, i, 0)),
        out_shape=jax.ShapeDtypeStruct((8, t, LANES), F32),
        compiler_params=_cparams(("arbitrary",)),
        name="rope_tables",
    )(positions.reshape(t, 1), invf)


def _rope(x, c, s):
    return x * c + pltpu.roll(x, HALF_LANES, 1) * s


PERM_BIG, PERM_NONE, PERM_IDX = 0, 1, 2


def _lane_permutations():
    h, p = ROT_DIM // 2, (HEAD_DIM - ROT_DIM) // 2
    big = np.concatenate([np.arange(0, h), np.arange(ROT_DIM, ROT_DIM + p),
                          np.arange(h, ROT_DIM), np.arange(ROT_DIM + p, HEAD_DIM)])
    lo, hi = _idx_lo_hi_dims()
    idx = np.concatenate([lo, IDX_DIM + lo, hi, IDX_DIM + hi])
    mats = np.zeros((3, LANES, LANES), np.float32)
    for kind, src in ((PERM_BIG, big), (PERM_NONE, np.arange(LANES)), (PERM_IDX, idx)):
        mats[kind, src, np.arange(LANES)] = 1.0
    return jnp.asarray(mats, BF16)


def _idx_lo_hi_dims():
    h, p = IDX_ROT_DIM // 2, (IDX_DIM - IDX_ROT_DIM) // 2
    lo = np.concatenate([np.arange(0, h), np.arange(IDX_ROT_DIM, IDX_ROT_DIM + p)])
    hi = np.concatenate([np.arange(h, IDX_ROT_DIM), np.arange(IDX_ROT_DIM + p, IDX_DIM)])
    return lo, hi


def _dsa_small_weights(w_tail):
    h, p = IDX_ROT_DIM // 2, (IDX_DIM - IDX_ROT_DIM) // 2
    klo = jnp.concatenate([w_tail[:, 0:h], w_tail[:, IDX_ROT_DIM:IDX_ROT_DIM + p]], axis=1)
    khi = jnp.concatenate([w_tail[:, h:IDX_ROT_DIM], w_tail[:, IDX_ROT_DIM + p:IDX_DIM]], axis=1)
    wi = jnp.pad(w_tail[:, IDX_DIM:], ((0, 0), (0, LANES - IDX_DIM - IDX_HEADS)))
    return jnp.concatenate([klo, khi, wi], axis=1).astype(BF16)


def _dsa_proj_kernel(sub, tn, x_hbm, g_ref, m_ref, w_ref, ws_ref, perm_ref, tab_ref,
                     o_ref, ok_ref, ow_ref, y_ref, wbf_ref, x_ref, x_sem):
    i = pl.program_id(0)
    j = pl.program_id(1)
    tm = x_ref.shape[0]
    first_k, first_v, first_qi = Q_W // tn, (Q_W + KV_W) // tn, (Q_W + 2 * KV_W) // tn

    def x_copy(tile):
        return pltpu.make_async_copy(x_hbm.at[pl.ds(tile * tm, tm), :], x_ref, x_sem.at[0])

    @pl.when(j == 0)
    def _():
        pl.when(i == 0)(x_copy(i).start)
        x_copy(i).wait()
        _write_norm_mod(x_ref, g_ref, m_ref, sub, y_ref)
        pl.when(i + 1 < pl.num_programs(0))(x_copy(i + 1).start)
        small = _dot(y_ref[...], ws_ref[...])
        by32 = pltpu.roll(small, HALF_LANES // 2, 1)
        by64 = pltpu.roll(small, HALF_LANES, 1)
        quarter = lax.broadcasted_iota(jnp.int32, small.shape, 1) // (HALF_LANES // 2)
        k_even = jnp.where(quarter == 0, small, jnp.where(quarter == 2, by32, 0.0))
        k_odd = jnp.where(quarter == 1, by32, jnp.where(quarter == 3, by64, 0.0))
        for s, kk in enumerate((k_even, k_odd)):
            ok_ref[:, s * LANES:(s + 1) * LANES] = _rope(
                kk, tab_ref[2 * ROPE_SET_IDX], tab_ref[2 * ROPE_SET_IDX + 1]).astype(BF16)
        ow_ref[...] = jnp.where(quarter == 0, by64, 0.0)

    @pl.when(pl.program_id(0) == 0)
    def _():
        kind = jnp.where(j < first_v, PERM_BIG, jnp.where(j < first_qi, PERM_NONE, PERM_IDX))
        perm = perm_ref[kind]
        for sl in range(tn // LANES):
            cols = slice(sl * LANES, (sl + 1) * LANES)
            wbf_ref[j, :, cols] = _dot(w_ref[:, cols].astype(BF16), perm).astype(BF16)

    rope_set = jnp.where(j < first_k, ROPE_SET_Q,
                         jnp.where(j < first_v, ROPE_SET_K,
                                   jnp.where(j < first_qi, ROPE_SET_NONE, ROPE_SET_IDX)))
    c = tab_ref[2 * rope_set]
    s = tab_ref[2 * rope_set + 1]
    y = y_ref[...]
    wide = 2 * LANES
    for part in range(tn // wide):
        acc = _dot(y, wbf_ref[j, :, part * wide:(part + 1) * wide])
        for sl in range(wide // LANES):
            cols = slice(part * wide + sl * LANES, part * wide + (sl + 1) * LANES)
            o_ref[:, cols] = _rope(acc[:, sl * LANES:(sl + 1) * LANES], c, s).astype(BF16)


def _dsa_proj(h, ada4, g_row, w_in, widx, w_small, tabs, layer, sub, seq, tm, tn):
    t, d = h.shape
    tpb = seq // tm
    return pl.pallas_call(
        functools.partial(_dsa_proj_kernel, sub, tn),
        grid=(t // tm, DSA_MAIN_W // tn),
        in_specs=[
            pl.BlockSpec(memory_space=pl.ANY),
            pl.BlockSpec((1, d), lambda i, j: (0, 0)),
            pl.BlockSpec((None, None, 3 * N_SUBLAYERS, d), lambda i, j: (layer, i // tpb, 0, 0)),
            _resident_weight_spec(widx, d, tn, DSA_MAIN_W // tn),
            pl.BlockSpec((d, LANES), lambda i, j: (0, 0)),
            pl.BlockSpec((3, LANES, LANES), lambda i, j: (0, 0, 0)),
            pl.BlockSpec((8, tm, LANES), lambda i, j: (0, i, 0)),
        ],
        out_specs=[
            pl.BlockSpec((tm, tn), lambda i, j: (i, j)),
            pl.BlockSpec((tm, 2 * LANES), lambda i, j: (i, 0)),
            pl.BlockSpec((tm, LANES), lambda i, j: (i, 0)),
        ],
        out_shape=[
            jax.ShapeDtypeStruct((t, DSA_MAIN_W), BF16),
            jax.ShapeDtypeStruct((t, 2 * LANES), BF16),
            jax.ShapeDtypeStruct((t, LANES), F32),
        ],
        scratch_shapes=[pltpu.VMEM((tm, d), BF16),
                        pltpu.VMEM((DSA_MAIN_W // tn, d, tn), BF16),
                        pltpu.VMEM((tm, d), F32),
                        pltpu.SemaphoreType.DMA((1,))],
        compiler_params=_cparams(("arbitrary", "arbitrary")),
        name=f"dsa_proj_l{layer}",
    )(h, g_row, ada4, w_in, w_small, _lane_permutations(), tabs)


def _dsa_attn_kernel(tq, tk, top_k, q_ref, qi_ref, wi_ref, k_ref, v_ref, kk_ref, o_ref,
                     score_ref, bias_ref, thr_ref, wib_ref, qg_ref, m_ref, acc_ref):
    i = pl.program_id(1)
    n_chunks = ((i + 1) * tq + tk - 1) // tk
    max_chunks = score_ref.shape[1] // tk
    t_pos = i * tq + lax.broadcasted_iota(jnp.int32, (tq, tk), 0)
    k_iota = lax.broadcasted_iota(jnp.int32, (tq, tk), 1)
    n_cb = tk // LANES

    for hh in range(IDX_HEADS):
        wib_ref[hh] = jnp.broadcast_to(wi_ref[:, hh:hh + 1], (tq, LANES))
    for g in range(N_KV_HEADS):
        for r in range(KV_GROUP):
            hd = g * KV_GROUP + r
            qg_ref[g, r * tq:(r + 1) * tq, :] = q_ref[:, hd * HEAD_DIM:(hd + 1) * HEAD_DIM]

    def score_chunk(c, carry):
        off = pl.multiple_of(c * tk, tk)
        k_even = kk_ref[pl.ds(off, tk), 0:LANES]
        k_odd = kk_ref[pl.ds(off, tk), LANES:2 * LANES]
        acc = [jnp.zeros((tq, LANES), F32) for _ in range(n_cb)]
        for pair in range(IDX_HEADS // 2):
            qp = qi_ref[:, pair * LANES:(pair + 1) * LANES]
            for par, kop in enumerate((k_even, k_odd)):
                r = _dot_t(qp, kop)
                w = wib_ref[2 * pair + par]
                for cb in range(n_cb):
                    acc[cb] = acc[cb] + w * jnp.maximum(r[:, cb * LANES:(cb + 1) * LANES], 0.0)
        sc = jnp.concatenate(acc, axis=1)
        score_ref[:, pl.ds(off, tk)] = jnp.where(k_iota + off <= t_pos, sc, NEG)
        return carry

    lax.fori_loop(0, n_chunks, score_chunk, 0)

    def key_to_f32(key):
        return pltpu.bitcast(key ^ ((key >> 31) & 0x7FFFFFFF), F32)

    k_f = float(top_k)

    def select(n_static):
        def bit_step(n, thr_key):
            cand = thr_key + lax.shift_left(jnp.int32(1), 31 - n)
            thr = key_to_f32(cand)
            cnt = jnp.zeros((tq, LANES), F32)
            for cb in range(n_static * n_cb):
                s = score_ref[:, cb * LANES:(cb + 1) * LANES]
                cnt = cnt + jnp.where(s >= thr, 1.0, 0.0)
            total = jnp.sum(cnt, axis=1, keepdims=True)
            return jnp.where(total >= k_f, cand, thr_key)

        thr_key = lax.fori_loop(0, 32, bit_step, jnp.full((tq, LANES), INT32_MIN, jnp.int32),
                                unroll=4)
        thr_ref[...] = key_to_f32(thr_key)

    for n_static in range(1, max_chunks + 1):
        pl.when(n_chunks == n_static)(functools.partial(select, n_static))

    def bias_chunk(c, carry):
        off = pl.multiple_of(c * tk, tk)
        thr = thr_ref[...]
        causal = k_iota + off <= t_pos
        for cb in range(n_cb):
            cols = pl.ds(off + cb * LANES, LANES)
            sel = score_ref[:, cols] >= thr
            bias_ref[:, cols] = jnp.where(
                sel, jnp.where(causal[:, cb * LANES:(cb + 1) * LANES], 0.0, NEG), NEG)
        return carry

    lax.fori_loop(0, n_chunks, bias_chunk, 0)

    def count_rows(pred):
        def body(c, cnt):
            off = pl.multiple_of(c * tk, tk)
            thr = thr_ref[...]
            for cb in range(n_cb):
                s = score_ref[:, pl.ds(off + cb * LANES, LANES)]
                cnt = cnt + jnp.where(pred(s, thr), 1.0, 0.0)
            return cnt
        cnt = lax.fori_loop(0, n_chunks, body, jnp.zeros((tq, LANES), F32))
        return jnp.sum(cnt, axis=1, keepdims=True)

    n_ge = count_rows(lambda s, thr: s >= thr)

    @pl.when(jnp.max(n_ge) > k_f)
    def _():
        n_take = k_f - count_rows(lambda s, thr: s > thr)
        before = jnp.where(
            lax.broadcasted_iota(jnp.int32, (tk, tk), 0) < lax.broadcasted_iota(jnp.int32, (tk, tk), 1),
            1.0, 0.0).astype(BF16)

        def tie_chunk(c, seen):
            off = pl.multiple_of(c * tk, tk)
            sc = score_ref[:, pl.ds(off, tk)]
            thr = jnp.concatenate([thr_ref[...]] * n_cb, axis=1)
            eq = jnp.where(sc == thr, 1.0, 0.0)
            rank = _dot(eq.astype(BF16), before) + seen
            keep = jnp.where(sc > thr, 1.0, jnp.where(rank < n_take, eq, 0.0))
            causal = k_iota + off <= t_pos
            bias_ref[:, pl.ds(off, tk)] = jnp.where(causal, jnp.where(keep > 0.0, 0.0, NEG), NEG)
            return seen + jnp.sum(eq, axis=1, keepdims=True)

        lax.fori_loop(0, n_chunks, tie_chunk, jnp.zeros((tq, 1), F32))

    rows = KV_GROUP * tq
    m_ref[...] = jnp.full(m_ref.shape, NEG, F32)
    acc_ref[...] = jnp.zeros(acc_ref.shape, F32)
    def attend(off, width):
        ones = jnp.ones((width, LANES), BF16)
        b = bias_ref[:, pl.ds(off, width)]
        for g in range(N_KV_HEADS):
            kc = k_ref[pl.ds(off, width), g * HEAD_DIM:(g + 1) * HEAD_DIM]
            vc = v_ref[pl.ds(off, width), g * HEAD_DIM:(g + 1) * HEAD_DIM]
            s = _dot_t(qg_ref[g], kc)
            s = (s.reshape(KV_GROUP, tq, width) + b[None]).reshape(rows, width)
            m_old = m_ref[g]
            m_new = jnp.maximum(m_old, jnp.max(s, axis=1, keepdims=True))
            alpha = jnp.exp2(m_old - m_new)
            p = jnp.exp2(s - jnp.concatenate([m_new] * (width // LANES), axis=1))
            pv = _dot(p.astype(BF16), jnp.concatenate([vc, ones], axis=1))
            acc_ref[g] = jnp.concatenate([alpha, alpha], axis=1) * acc_ref[g] + pv
            m_ref[g] = m_new

    def attn_chunk(c, carry):
        attend(pl.multiple_of(c * tk, tk), tk)
        return carry

    lax.fori_loop(0, n_chunks, attn_chunk, 0)

    for g in range(N_KV_HEADS):
        out = acc_ref[g, :, 0:HEAD_DIM] / acc_ref[g, :, HEAD_DIM:2 * HEAD_DIM]
        for r in range(KV_GROUP):
            hd = g * KV_GROUP + r
            o_ref[:, hd * HEAD_DIM:(hd + 1) * HEAD_DIM] = out[r * tq:(r + 1) * tq].astype(BF16)


def _dsa_attn(main, kk, wi, batch, seq, tq, tk, top_k):
    t = main.shape[0]
    nq = seq // tq
    rows = KV_GROUP * tq
    return pl.pallas_call(
        functools.partial(_dsa_attn_kernel, tq, tk, top_k),
        grid=(batch, nq),
        in_specs=[
            pl.BlockSpec((tq, Q_W), lambda b, i: (b * nq + i, 0)),
            pl.BlockSpec((tq, QI_W), lambda b, i: (b * nq + i, (Q_W + 2 * KV_W) // QI_W)),
            pl.BlockSpec((tq, LANES), lambda b, i: (b * nq + i, 0)),
            pl.BlockSpec((seq, KV_W), lambda b, i: (b, Q_W // KV_W)),
            pl.BlockSpec((seq, KV_W), lambda b, i: (b, Q_W // KV_W + 1)),
            pl.BlockSpec((seq, 2 * LANES), lambda b, i: (b, 0)),
        ],
        out_specs=pl.BlockSpec((tq, Q_W), lambda b, i: (b * nq + i, 0)),
        out_shape=jax.ShapeDtypeStruct((t, Q_W), BF16),
        scratch_shapes=[
            pltpu.VMEM((tq, seq), F32),
            pltpu.VMEM((tq, seq), F32),
            pltpu.VMEM((tq, LANES), F32),
            pltpu.VMEM((IDX_HEADS, tq, LANES), F32),
            pltpu.VMEM((N_KV_HEADS, rows, HEAD_DIM), BF16),
            pltpu.VMEM((N_KV_HEADS, rows, LANES), F32),
            pltpu.VMEM((N_KV_HEADS, rows, 2 * HEAD_DIM), F32),
        ],
        compiler_params=_cparams(("arbitrary", "arbitrary")),
        name="dsa_attn",
    )(main, main, wi, main, main, kk)


def _tiles(seq, d_ff):
    tm = min(1024, seq)
    tm_ffn = min(2048, seq)
    tf = 256 if d_ff % 256 == 0 else d_ff
    tq = min(256, seq)
    tk = min(256, seq)
    return tm, tm_ffn, tf, tq, tk


def kernel(x, c, positions, ada_w, ada_b, norm_g, final_g, ffn_wgu, ffn_wd,
           pool_w_in, pool_w_grp, pool_scale, pool_w_out, dsa_w_in, dsa_w_out):
    batch, seq, d = x.shape
    depth = ada_w.shape[0]
    d_ff = ffn_wd.shape[2]
    t = batch * seq
    tm, tm_ffn, tf, tq, tk = _tiles(seq, d_ff)
    tn = min(512, d)
    top_k = min(TOPK_MAX, seq // 4)
    assert seq % tm == 0 and seq % tm_ffn == 0 and seq % tq == 0 and seq % tk == 0 and tk >= top_k
    assert dsa_w_in.shape[2] == DSA_MAIN_W + DSA_SMALL_W and DSA_MAIN_W % tn == 0

    rows = -(-batch // SUBLANES) * SUBLANES
    c8 = jnp.pad(c, ((0, rows - batch), (0, 0)))
    ada_n = ada_w.shape[2]
    ada = _ada(c8, ada_w, ada_b, 1024 if ada_n % 1024 == 0 else d)
    ada4 = ada.reshape(depth, rows, 3 * N_SUBLAYERS, d)

    tabs = _rope_tables(positions, tm) if depth > 1 else None

    h = x.reshape(t, d)
    for layer in range(depth):
        j = layer // 2
        h = _ffn(h, ada4, norm_g, ffn_wgu, ffn_wd, layer, 0, 0, seq, tm_ffn, tf)
        g_row = norm_g[layer, 1][None, :]
        if layer % 2 == 0:
            u = _norm_linear(h, ada4, g_row, pool_w_in, j, layer, 1, seq, tm, tn)
            a = _pool_core(u, pool_w_grp, pool_scale, j, batch, seq)
            h = _linear_residual(a, pool_w_out, j, h, ada4, layer, 1, seq, tm, tn)
        else:
            w_small = _dsa_small_weights(dsa_w_in[j, :, DSA_MAIN_W:])
            main, kk, wi = _dsa_proj(h, ada4, g_row, dsa_w_in, j, w_small, tabs,
                                     layer, 1, seq, tm, tn)
            o = _dsa_attn(main, kk, wi, batch, seq, tq, tk, top_k)
            h = _linear_residual(o, dsa_w_out, j, h, ada4, layer, 1, seq, tm, tn)
        last = layer == depth - 1
        h = _ffn(h, ada4, norm_g, ffn_wgu, ffn_wd, layer, 1, 2, seq, tm_ffn, tf,
                 final_g=final_g if last else None)
    return h.reshape(batch, seq, d)
```

```python
import functools

import jax
import jax.numpy as jnp
import numpy as np
from jax import lax
from jax.experimental import pallas as pl
from jax.experimental.pallas import tpu as pltpu

F32 = jnp.float32
BF16 = jnp.bfloat16

N_SUBLAYERS = 3
POOL_WINDOWS = (2, 4, 8, 16)
N_HEADS = 16
N_KV_HEADS = 4
KV_GROUP = N_HEADS // N_KV_HEADS
HEAD_DIM = 128
ROT_DIM = HEAD_DIM // 4
ROPE_THETA = 500000.0
IDX_HEADS = 16
IDX_DIM = 64
IDX_ROT_DIM = IDX_DIM // 4
TOPK_MAX = 256
EPS = 1e-6
NEG = -1e30
Q_W = N_HEADS * HEAD_DIM
KV_W = N_KV_HEADS * HEAD_DIM
QI_W = IDX_HEADS * IDX_DIM
DSA_MAIN_W = Q_W + 2 * KV_W + QI_W
DSA_SMALL_W = IDX_DIM + IDX_HEADS
Q_PRESCALE = HEAD_DIM ** -0.5 * 1.4426950408889634

LANES = 128
SUBLANES = 8
VMEM_LIMIT_BYTES = 56 * 1024 * 1024
INT32_MIN = -(2 ** 31)


def _cparams(semantics):
    return pltpu.CompilerParams(dimension_semantics=semantics,
                                vmem_limit_bytes=VMEM_LIMIT_BYTES)


def _rmsnorm(x, g):
    ms = jnp.mean(x * x, axis=-1, keepdims=True)
    return x * lax.rsqrt(ms + EPS) * g


def _rmsnorm_rows(ref, rows, g_ref):
    ref[rows, :] = _rmsnorm(ref[rows, :], g_ref[...])


NORM_SLAB = 16
NORM_UNROLL = 8
FFN_ROW_SLAB = 128
FFN_SLAB_SLOTS = 4
FFN_DOWN_COLS = 256


def _for_row_slabs(n_rows, slab, body, unroll=1):
    def step(r, carry):
        body(pl.ds(pl.multiple_of(r * slab, slab), slab))
        return carry
    lax.fori_loop(0, n_rows // slab, step, 0, unroll=unroll)


def _write_norm_mod(x_ref, g_ref, m_ref, sub, y_ref, copy_ref=None):
    shift = m_ref[3 * sub + 0:3 * sub + 1, :]
    gs = g_ref[...] * (1.0 + m_ref[3 * sub + 1:3 * sub + 2, :])
    slab = min(NORM_SLAB, x_ref.shape[0])

    def norm_rows(rows):
        x = x_ref[rows, :]
        r = lax.rsqrt(jnp.mean(x * x, axis=-1, keepdims=True) + EPS)
        y_ref[rows, :] = ((x * r) * gs + shift).astype(BF16)
        if copy_ref is not None:
            copy_ref[rows, :] = x

    _for_row_slabs(x_ref.shape[0], slab, norm_rows, unroll=NORM_UNROLL)


def _silu(x):
    return x * jax.nn.sigmoid(x)


def _dot(a, b):
    return jnp.dot(a, b, preferred_element_type=F32)


def _dot_t(a, b):
    return lax.dot_general(a, b, (((1,), (1,)), ((), ())), preferred_element_type=F32)


def _ada_kernel(c_ref, w_ref, b_ref, o_ref):
    sc = _silu(c_ref[...]).astype(BF16)
    o_ref[...] = _dot(sc, w_ref[...].astype(BF16)) + b_ref[...]


def _ada(c8, ada_w, ada_b, tn):
    depth, d, n = ada_w.shape
    rows = c8.shape[0]
    return pl.pallas_call(
        _ada_kernel,
        grid=(depth, n // tn),
        in_specs=[
            pl.BlockSpec((rows, d), lambda l, j: (0, 0)),
            pl.BlockSpec((None, d, tn), lambda l, j: (l, 0, j)),
            pl.BlockSpec((None, 1, tn), lambda l, j: (l, 0, j)),
        ],
        out_specs=pl.BlockSpec((None, rows, tn), lambda l, j: (l, 0, j)),
        out_shape=jax.ShapeDtypeStruct((depth, rows, n), F32),
        compiler_params=_cparams(("arbitrary", "arbitrary")),
        name="ada",
    )(c8, ada_w, ada_b.reshape(depth, 1, n))


def _ffn_kernel(sub, final, rs, *refs):
    if final:
        (x_hbm, g_ref, m_ref, wg_ref, wu_ref, wd_ref, fg_ref, o_hbm,
         acc_ref, y_ref, xbuf, in_sem, out_sem) = refs
    else:
        (x_hbm, g_ref, m_ref, wg_ref, wu_ref, wd_ref, o_hbm,
         acc_ref, y_ref, xbuf, in_sem, out_sem) = refs
    k = pl.program_id(1)
    nk = pl.num_programs(1)
    tm, d_out = acc_ref.shape
    n_slabs = tm // rs
    n_slots = xbuf.shape[0]
    n_first = min(n_slots, n_slabs)
    row0 = pl.program_id(0) * tm

    def x_copy(s, tile_row0=row0):
        slot = s % n_slots
        return pltpu.make_async_copy(x_hbm.at[pl.ds(tile_row0 + s * rs, rs), :], xbuf.at[slot],
                                     in_sem.at[slot])

    def o_copy(s):
        return pltpu.make_async_copy(acc_ref.at[pl.ds(s * rs, rs), :],
                                     o_hbm.at[pl.ds(row0 + s * rs, rs), :], out_sem.at[s])

    @pl.when(k == 0)
    def _():
        @pl.when(pl.program_id(0) == 0)
        def _():
            for s in range(n_first):
                x_copy(s).start()

        for s in range(n_slabs):
            x_copy(s).wait()
            pl.when(pl.program_id(0) > 0)(o_copy(s).wait)
            rows = pl.ds(s * rs, rs)
            _write_norm_mod(xbuf.at[s % n_slots], g_ref, m_ref, sub, y_ref.at[rows, :],
                            copy_ref=acc_ref.at[rows, :])
            if s + n_slots < n_slabs:
                x_copy(s + n_slots).start()

    @pl.when(jnp.logical_and(k == nk - 1, pl.program_id(0) + 1 < pl.num_programs(0)))
    def _():
        for s in range(n_first):
            x_copy(s, row0 + tm).start()

    y = y_ref[...]
    g = _dot(y, wg_ref[...].astype(BF16))
    u = _dot(y, wu_ref[...].astype(BF16))
    a = (_silu(g) * u).astype(BF16)
    half_gate = 0.5 * m_ref[3 * sub + 2:3 * sub + 3, :]
    nc = min(FFN_DOWN_COLS, d_out)
    for n in range(d_out // nc):
        cols = slice(n * nc, (n + 1) * nc)
        acc_ref[:, cols] += half_gate[:, cols] * _dot(a, wd_ref[:, cols].astype(BF16))

    @pl.when(k == nk - 1)
    def _():
        for s in range(n_slabs):
            if final:
                _for_row_slabs(
                    rs, min(NORM_SLAB, rs),
                    lambda rows, s=s: _rmsnorm_rows(acc_ref.at[pl.ds(s * rs, rs), :], rows, fg_ref),
                    unroll=NORM_UNROLL)
            o_copy(s).start()

        @pl.when(pl.program_id(0) == pl.num_programs(0) - 1)
        def _():
            for s in range(n_slabs):
                o_copy(s).wait()


def _ffn(h, ada4, norm_g, ffn_wgu, ffn_wd, layer, which, sub, seq, tm, tf, final_g=None):
    t, d = h.shape
    d_ff = ffn_wd.shape[2]
    nk = d_ff // tf
    tpb = seq // tm
    rs = min(FFN_ROW_SLAB, tm)
    final = final_g is not None
    in_specs = [
        pl.BlockSpec(memory_space=pl.ANY),
        pl.BlockSpec((1, d), lambda i, k: (0, 0)),
        pl.BlockSpec((None, None, 3 * N_SUBLAYERS, d), lambda i, k: (layer, i // tpb, 0, 0)),
        pl.BlockSpec((None, None, d, tf), lambda i, k: (layer, which, 0, k)),
        pl.BlockSpec((None, None, d, tf), lambda i, k: (layer, which, 0, nk + k)),
        pl.BlockSpec((None, None, tf, d), lambda i, k: (layer, which, k, 0)),
    ]
    args = [h, norm_g[layer, sub][None, :], ada4, ffn_wgu, ffn_wgu, ffn_wd]
    if final:
        in_specs.append(pl.BlockSpec((1, d), lambda i, k: (0, 0)))
        args.append(final_g[None, :])
    return pl.pallas_call(
        functools.partial(_ffn_kernel, sub, final, rs),
        grid=(t // tm, nk),
        in_specs=in_specs,
        out_specs=pl.BlockSpec(memory_space=pl.ANY),
        out_shape=jax.ShapeDtypeStruct((t, d), F32),
        scratch_shapes=[
            pltpu.VMEM((tm, d), F32),
            pltpu.VMEM((tm, d), BF16),
            pltpu.VMEM((FFN_SLAB_SLOTS, rs, d), F32),
            pltpu.SemaphoreType.DMA((FFN_SLAB_SLOTS,)),
            pltpu.SemaphoreType.DMA((tm // rs,)),
        ],
        compiler_params=_cparams(("arbitrary", "arbitrary")),
        name=f"ffn_l{layer}_{which}",
    )(*args)


def _keep_bf16_weight(w_ref, wbf_ref):
    j = pl.program_id(1)

    @pl.when(pl.program_id(0) == 0)
    def _():
        wbf_ref[j] = w_ref[...].astype(BF16)

    return wbf_ref[j]


def _resident_weight_spec(widx, kdim, tn, nj):
    return pl.BlockSpec((None, kdim, tn), lambda i, j: (widx, 0, jnp.where(i == 0, j, nj - 1)))


def _norm_linear_kernel(sub, x_ref, g_ref, m_ref, w_ref, o_ref, y_ref, wbf_ref):
    @pl.when(pl.program_id(1) == 0)
    def _():
        _write_norm_mod(x_ref, g_ref, m_ref, sub, y_ref)

    o_ref[...] = _dot(y_ref[...], _keep_bf16_weight(w_ref, wbf_ref))


def _norm_linear(h, ada4, g_row, w, widx, layer, sub, seq, tm, tn):
    t, d = h.shape
    n = w.shape[2]
    tpb = seq // tm
    return pl.pallas_call(
        functools.partial(_norm_linear_kernel, sub),
        grid=(t // tm, n // tn),
        in_specs=[
            pl.BlockSpec((tm, d), lambda i, j: (i, 0)),
            pl.BlockSpec((1, d), lambda i, j: (0, 0)),
            pl.BlockSpec((None, None, 3 * N_SUBLAYERS, d), lambda i, j: (layer, i // tpb, 0, 0)),
            _resident_weight_spec(widx, d, tn, n // tn),
        ],
        out_specs=pl.BlockSpec((tm, tn), lambda i, j: (i, j)),
        out_shape=jax.ShapeDtypeStruct((t, n), F32),
        scratch_shapes=[pltpu.VMEM((tm, d), BF16), pltpu.VMEM((n // tn, d, tn), BF16)],
        compiler_params=_cparams(("arbitrary", "arbitrary")),
        name=f"norm_linear_l{layer}",
    )(h, g_row, ada4, w)


def _pool_kernel(u_ref, w_ref, ls_ref, o_ref):
    g = pl.program_id(1)
    u = u_ref[...]
    t = lax.broadcasted_iota(jnp.int32, u.shape, 0)

    def shifted(v, d):
        return jnp.where(t >= d, pltpu.roll(v, d, 0), 0.0)

    for gi, win in enumerate(POOL_WINDOWS):
        @pl.when(g == gi)
        def _(win=win):
            acc = u
            d = 1
            while d < win:
                acc = acc + shifted(acc, d)
                d *= 2
            cnt = jnp.minimum(t + 1, win).astype(F32)
            p = (acc / cnt - u).astype(BF16)
            v = _dot(p, w_ref[...].astype(BF16)) * ls_ref[...]
            o_ref[...] = v.astype(BF16)


def _pool_core(u, w_grp, ls, widx, batch, seq):
    t, d = u.shape
    _, groups, c, _ = w_grp.shape
    return pl.pallas_call(
        _pool_kernel,
        grid=(batch, groups),
        in_specs=[
            pl.BlockSpec((seq, c), lambda b, g: (b, g)),
            pl.BlockSpec((None, None, c, c), lambda b, g: (widx, g, 0, 0)),
            pl.BlockSpec((None, 1, c), lambda b, g: (widx, 0, g)),
        ],
        out_specs=pl.BlockSpec((seq, c), lambda b, g: (b, g)),
        out_shape=jax.ShapeDtypeStruct((t, d), BF16),
        compiler_params=_cparams(("arbitrary", "arbitrary")),
        name="pool_core",
    )(u, w_grp, ls.reshape(ls.shape[0], 1, d))


def _linear_residual_kernel(sub, a_ref, w_ref, h_ref, m_ref, o_ref, wbf_ref):
    gate = m_ref[3 * sub + 2:3 * sub + 3, :]
    o_ref[...] = h_ref[...] + gate * _dot(a_ref[...], _keep_bf16_weight(w_ref, wbf_ref))


def _linear_residual(a, w, widx, h, ada4, layer, sub, seq, tm, tn):
    t, kdim = a.shape
    n = w.shape[2]
    tpb = seq // tm
    return pl.pallas_call(
        functools.partial(_linear_residual_kernel, sub),
        grid=(t // tm, n // tn),
        in_specs=[
            pl.BlockSpec((tm, kdim), lambda i, j: (i, 0)),
            _resident_weight_spec(widx, kdim, tn, n // tn),
            pl.BlockSpec((tm, tn), lambda i, j: (i, j)),
            pl.BlockSpec((None, None, 3 * N_SUBLAYERS, tn), lambda i, j: (layer, i // tpb, 0, j)),
        ],
        out_specs=pl.BlockSpec((tm, tn), lambda i, j: (i, j)),
        out_shape=jax.ShapeDtypeStruct((t, n), F32),
        scratch_shapes=[pltpu.VMEM((n // tn, kdim, tn), BF16)],
        compiler_params=_cparams(("arbitrary", "arbitrary")),
        name=f"linear_residual_l{layer}",
    )(a, w, h, ada4)


ROPE_SET_Q, ROPE_SET_K, ROPE_SET_NONE, ROPE_SET_IDX = 0, 1, 2, 3
HALF_LANES = LANES // 2


def _rope_table_kernel(pos_ref, invf_ref, o_ref):
    pos = pos_ref[...].astype(F32)
    lane = lax.broadcasted_iota(jnp.int32, (pos.shape[0], LANES), 1)
    first = lane < HALF_LANES

    def tables(invf, is_rot):
        ang = pos * invf
        c = jnp.where(is_rot, jnp.cos(ang), 1.0)
        s = jnp.where(is_rot, jnp.where(first, -jnp.sin(ang), jnp.sin(ang)), 0.0)
        return c, s

    cb, sb = tables(invf_ref[:, 0:LANES], (lane & (HALF_LANES - 1)) < ROT_DIM // 2)
    ci, si = tables(invf_ref[:, LANES:2 * LANES], (lane & (HALF_LANES // 2 - 1)) < IDX_ROT_DIM // 2)
    o_ref[2 * ROPE_SET_Q] = cb * Q_PRESCALE
    o_ref[2 * ROPE_SET_Q + 1] = sb * Q_PRESCALE
    o_ref[2 * ROPE_SET_K] = cb
    o_ref[2 * ROPE_SET_K + 1] = sb
    o_ref[2 * ROPE_SET_NONE] = jnp.ones_like(cb)
    o_ref[2 * ROPE_SET_NONE + 1] = jnp.zeros_like(cb)
    o_ref[2 * ROPE_SET_IDX] = ci
    o_ref[2 * ROPE_SET_IDX + 1] = si


def _rope_tables(positions, tm):
    t = positions.size
    lane = jnp.arange(LANES)

    def inv_freq(rot_dim):
        half = rot_dim // 2
        return ROPE_THETA ** (-jnp.arange(half, dtype=F32) / half)

    invf = jnp.concatenate([inv_freq(ROT_DIM)[lane % (ROT_DIM // 2)],
                            inv_freq(IDX_ROT_DIM)[lane % (IDX_ROT_DIM // 2)]])[None, :]
    return pl.pallas_call(
        _rope_table_kernel,
        grid=(t // tm,),
        in_specs=[
            pl.BlockSpec((tm, 1), lambda i: (i, 0)),
            pl.BlockSpec((1, 2 * LANES), lambda i: (0, 0)),
        ],
        out_specs=pl.BlockSpec((8, tm, LANES), lambda i: (0, i, 0)),
        out_shape=jax.ShapeDtypeStruct((8, t, LANES), F32),
        compiler_params=_cparams(("arbitrary",)),
        name="rope_tables",
    )(positions.reshape(t, 1), invf)


def _rope(x, c, s):
    return x * c + pltpu.roll(x, HALF_LANES, 1) * s


PERM_BIG, PERM_NONE, PERM_IDX = 0, 1, 2


def _lane_permutations():
    h, p = ROT_DIM // 2, (HEAD_DIM - ROT_DIM) // 2
    big = np.concatenate([np.arange(0, h), np.arange(ROT_DIM, ROT_DIM + p),
                          np.arange(h, ROT_DIM), np.arange(ROT_DIM + p, HEAD_DIM)])
    lo, hi = _idx_lo_hi_dims()
    idx = np.concatenate([lo, IDX_DIM + lo, hi, IDX_DIM + hi])
    mats = np.zeros((3, LANES, LANES), np.float32)
    for kind, src in ((PERM_BIG, big), (PERM_NONE, np.arange(LANES)), (PERM_IDX, idx)):
        mats[kind, np.arange(LANES), src] = 1.0
    return jnp.asarray(mats, BF16)


def _idx_lo_hi_dims():
    h, p = IDX_ROT_DIM // 2, (IDX_DIM - IDX_ROT_DIM) // 2
    lo = np.concatenate([np.arange(0, h), np.arange(IDX_ROT_DIM, IDX_ROT_DIM + p)])
    hi = np.concatenate([np.arange(h, IDX_ROT_DIM), np.arange(IDX_ROT_DIM + p, IDX_DIM)])
    return lo, hi


def _dsa_small_weights(w_tail):
    h, p = IDX_ROT_DIM // 2, (IDX_DIM - IDX_ROT_DIM) // 2
    klo = jnp.concatenate([w_tail[:, 0:h], w_tail[:, IDX_ROT_DIM:IDX_ROT_DIM + p]], axis=1)
    khi = jnp.concatenate([w_tail[:, h:IDX_ROT_DIM], w_tail[:, IDX_ROT_DIM + p:IDX_DIM]], axis=1)
    wi = jnp.pad(w_tail[:, IDX_DIM:], ((0, 0), (0, LANES - IDX_DIM - IDX_HEADS)))
    return jnp.concatenate([klo, khi, wi], axis=1).astype(BF16)


def _dsa_proj_kernel(sub, tn, x_hbm, g_ref, m_ref, w_ref, ws_ref, perm_ref, tab_ref,
                     o_ref, ok_ref, ow_ref, y_ref, wbf_ref, x_ref, x_sem):
    i = pl.program_id(0)
    j = pl.program_id(1)
    tm = x_ref.shape[0]
    first_k, first_v, first_qi = Q_W // tn, (Q_W + KV_W) // tn, (Q_W + 2 * KV_W) // tn

    def x_copy(tile):
        return pltpu.make_async_copy(x_hbm.at[pl.ds(tile * tm, tm), :], x_ref, x_sem.at[0])

    @pl.when(j == 0)
    def _():
        pl.when(i == 0)(x_copy(i).start)
        x_copy(i).wait()
        _write_norm_mod(x_ref, g_ref, m_ref, sub, y_ref)
        pl.when(i + 1 < pl.num_programs(0))(x_copy(i + 1).start)
        small = _dot(y_ref[...], ws_ref[...])
        by32 = pltpu.roll(small, HALF_LANES // 2, 1)
        by64 = pltpu.roll(small, HALF_LANES, 1)
        quarter = lax.broadcasted_iota(jnp.int32, small.shape, 1) // (HALF_LANES // 2)
        k_even = jnp.where(quarter == 0, small, jnp.where(quarter == 2, by32, 0.0))
        k_odd = jnp.where(quarter == 1, by32, jnp.where(quarter == 3, by64, 0.0))
        for s, kk in enumerate((k_even, k_odd)):
            ok_ref[:, s * LANES:(s + 1) * LANES] = _rope(
                kk, tab_ref[2 * ROPE_SET_IDX], tab_ref[2 * ROPE_SET_IDX + 1]).astype(BF16)
        ow_ref[...] = jnp.where(quarter == 0, by64, 0.0)

    @pl.when(pl.program_id(0) == 0)
    def _():
        kind = jnp.where(j < first_v, PERM_BIG, jnp.where(j < first_qi, PERM_NONE, PERM_IDX))
        perm = perm_ref[kind]
        for sl in range(tn // LANES):
            rows = slice(sl * LANES, (sl + 1) * LANES)
            wbf_ref[j, rows, :] = _dot(perm, w_ref[rows, :].astype(BF16)).astype(BF16)

    rope_set = jnp.where(j < first_k, ROPE_SET_Q,
                         jnp.where(j < first_v, ROPE_SET_K,
                                   jnp.where(j < first_qi, ROPE_SET_NONE, ROPE_SET_IDX)))
    c = tab_ref[2 * rope_set]
    s = tab_ref[2 * rope_set + 1]
    y = y_ref[...]
    wide = 2 * LANES
    for part in range(tn // wide):
        acc = _dot_t(y, wbf_ref[j, part * wide:(part + 1) * wide, :])
        for sl in range(wide // LANES):
            cols = slice(part * wide + sl * LANES, part * wide + (sl + 1) * LANES)
            o_ref[:, cols] = _rope(acc[:, sl * LANES:(sl + 1) * LANES], c, s).astype(BF16)


def _dsa_proj(h, ada4, g_row, w_in, widx, w_small, tabs, layer, sub, seq, tm, tn):
    t, d = h.shape
    tpb = seq // tm
    return pl.pallas_call(
        functools.partial(_dsa_proj_kernel, sub, tn),
        grid=(t // tm, DSA_MAIN_W // tn),
        in_specs=[
            pl.BlockSpec(memory_space=pl.ANY),
            pl.BlockSpec((1, d), lambda i, j: (0, 0)),
            pl.BlockSpec((None, None, 3 * N_SUBLAYERS, d), lambda i, j: (layer, i // tpb, 0, 0)),
            pl.BlockSpec((None, tn, d),
                         lambda i, j: (widx, jnp.where(i == 0, j, DSA_MAIN_W // tn - 1), 0)),
            pl.BlockSpec((d, LANES), lambda i, j: (0, 0)),
            pl.BlockSpec((3, LANES, LANES), lambda i, j: (0, 0, 0)),
            pl.BlockSpec((8, tm, LANES), lambda i, j: (0, i, 0)),
        ],
        out_specs=[
            pl.BlockSpec((tm, tn), lambda i, j: (i, j)),
            pl.BlockSpec((tm, 2 * LANES), lambda i, j: (i, 0)),
            pl.BlockSpec((tm, LANES), lambda i, j: (i, 0)),
        ],
        out_shape=[
            jax.ShapeDtypeStruct((t, DSA_MAIN_W), BF16),
            jax.ShapeDtypeStruct((t, 2 * LANES), BF16),
            jax.ShapeDtypeStruct((t, LANES), F32),
        ],
        scratch_shapes=[pltpu.VMEM((tm, d), BF16),
                        pltpu.VMEM((DSA_MAIN_W // tn, tn, d), BF16),
                        pltpu.VMEM((tm, d), F32),
                        pltpu.SemaphoreType.DMA((1,))],
        compiler_params=_cparams(("arbitrary", "arbitrary")),
        name=f"dsa_proj_l{layer}",
    )(h, g_row, ada4, w_in, w_small, _lane_permutations(), tabs)


def _dsa_attn_kernel(tq, tk, top_k, q_ref, qi_ref, wi_ref, k_ref, v_ref, kk_ref, o_ref,
                     score_ref, bias_ref, thr_ref, wib_ref, qg_ref, m_ref, acc_ref):
    i = pl.program_id(1)
    n_chunks = ((i + 1) * tq + tk - 1) // tk
    max_chunks = score_ref.shape[1] // tk
    t_pos = i * tq + lax.broadcasted_iota(jnp.int32, (tq, tk), 0)
    k_iota = lax.broadcasted_iota(jnp.int32, (tq, tk), 1)
    n_cb = tk // LANES

    for hh in range(IDX_HEADS):
        wib_ref[hh] = jnp.broadcast_to(wi_ref[:, hh:hh + 1], (tq, LANES))
    for g in range(N_KV_HEADS):
        for r in range(KV_GROUP):
            hd = g * KV_GROUP + r
            qg_ref[g, r * tq:(r + 1) * tq, :] = q_ref[:, hd * HEAD_DIM:(hd + 1) * HEAD_DIM]

    def score_chunk(c, carry):
        off = pl.multiple_of(c * tk, tk)
        k_even = kk_ref[pl.ds(off, tk), 0:LANES]
        k_odd = kk_ref[pl.ds(off, tk), LANES:2 * LANES]
        acc = [jnp.zeros((tq, LANES), F32) for _ in range(n_cb)]
        for pair in range(IDX_HEADS // 2):
            qp = qi_ref[:, pair * LANES:(pair + 1) * LANES]
            for par, kop in enumerate((k_even, k_odd)):
                r = _dot_t(qp, kop)
                w = wib_ref[2 * pair + par]
                for cb in range(n_cb):
                    acc[cb] = acc[cb] + w * jnp.maximum(r[:, cb * LANES:(cb + 1) * LANES], 0.0)
        sc = jnp.concatenate(acc, axis=1)
        score_ref[:, pl.ds(off, tk)] = jnp.where(k_iota + off <= t_pos, sc, NEG)
        return carry

    lax.fori_loop(0, n_chunks, score_chunk, 0)

    def key_to_f32(key):
        return pltpu.bitcast(key ^ ((key >> 31) & 0x7FFFFFFF), F32)

    k_f = float(top_k)

    def select(n_static):
        def bit_step(n, thr_key):
            cand = thr_key + lax.shift_left(jnp.int32(1), 31 - n)
            thr = key_to_f32(cand)
            cnt = jnp.zeros((tq, LANES), F32)
            for cb in range(n_static * n_cb):
                s = score_ref[:, cb * LANES:(cb + 1) * LANES]
                cnt = cnt + jnp.where(s >= thr, 1.0, 0.0)
            total = jnp.sum(cnt, axis=1, keepdims=True)
            return jnp.where(total >= k_f, cand, thr_key)

        thr_key = lax.fori_loop(0, 32, bit_step, jnp.full((tq, LANES), INT32_MIN, jnp.int32),
                                unroll=4)
        thr_ref[...] = key_to_f32(thr_key)

    for n_static in range(1, max_chunks + 1):
        pl.when(n_chunks == n_static)(functools.partial(select, n_static))

    def bias_chunk(c, carry):
        off = pl.multiple_of(c * tk, tk)
        thr = thr_ref[...]
        causal = k_iota + off <= t_pos
        for cb in range(n_cb):
            cols = pl.ds(off + cb * LANES, LANES)
            sel = score_ref[:, cols] >= thr
            bias_ref[:, cols] = jnp.where(
                sel, jnp.where(causal[:, cb * LANES:(cb + 1) * LANES], 0.0, NEG), NEG)
        return carry

    lax.fori_loop(0, n_chunks, bias_chunk, 0)

    def count_rows(pred):
        def body(c, cnt):
            off = pl.multiple_of(c * tk, tk)
            thr = thr_ref[...]
            for cb in range(n_cb):
                s = score_ref[:, pl.ds(off + cb * LANES, LANES)]
                cnt = cnt + jnp.where(pred(s, thr), 1.0, 0.0)
            return cnt
        cnt = lax.fori_loop(0, n_chunks, body, jnp.zeros((tq, LANES), F32))
        return jnp.sum(cnt, axis=1, keepdims=True)

    n_ge = count_rows(lambda s, thr: s >= thr)

    @pl.when(jnp.max(n_ge) > k_f)
    def _():
        n_take = k_f - count_rows(lambda s, thr: s > thr)
        before = jnp.where(
            lax.broadcasted_iota(jnp.int32, (tk, tk), 0) < lax.broadcasted_iota(jnp.int32, (tk, tk), 1),
            1.0, 0.0).astype(BF16)

        def tie_chunk(c, seen):
            off = pl.multiple_of(c * tk, tk)
            sc = score_ref[:, pl.ds(off, tk)]
            thr = jnp.concatenate([thr_ref[...]] * n_cb, axis=1)
            eq = jnp.where(sc == thr, 1.0, 0.0)
            rank = _dot(eq.astype(BF16), before) + seen
            keep = jnp.where(sc > thr, 1.0, jnp.where(rank < n_take, eq, 0.0))
            causal = k_iota + off <= t_pos
            bias_ref[:, pl.ds(off, tk)] = jnp.where(causal, jnp.where(keep > 0.0, 0.0, NEG), NEG)
            return seen + jnp.sum(eq, axis=1, keepdims=True)

        lax.fori_loop(0, n_chunks, tie_chunk, jnp.zeros((tq, 1), F32))

    rows = KV_GROUP * tq
    m_ref[...] = jnp.full(m_ref.shape, NEG, F32)
    acc_ref[...] = jnp.zeros(acc_ref.shape, F32)
    def attend(off, width):
        ones = jnp.ones((width, LANES), BF16)
        b = bias_ref[:, pl.ds(off, width)]
        for g in range(N_KV_HEADS):
            kc = k_ref[pl.ds(off, width), g * HEAD_DIM:(g + 1) * HEAD_DIM]
            vc = v_ref[pl.ds(off, width), g * HEAD_DIM:(g + 1) * HEAD_DIM]
            s = _dot_t(qg_ref[g], kc)
            s = (s.reshape(KV_GROUP, tq, width) + b[None]).reshape(rows, width)
            m_old = m_ref[g]
            m_new = jnp.maximum(m_old, jnp.max(s, axis=1, keepdims=True))
            alpha = jnp.exp2(m_old - m_new)
            p = jnp.exp2(s - jnp.concatenate([m_new] * (width // LANES), axis=1))
            pv = _dot(p.astype(BF16), jnp.concatenate([vc, ones], axis=1))
            acc_ref[g] = jnp.concatenate([alpha, alpha], axis=1) * acc_ref[g] + pv
            m_ref[g] = m_new

    def attn_chunk(c, carry):
        attend(pl.multiple_of(c * tk, tk), tk)
        return carry

    lax.fori_loop(0, n_chunks, attn_chunk, 0)

    for g in range(N_KV_HEADS):
        out = acc_ref[g, :, 0:HEAD_DIM] / acc_ref[g, :, HEAD_DIM:2 * HEAD_DIM]
        for r in range(KV_GROUP):
            hd = g * KV_GROUP + r
            o_ref[:, hd * HEAD_DIM:(hd + 1) * HEAD_DIM] = out[r * tq:(r + 1) * tq].astype(BF16)


def _dsa_attn(main, kk, wi, batch, seq, tq, tk, top_k):
    t = main.shape[0]
    nq = seq // tq
    rows = KV_GROUP * tq
    return pl.pallas_call(
        functools.partial(_dsa_attn_kernel, tq, tk, top_k),
        grid=(batch, nq),
        in_specs=[
            pl.BlockSpec((tq, Q_W), lambda b, i: (b * nq + i, 0)),
            pl.BlockSpec((tq, QI_W), lambda b, i: (b * nq + i, (Q_W + 2 * KV_W) // QI_W)),
            pl.BlockSpec((tq, LANES), lambda b, i: (b * nq + i, 0)),
            pl.BlockSpec((seq, KV_W), lambda b, i: (b, Q_W // KV_W)),
            pl.BlockSpec((seq, KV_W), lambda b, i: (b, Q_W // KV_W + 1)),
            pl.BlockSpec((seq, 2 * LANES), lambda b, i: (b, 0)),
        ],
        out_specs=pl.BlockSpec((tq, Q_W), lambda b, i: (b * nq + i, 0)),
        out_shape=jax.ShapeDtypeStruct((t, Q_W), BF16),
        scratch_shapes=[
            pltpu.VMEM((tq, seq), F32),
            pltpu.VMEM((tq, seq), F32),
            pltpu.VMEM((tq, LANES), F32),
            pltpu.VMEM((IDX_HEADS, tq, LANES), F32),
            pltpu.VMEM((N_KV_HEADS, rows, HEAD_DIM), BF16),
            pltpu.VMEM((N_KV_HEADS, rows, LANES), F32),
            pltpu.VMEM((N_KV_HEADS, rows, 2 * HEAD_DIM), F32),
        ],
        compiler_params=_cparams(("arbitrary", "arbitrary")),
        name="dsa_attn",
    )(main, main, wi, main, main, kk)


def _tiles(seq, d_ff):
    tm = min(1024, seq)
    tm_ffn = min(2048, seq)
    tf = 256 if d_ff % 256 == 0 else d_ff
    tq = min(256, seq)
    tk = min(256, seq)
    return tm, tm_ffn, tf, tq, tk


def kernel(x, c, positions, ada_w, ada_b, norm_g, final_g, ffn_wgu, ffn_wd,
           pool_w_in, pool_w_grp, pool_scale, pool_w_out, dsa_w_in, dsa_w_out):
    batch, seq, d = x.shape
    depth = ada_w.shape[0]
    d_ff = ffn_wd.shape[2]
    t = batch * seq
    tm, tm_ffn, tf, tq, tk = _tiles(seq, d_ff)
    tn = min(512, d)
    top_k = min(TOPK_MAX, seq // 4)
    assert seq % tm == 0 and seq % tm_ffn == 0 and seq % tq == 0 and seq % tk == 0 and tk >= top_k
    assert dsa_w_in.shape[2] == DSA_MAIN_W + DSA_SMALL_W and DSA_MAIN_W % tn == 0

    rows = -(-batch // SUBLANES) * SUBLANES
    c8 = jnp.pad(c, ((0, rows - batch), (0, 0)))
    ada_n = ada_w.shape[2]
    ada = _ada(c8, ada_w, ada_b, 1024 if ada_n % 1024 == 0 else d)
    ada4 = ada.reshape(depth, rows, 3 * N_SUBLAYERS, d)

    tabs = _rope_tables(positions, tm) if depth > 1 else None

    h = x.reshape(t, d)
    for layer in range(depth):
        j = layer // 2
        h = _ffn(h, ada4, norm_g, ffn_wgu, ffn_wd, layer, 0, 0, seq, tm_ffn, tf)
        g_row = norm_g[layer, 1][None, :]
        if layer % 2 == 0:
            u = _norm_linear(h, ada4, g_row, pool_w_in, j, layer, 1, seq, tm, tn)
            a = _pool_core(u, pool_w_grp, pool_scale, j, batch, seq)
            h = _linear_residual(a, pool_w_out, j, h, ada4, layer, 1, seq, tm, tn)
        else:
            w_small = _dsa_small_weights(dsa_w_in[j, :, DSA_MAIN_W:])
            main, kk, wi = _dsa_proj(h, ada4, g_row, jnp.swapaxes(dsa_w_in, 1, 2), j, w_small,
                                     tabs, layer, 1, seq, tm, tn)
            o = _dsa_attn(main, kk, wi, batch, seq, tq, tk, top_k)
            h = _linear_residual(o, dsa_w_out, j, h, ada4, layer, 1, seq, tm, tn)
        last = layer == depth - 1
        h = _ffn(h, ada4, norm_g, ffn_wgu, ffn_wd, layer, 1, 2, seq, tm_ffn, tf,
                 final_g=final_g if last else None)
    return h.reshape(batch, seq, d)
```

```python
import functools

import jax
import jax.numpy as jnp
import numpy as np
from jax import lax
from jax.experimental import pallas as pl
from jax.experimental.pallas import tpu as pltpu

F32 = jnp.float32
BF16 = jnp.bfloat16

N_SUBLAYERS = 3
POOL_WINDOWS = (2, 4, 8, 16)
N_HEADS = 16
N_KV_HEADS = 4
KV_GROUP = N_HEADS // N_KV_HEADS
HEAD_DIM = 128
ROT_DIM = HEAD_DIM // 4
ROPE_THETA = 500000.0
IDX_HEADS = 16
IDX_DIM = 64
IDX_ROT_DIM = IDX_DIM // 4
TOPK_MAX = 256
EPS = 1e-6
NEG = -1e30
Q_W = N_HEADS * HEAD_DIM
KV_W = N_KV_HEADS * HEAD_DIM
QI_W = IDX_HEADS * IDX_DIM
DSA_MAIN_W = Q_W + 2 * KV_W + QI_W
DSA_SMALL_W = IDX_DIM + IDX_HEADS
Q_PRESCALE = HEAD_DIM ** -0.5 * 1.4426950408889634

LANES = 128
SUBLANES = 8
VMEM_LIMIT_BYTES = 56 * 1024 * 1024
INT32_MIN = -(2 ** 31)


def _cparams(semantics):
    return pltpu.CompilerParams(dimension_semantics=semantics,
                                vmem_limit_bytes=VMEM_LIMIT_BYTES)


def _rmsnorm(x, g):
    ms = jnp.mean(x * x, axis=-1, keepdims=True)
    return x * lax.rsqrt(ms + EPS) * g


def _rmsnorm_rows(ref, rows, g_ref):
    ref[rows, :] = _rmsnorm(ref[rows, :], g_ref[...])


NORM_SLAB = 16
NORM_UNROLL = 8
FFN_ROW_SLAB = 128
FFN_SLAB_SLOTS = 4
FFN_DOWN_COLS = 256


def _for_row_slabs(n_rows, slab, body, unroll=1):
    def step(r, carry):
        body(pl.ds(pl.multiple_of(r * slab, slab), slab))
        return carry
    lax.fori_loop(0, n_rows // slab, step, 0, unroll=unroll)


def _write_norm_mod(x_ref, g_ref, m_ref, sub, y_ref, copy_ref=None):
    shift = m_ref[3 * sub + 0:3 * sub + 1, :]
    gs = g_ref[...] * (1.0 + m_ref[3 * sub + 1:3 * sub + 2, :])
    slab = min(NORM_SLAB, x_ref.shape[0])

    def norm_rows(rows):
        x = x_ref[rows, :]
        r = lax.rsqrt(jnp.mean(x * x, axis=-1, keepdims=True) + EPS)
        y_ref[rows, :] = ((x * r) * gs + shift).astype(BF16)
        if copy_ref is not None:
            copy_ref[rows, :] = x

    _for_row_slabs(x_ref.shape[0], slab, norm_rows, unroll=NORM_UNROLL)


def _silu(x):
    return x * jax.nn.sigmoid(x)


def _dot(a, b):
    return jnp.dot(a, b, preferred_element_type=F32)


def _dot_t(a, b):
    return lax.dot_general(a, b, (((1,), (1,)), ((), ())), preferred_element_type=F32)


def _ada_kernel(c_ref, w_ref, b_ref, o_ref):
    sc = _silu(c_ref[...]).astype(BF16)
    o_ref[...] = _dot(sc, w_ref[...].astype(BF16)) + b_ref[...]


def _ada(c8, ada_w, ada_b, tn):
    depth, d, n = ada_w.shape
    rows = c8.shape[0]
    return pl.pallas_call(
        _ada_kernel,
        grid=(depth, n // tn),
        in_specs=[
            pl.BlockSpec((rows, d), lambda l, j: (0, 0)),
            pl.BlockSpec((None, d, tn), lambda l, j: (l, 0, j)),
            pl.BlockSpec((None, 1, tn), lambda l, j: (l, 0, j)),
        ],
        out_specs=pl.BlockSpec((None, rows, tn), lambda l, j: (l, 0, j)),
        out_shape=jax.ShapeDtypeStruct((depth, rows, n), F32),
        compiler_params=_cparams(("arbitrary", "arbitrary")),
        name="ada",
    )(c8, ada_w, ada_b.reshape(depth, 1, n))


def _ffn_kernel(sub, final, rs, *refs):
    if final:
        (x_hbm, g_ref, m_ref, wg_ref, wu_ref, wd_ref, fg_ref, o_hbm,
         acc_ref, y_ref, xbuf, in_sem, out_sem) = refs
    else:
        (x_hbm, g_ref, m_ref, wg_ref, wu_ref, wd_ref, o_hbm,
         acc_ref, y_ref, xbuf, in_sem, out_sem) = refs
    k = pl.program_id(1)
    nk = pl.num_programs(1)
    tm, d_out = acc_ref.shape
    n_slabs = tm // rs
    n_slots = xbuf.shape[0]
    n_first = min(n_slots, n_slabs)
    row0 = pl.program_id(0) * tm

    def x_copy(s, tile_row0=row0):
        slot = s % n_slots
        return pltpu.make_async_copy(x_hbm.at[pl.ds(tile_row0 + s * rs, rs), :], xbuf.at[slot],
                                     in_sem.at[slot])

    def o_copy(s):
        return pltpu.make_async_copy(acc_ref.at[pl.ds(s * rs, rs), :],
                                     o_hbm.at[pl.ds(row0 + s * rs, rs), :], out_sem.at[s])

    @pl.when(k == 0)
    def _():
        @pl.when(pl.program_id(0) == 0)
        def _():
            for s in range(n_first):
                x_copy(s).start()

        for s in range(n_slabs):
            x_copy(s).wait()
            pl.when(pl.program_id(0) > 0)(o_copy(s).wait)
            rows = pl.ds(s * rs, rs)
            _write_norm_mod(xbuf.at[s % n_slots], g_ref, m_ref, sub, y_ref.at[rows, :],
                            copy_ref=acc_ref.at[rows, :])
            if s + n_slots < n_slabs:
                x_copy(s + n_slots).start()

    @pl.when(jnp.logical_and(k == nk - 1, pl.program_id(0) + 1 < pl.num_programs(0)))
    def _():
        for s in range(n_first):
            x_copy(s, row0 + tm).start()

    y = y_ref[...]
    g = _dot(y, wg_ref[...].astype(BF16))
    u = _dot(y, wu_ref[...].astype(BF16))
    a = (_silu(g) * u).astype(BF16)
    half_gate = 0.5 * m_ref[3 * sub + 2:3 * sub + 3, :]
    nc = min(FFN_DOWN_COLS, d_out)
    for n in range(d_out // nc):
        cols = slice(n * nc, (n + 1) * nc)
        acc_ref[:, cols] += half_gate[:, cols] * _dot(a, wd_ref[:, cols].astype(BF16))

    @pl.when(k == nk - 1)
    def _():
        for s in range(n_slabs):
            if final:
                _for_row_slabs(
                    rs, min(NORM_SLAB, rs),
                    lambda rows, s=s: _rmsnorm_rows(acc_ref.at[pl.ds(s * rs, rs), :], rows, fg_ref),
                    unroll=NORM_UNROLL)
            o_copy(s).start()

        @pl.when(pl.program_id(0) == pl.num_programs(0) - 1)
        def _():
            for s in range(n_slabs):
                o_copy(s).wait()


def _ffn(h, ada4, norm_g, ffn_wgu, ffn_wd, layer, which, sub, seq, tm, tf, final_g=None):
    t, d = h.shape
    d_ff = ffn_wd.shape[2]
    nk = d_ff // tf
    tpb = seq // tm
    rs = min(FFN_ROW_SLAB, tm)
    final = final_g is not None
    in_specs = [
        pl.BlockSpec(memory_space=pl.ANY),
        pl.BlockSpec((1, d), lambda i, k: (0, 0)),
        pl.BlockSpec((None, None, 3 * N_SUBLAYERS, d), lambda i, k: (layer, i // tpb, 0, 0)),
        pl.BlockSpec((None, None, d, tf), lambda i, k: (layer, which, 0, k)),
        pl.BlockSpec((None, None, d, tf), lambda i, k: (layer, which, 0, nk + k)),
        pl.BlockSpec((None, None, tf, d), lambda i, k: (layer, which, k, 0)),
    ]
    args = [h, norm_g[layer, sub][None, :], ada4, ffn_wgu, ffn_wgu, ffn_wd]
    if final:
        in_specs.append(pl.BlockSpec((1, d), lambda i, k: (0, 0)))
        args.append(final_g[None, :])
    return pl.pallas_call(
        functools.partial(_ffn_kernel, sub, final, rs),
        grid=(t // tm, nk),
        in_specs=in_specs,
        out_specs=pl.BlockSpec(memory_space=pl.ANY),
        out_shape=jax.ShapeDtypeStruct((t, d), F32),
        scratch_shapes=[
            pltpu.VMEM((tm, d), F32),
            pltpu.VMEM((tm, d), BF16),
            pltpu.VMEM((FFN_SLAB_SLOTS, rs, d), F32),
            pltpu.SemaphoreType.DMA((FFN_SLAB_SLOTS,)),
            pltpu.SemaphoreType.DMA((tm // rs,)),
        ],
        compiler_params=_cparams(("arbitrary", "arbitrary")),
        name=f"ffn_l{layer}_{which}",
    )(*args)


def _keep_bf16_weight(w_ref, wbf_ref):
    j = pl.program_id(1)

    @pl.when(pl.program_id(0) == 0)
    def _():
        wbf_ref[j] = w_ref[...].astype(BF16)

    return wbf_ref[j]


def _resident_weight_spec(widx, kdim, tn, nj):
    return pl.BlockSpec((None, kdim, tn), lambda i, j: (widx, 0, jnp.where(i == 0, j, nj - 1)))


def _norm_linear_kernel(sub, x_ref, g_ref, m_ref, w_ref, o_ref, y_ref, wbf_ref):
    @pl.when(pl.program_id(1) == 0)
    def _():
        _write_norm_mod(x_ref, g_ref, m_ref, sub, y_ref)

    o_ref[...] = _dot(y_ref[...], _keep_bf16_weight(w_ref, wbf_ref))


def _norm_linear(h, ada4, g_row, w, widx, layer, sub, seq, tm, tn):
    t, d = h.shape
    n = w.shape[2]
    tpb = seq // tm
    return pl.pallas_call(
        functools.partial(_norm_linear_kernel, sub),
        grid=(t // tm, n // tn),
        in_specs=[
            pl.BlockSpec((tm, d), lambda i, j: (i, 0)),
            pl.BlockSpec((1, d), lambda i, j: (0, 0)),
            pl.BlockSpec((None, None, 3 * N_SUBLAYERS, d), lambda i, j: (layer, i // tpb, 0, 0)),
            _resident_weight_spec(widx, d, tn, n // tn),
        ],
        out_specs=pl.BlockSpec((tm, tn), lambda i, j: (i, j)),
        out_shape=jax.ShapeDtypeStruct((t, n), F32),
        scratch_shapes=[pltpu.VMEM((tm, d), BF16), pltpu.VMEM((n // tn, d, tn), BF16)],
        compiler_params=_cparams(("arbitrary", "arbitrary")),
        name=f"norm_linear_l{layer}",
    )(h, g_row, ada4, w)


def _pool_kernel(u_ref, w_ref, ls_ref, o_ref):
    g = pl.program_id(1)
    u = u_ref[...]
    t = lax.broadcasted_iota(jnp.int32, u.shape, 0)

    def shifted(v, d):
        return jnp.where(t >= d, pltpu.roll(v, d, 0), 0.0)

    for gi, win in enumerate(POOL_WINDOWS):
        @pl.when(g == gi)
        def _(win=win):
            acc = u
            d = 1
            while d < win:
                acc = acc + shifted(acc, d)
                d *= 2
            cnt = jnp.minimum(t + 1, win).astype(F32)
            p = (acc / cnt - u).astype(BF16)
            v = _dot(p, w_ref[...].astype(BF16)) * ls_ref[...]
            o_ref[...] = v.astype(BF16)


def _pool_core(u, w_grp, ls, widx, batch, seq):
    t, d = u.shape
    _, groups, c, _ = w_grp.shape
    return pl.pallas_call(
        _pool_kernel,
        grid=(batch, groups),
        in_specs=[
            pl.BlockSpec((seq, c), lambda b, g: (b, g)),
            pl.BlockSpec((None, None, c, c), lambda b, g: (widx, g, 0, 0)),
            pl.BlockSpec((None, 1, c), lambda b, g: (widx, 0, g)),
        ],
        out_specs=pl.BlockSpec((seq, c), lambda b, g: (b, g)),
        out_shape=jax.ShapeDtypeStruct((t, d), BF16),
        compiler_params=_cparams(("arbitrary", "arbitrary")),
        name="pool_core",
    )(u, w_grp, ls.reshape(ls.shape[0], 1, d))


def _linear_residual_kernel(sub, a_ref, w_ref, h_ref, m_ref, o_ref, wbf_ref):
    gate = m_ref[3 * sub + 2:3 * sub + 3, :]
    o_ref[...] = h_ref[...] + gate * _dot(a_ref[...], _keep_bf16_weight(w_ref, wbf_ref))


def _linear_residual(a, w, widx, h, ada4, layer, sub, seq, tm, tn):
    t, kdim = a.shape
    n = w.shape[2]
    tpb = seq // tm
    return pl.pallas_call(
        functools.partial(_linear_residual_kernel, sub),
        grid=(t // tm, n // tn),
        in_specs=[
            pl.BlockSpec((tm, kdim), lambda i, j: (i, 0)),
            _resident_weight_spec(widx, kdim, tn, n // tn),
            pl.BlockSpec((tm, tn), lambda i, j: (i, j)),
            pl.BlockSpec((None, None, 3 * N_SUBLAYERS, tn), lambda i, j: (layer, i // tpb, 0, j)),
        ],
        out_specs=pl.BlockSpec((tm, tn), lambda i, j: (i, j)),
        out_shape=jax.ShapeDtypeStruct((t, n), F32),
        scratch_shapes=[pltpu.VMEM((n // tn, kdim, tn), BF16)],
        compiler_params=_cparams(("arbitrary", "arbitrary")),
        name=f"linear_residual_l{layer}",
    )(a, w, h, ada4)


ROPE_SET_Q, ROPE_SET_K, ROPE_SET_NONE, ROPE_SET_IDX = 0, 1, 2, 3
HALF_LANES = LANES // 2


def _rope_table_kernel(pos_ref, invf_ref, o_ref):
    pos = pos_ref[...].astype(F32)
    lane = lax.broadcasted_iota(jnp.int32, (pos.shape[0], LANES), 1)
    ang = pos * invf_ref[...]
    cos = jnp.cos(ang)
    sin = jnp.where(lane < HALF_LANES, -jnp.sin(ang), jnp.sin(ang))
    rot_big = (lane & (HALF_LANES - 1)) < ROT_DIM // 2
    cb = jnp.where(rot_big, cos, 1.0)
    sb = jnp.where(rot_big, sin, 0.0)
    rot_idx = (lane & (HALF_LANES // 2 - 1)) < IDX_ROT_DIM // 2
    even_quarter = (lane & (HALF_LANES // 2)) == 0
    shift = ROT_DIM // 2

    def idx_table(v, rest):
        from_above = pltpu.roll(v, LANES - shift, 1)
        from_below = pltpu.roll(v, shift, 1)
        return jnp.where(rot_idx, jnp.where(even_quarter, from_above, from_below), rest)

    ci = idx_table(cos, 1.0)
    si = idx_table(sin, 0.0)
    o_ref[2 * ROPE_SET_Q] = cb * Q_PRESCALE
    o_ref[2 * ROPE_SET_Q + 1] = sb * Q_PRESCALE
    o_ref[2 * ROPE_SET_K] = cb
    o_ref[2 * ROPE_SET_K + 1] = sb
    o_ref[2 * ROPE_SET_NONE] = jnp.ones_like(cb)
    o_ref[2 * ROPE_SET_NONE + 1] = jnp.zeros_like(cb)
    o_ref[2 * ROPE_SET_IDX] = ci
    o_ref[2 * ROPE_SET_IDX + 1] = si


def _rope_tables(positions, tm):
    t = positions.size

    def inv_freq(rot_dim):
        half = rot_dim // 2
        return ROPE_THETA ** (-jnp.arange(half, dtype=F32) / half)

    half_row = jnp.concatenate([
        inv_freq(ROT_DIM), inv_freq(IDX_ROT_DIM),
        jnp.zeros((HALF_LANES - ROT_DIM // 2 - IDX_ROT_DIM // 2,), F32)])
    invf = jnp.tile(half_row, 2)[None, :]
    return pl.pallas_call(
        _rope_table_kernel,
        grid=(t // tm,),
        in_specs=[
            pl.BlockSpec((tm, 1), lambda i: (i, 0)),
            pl.BlockSpec((1, LANES), lambda i: (0, 0)),
        ],
        out_specs=pl.BlockSpec((8, tm, LANES), lambda i: (0, i, 0)),
        out_shape=jax.ShapeDtypeStruct((8, t, LANES), F32),
        compiler_params=_cparams(("arbitrary",)),
        name="rope_tables",
    )(positions.reshape(t, 1), invf)


def _rope(x, c, s):
    return x * c + pltpu.roll(x, HALF_LANES, 1) * s


PERM_BIG, PERM_NONE, PERM_IDX = 0, 1, 2


def _lane_permutations():
    h, p = ROT_DIM // 2, (HEAD_DIM - ROT_DIM) // 2
    big = np.concatenate([np.arange(0, h), np.arange(ROT_DIM, ROT_DIM + p),
                          np.arange(h, ROT_DIM), np.arange(ROT_DIM + p, HEAD_DIM)])
    lo, hi = _idx_lo_hi_dims()
    idx = np.concatenate([lo, IDX_DIM + lo, hi, IDX_DIM + hi])
    mats = np.zeros((3, LANES, LANES), np.float32)
    for kind, src in ((PERM_BIG, big), (PERM_NONE, np.arange(LANES)), (PERM_IDX, idx)):
        mats[kind, np.arange(LANES), src] = 1.0
    return jnp.asarray(mats, BF16)


def _idx_lo_hi_dims():
    h, p = IDX_ROT_DIM // 2, (IDX_DIM - IDX_ROT_DIM) // 2
    lo = np.concatenate([np.arange(0, h), np.arange(IDX_ROT_DIM, IDX_ROT_DIM + p)])
    hi = np.concatenate([np.arange(h, IDX_ROT_DIM), np.arange(IDX_ROT_DIM + p, IDX_DIM)])
    return lo, hi


def _dsa_small_weights(w_tail):
    h, p = IDX_ROT_DIM // 2, (IDX_DIM - IDX_ROT_DIM) // 2
    klo = jnp.concatenate([w_tail[:, 0:h], w_tail[:, IDX_ROT_DIM:IDX_ROT_DIM + p]], axis=1)
    khi = jnp.concatenate([w_tail[:, h:IDX_ROT_DIM], w_tail[:, IDX_ROT_DIM + p:IDX_DIM]], axis=1)
    wi = jnp.pad(w_tail[:, IDX_DIM:], ((0, 0), (0, LANES - IDX_DIM - IDX_HEADS)))
    return jnp.concatenate([klo, khi, wi], axis=1).astype(BF16)


def _dsa_proj_kernel(sub, tn, x_hbm, g_ref, m_ref, w_ref, ws_ref, perm_ref, tab_ref,
                     o_ref, ok_ref, ow_ref, y_ref, wbf_ref, x_ref, x_sem):
    i = pl.program_id(0)
    j = pl.program_id(1)
    tm = x_ref.shape[0]
    first_k, first_v, first_qi = Q_W // tn, (Q_W + KV_W) // tn, (Q_W + 2 * KV_W) // tn

    def x_copy(tile):
        return pltpu.make_async_copy(x_hbm.at[pl.ds(tile * tm, tm), :], x_ref, x_sem.at[0])

    @pl.when(j == 0)
    def _():
        pl.when(i == 0)(x_copy(i).start)
        x_copy(i).wait()
        _write_norm_mod(x_ref, g_ref, m_ref, sub, y_ref)
        pl.when(i + 1 < pl.num_programs(0))(x_copy(i + 1).start)
        small = _dot(y_ref[...], ws_ref[...])
        by32 = pltpu.roll(small, HALF_LANES // 2, 1)
        by64 = pltpu.roll(small, HALF_LANES, 1)
        quarter = lax.broadcasted_iota(jnp.int32, small.shape, 1) // (HALF_LANES // 2)
        k_even = jnp.where(quarter == 0, small, jnp.where(quarter == 2, by32, 0.0))
        k_odd = jnp.where(quarter == 1, by32, jnp.where(quarter == 3, by64, 0.0))
        for s, kk in enumerate((k_even, k_odd)):
            ok_ref[:, s * LANES:(s + 1) * LANES] = _rope(
                kk, tab_ref[2 * ROPE_SET_IDX], tab_ref[2 * ROPE_SET_IDX + 1]).astype(BF16)
        ow_ref[...] = jnp.where(quarter == 0, by64, 0.0)

    @pl.when(pl.program_id(0) == 0)
    def _():
        kind = jnp.where(j < first_v, PERM_BIG, jnp.where(j < first_qi, PERM_NONE, PERM_IDX))
        perm = perm_ref[kind]
        for sl in range(tn // LANES):
            rows = slice(sl * LANES, (sl + 1) * LANES)
            wbf_ref[j, rows, :] = _dot(perm, w_ref[rows, :].astype(BF16)).astype(BF16)

    rope_set = jnp.where(j < first_k, ROPE_SET_Q,
                         jnp.where(j < first_v, ROPE_SET_K,
                                   jnp.where(j < first_qi, ROPE_SET_NONE, ROPE_SET_IDX)))
    c = tab_ref[2 * rope_set]
    s = tab_ref[2 * rope_set + 1]
    y = y_ref[...]
    wide = 2 * LANES
    for part in range(tn // wide):
        acc = _dot_t(y, wbf_ref[j, part * wide:(part + 1) * wide, :])
        for sl in range(wide // LANES):
            cols = slice(part * wide + sl * LANES, part * wide + (sl + 1) * LANES)
            o_ref[:, cols] = _rope(acc[:, sl * LANES:(sl + 1) * LANES], c, s).astype(BF16)


def _dsa_proj(h, ada4, g_row, w_in, widx, w_small, tabs, layer, sub, seq, tm, tn):
    t, d = h.shape
    tpb = seq // tm
    return pl.pallas_call(
        functools.partial(_dsa_proj_kernel, sub, tn),
        grid=(t // tm, DSA_MAIN_W // tn),
        in_specs=[
            pl.BlockSpec(memory_space=pl.ANY),
            pl.BlockSpec((1, d), lambda i, j: (0, 0)),
            pl.BlockSpec((None, None, 3 * N_SUBLAYERS, d), lambda i, j: (layer, i // tpb, 0, 0)),
            pl.BlockSpec((None, tn, d),
                         lambda i, j: (widx, jnp.where(i == 0, j, DSA_MAIN_W // tn - 1), 0)),
            pl.BlockSpec((d, LANES), lambda i, j: (0, 0)),
            pl.BlockSpec((3, LANES, LANES), lambda i, j: (0, 0, 0)),
            pl.BlockSpec((8, tm, LANES), lambda i, j: (0, i, 0)),
        ],
        out_specs=[
            pl.BlockSpec((tm, tn), lambda i, j: (i, j)),
            pl.BlockSpec((tm, 2 * LANES), lambda i, j: (i, 0)),
            pl.BlockSpec((tm, LANES), lambda i, j: (i, 0)),
        ],
        out_shape=[
            jax.ShapeDtypeStruct((t, DSA_MAIN_W), BF16),
            jax.ShapeDtypeStruct((t, 2 * LANES), BF16),
            jax.ShapeDtypeStruct((t, LANES), F32),
        ],
        scratch_shapes=[pltpu.VMEM((tm, d), BF16),
                        pltpu.VMEM((DSA_MAIN_W // tn, tn, d), BF16),
                        pltpu.VMEM((tm, d), F32),
                        pltpu.SemaphoreType.DMA((1,))],
        compiler_params=_cparams(("arbitrary", "arbitrary")),
        name=f"dsa_proj_l{layer}",
    )(h, g_row, ada4, w_in, w_small, _lane_permutations(), tabs)


def _dsa_attn_kernel(tq, tk, top_k, q_ref, qi_ref, wi_ref, k_ref, v_ref, kk_ref, o_ref,
                     score_ref, bias_ref, thr_ref, wib_ref, qg_ref, m_ref, acc_ref):
    i = pl.program_id(1)
    n_chunks = ((i + 1) * tq + tk - 1) // tk
    max_chunks = score_ref.shape[1] // tk
    t_pos = i * tq + lax.broadcasted_iota(jnp.int32, (tq, tk), 0)
    k_iota = lax.broadcasted_iota(jnp.int32, (tq, tk), 1)
    n_cb = tk // LANES

    for hh in range(IDX_HEADS):
        wib_ref[hh] = jnp.broadcast_to(wi_ref[:, hh:hh + 1], (tq, LANES))
    for g in range(N_KV_HEADS):
        for r in range(KV_GROUP):
            hd = g * KV_GROUP + r
            qg_ref[g, r * tq:(r + 1) * tq, :] = q_ref[:, hd * HEAD_DIM:(hd + 1) * HEAD_DIM]

    def score_chunk(c, carry):
        off = pl.multiple_of(c * tk, tk)
        k_even = kk_ref[pl.ds(off, tk), 0:LANES]
        k_odd = kk_ref[pl.ds(off, tk), LANES:2 * LANES]
        acc = [jnp.zeros((tq, LANES), F32) for _ in range(n_cb)]
        for pair in range(IDX_HEADS // 2):
            qp = qi_ref[:, pair * LANES:(pair + 1) * LANES]
            for par, kop in enumerate((k_even, k_odd)):
                r = _dot_t(qp, kop)
                w = wib_ref[2 * pair + par]
                for cb in range(n_cb):
                    acc[cb] = acc[cb] + w * jnp.maximum(r[:, cb * LANES:(cb + 1) * LANES], 0.0)
        sc = jnp.concatenate(acc, axis=1)
        score_ref[:, pl.ds(off, tk)] = jnp.where(k_iota + off <= t_pos, sc, NEG)
        return carry

    lax.fori_loop(0, n_chunks, score_chunk, 0)

    def key_to_f32(key):
        return pltpu.bitcast(key ^ ((key >> 31) & 0x7FFFFFFF), F32)

    k_f = float(top_k)

    def select(n_static):
        def bit_step(n, thr_key):
            cand = thr_key + lax.shift_left(jnp.int32(1), 31 - n)
            thr = key_to_f32(cand)
            cnt = jnp.zeros((tq, LANES), F32)
            for cb in range(n_static * n_cb):
                s = score_ref[:, cb * LANES:(cb + 1) * LANES]
                cnt = cnt + jnp.where(s >= thr, 1.0, 0.0)
            total = jnp.sum(cnt, axis=1, keepdims=True)
            return jnp.where(total >= k_f, cand, thr_key)

        thr_key = lax.fori_loop(0, 32, bit_step, jnp.full((tq, LANES), INT32_MIN, jnp.int32),
                                unroll=4)
        thr_ref[...] = key_to_f32(thr_key)

    for n_static in range(1, max_chunks + 1):
        pl.when(n_chunks == n_static)(functools.partial(select, n_static))

    def bias_chunk(c, carry):
        off = pl.multiple_of(c * tk, tk)
        thr = thr_ref[...]
        causal = k_iota + off <= t_pos
        for cb in range(n_cb):
            cols = pl.ds(off + cb * LANES, LANES)
            sel = score_ref[:, cols] >= thr
            bias_ref[:, cols] = jnp.where(
                sel, jnp.where(causal[:, cb * LANES:(cb + 1) * LANES], 0.0, NEG), NEG)
        return carry

    lax.fori_loop(0, n_chunks, bias_chunk, 0)

    def count_rows(pred):
        def body(c, cnt):
            off = pl.multiple_of(c * tk, tk)
            thr = thr_ref[...]
            for cb in range(n_cb):
                s = score_ref[:, pl.ds(off + cb * LANES, LANES)]
                cnt = cnt + jnp.where(pred(s, thr), 1.0, 0.0)
            return cnt
        cnt = lax.fori_loop(0, n_chunks, body, jnp.zeros((tq, LANES), F32))
        return jnp.sum(cnt, axis=1, keepdims=True)

    n_ge = count_rows(lambda s, thr: s >= thr)

    @pl.when(jnp.max(n_ge) > k_f)
    def _():
        n_take = k_f - count_rows(lambda s, thr: s > thr)
        before = jnp.where(
            lax.broadcasted_iota(jnp.int32, (tk, tk), 0) < lax.broadcasted_iota(jnp.int32, (tk, tk), 1),
            1.0, 0.0).astype(BF16)

        def tie_chunk(c, seen):
            off = pl.multiple_of(c * tk, tk)
            sc = score_ref[:, pl.ds(off, tk)]
            thr = jnp.concatenate([thr_ref[...]] * n_cb, axis=1)
            eq = jnp.where(sc == thr, 1.0, 0.0)
            rank = _dot(eq.astype(BF16), before) + seen
            keep = jnp.where(sc > thr, 1.0, jnp.where(rank < n_take, eq, 0.0))
            causal = k_iota + off <= t_pos
            bias_ref[:, pl.ds(off, tk)] = jnp.where(causal, jnp.where(keep > 0.0, 0.0, NEG), NEG)
            return seen + jnp.sum(eq, axis=1, keepdims=True)

        lax.fori_loop(0, n_chunks, tie_chunk, jnp.zeros((tq, 1), F32))

    rows = KV_GROUP * tq
    m_ref[...] = jnp.full(m_ref.shape, NEG, F32)
    acc_ref[...] = jnp.zeros(acc_ref.shape, F32)
    def attend(off, width):
        ones = jnp.ones((width, LANES), BF16)
        b = bias_ref[:, pl.ds(off, width)]
        for g in range(N_KV_HEADS):
            kc = k_ref[pl.ds(off, width), g * HEAD_DIM:(g + 1) * HEAD_DIM]
            vc = v_ref[pl.ds(off, width), g * HEAD_DIM:(g + 1) * HEAD_DIM]
            s = _dot_t(qg_ref[g], kc)
            s = (s.reshape(KV_GROUP, tq, width) + b[None]).reshape(rows, width)
            m_old = m_ref[g]
            m_new = jnp.maximum(m_old, jnp.max(s, axis=1, keepdims=True))
            alpha = jnp.exp2(m_old - m_new)
            p = jnp.exp2(s - jnp.concatenate([m_new] * (width // LANES), axis=1))
            pv = _dot(p.astype(BF16), jnp.concatenate([vc, ones], axis=1))
            acc_ref[g] = jnp.concatenate([alpha, alpha], axis=1) * acc_ref[g] + pv
            m_ref[g] = m_new

    def attn_chunk(c, carry):
        attend(pl.multiple_of(c * tk, tk), tk)
        return carry

    lax.fori_loop(0, n_chunks, attn_chunk, 0)

    for g in range(N_KV_HEADS):
        out = acc_ref[g, :, 0:HEAD_DIM] / acc_ref[g, :, HEAD_DIM:2 * HEAD_DIM]
        for r in range(KV_GROUP):
            hd = g * KV_GROUP + r
            o_ref[:, hd * HEAD_DIM:(hd + 1) * HEAD_DIM] = out[r * tq:(r + 1) * tq].astype(BF16)


def _dsa_attn(main, kk, wi, batch, seq, tq, tk, top_k):
    t = main.shape[0]
    nq = seq // tq
    rows = KV_GROUP * tq
    return pl.pallas_call(
        functools.partial(_dsa_attn_kernel, tq, tk, top_k),
        grid=(batch, nq),
        in_specs=[
            pl.BlockSpec((tq, Q_W), lambda b, i: (b * nq + i, 0)),
            pl.BlockSpec((tq, QI_W), lambda b, i: (b * nq + i, (Q_W + 2 * KV_W) // QI_W)),
            pl.BlockSpec((tq, LANES), lambda b, i: (b * nq + i, 0)),
            pl.BlockSpec((seq, KV_W), lambda b, i: (b, Q_W // KV_W)),
            pl.BlockSpec((seq, KV_W), lambda b, i: (b, Q_W // KV_W + 1)),
            pl.BlockSpec((seq, 2 * LANES), lambda b, i: (b, 0)),
        ],
        out_specs=pl.BlockSpec((tq, Q_W), lambda b, i: (b * nq + i, 0)),
        out_shape=jax.ShapeDtypeStruct((t, Q_W), BF16),
        scratch_shapes=[
            pltpu.VMEM((tq, seq), F32),
            pltpu.VMEM((tq, seq), F32),
            pltpu.VMEM((tq, LANES), F32),
            pltpu.VMEM((IDX_HEADS, tq, LANES), F32),
            pltpu.VMEM((N_KV_HEADS, rows, HEAD_DIM), BF16),
            pltpu.VMEM((N_KV_HEADS, rows, LANES), F32),
            pltpu.VMEM((N_KV_HEADS, rows, 2 * HEAD_DIM), F32),
        ],
        compiler_params=_cparams(("arbitrary", "arbitrary")),
        name="dsa_attn",
    )(main, main, wi, main, main, kk)


def _tiles(seq, d_ff):
    tm = min(1024, seq)
    tm_ffn = min(2048, seq)
    tf = 256 if d_ff % 256 == 0 else d_ff
    tq = min(256, seq)
    tk = min(256, seq)
    return tm, tm_ffn, tf, tq, tk


def kernel(x, c, positions, ada_w, ada_b, norm_g, final_g, ffn_wgu, ffn_wd,
           pool_w_in, pool_w_grp, pool_scale, pool_w_out, dsa_w_in, dsa_w_out):
    batch, seq, d = x.shape
    depth = ada_w.shape[0]
    d_ff = ffn_wd.shape[2]
    t = batch * seq
    tm, tm_ffn, tf, tq, tk = _tiles(seq, d_ff)
    tn = min(512, d)
    top_k = min(TOPK_MAX, seq // 4)
    assert seq % tm == 0 and seq % tm_ffn == 0 and seq % tq == 0 and seq % tk == 0 and tk >= top_k
    assert dsa_w_in.shape[2] == DSA_MAIN_W + DSA_SMALL_W and DSA_MAIN_W % tn == 0

    rows = -(-batch // SUBLANES) * SUBLANES
    c8 = jnp.pad(c, ((0, rows - batch), (0, 0)))
    ada_n = ada_w.shape[2]
    ada = _ada(c8, ada_w, ada_b, 1024 if ada_n % 1024 == 0 else d)
    ada4 = ada.reshape(depth, rows, 3 * N_SUBLAYERS, d)

    tabs = _rope_tables(positions, tm) if depth > 1 else None

    h = x.reshape(t, d)
    for layer in range(depth):
        j = layer // 2
        h = _ffn(h, ada4, norm_g, ffn_wgu, ffn_wd, layer, 0, 0, seq, tm_ffn, tf)
        g_row = norm_g[layer, 1][None, :]
        if layer % 2 == 0:
            u = _norm_linear(h, ada4, g_row, pool_w_in, j, layer, 1, seq, tm, tn)
            a = _pool_core(u, pool_w_grp, pool_scale, j, batch, seq)
            h = _linear_residual(a, pool_w_out, j, h, ada4, layer, 1, seq, tm, tn)
        else:
            w_small = _dsa_small_weights(dsa_w_in[j, :, DSA_MAIN_W:])
            main, kk, wi = _dsa_proj(h, ada4, g_row, jnp.swapaxes(dsa_w_in, 1, 2), j, w_small,
                                     tabs, layer, 1, seq, tm, tn)
            o = _dsa_attn(main, kk, wi, batch, seq, tq, tk, top_k)
            h = _linear_residual(o, dsa_w_out, j, h, ada4, layer, 1, seq, tm, tn)
        last = layer == depth - 1
        h = _ffn(h, ada4, norm_g, ffn_wgu, ffn_wd, layer, 1, 2, seq, tm_ffn, tf,
                 final_g=final_g if last else None)
    return h.reshape(batch, seq, d)
```

```python
import functools

import jax
import jax.numpy as jnp
import numpy as np
from jax import lax
from jax.experimental import pallas as pl
from jax.experimental.pallas import tpu as pltpu

F32 = jnp.float32
BF16 = jnp.bfloat16

N_SUBLAYERS = 3
POOL_WINDOWS = (2, 4, 8, 16)
N_HEADS = 16
N_KV_HEADS = 4
KV_GROUP = N_HEADS // N_KV_HEADS
HEAD_DIM = 128
ROT_DIM = HEAD_DIM // 4
ROPE_THETA = 500000.0
IDX_HEADS = 16
IDX_DIM = 64
IDX_ROT_DIM = IDX_DIM // 4
TOPK_MAX = 256
EPS = 1e-6
NEG = -1e30
Q_W = N_HEADS * HEAD_DIM
KV_W = N_KV_HEADS * HEAD_DIM
QI_W = IDX_HEADS * IDX_DIM
DSA_MAIN_W = Q_W + 2 * KV_W + QI_W
DSA_SMALL_W = IDX_DIM + IDX_HEADS
Q_PRESCALE = HEAD_DIM ** -0.5 * 1.4426950408889634

LANES = 128
SUBLANES = 8
VMEM_LIMIT_BYTES = 56 * 1024 * 1024
INT32_MIN = -(2 ** 31)


def _cparams(semantics):
    return pltpu.CompilerParams(dimension_semantics=semantics,
                                vmem_limit_bytes=VMEM_LIMIT_BYTES)


def _rmsnorm(x, g):
    ms = jnp.mean(x * x, axis=-1, keepdims=True)
    return x * lax.rsqrt(ms + EPS) * g


def _rmsnorm_rows(ref, rows, g_ref):
    ref[rows, :] = _rmsnorm(ref[rows, :], g_ref[...])


NORM_SLAB = 16
NORM_UNROLL = 8
FFN_ROW_SLAB = 128
FFN_SLAB_SLOTS = 4
FFN_DOWN_COLS = 256


def _for_row_slabs(n_rows, slab, body, unroll=1):
    def step(r, carry):
        body(pl.ds(pl.multiple_of(r * slab, slab), slab))
        return carry
    lax.fori_loop(0, n_rows // slab, step, 0, unroll=unroll)


def _write_norm_mod(x_ref, g_ref, m_ref, sub, y_ref, copy_ref=None):
    shift = m_ref[3 * sub + 0:3 * sub + 1, :]
    gs = g_ref[...] * (1.0 + m_ref[3 * sub + 1:3 * sub + 2, :])
    slab = min(NORM_SLAB, x_ref.shape[0])

    def norm_rows(rows):
        x = x_ref[rows, :]
        r = lax.rsqrt(jnp.mean(x * x, axis=-1, keepdims=True) + EPS)
        y_ref[rows, :] = ((x * r) * gs + shift).astype(BF16)
        if copy_ref is not None:
            copy_ref[rows, :] = x

    _for_row_slabs(x_ref.shape[0], slab, norm_rows, unroll=NORM_UNROLL)


def _silu(x):
    return x * jax.nn.sigmoid(x)


def _dot(a, b):
    return jnp.dot(a, b, preferred_element_type=F32)


def _dot_t(a, b):
    return lax.dot_general(a, b, (((1,), (1,)), ((), ())), preferred_element_type=F32)


def _ada_kernel(n_tab_blocks, c_ref, w_ref, b_ref, pos_ref, invf_ref, o_ref, tab_ref):
    sc = _silu(c_ref[...]).astype(BF16)
    o_ref[...] = _dot(sc, w_ref[...].astype(BF16)) + b_ref[...]
    step = pl.program_id(0) * pl.num_programs(1) + pl.program_id(1)

    @pl.when(step < n_tab_blocks)
    def _():
        _write_rope_tables(pos_ref, invf_ref, tab_ref)


def _ada(c8, ada_w, ada_b, tn, positions):
    depth, d, n = ada_w.shape
    rows = c8.shape[0]
    n_steps = depth * (n // tn)
    t = positions.size
    rt = SUBLANES
    while t // rt > n_steps or t % rt:
        rt *= 2
    n_tab = t // rt
    nj = n // tn

    def tab_block(l, j):
        return jnp.minimum(l * nj + j, n_tab - 1)

    return pl.pallas_call(
        functools.partial(_ada_kernel, n_tab),
        grid=(depth, nj),
        in_specs=[
            pl.BlockSpec((rows, d), lambda l, j: (0, 0)),
            pl.BlockSpec((None, d, tn), lambda l, j: (l, 0, j)),
            pl.BlockSpec((None, 1, tn), lambda l, j: (l, 0, j)),
            pl.BlockSpec((rt, 1), lambda l, j: (tab_block(l, j), 0)),
            pl.BlockSpec((1, LANES), lambda l, j: (0, 0)),
        ],
        out_specs=[
            pl.BlockSpec((None, rows, tn), lambda l, j: (l, 0, j)),
            pl.BlockSpec((8, rt, LANES), lambda l, j: (0, tab_block(l, j), 0)),
        ],
        out_shape=[
            jax.ShapeDtypeStruct((depth, rows, n), F32),
            jax.ShapeDtypeStruct((8, t, LANES), F32),
        ],
        compiler_params=_cparams(("arbitrary", "arbitrary")),
        name="ada",
    )(c8, ada_w, ada_b.reshape(depth, 1, n), positions.reshape(t, 1), _rope_inv_freq_row())


def _ffn_kernel(sub, final, rs, *refs):
    if final:
        (x_hbm, g_ref, m_ref, wg_ref, wu_ref, wd_ref, fg_ref, o_hbm,
         acc_ref, y_ref, xbuf, in_sem, out_sem) = refs
    else:
        (x_hbm, g_ref, m_ref, wg_ref, wu_ref, wd_ref, o_hbm,
         acc_ref, y_ref, xbuf, in_sem, out_sem) = refs
    k = pl.program_id(1)
    nk = pl.num_programs(1)
    tm, d_out = acc_ref.shape
    n_slabs = tm // rs
    n_slots = xbuf.shape[0]
    n_first = min(n_slots, n_slabs)
    row0 = pl.program_id(0) * tm

    def x_copy(s, tile_row0=row0):
        slot = s % n_slots
        return pltpu.make_async_copy(x_hbm.at[pl.ds(tile_row0 + s * rs, rs), :], xbuf.at[slot],
                                     in_sem.at[slot])

    def o_copy(s):
        return pltpu.make_async_copy(acc_ref.at[pl.ds(s * rs, rs), :],
                                     o_hbm.at[pl.ds(row0 + s * rs, rs), :], out_sem.at[s])

    @pl.when(k == 0)
    def _():
        @pl.when(pl.program_id(0) == 0)
        def _():
            for s in range(n_first):
                x_copy(s).start()

        for s in range(n_slabs):
            x_copy(s).wait()
            pl.when(pl.program_id(0) > 0)(o_copy(s).wait)
            rows = pl.ds(s * rs, rs)
            _write_norm_mod(xbuf.at[s % n_slots], g_ref, m_ref, sub, y_ref.at[rows, :],
                            copy_ref=acc_ref.at[rows, :])
            if s + n_slots < n_slabs:
                x_copy(s + n_slots).start()

    @pl.when(jnp.logical_and(k == nk - 1, pl.program_id(0) + 1 < pl.num_programs(0)))
    def _():
        for s in range(n_first):
            x_copy(s, row0 + tm).start()

    y = y_ref[...]
    g = _dot(y, wg_ref[...].astype(BF16))
    u = _dot(y, wu_ref[...].astype(BF16))
    a = (_silu(g) * u).astype(BF16)
    half_gate = 0.5 * m_ref[3 * sub + 2:3 * sub + 3, :]
    nc = min(FFN_DOWN_COLS, d_out)
    for n in range(d_out // nc):
        cols = slice(n * nc, (n + 1) * nc)
        acc_ref[:, cols] += half_gate[:, cols] * _dot(a, wd_ref[:, cols].astype(BF16))

    @pl.when(k == nk - 1)
    def _():
        for s in range(n_slabs):
            if final:
                _for_row_slabs(
                    rs, min(NORM_SLAB, rs),
                    lambda rows, s=s: _rmsnorm_rows(acc_ref.at[pl.ds(s * rs, rs), :], rows, fg_ref),
                    unroll=NORM_UNROLL)
            o_copy(s).start()

        @pl.when(pl.program_id(0) == pl.num_programs(0) - 1)
        def _():
            for s in range(n_slabs):
                o_copy(s).wait()


def _ffn(h, ada4, norm_g, ffn_wgu, ffn_wd, layer, which, sub, seq, tm, tf, final_g=None):
    t, d = h.shape
    d_ff = ffn_wd.shape[2]
    nk = d_ff // tf
    tpb = seq // tm
    rs = min(FFN_ROW_SLAB, tm)
    final = final_g is not None
    in_specs = [
        pl.BlockSpec(memory_space=pl.ANY),
        pl.BlockSpec((1, d), lambda i, k: (0, 0)),
        pl.BlockSpec((None, None, 3 * N_SUBLAYERS, d), lambda i, k: (layer, i // tpb, 0, 0)),
        pl.BlockSpec((None, None, d, tf), lambda i, k: (layer, which, 0, k)),
        pl.BlockSpec((None, None, d, tf), lambda i, k: (layer, which, 0, nk + k)),
        pl.BlockSpec((None, None, tf, d), lambda i, k: (layer, which, k, 0)),
    ]
    args = [h, norm_g[layer, sub][None, :], ada4, ffn_wgu, ffn_wgu, ffn_wd]
    if final:
        in_specs.append(pl.BlockSpec((1, d), lambda i, k: (0, 0)))
        args.append(final_g[None, :])
    return pl.pallas_call(
        functools.partial(_ffn_kernel, sub, final, rs),
        grid=(t // tm, nk),
        in_specs=in_specs,
        out_specs=pl.BlockSpec(memory_space=pl.ANY),
        out_shape=jax.ShapeDtypeStruct((t, d), F32),
        scratch_shapes=[
            pltpu.VMEM((tm, d), F32),
            pltpu.VMEM((tm, d), BF16),
            pltpu.VMEM((FFN_SLAB_SLOTS, rs, d), F32),
            pltpu.SemaphoreType.DMA((FFN_SLAB_SLOTS,)),
            pltpu.SemaphoreType.DMA((tm // rs,)),
        ],
        compiler_params=_cparams(("arbitrary", "arbitrary")),
        name=f"ffn_l{layer}_{which}",
    )(*args)


def _keep_bf16_weight(w_ref, wbf_ref):
    j = pl.program_id(1)

    @pl.when(pl.program_id(0) == 0)
    def _():
        wbf_ref[j] = w_ref[...].astype(BF16)

    return wbf_ref[j]


def _resident_weight_spec(widx, kdim, tn, nj):
    return pl.BlockSpec((None, kdim, tn), lambda i, j: (widx, 0, jnp.where(i == 0, j, nj - 1)))


def _norm_linear_kernel(sub, x_ref, g_ref, m_ref, w_ref, o_ref, y_ref, wbf_ref):
    @pl.when(pl.program_id(1) == 0)
    def _():
        _write_norm_mod(x_ref, g_ref, m_ref, sub, y_ref)

    o_ref[...] = _dot(y_ref[...], _keep_bf16_weight(w_ref, wbf_ref))


def _norm_linear(h, ada4, g_row, w, widx, layer, sub, seq, tm, tn):
    t, d = h.shape
    n = w.shape[2]
    tpb = seq // tm
    return pl.pallas_call(
        functools.partial(_norm_linear_kernel, sub),
        grid=(t // tm, n // tn),
        in_specs=[
            pl.BlockSpec((tm, d), lambda i, j: (i, 0)),
            pl.BlockSpec((1, d), lambda i, j: (0, 0)),
            pl.BlockSpec((None, None, 3 * N_SUBLAYERS, d), lambda i, j: (layer, i // tpb, 0, 0)),
            _resident_weight_spec(widx, d, tn, n // tn),
        ],
        out_specs=pl.BlockSpec((tm, tn), lambda i, j: (i, j)),
        out_shape=jax.ShapeDtypeStruct((t, n), F32),
        scratch_shapes=[pltpu.VMEM((tm, d), BF16), pltpu.VMEM((n // tn, d, tn), BF16)],
        compiler_params=_cparams(("arbitrary", "arbitrary")),
        name=f"norm_linear_l{layer}",
    )(h, g_row, ada4, w)


def _pool_kernel(u_ref, w_ref, ls_ref, o_ref):
    g = pl.program_id(1)
    u = u_ref[...]
    t = lax.broadcasted_iota(jnp.int32, u.shape, 0)

    def shifted(v, d):
        return jnp.where(t >= d, pltpu.roll(v, d, 0), 0.0)

    for gi, win in enumerate(POOL_WINDOWS):
        @pl.when(g == gi)
        def _(win=win):
            acc = u
            d = 1
            while d < win:
                acc = acc + shifted(acc, d)
                d *= 2
            cnt = jnp.minimum(t + 1, win).astype(F32)
            p = (acc / cnt - u).astype(BF16)
            v = _dot(p, w_ref[...].astype(BF16)) * ls_ref[...]
            o_ref[...] = v.astype(BF16)


def _pool_core(u, w_grp, ls, widx, batch, seq):
    t, d = u.shape
    _, groups, c, _ = w_grp.shape
    return pl.pallas_call(
        _pool_kernel,
        grid=(batch, groups),
        in_specs=[
            pl.BlockSpec((seq, c), lambda b, g: (b, g)),
            pl.BlockSpec((None, None, c, c), lambda b, g: (widx, g, 0, 0)),
            pl.BlockSpec((None, 1, c), lambda b, g: (widx, 0, g)),
        ],
        out_specs=pl.BlockSpec((seq, c), lambda b, g: (b, g)),
        out_shape=jax.ShapeDtypeStruct((t, d), BF16),
        compiler_params=_cparams(("arbitrary", "arbitrary")),
        name="pool_core",
    )(u, w_grp, ls.reshape(ls.shape[0], 1, d))


def _linear_residual_kernel(sub, a_ref, w_ref, h_ref, m_ref, o_ref, wbf_ref):
    gate = m_ref[3 * sub + 2:3 * sub + 3, :]
    o_ref[...] = h_ref[...] + gate * _dot(a_ref[...], _keep_bf16_weight(w_ref, wbf_ref))


def _linear_residual(a, w, widx, h, ada4, layer, sub, seq, tm, tn):
    t, kdim = a.shape
    n = w.shape[2]
    tpb = seq // tm
    return pl.pallas_call(
        functools.partial(_linear_residual_kernel, sub),
        grid=(t // tm, n // tn),
        in_specs=[
            pl.BlockSpec((tm, kdim), lambda i, j: (i, 0)),
            _resident_weight_spec(widx, kdim, tn, n // tn),
            pl.BlockSpec((tm, tn), lambda i, j: (i, j)),
            pl.BlockSpec((None, None, 3 * N_SUBLAYERS, tn), lambda i, j: (layer, i // tpb, 0, j)),
        ],
        out_specs=pl.BlockSpec((tm, tn), lambda i, j: (i, j)),
        out_shape=jax.ShapeDtypeStruct((t, n), F32),
        scratch_shapes=[pltpu.VMEM((n // tn, kdim, tn), BF16)],
        compiler_params=_cparams(("arbitrary", "arbitrary")),
        name=f"linear_residual_l{layer}",
    )(a, w, h, ada4)


ROPE_SET_Q, ROPE_SET_K, ROPE_SET_NONE, ROPE_SET_IDX = 0, 1, 2, 3
HALF_LANES = LANES // 2


def _write_rope_tables(pos_ref, invf_ref, o_ref):
    pos = pos_ref[...].astype(F32)
    lane = lax.broadcasted_iota(jnp.int32, (pos.shape[0], LANES), 1)
    ang = pos * invf_ref[...]
    cos = jnp.cos(ang)
    sin = jnp.where(lane < HALF_LANES, -jnp.sin(ang), jnp.sin(ang))
    rot_big = (lane & (HALF_LANES - 1)) < ROT_DIM // 2
    cb = jnp.where(rot_big, cos, 1.0)
    sb = jnp.where(rot_big, sin, 0.0)
    rot_idx = (lane & (HALF_LANES // 2 - 1)) < IDX_ROT_DIM // 2
    even_quarter = (lane & (HALF_LANES // 2)) == 0
    shift = ROT_DIM // 2

    def idx_table(v, rest):
        from_above = pltpu.roll(v, LANES - shift, 1)
        from_below = pltpu.roll(v, shift, 1)
        return jnp.where(rot_idx, jnp.where(even_quarter, from_above, from_below), rest)

    ci = idx_table(cos, 1.0)
    si = idx_table(sin, 0.0)
    o_ref[2 * ROPE_SET_Q] = cb * Q_PRESCALE
    o_ref[2 * ROPE_SET_Q + 1] = sb * Q_PRESCALE
    o_ref[2 * ROPE_SET_K] = cb
    o_ref[2 * ROPE_SET_K + 1] = sb
    o_ref[2 * ROPE_SET_NONE] = jnp.ones_like(cb)
    o_ref[2 * ROPE_SET_NONE + 1] = jnp.zeros_like(cb)
    o_ref[2 * ROPE_SET_IDX] = ci
    o_ref[2 * ROPE_SET_IDX + 1] = si


def _rope_inv_freq_row():
    def inv_freq(rot_dim):
        half = rot_dim // 2
        return ROPE_THETA ** (-jnp.arange(half, dtype=F32) / half)

    half_row = jnp.concatenate([
        inv_freq(ROT_DIM), inv_freq(IDX_ROT_DIM),
        jnp.zeros((HALF_LANES - ROT_DIM // 2 - IDX_ROT_DIM // 2,), F32)])
    return jnp.tile(half_row, 2)[None, :]


def _rope(x, c, s):
    return x * c + pltpu.roll(x, HALF_LANES, 1) * s


PERM_BIG, PERM_NONE, PERM_IDX = 0, 1, 2


def _lane_permutations():
    h, p = ROT_DIM // 2, (HEAD_DIM - ROT_DIM) // 2
    big = np.concatenate([np.arange(0, h), np.arange(ROT_DIM, ROT_DIM + p),
                          np.arange(h, ROT_DIM), np.arange(ROT_DIM + p, HEAD_DIM)])
    lo, hi = _idx_lo_hi_dims()
    idx = np.concatenate([lo, IDX_DIM + lo, hi, IDX_DIM + hi])
    mats = np.zeros((3, LANES, LANES), np.float32)
    for kind, src in ((PERM_BIG, big), (PERM_NONE, np.arange(LANES)), (PERM_IDX, idx)):
        mats[kind, np.arange(LANES), src] = 1.0
    return jnp.asarray(mats, BF16)


def _idx_lo_hi_dims():
    h, p = IDX_ROT_DIM // 2, (IDX_DIM - IDX_ROT_DIM) // 2
    lo = np.concatenate([np.arange(0, h), np.arange(IDX_ROT_DIM, IDX_ROT_DIM + p)])
    hi = np.concatenate([np.arange(h, IDX_ROT_DIM), np.arange(IDX_ROT_DIM + p, IDX_DIM)])
    return lo, hi


def _dsa_small_weights(w_tail):
    h, p = IDX_ROT_DIM // 2, (IDX_DIM - IDX_ROT_DIM) // 2
    klo = jnp.concatenate([w_tail[:, 0:h], w_tail[:, IDX_ROT_DIM:IDX_ROT_DIM + p]], axis=1)
    khi = jnp.concatenate([w_tail[:, h:IDX_ROT_DIM], w_tail[:, IDX_ROT_DIM + p:IDX_DIM]], axis=1)
    wi = jnp.pad(w_tail[:, IDX_DIM:], ((0, 0), (0, LANES - IDX_DIM - IDX_HEADS)))
    return jnp.concatenate([klo, khi, wi], axis=1).astype(BF16)


def _dsa_proj_kernel(sub, tn, x_hbm, g_ref, m_ref, w_ref, ws_ref, perm_ref, tab_ref,
                     o_ref, ok_ref, ow_ref, y_ref, wbf_ref, x_ref, x_sem):
    i = pl.program_id(0)
    j = pl.program_id(1)
    tm = x_ref.shape[0]
    first_k, first_v, first_qi = Q_W // tn, (Q_W + KV_W) // tn, (Q_W + 2 * KV_W) // tn

    def x_copy(tile):
        return pltpu.make_async_copy(x_hbm.at[pl.ds(tile * tm, tm), :], x_ref, x_sem.at[0])

    @pl.when(j == 0)
    def _():
        pl.when(i == 0)(x_copy(i).start)
        x_copy(i).wait()
        _write_norm_mod(x_ref, g_ref, m_ref, sub, y_ref)
        pl.when(i + 1 < pl.num_programs(0))(x_copy(i + 1).start)
        small = _dot(y_ref[...], ws_ref[...])
        by32 = pltpu.roll(small, HALF_LANES // 2, 1)
        by64 = pltpu.roll(small, HALF_LANES, 1)
        quarter = lax.broadcasted_iota(jnp.int32, small.shape, 1) // (HALF_LANES // 2)
        k_even = jnp.where(quarter == 0, small, jnp.where(quarter == 2, by32, 0.0))
        k_odd = jnp.where(quarter == 1, by32, jnp.where(quarter == 3, by64, 0.0))
        for s, kk in enumerate((k_even, k_odd)):
            ok_ref[:, s * LANES:(s + 1) * LANES] = _rope(
                kk, tab_ref[2 * ROPE_SET_IDX], tab_ref[2 * ROPE_SET_IDX + 1]).astype(BF16)
        ow_ref[...] = jnp.where(quarter == 0, by64, 0.0)

    @pl.when(pl.program_id(0) == 0)
    def _():
        kind = jnp.where(j < first_v, PERM_BIG, jnp.where(j < first_qi, PERM_NONE, PERM_IDX))
        perm = perm_ref[kind]
        for sl in range(tn // LANES):
            rows = slice(sl * LANES, (sl + 1) * LANES)
            wbf_ref[j, rows, :] = _dot(perm, w_ref[rows, :].astype(BF16)).astype(BF16)

    rope_set = jnp.where(j < first_k, ROPE_SET_Q,
                         jnp.where(j < first_v, ROPE_SET_K,
                                   jnp.where(j < first_qi, ROPE_SET_NONE, ROPE_SET_IDX)))
    c = tab_ref[2 * rope_set]
    s = tab_ref[2 * rope_set + 1]
    y = y_ref[...]
    wide = 2 * LANES
    for part in range(tn // wide):
        acc = _dot_t(y, wbf_ref[j, part * wide:(part + 1) * wide, :])
        for sl in range(wide // LANES):
            cols = slice(part * wide + sl * LANES, part * wide + (sl + 1) * LANES)
            o_ref[:, cols] = _rope(acc[:, sl * LANES:(sl + 1) * LANES], c, s).astype(BF16)


def _dsa_proj(h, ada4, g_row, w_in, widx, w_small, tabs, layer, sub, seq, tm, tn):
    t, d = h.shape
    tpb = seq // tm
    return pl.pallas_call(
        functools.partial(_dsa_proj_kernel, sub, tn),
        grid=(t // tm, DSA_MAIN_W // tn),
        in_specs=[
            pl.BlockSpec(memory_space=pl.ANY),
            pl.BlockSpec((1, d), lambda i, j: (0, 0)),
            pl.BlockSpec((None, None, 3 * N_SUBLAYERS, d), lambda i, j: (layer, i // tpb, 0, 0)),
            pl.BlockSpec((None, tn, d),
                         lambda i, j: (widx, jnp.where(i == 0, j, DSA_MAIN_W // tn - 1), 0)),
            pl.BlockSpec((d, LANES), lambda i, j: (0, 0)),
            pl.BlockSpec((3, LANES, LANES), lambda i, j: (0, 0, 0)),
            pl.BlockSpec((8, tm, LANES), lambda i, j: (0, i, 0)),
        ],
        out_specs=[
            pl.BlockSpec((tm, tn), lambda i, j: (i, j)),
            pl.BlockSpec((tm, 2 * LANES), lambda i, j: (i, 0)),
            pl.BlockSpec((tm, LANES), lambda i, j: (i, 0)),
        ],
        out_shape=[
            jax.ShapeDtypeStruct((t, DSA_MAIN_W), BF16),
            jax.ShapeDtypeStruct((t, 2 * LANES), BF16),
            jax.ShapeDtypeStruct((t, LANES), F32),
        ],
        scratch_shapes=[pltpu.VMEM((tm, d), BF16),
                        pltpu.VMEM((DSA_MAIN_W // tn, tn, d), BF16),
                        pltpu.VMEM((tm, d), F32),
                        pltpu.SemaphoreType.DMA((1,))],
        compiler_params=_cparams(("arbitrary", "arbitrary")),
        name=f"dsa_proj_l{layer}",
    )(h, g_row, ada4, w_in, w_small, _lane_permutations(), tabs)


def _dsa_attn_kernel(tq, tk, top_k, q_ref, qi_ref, wi_ref, k_ref, v_ref, kk_ref, o_ref,
                     score_ref, bias_ref, thr_ref, wib_ref, qg_ref, m_ref, acc_ref):
    i = pl.program_id(1)
    n_chunks = ((i + 1) * tq + tk - 1) // tk
    max_chunks = score_ref.shape[1] // tk
    t_pos = i * tq + lax.broadcasted_iota(jnp.int32, (tq, tk), 0)
    k_iota = lax.broadcasted_iota(jnp.int32, (tq, tk), 1)
    n_cb = tk // LANES

    for hh in range(IDX_HEADS):
        wib_ref[hh] = jnp.broadcast_to(wi_ref[:, hh:hh + 1], (tq, LANES))
    for g in range(N_KV_HEADS):
        for r in range(KV_GROUP):
            hd = g * KV_GROUP + r
            qg_ref[g, r * tq:(r + 1) * tq, :] = q_ref[:, hd * HEAD_DIM:(hd + 1) * HEAD_DIM]

    def score_chunk(c, carry):
        off = pl.multiple_of(c * tk, tk)
        k_even = kk_ref[pl.ds(off, tk), 0:LANES]
        k_odd = kk_ref[pl.ds(off, tk), LANES:2 * LANES]
        acc = [jnp.zeros((tq, LANES), F32) for _ in range(n_cb)]
        for pair in range(IDX_HEADS // 2):
            qp = qi_ref[:, pair * LANES:(pair + 1) * LANES]
            for par, kop in enumerate((k_even, k_odd)):
                r = _dot_t(qp, kop)
                w = wib_ref[2 * pair + par]
                for cb in range(n_cb):
                    acc[cb] = acc[cb] + w * jnp.maximum(r[:, cb * LANES:(cb + 1) * LANES], 0.0)
        sc = jnp.concatenate(acc, axis=1)
        score_ref[:, pl.ds(off, tk)] = jnp.where(k_iota + off <= t_pos, sc, NEG)
        return carry

    lax.fori_loop(0, n_chunks, score_chunk, 0)

    def key_to_f32(key):
        return pltpu.bitcast(key ^ ((key >> 31) & 0x7FFFFFFF), F32)

    k_f = float(top_k)

    def select(n_static):
        def bit_step(n, thr_key):
            cand = thr_key + lax.shift_left(jnp.int32(1), 31 - n)
            thr = key_to_f32(cand)
            cnt = jnp.zeros((tq, LANES), F32)
            for cb in range(n_static * n_cb):
                s = score_ref[:, cb * LANES:(cb + 1) * LANES]
                cnt = cnt + jnp.where(s >= thr, 1.0, 0.0)
            total = jnp.sum(cnt, axis=1, keepdims=True)
            return jnp.where(total >= k_f, cand, thr_key)

        thr_key = lax.fori_loop(0, 32, bit_step, jnp.full((tq, LANES), INT32_MIN, jnp.int32),
                                unroll=4)
        thr_ref[...] = key_to_f32(thr_key)

    for n_static in range(1, max_chunks + 1):
        pl.when(n_chunks == n_static)(functools.partial(select, n_static))

    def bias_chunk(c, carry):
        off = pl.multiple_of(c * tk, tk)
        thr = thr_ref[...]
        causal = k_iota + off <= t_pos
        for cb in range(n_cb):
            cols = pl.ds(off + cb * LANES, LANES)
            sel = score_ref[:, cols] >= thr
            bias_ref[:, cols] = jnp.where(
                sel, jnp.where(causal[:, cb * LANES:(cb + 1) * LANES], 0.0, NEG), NEG)
        return carry

    lax.fori_loop(0, n_chunks, bias_chunk, 0)

    def count_rows(pred):
        def body(c, cnt):
            off = pl.multiple_of(c * tk, tk)
            thr = thr_ref[...]
            for cb in range(n_cb):
                s = score_ref[:, pl.ds(off + cb * LANES, LANES)]
                cnt = cnt + jnp.where(pred(s, thr), 1.0, 0.0)
            return cnt
        cnt = lax.fori_loop(0, n_chunks, body, jnp.zeros((tq, LANES), F32))
        return jnp.sum(cnt, axis=1, keepdims=True)

    n_ge = count_rows(lambda s, thr: s >= thr)

    @pl.when(jnp.max(n_ge) > k_f)
    def _():
        n_take = k_f - count_rows(lambda s, thr: s > thr)
        before = jnp.where(
            lax.broadcasted_iota(jnp.int32, (tk, tk), 0) < lax.broadcasted_iota(jnp.int32, (tk, tk), 1),
            1.0, 0.0).astype(BF16)

        def tie_chunk(c, seen):
            off = pl.multiple_of(c * tk, tk)
            sc = score_ref[:, pl.ds(off, tk)]
            thr = jnp.concatenate([thr_ref[...]] * n_cb, axis=1)
            eq = jnp.where(sc == thr, 1.0, 0.0)
            rank = _dot(eq.astype(BF16), before) + seen
            keep = jnp.where(sc > thr, 1.0, jnp.where(rank < n_take, eq, 0.0))
            causal = k_iota + off <= t_pos
            bias_ref[:, pl.ds(off, tk)] = jnp.where(causal, jnp.where(keep > 0.0, 0.0, NEG), NEG)
            return seen + jnp.sum(eq, axis=1, keepdims=True)

        lax.fori_loop(0, n_chunks, tie_chunk, jnp.zeros((tq, 1), F32))

    rows = KV_GROUP * tq
    m_ref[...] = jnp.full(m_ref.shape, NEG, F32)
    acc_ref[...] = jnp.zeros(acc_ref.shape, F32)
    def attend(off, width):
        ones = jnp.ones((width, LANES), BF16)
        b = bias_ref[:, pl.ds(off, width)]
        for g in range(N_KV_HEADS):
            kc = k_ref[pl.ds(off, width), g * HEAD_DIM:(g + 1) * HEAD_DIM]
            vc = v_ref[pl.ds(off, width), g * HEAD_DIM:(g + 1) * HEAD_DIM]
            s = _dot_t(qg_ref[g], kc)
            s = (s.reshape(KV_GROUP, tq, width) + b[None]).reshape(rows, width)
            m_old = m_ref[g]
            m_new = jnp.maximum(m_old, jnp.max(s, axis=1, keepdims=True))
            alpha = jnp.exp2(m_old - m_new)
            p = jnp.exp2(s - jnp.concatenate([m_new] * (width // LANES), axis=1))
            pv = _dot(p.astype(BF16), jnp.concatenate([vc, ones], axis=1))
            acc_ref[g] = jnp.concatenate([alpha, alpha], axis=1) * acc_ref[g] + pv
            m_ref[g] = m_new

    def attn_chunk(c, carry):
        attend(pl.multiple_of(c * tk, tk), tk)
        return carry

    lax.fori_loop(0, n_chunks, attn_chunk, 0)

    for g in range(N_KV_HEADS):
        out = acc_ref[g, :, 0:HEAD_DIM] / acc_ref[g, :, HEAD_DIM:2 * HEAD_DIM]
        for r in range(KV_GROUP):
            hd = g * KV_GROUP + r
            o_ref[:, hd * HEAD_DIM:(hd + 1) * HEAD_DIM] = out[r * tq:(r + 1) * tq].astype(BF16)


def _dsa_attn(main, kk, wi, batch, seq, tq, tk, top_k):
    t = main.shape[0]
    nq = seq // tq
    rows = KV_GROUP * tq
    return pl.pallas_call(
        functools.partial(_dsa_attn_kernel, tq, tk, top_k),
        grid=(batch, nq),
        in_specs=[
            pl.BlockSpec((tq, Q_W), lambda b, i: (b * nq + i, 0)),
            pl.BlockSpec((tq, QI_W), lambda b, i: (b * nq + i, (Q_W + 2 * KV_W) // QI_W)),
            pl.BlockSpec((tq, LANES), lambda b, i: (b * nq + i, 0)),
            pl.BlockSpec((seq, KV_W), lambda b, i: (b, Q_W // KV_W)),
            pl.BlockSpec((seq, KV_W), lambda b, i: (b, Q_W // KV_W + 1)),
            pl.BlockSpec((seq, 2 * LANES), lambda b, i: (b, 0)),
        ],
        out_specs=pl.BlockSpec((tq, Q_W), lambda b, i: (b * nq + i, 0)),
        out_shape=jax.ShapeDtypeStruct((t, Q_W), BF16),
        scratch_shapes=[
            pltpu.VMEM((tq, seq), F32),
            pltpu.VMEM((tq, seq), F32),
            pltpu.VMEM((tq, LANES), F32),
            pltpu.VMEM((IDX_HEADS, tq, LANES), F32),
            pltpu.VMEM((N_KV_HEADS, rows, HEAD_DIM), BF16),
            pltpu.VMEM((N_KV_HEADS, rows, LANES), F32),
            pltpu.VMEM((N_KV_HEADS, rows, 2 * HEAD_DIM), F32),
        ],
        compiler_params=_cparams(("arbitrary", "arbitrary")),
        name="dsa_attn",
    )(main, main, wi, main, main, kk)


def _tiles(seq, d_ff):
    tm = min(1024, seq)
    tm_ffn = min(2048, seq)
    tf = 256 if d_ff % 256 == 0 else d_ff
    tq = min(256, seq)
    tk = min(256, seq)
    return tm, tm_ffn, tf, tq, tk


def kernel(x, c, positions, ada_w, ada_b, norm_g, final_g, ffn_wgu, ffn_wd,
           pool_w_in, pool_w_grp, pool_scale, pool_w_out, dsa_w_in, dsa_w_out):
    batch, seq, d = x.shape
    depth = ada_w.shape[0]
    d_ff = ffn_wd.shape[2]
    t = batch * seq
    tm, tm_ffn, tf, tq, tk = _tiles(seq, d_ff)
    tn = min(512, d)
    top_k = min(TOPK_MAX, seq // 4)
    assert seq % tm == 0 and seq % tm_ffn == 0 and seq % tq == 0 and seq % tk == 0 and tk >= top_k
    assert dsa_w_in.shape[2] == DSA_MAIN_W + DSA_SMALL_W and DSA_MAIN_W % tn == 0

    rows = -(-batch // SUBLANES) * SUBLANES
    c8 = jnp.pad(c, ((0, rows - batch), (0, 0)))
    ada_n = ada_w.shape[2]
    ada, tabs = _ada(c8, ada_w, ada_b, 1024 if ada_n % 1024 == 0 else d, positions)
    ada4 = ada.reshape(depth, rows, 3 * N_SUBLAYERS, d)

    h = x.reshape(t, d)
    for layer in range(depth):
        j = layer // 2
        h = _ffn(h, ada4, norm_g, ffn_wgu, ffn_wd, layer, 0, 0, seq, tm_ffn, tf)
        g_row = norm_g[layer, 1][None, :]
        if layer % 2 == 0:
            u = _norm_linear(h, ada4, g_row, pool_w_in, j, layer, 1, seq, tm, tn)
            a = _pool_core(u, pool_w_grp, pool_scale, j, batch, seq)
            h = _linear_residual(a, pool_w_out, j, h, ada4, layer, 1, seq, tm, tn)
        else:
            w_small = _dsa_small_weights(dsa_w_in[j, :, DSA_MAIN_W:])
            main, kk, wi = _dsa_proj(h, ada4, g_row, jnp.swapaxes(dsa_w_in, 1, 2), j, w_small,
                                     tabs, layer, 1, seq, tm, tn)
            o = _dsa_attn(main, kk, wi, batch, seq, tq, tk, top_k)
            h = _linear_residual(o, dsa_w_out, j, h, ada4, layer, 1, seq, tm, tn)
        last = layer == depth - 1
        h = _ffn(h, ada4, norm_g, ffn_wgu, ffn_wd, layer, 1, 2, seq, tm_ffn, tf,
                 final_g=final_g if last else None)
    return h.reshape(batch, seq, d)
```

```python
import functools

import jax
import jax.numpy as jnp
import numpy as np
from jax import lax
from jax.experimental import pallas as pl
from jax.experimental.pallas import tpu as pltpu

F32 = jnp.float32
BF16 = jnp.bfloat16

N_SUBLAYERS = 3
POOL_WINDOWS = (2, 4, 8, 16)
N_HEADS = 16
N_KV_HEADS = 4
KV_GROUP = N_HEADS // N_KV_HEADS
HEAD_DIM = 128
ROT_DIM = HEAD_DIM // 4
ROPE_THETA = 500000.0
IDX_HEADS = 16
IDX_DIM = 64
IDX_ROT_DIM = IDX_DIM // 4
TOPK_MAX = 256
EPS = 1e-6
NEG = -1e30
Q_W = N_HEADS * HEAD_DIM
KV_W = N_KV_HEADS * HEAD_DIM
QI_W = IDX_HEADS * IDX_DIM
DSA_MAIN_W = Q_W + 2 * KV_W + QI_W
DSA_SMALL_W = IDX_DIM + IDX_HEADS
Q_PRESCALE = HEAD_DIM ** -0.5 * 1.4426950408889634

LANES = 128
SUBLANES = 8
VMEM_LIMIT_BYTES = 56 * 1024 * 1024
INT32_MIN = -(2 ** 31)
KEY_BITS = 32
SELECT_UNROLL = 4
N_ROPE_TABLES = 8


def _cparams(semantics):
    return pltpu.CompilerParams(dimension_semantics=semantics,
                                vmem_limit_bytes=VMEM_LIMIT_BYTES)


def _rmsnorm(x, g):
    ms = jnp.mean(x * x, axis=-1, keepdims=True)
    return x * lax.rsqrt(ms + EPS) * g


def _rmsnorm_rows(ref, rows, g_ref):
    ref[rows, :] = _rmsnorm(ref[rows, :], g_ref[...])


NORM_SLAB = 16
NORM_UNROLL = 8
FFN_ROW_SLAB = 128
FFN_SLAB_SLOTS = 4
FFN_DOWN_COLS = 256


def _for_row_slabs(n_rows, slab, body, unroll=1):
    def step(r, carry):
        body(pl.ds(pl.multiple_of(r * slab, slab), slab))
        return carry
    lax.fori_loop(0, n_rows // slab, step, 0, unroll=unroll)


def _write_norm_mod(x_ref, g_ref, m_ref, sub, y_ref, copy_ref=None):
    shift = m_ref[3 * sub + 0:3 * sub + 1, :]
    gs = g_ref[...] * (1.0 + m_ref[3 * sub + 1:3 * sub + 2, :])
    slab = min(NORM_SLAB, x_ref.shape[0])

    def norm_rows(rows):
        x = x_ref[rows, :]
        r = lax.rsqrt(jnp.mean(x * x, axis=-1, keepdims=True) + EPS)
        y_ref[rows, :] = ((x * r) * gs + shift).astype(BF16)
        if copy_ref is not None:
            copy_ref[rows, :] = x

    _for_row_slabs(x_ref.shape[0], slab, norm_rows, unroll=NORM_UNROLL)


def _silu(x):
    return x * jax.nn.sigmoid(x)


def _dot(a, b):
    return jnp.dot(a, b, preferred_element_type=F32)


def _dot_t(a, b):
    return lax.dot_general(a, b, (((1,), (1,)), ((), ())), preferred_element_type=F32)


def _ada_kernel(n_tab_blocks, c_ref, w_ref, b_ref, pos_ref, invf_ref, o_ref, tab_ref):
    sc = _silu(c_ref[...]).astype(BF16)
    o_ref[...] = _dot(sc, w_ref[...].astype(BF16)) + b_ref[...]
    step = pl.program_id(0) * pl.num_programs(1) + pl.program_id(1)

    @pl.when(step < n_tab_blocks)
    def _():
        _write_rope_tables(pos_ref, invf_ref, tab_ref)


def _ada(c8, ada_w, ada_b, tn, positions):
    depth, d, n = ada_w.shape
    rows = c8.shape[0]
    n_steps = depth * (n // tn)
    t = positions.size
    rt = SUBLANES
    while t // rt > n_steps or t % rt:
        rt *= 2
    n_tab = t // rt
    nj = n // tn

    def tab_block(l, j):
        return jnp.minimum(l * nj + j, n_tab - 1)

    return pl.pallas_call(
        functools.partial(_ada_kernel, n_tab),
        grid=(depth, nj),
        in_specs=[
            pl.BlockSpec((rows, d), lambda l, j: (0, 0)),
            pl.BlockSpec((None, d, tn), lambda l, j: (l, 0, j)),
            pl.BlockSpec((None, 1, tn), lambda l, j: (l, 0, j)),
            pl.BlockSpec((rt, 1), lambda l, j: (tab_block(l, j), 0)),
            pl.BlockSpec((1, LANES), lambda l, j: (0, 0)),
        ],
        out_specs=[
            pl.BlockSpec((None, rows, tn), lambda l, j: (l, 0, j)),
            pl.BlockSpec((N_ROPE_TABLES, rt, LANES), lambda l, j: (0, tab_block(l, j), 0)),
        ],
        out_shape=[
            jax.ShapeDtypeStruct((depth, rows, n), F32),
            jax.ShapeDtypeStruct((N_ROPE_TABLES, t, LANES), F32),
        ],
        compiler_params=_cparams(("arbitrary", "arbitrary")),
        name="ada",
    )(c8, ada_w, ada_b.reshape(depth, 1, n), positions.reshape(t, 1), _rope_inv_freq_row())


def _ffn_kernel(sub, final, rs, *refs):
    if final:
        (x_hbm, g_ref, m_ref, wg_ref, wu_ref, wd_ref, fg_ref, o_hbm,
         acc_ref, y_ref, xbuf, in_sem, out_sem) = refs
    else:
        (x_hbm, g_ref, m_ref, wg_ref, wu_ref, wd_ref, o_hbm,
         acc_ref, y_ref, xbuf, in_sem, out_sem) = refs
    k = pl.program_id(1)
    nk = pl.num_programs(1)
    tm, d_out = acc_ref.shape
    n_slabs = tm // rs
    n_slots = xbuf.shape[0]
    n_first = min(n_slots, n_slabs)
    row0 = pl.program_id(0) * tm

    def x_copy(s, tile_row0=row0):
        slot = s % n_slots
        return pltpu.make_async_copy(x_hbm.at[pl.ds(tile_row0 + s * rs, rs), :], xbuf.at[slot],
                                     in_sem.at[slot])

    def o_copy(s):
        return pltpu.make_async_copy(acc_ref.at[pl.ds(s * rs, rs), :],
                                     o_hbm.at[pl.ds(row0 + s * rs, rs), :], out_sem.at[s])

    @pl.when(k == 0)
    def _():
        @pl.when(pl.program_id(0) == 0)
        def _():
            for s in range(n_first):
                x_copy(s).start()

        for s in range(n_slabs):
            x_copy(s).wait()
            pl.when(pl.program_id(0) > 0)(o_copy(s).wait)
            rows = pl.ds(s * rs, rs)
            _write_norm_mod(xbuf.at[s % n_slots], g_ref, m_ref, sub, y_ref.at[rows, :],
                            copy_ref=acc_ref.at[rows, :])
            if s + n_slots < n_slabs:
                x_copy(s + n_slots).start()

    @pl.when(jnp.logical_and(k == nk - 1, pl.program_id(0) + 1 < pl.num_programs(0)))
    def _():
        for s in range(n_first):
            x_copy(s, row0 + tm).start()

    y = y_ref[...]
    g = _dot(y, wg_ref[...].astype(BF16))
    u = _dot(y, wu_ref[...].astype(BF16))
    a = (_silu(g) * u).astype(BF16)
    half_gate = 0.5 * m_ref[3 * sub + 2:3 * sub + 3, :]
    nc = min(FFN_DOWN_COLS, d_out)
    for n in range(d_out // nc):
        cols = slice(n * nc, (n + 1) * nc)
        acc_ref[:, cols] += half_gate[:, cols] * _dot(a, wd_ref[:, cols].astype(BF16))

    @pl.when(k == nk - 1)
    def _():
        for s in range(n_slabs):
            if final:
                _for_row_slabs(
                    rs, min(NORM_SLAB, rs),
                    lambda rows, s=s: _rmsnorm_rows(acc_ref.at[pl.ds(s * rs, rs), :], rows, fg_ref),
                    unroll=NORM_UNROLL)
            o_copy(s).start()

        @pl.when(pl.program_id(0) == pl.num_programs(0) - 1)
        def _():
            for s in range(n_slabs):
                o_copy(s).wait()


def _ffn(h, ada4, norm_g, ffn_wgu, ffn_wd, layer, which, sub, seq, tm, tf, final_g=None):
    t, d = h.shape
    d_ff = ffn_wd.shape[2]
    nk = d_ff // tf
    tpb = seq // tm
    rs = min(FFN_ROW_SLAB, tm)
    final = final_g is not None
    in_specs = [
        pl.BlockSpec(memory_space=pl.ANY),
        pl.BlockSpec((1, d), lambda i, k: (0, 0)),
        pl.BlockSpec((None, None, 3 * N_SUBLAYERS, d), lambda i, k: (layer, i // tpb, 0, 0)),
        pl.BlockSpec((None, None, d, tf), lambda i, k: (layer, which, 0, k)),
        pl.BlockSpec((None, None, d, tf), lambda i, k: (layer, which, 0, nk + k)),
        pl.BlockSpec((None, None, tf, d), lambda i, k: (layer, which, k, 0)),
    ]
    args = [h, norm_g[layer, sub][None, :], ada4, ffn_wgu, ffn_wgu, ffn_wd]
    if final:
        in_specs.append(pl.BlockSpec((1, d), lambda i, k: (0, 0)))
        args.append(final_g[None, :])
    return pl.pallas_call(
        functools.partial(_ffn_kernel, sub, final, rs),
        grid=(t // tm, nk),
        in_specs=in_specs,
        out_specs=pl.BlockSpec(memory_space=pl.ANY),
        out_shape=jax.ShapeDtypeStruct((t, d), F32),
        scratch_shapes=[
            pltpu.VMEM((tm, d), F32),
            pltpu.VMEM((tm, d), BF16),
            pltpu.VMEM((FFN_SLAB_SLOTS, rs, d), F32),
            pltpu.SemaphoreType.DMA((FFN_SLAB_SLOTS,)),
            pltpu.SemaphoreType.DMA((tm // rs,)),
        ],
        compiler_params=_cparams(("arbitrary", "arbitrary")),
        name=f"ffn_l{layer}_{which}",
    )(*args)


def _keep_bf16_weight(w_ref, wbf_ref):
    j = pl.program_id(1)

    @pl.when(pl.program_id(0) == 0)
    def _():
        wbf_ref[j] = w_ref[...].astype(BF16)

    return wbf_ref[j]


def _resident_weight_spec(widx, kdim, tn, nj):
    return pl.BlockSpec((None, kdim, tn), lambda i, j: (widx, 0, jnp.where(i == 0, j, nj - 1)))


def _norm_linear_kernel(sub, x_ref, g_ref, m_ref, w_ref, o_ref, y_ref, wbf_ref):
    @pl.when(pl.program_id(1) == 0)
    def _():
        _write_norm_mod(x_ref, g_ref, m_ref, sub, y_ref)

    o_ref[...] = _dot(y_ref[...], _keep_bf16_weight(w_ref, wbf_ref))


def _norm_linear(h, ada4, g_row, w, widx, layer, sub, seq, tm, tn):
    t, d = h.shape
    n = w.shape[2]
    tpb = seq // tm
    return pl.pallas_call(
        functools.partial(_norm_linear_kernel, sub),
        grid=(t // tm, n // tn),
        in_specs=[
            pl.BlockSpec((tm, d), lambda i, j: (i, 0)),
            pl.BlockSpec((1, d), lambda i, j: (0, 0)),
            pl.BlockSpec((None, None, 3 * N_SUBLAYERS, d), lambda i, j: (layer, i // tpb, 0, 0)),
            _resident_weight_spec(widx, d, tn, n // tn),
        ],
        out_specs=pl.BlockSpec((tm, tn), lambda i, j: (i, j)),
        out_shape=jax.ShapeDtypeStruct((t, n), F32),
        scratch_shapes=[pltpu.VMEM((tm, d), BF16), pltpu.VMEM((n // tn, d, tn), BF16)],
        compiler_params=_cparams(("arbitrary", "arbitrary")),
        name=f"norm_linear_l{layer}",
    )(h, g_row, ada4, w)


def _pool_kernel(u_ref, w_ref, ls_ref, o_ref):
    g = pl.program_id(1)
    u = u_ref[...]
    t = lax.broadcasted_iota(jnp.int32, u.shape, 0)

    def shifted(v, d):
        return jnp.where(t >= d, pltpu.roll(v, d, 0), 0.0)

    for gi, win in enumerate(POOL_WINDOWS):
        @pl.when(g == gi)
        def _(win=win):
            acc = u
            d = 1
            while d < win:
                acc = acc + shifted(acc, d)
                d *= 2
            cnt = jnp.minimum(t + 1, win).astype(F32)
            p = (acc / cnt - u).astype(BF16)
            v = _dot(p, w_ref[...].astype(BF16)) * ls_ref[...]
            o_ref[...] = v.astype(BF16)


def _pool_core(u, w_grp, ls, widx, batch, seq):
    t, d = u.shape
    _, groups, c, _ = w_grp.shape
    return pl.pallas_call(
        _pool_kernel,
        grid=(batch, groups),
        in_specs=[
            pl.BlockSpec((seq, c), lambda b, g: (b, g)),
            pl.BlockSpec((None, None, c, c), lambda b, g: (widx, g, 0, 0)),
            pl.BlockSpec((None, 1, c), lambda b, g: (widx, 0, g)),
        ],
        out_specs=pl.BlockSpec((seq, c), lambda b, g: (b, g)),
        out_shape=jax.ShapeDtypeStruct((t, d), BF16),
        compiler_params=_cparams(("arbitrary", "arbitrary")),
        name="pool_core",
    )(u, w_grp, ls.reshape(ls.shape[0], 1, d))


def _linear_residual_kernel(sub, a_ref, w_ref, h_ref, m_ref, o_ref, wbf_ref):
    gate = m_ref[3 * sub + 2:3 * sub + 3, :]
    o_ref[...] = h_ref[...] + gate * _dot(a_ref[...], _keep_bf16_weight(w_ref, wbf_ref))


def _linear_residual(a, w, widx, h, ada4, layer, sub, seq, tm, tn):
    t, kdim = a.shape
    n = w.shape[2]
    tpb = seq // tm
    return pl.pallas_call(
        functools.partial(_linear_residual_kernel, sub),
        grid=(t // tm, n // tn),
        in_specs=[
            pl.BlockSpec((tm, kdim), lambda i, j: (i, 0)),
            _resident_weight_spec(widx, kdim, tn, n // tn),
            pl.BlockSpec((tm, tn), lambda i, j: (i, j)),
            pl.BlockSpec((None, None, 3 * N_SUBLAYERS, tn), lambda i, j: (layer, i // tpb, 0, j)),
        ],
        out_specs=pl.BlockSpec((tm, tn), lambda i, j: (i, j)),
        out_shape=jax.ShapeDtypeStruct((t, n), F32),
        scratch_shapes=[pltpu.VMEM((n // tn, kdim, tn), BF16)],
        compiler_params=_cparams(("arbitrary", "arbitrary")),
        name=f"linear_residual_l{layer}",
    )(a, w, h, ada4)


ROPE_SET_Q, ROPE_SET_K, ROPE_SET_NONE, ROPE_SET_IDX = 0, 1, 2, 3
HALF_LANES = LANES // 2


def _write_rope_tables(pos_ref, invf_ref, o_ref):
    pos = pos_ref[...].astype(F32)
    lane = lax.broadcasted_iota(jnp.int32, (pos.shape[0], LANES), 1)
    ang = pos * invf_ref[...]
    cos = jnp.cos(ang)
    sin = jnp.where(lane < HALF_LANES, -jnp.sin(ang), jnp.sin(ang))
    rot_big = (lane & (HALF_LANES - 1)) < ROT_DIM // 2
    cb = jnp.where(rot_big, cos, 1.0)
    sb = jnp.where(rot_big, sin, 0.0)
    rot_idx = (lane & (HALF_LANES // 2 - 1)) < IDX_ROT_DIM // 2
    even_quarter = (lane & (HALF_LANES // 2)) == 0
    shift = ROT_DIM // 2

    def idx_table(v, rest):
        from_above = pltpu.roll(v, LANES - shift, 1)
        from_below = pltpu.roll(v, shift, 1)
        return jnp.where(rot_idx, jnp.where(even_quarter, from_above, from_below), rest)

    ci = idx_table(cos, 1.0)
    si = idx_table(sin, 0.0)
    o_ref[2 * ROPE_SET_Q] = cb * Q_PRESCALE
    o_ref[2 * ROPE_SET_Q + 1] = sb * Q_PRESCALE
    o_ref[2 * ROPE_SET_K] = cb
    o_ref[2 * ROPE_SET_K + 1] = sb
    o_ref[2 * ROPE_SET_NONE] = jnp.ones_like(cb)
    o_ref[2 * ROPE_SET_NONE + 1] = jnp.zeros_like(cb)
    o_ref[2 * ROPE_SET_IDX] = ci
    o_ref[2 * ROPE_SET_IDX + 1] = si


def _rope_inv_freq_row():
    def inv_freq(rot_dim):
        half = rot_dim // 2
        return ROPE_THETA ** (-jnp.arange(half, dtype=F32) / half)

    half_row = jnp.concatenate([
        inv_freq(ROT_DIM), inv_freq(IDX_ROT_DIM),
        jnp.zeros((HALF_LANES - ROT_DIM // 2 - IDX_ROT_DIM // 2,), F32)])
    return jnp.tile(half_row, 2)[None, :]


def _rope(x, c, s):
    return x * c + pltpu.roll(x, HALF_LANES, 1) * s


PERM_BIG, PERM_NONE, PERM_IDX = 0, 1, 2


def _lane_permutations():
    h, p = ROT_DIM // 2, (HEAD_DIM - ROT_DIM) // 2
    big = np.concatenate([np.arange(0, h), np.arange(ROT_DIM, ROT_DIM + p),
                          np.arange(h, ROT_DIM), np.arange(ROT_DIM + p, HEAD_DIM)])
    lo, hi = _idx_lo_hi_dims()
    idx = np.concatenate([lo, IDX_DIM + lo, hi, IDX_DIM + hi])
    mats = np.zeros((3, LANES, LANES), np.float32)
    for kind, src in ((PERM_BIG, big), (PERM_NONE, np.arange(LANES)), (PERM_IDX, idx)):
        mats[kind, np.arange(LANES), src] = 1.0
    return jnp.asarray(mats, BF16)


def _idx_lo_hi_dims():
    h, p = IDX_ROT_DIM // 2, (IDX_DIM - IDX_ROT_DIM) // 2
    lo = np.concatenate([np.arange(0, h), np.arange(IDX_ROT_DIM, IDX_ROT_DIM + p)])
    hi = np.concatenate([np.arange(h, IDX_ROT_DIM), np.arange(IDX_ROT_DIM + p, IDX_DIM)])
    return lo, hi


def _dsa_small_weights(w_tail):
    h, p = IDX_ROT_DIM // 2, (IDX_DIM - IDX_ROT_DIM) // 2
    klo = jnp.concatenate([w_tail[:, 0:h], w_tail[:, IDX_ROT_DIM:IDX_ROT_DIM + p]], axis=1)
    khi = jnp.concatenate([w_tail[:, h:IDX_ROT_DIM], w_tail[:, IDX_ROT_DIM + p:IDX_DIM]], axis=1)
    wi = jnp.pad(w_tail[:, IDX_DIM:], ((0, 0), (0, LANES - IDX_DIM - IDX_HEADS)))
    return jnp.concatenate([klo, khi, wi], axis=1).astype(BF16)


def _dsa_proj_kernel(sub, tn, x_hbm, g_ref, m_ref, w_ref, ws_ref, perm_ref, tab_ref,
                     o_ref, ok_ref, ow_ref, y_ref, wbf_ref, x_ref, x_sem):
    i = pl.program_id(0)
    j = pl.program_id(1)
    tm = x_ref.shape[0]
    first_k, first_v, first_qi = Q_W // tn, (Q_W + KV_W) // tn, (Q_W + 2 * KV_W) // tn

    def x_copy(tile):
        return pltpu.make_async_copy(x_hbm.at[pl.ds(tile * tm, tm), :], x_ref, x_sem.at[0])

    @pl.when(j == 0)
    def _():
        pl.when(i == 0)(x_copy(i).start)
        x_copy(i).wait()
        _write_norm_mod(x_ref, g_ref, m_ref, sub, y_ref)
        pl.when(i + 1 < pl.num_programs(0))(x_copy(i + 1).start)
        small = _dot(y_ref[...], ws_ref[...])
        by32 = pltpu.roll(small, HALF_LANES // 2, 1)
        by64 = pltpu.roll(small, HALF_LANES, 1)
        quarter = lax.broadcasted_iota(jnp.int32, small.shape, 1) // (HALF_LANES // 2)
        k_even = jnp.where(quarter == 0, small, jnp.where(quarter == 2, by32, 0.0))
        k_odd = jnp.where(quarter == 1, by32, jnp.where(quarter == 3, by64, 0.0))
        for s, kk in enumerate((k_even, k_odd)):
            ok_ref[:, s * LANES:(s + 1) * LANES] = _rope(
                kk, tab_ref[2 * ROPE_SET_IDX], tab_ref[2 * ROPE_SET_IDX + 1]).astype(BF16)
        ow_ref[...] = jnp.where(quarter == 0, by64, 0.0)

    @pl.when(pl.program_id(0) == 0)
    def _():
        kind = jnp.where(j < first_v, PERM_BIG, jnp.where(j < first_qi, PERM_NONE, PERM_IDX))
        perm = perm_ref[kind]
        for sl in range(tn // LANES):
            rows = slice(sl * LANES, (sl + 1) * LANES)
            wbf_ref[j, rows, :] = _dot(perm, w_ref[rows, :].astype(BF16)).astype(BF16)

    rope_set = jnp.where(j < first_k, ROPE_SET_Q,
                         jnp.where(j < first_v, ROPE_SET_K,
                                   jnp.where(j < first_qi, ROPE_SET_NONE, ROPE_SET_IDX)))
    c = tab_ref[2 * rope_set]
    s = tab_ref[2 * rope_set + 1]
    y = y_ref[...]
    wide = 2 * LANES
    for part in range(tn // wide):
        acc = _dot_t(y, wbf_ref[j, part * wide:(part + 1) * wide, :])
        for sl in range(wide // LANES):
            cols = slice(part * wide + sl * LANES, part * wide + (sl + 1) * LANES)
            o_ref[:, cols] = _rope(acc[:, sl * LANES:(sl + 1) * LANES], c, s).astype(BF16)


def _dsa_proj(h, ada4, g_row, w_in, widx, w_small, tabs, layer, sub, seq, tm, tn):
    t, d = h.shape
    tpb = seq // tm
    return pl.pallas_call(
        functools.partial(_dsa_proj_kernel, sub, tn),
        grid=(t // tm, DSA_MAIN_W // tn),
        in_specs=[
            pl.BlockSpec(memory_space=pl.ANY),
            pl.BlockSpec((1, d), lambda i, j: (0, 0)),
            pl.BlockSpec((None, None, 3 * N_SUBLAYERS, d), lambda i, j: (layer, i // tpb, 0, 0)),
            pl.BlockSpec((None, tn, d),
                         lambda i, j: (widx, jnp.where(i == 0, j, DSA_MAIN_W // tn - 1), 0)),
            pl.BlockSpec((d, LANES), lambda i, j: (0, 0)),
            pl.BlockSpec((3, LANES, LANES), lambda i, j: (0, 0, 0)),
            pl.BlockSpec((N_ROPE_TABLES, tm, LANES), lambda i, j: (0, i, 0)),
        ],
        out_specs=[
            pl.BlockSpec((tm, tn), lambda i, j: (i, j)),
            pl.BlockSpec((tm, 2 * LANES), lambda i, j: (i, 0)),
            pl.BlockSpec((tm, LANES), lambda i, j: (i, 0)),
        ],
        out_shape=[
            jax.ShapeDtypeStruct((t, DSA_MAIN_W), BF16),
            jax.ShapeDtypeStruct((t, 2 * LANES), BF16),
            jax.ShapeDtypeStruct((t, LANES), F32),
        ],
        scratch_shapes=[pltpu.VMEM((tm, d), BF16),
                        pltpu.VMEM((DSA_MAIN_W // tn, tn, d), BF16),
                        pltpu.VMEM((tm, d), F32),
                        pltpu.SemaphoreType.DMA((1,))],
        compiler_params=_cparams(("arbitrary", "arbitrary")),
        name=f"dsa_proj_l{layer}",
    )(h, g_row, ada4, w_in, w_small, _lane_permutations(), tabs)


def _dsa_attn_kernel(tq, tk, top_k, q_ref, qi_ref, wi_ref, k_ref, v_ref, kk_ref, o_ref,
                     score_ref, bias_ref, thr_ref, wib_ref, qg_ref, m_ref, acc_ref):
    i = pl.program_id(1)
    n_chunks = ((i + 1) * tq + tk - 1) // tk
    max_chunks = score_ref.shape[1] // tk
    t_pos = i * tq + lax.broadcasted_iota(jnp.int32, (tq, tk), 0)
    k_iota = lax.broadcasted_iota(jnp.int32, (tq, tk), 1)
    n_cb = tk // LANES

    for hh in range(IDX_HEADS):
        wib_ref[hh] = jnp.broadcast_to(wi_ref[:, hh:hh + 1], (tq, LANES))
    for g in range(N_KV_HEADS):
        for r in range(KV_GROUP):
            hd = g * KV_GROUP + r
            qg_ref[g, r * tq:(r + 1) * tq, :] = q_ref[:, hd * HEAD_DIM:(hd + 1) * HEAD_DIM]

    def score_chunk(c, carry):
        off = pl.multiple_of(c * tk, tk)
        k_even = kk_ref[pl.ds(off, tk), 0:LANES]
        k_odd = kk_ref[pl.ds(off, tk), LANES:2 * LANES]
        acc = [jnp.zeros((tq, LANES), F32) for _ in range(n_cb)]
        for pair in range(IDX_HEADS // 2):
            qp = qi_ref[:, pair * LANES:(pair + 1) * LANES]
            for par, kop in enumerate((k_even, k_odd)):
                r = _dot_t(qp, kop)
                w = wib_ref[2 * pair + par]
                for cb in range(n_cb):
                    acc[cb] = acc[cb] + w * jnp.maximum(r[:, cb * LANES:(cb + 1) * LANES], 0.0)
        sc = jnp.concatenate(acc, axis=1)
        score_ref[:, pl.ds(off, tk)] = jnp.where(k_iota + off <= t_pos, sc, NEG)
        return carry

    lax.fori_loop(0, n_chunks, score_chunk, 0)

    def key_to_f32(key):
        return pltpu.bitcast(key ^ ((key >> 31) & 0x7FFFFFFF), F32)

    k_f = float(top_k)

    def select(n_static):
        def bit_step(n, thr_key):
            cand = thr_key + lax.shift_left(jnp.int32(1), KEY_BITS - 1 - n)
            thr = key_to_f32(cand)
            cnt = jnp.zeros((tq, LANES), F32)
            for cb in range(n_static * n_cb):
                s = score_ref[:, cb * LANES:(cb + 1) * LANES]
                cnt = cnt + jnp.where(s >= thr, 1.0, 0.0)
            total = jnp.sum(cnt, axis=1, keepdims=True)
            return jnp.where(total >= k_f, cand, thr_key)

        thr_key = lax.fori_loop(0, KEY_BITS, bit_step,
                                jnp.full((tq, LANES), INT32_MIN, jnp.int32), unroll=SELECT_UNROLL)
        thr_ref[...] = key_to_f32(thr_key)

    for n_static in range(1, max_chunks + 1):
        pl.when(n_chunks == n_static)(functools.partial(select, n_static))

    def bias_chunk(c, cnt):
        off = pl.multiple_of(c * tk, tk)
        thr = thr_ref[...]
        causal = k_iota + off <= t_pos
        for cb in range(n_cb):
            cols = pl.ds(off + cb * LANES, LANES)
            sel = score_ref[:, cols] >= thr
            bias_ref[:, cols] = jnp.where(
                sel, jnp.where(causal[:, cb * LANES:(cb + 1) * LANES], 0.0, NEG), NEG)
            cnt = cnt + jnp.where(sel, 1.0, 0.0)
        return cnt

    n_ge = jnp.sum(lax.fori_loop(0, n_chunks, bias_chunk, jnp.zeros((tq, LANES), F32)),
                   axis=1, keepdims=True)

    @pl.when(jnp.max(n_ge) > k_f)
    def _():
        def count_greater(c, cnt):
            off = pl.multiple_of(c * tk, tk)
            thr = thr_ref[...]
            for cb in range(n_cb):
                s = score_ref[:, pl.ds(off + cb * LANES, LANES)]
                cnt = cnt + jnp.where(s > thr, 1.0, 0.0)
            return cnt

        n_greater = jnp.sum(lax.fori_loop(0, n_chunks, count_greater,
                                          jnp.zeros((tq, LANES), F32)), axis=1, keepdims=True)
        n_take = k_f - n_greater
        before = jnp.where(
            lax.broadcasted_iota(jnp.int32, (tk, tk), 0) < lax.broadcasted_iota(jnp.int32, (tk, tk), 1),
            1.0, 0.0).astype(BF16)

        def tie_chunk(c, seen):
            off = pl.multiple_of(c * tk, tk)
            sc = score_ref[:, pl.ds(off, tk)]
            thr = jnp.concatenate([thr_ref[...]] * n_cb, axis=1)
            eq = jnp.where(sc == thr, 1.0, 0.0)
            rank = _dot(eq.astype(BF16), before) + seen
            keep = jnp.where(sc > thr, 1.0, jnp.where(rank < n_take, eq, 0.0))
            causal = k_iota + off <= t_pos
            bias_ref[:, pl.ds(off, tk)] = jnp.where(causal, jnp.where(keep > 0.0, 0.0, NEG), NEG)
            return seen + jnp.sum(eq, axis=1, keepdims=True)

        lax.fori_loop(0, n_chunks, tie_chunk, jnp.zeros((tq, 1), F32))

    rows = KV_GROUP * tq
    ones = jnp.ones((tk, LANES), BF16)

    def attend(off, first):
        b = bias_ref[:, pl.ds(off, tk)]
        for g in range(N_KV_HEADS):
            kc = k_ref[pl.ds(off, tk), g * HEAD_DIM:(g + 1) * HEAD_DIM]
            vc = v_ref[pl.ds(off, tk), g * HEAD_DIM:(g + 1) * HEAD_DIM]
            s = _dot_t(qg_ref[g], kc)
            s = (s.reshape(KV_GROUP, tq, tk) + b[None]).reshape(rows, tk)
            m_new = jnp.broadcast_to(jnp.max(s, axis=1, keepdims=True), (rows, LANES))
            if not first:
                m_old = m_ref[g]
                m_new = jnp.maximum(m_old, m_new)
            p = jnp.exp2(s - jnp.concatenate([m_new] * n_cb, axis=1))
            pv = _dot(p.astype(BF16), jnp.concatenate([vc, ones], axis=1))
            if first:
                acc_ref[g] = pv
            else:
                alpha = jnp.exp2(m_old - m_new)
                acc_ref[g] = jnp.concatenate([alpha, alpha], axis=1) * acc_ref[g] + pv
            m_ref[g] = m_new

    attend(0, True)

    def attn_chunk(c, carry):
        attend(pl.multiple_of(c * tk, tk), False)
        return carry

    lax.fori_loop(1, n_chunks, attn_chunk, 0)

    for g in range(N_KV_HEADS):
        out = acc_ref[g, :, 0:HEAD_DIM] / acc_ref[g, :, HEAD_DIM:2 * HEAD_DIM]
        for r in range(KV_GROUP):
            hd = g * KV_GROUP + r
            o_ref[:, hd * HEAD_DIM:(hd + 1) * HEAD_DIM] = out[r * tq:(r + 1) * tq].astype(BF16)


def _dsa_attn(main, kk, wi, batch, seq, tq, tk, top_k):
    t = main.shape[0]
    nq = seq // tq
    rows = KV_GROUP * tq
    return pl.pallas_call(
        functools.partial(_dsa_attn_kernel, tq, tk, top_k),
        grid=(batch, nq),
        in_specs=[
            pl.BlockSpec((tq, Q_W), lambda b, i: (b * nq + i, 0)),
            pl.BlockSpec((tq, QI_W), lambda b, i: (b * nq + i, (Q_W + 2 * KV_W) // QI_W)),
            pl.BlockSpec((tq, LANES), lambda b, i: (b * nq + i, 0)),
            pl.BlockSpec((seq, KV_W), lambda b, i: (b, Q_W // KV_W)),
            pl.BlockSpec((seq, KV_W), lambda b, i: (b, Q_W // KV_W + 1)),
            pl.BlockSpec((seq, 2 * LANES), lambda b, i: (b, 0)),
        ],
        out_specs=pl.BlockSpec((tq, Q_W), lambda b, i: (b * nq + i, 0)),
        out_shape=jax.ShapeDtypeStruct((t, Q_W), BF16),
        scratch_shapes=[
            pltpu.VMEM((tq, seq), F32),
            pltpu.VMEM((tq, seq), F32),
            pltpu.VMEM((tq, LANES), F32),
            pltpu.VMEM((IDX_HEADS, tq, LANES), F32),
            pltpu.VMEM((N_KV_HEADS, rows, HEAD_DIM), BF16),
            pltpu.VMEM((N_KV_HEADS, rows, LANES), F32),
            pltpu.VMEM((N_KV_HEADS, rows, 2 * HEAD_DIM), F32),
        ],
        compiler_params=_cparams(("arbitrary", "arbitrary")),
        name="dsa_attn",
    )(main, main, wi, main, main, kk)


def _tiles(seq, d_ff):
    tm = min(1024, seq)
    tm_ffn = min(2048, seq)
    tf = 256 if d_ff % 256 == 0 else d_ff
    tq = min(256, seq)
    tk = min(256, seq)
    return tm, tm_ffn, tf, tq, tk


def kernel(x, c, positions, ada_w, ada_b, norm_g, final_g, ffn_wgu, ffn_wd,
           pool_w_in, pool_w_grp, pool_scale, pool_w_out, dsa_w_in, dsa_w_out):
    batch, seq, d = x.shape
    depth = ada_w.shape[0]
    d_ff = ffn_wd.shape[2]
    t = batch * seq
    tm, tm_ffn, tf, tq, tk = _tiles(seq, d_ff)
    tn = min(512, d)
    top_k = min(TOPK_MAX, seq // 4)
    assert seq % tm == 0 and seq % tm_ffn == 0 and seq % tq == 0 and seq % tk == 0 and tk >= top_k
    assert dsa_w_in.shape[2] == DSA_MAIN_W + DSA_SMALL_W and DSA_MAIN_W % tn == 0

    rows = -(-batch // SUBLANES) * SUBLANES
    c8 = jnp.pad(c, ((0, rows - batch), (0, 0)))
    ada_n = ada_w.shape[2]
    ada, tabs = _ada(c8, ada_w, ada_b, 1024 if ada_n % 1024 == 0 else d, positions)
    ada4 = ada.reshape(depth, rows, 3 * N_SUBLAYERS, d)

    h = x.reshape(t, d)
    for layer in range(depth):
        j = layer // 2
        h = _ffn(h, ada4, norm_g, ffn_wgu, ffn_wd, layer, 0, 0, seq, tm_ffn, tf)
        g_row = norm_g[layer, 1][None, :]
        if layer % 2 == 0:
            u = _norm_linear(h, ada4, g_row, pool_w_in, j, layer, 1, seq, tm, tn)
            a = _pool_core(u, pool_w_grp, pool_scale, j, batch, seq)
            h = _linear_residual(a, pool_w_out, j, h, ada4, layer, 1, seq, tm, tn)
        else:
            w_small = _dsa_small_weights(dsa_w_in[j, :, DSA_MAIN_W:])
            main, kk, wi = _dsa_proj(h, ada4, g_row, jnp.swapaxes(dsa_w_in, 1, 2), j, w_small,
                                     tabs, layer, 1, seq, tm, tn)
            o = _dsa_attn(main, kk, wi, batch, seq, tq, tk, top_k)
            h = _linear_residual(o, dsa_w_out, j, h, ada4, layer, 1, seq, tm, tn)
        last = layer == depth - 1
        h = _ffn(h, ada4, norm_g, ffn_wgu, ffn_wd, layer, 1, 2, seq, tm_ffn, tf,
                 final_g=final_g if last else None)
    return h.reshape(batch, seq, d)
```

```python
import functools

import jax
import jax.numpy as jnp
import numpy as np
from jax import lax
from jax.experimental import pallas as pl
from jax.experimental.pallas import tpu as pltpu

F32 = jnp.float32
BF16 = jnp.bfloat16

N_SUBLAYERS = 3
POOL_WINDOWS = (2, 4, 8, 16)
N_HEADS = 16
N_KV_HEADS = 4
KV_GROUP = N_HEADS // N_KV_HEADS
HEAD_DIM = 128
ROT_DIM = HEAD_DIM // 4
ROPE_THETA = 500000.0
IDX_HEADS = 16
IDX_DIM = 64
IDX_ROT_DIM = IDX_DIM // 4
TOPK_MAX = 256
EPS = 1e-6
NEG = -1e30
Q_W = N_HEADS * HEAD_DIM
KV_W = N_KV_HEADS * HEAD_DIM
QI_W = IDX_HEADS * IDX_DIM
DSA_MAIN_W = Q_W + 2 * KV_W + QI_W
DSA_SMALL_W = IDX_DIM + IDX_HEADS
Q_PRESCALE = HEAD_DIM ** -0.5 * 1.4426950408889634

LANES = 128
SUBLANES = 8
VMEM_LIMIT_BYTES = 56 * 1024 * 1024
INT32_MIN = -(2 ** 31)
KEY_BITS = 32
SELECT_UNROLL = 4
N_ROPE_TABLES = 8


def _cparams(semantics):
    return pltpu.CompilerParams(dimension_semantics=semantics,
                                vmem_limit_bytes=VMEM_LIMIT_BYTES)


def _rmsnorm(x, g):
    ms = jnp.mean(x * x, axis=-1, keepdims=True)
    return x * lax.rsqrt(ms + EPS) * g


def _rmsnorm_rows(ref, rows, g_ref):
    ref[rows, :] = _rmsnorm(ref[rows, :], g_ref[...])


NORM_SLAB = 16
NORM_UNROLL = 8
FFN_ROW_SLAB = 128
FFN_SLAB_SLOTS = 4
FFN_DOWN_COLS = 256
FFN_LAST_ROWS = 512


def _for_row_slabs(n_rows, slab, body, unroll=1):
    def step(r, carry):
        body(pl.ds(pl.multiple_of(r * slab, slab), slab))
        return carry
    lax.fori_loop(0, n_rows // slab, step, 0, unroll=unroll)


def _write_norm_mod(x_ref, g_ref, m_ref, sub, y_ref, copy_ref=None):
    shift = m_ref[3 * sub + 0:3 * sub + 1, :]
    gs = g_ref[...] * (1.0 + m_ref[3 * sub + 1:3 * sub + 2, :])
    slab = min(NORM_SLAB, x_ref.shape[0])

    def norm_rows(rows):
        x = x_ref[rows, :]
        r = lax.rsqrt(jnp.mean(x * x, axis=-1, keepdims=True) + EPS)
        y_ref[rows, :] = ((x * r) * gs + shift).astype(BF16)
        if copy_ref is not None:
            copy_ref[rows, :] = x

    _for_row_slabs(x_ref.shape[0], slab, norm_rows, unroll=NORM_UNROLL)


def _silu(x):
    return x * jax.nn.sigmoid(x)


def _dot(a, b):
    return jnp.dot(a, b, preferred_element_type=F32)


def _dot_t(a, b):
    return lax.dot_general(a, b, (((1,), (1,)), ((), ())), preferred_element_type=F32)


def _ada_kernel(n_tab_blocks, c_ref, w_ref, b_ref, pos_ref, invf_ref, o_ref, tab_ref):
    sc = _silu(c_ref[...]).astype(BF16)
    o_ref[...] = _dot(sc, w_ref[...].astype(BF16)) + b_ref[...]
    step = pl.program_id(0) * pl.num_programs(1) + pl.program_id(1)

    @pl.when(step < n_tab_blocks)
    def _():
        _write_rope_tables(pos_ref, invf_ref, tab_ref)


def _ada(c8, ada_w, ada_b, tn, positions):
    depth, d, n = ada_w.shape
    rows = c8.shape[0]
    n_steps = depth * (n // tn)
    t = positions.size
    rt = SUBLANES
    while t // rt > n_steps or t % rt:
        rt *= 2
    n_tab = t // rt
    nj = n // tn

    def tab_block(l, j):
        return jnp.minimum(l * nj + j, n_tab - 1)

    return pl.pallas_call(
        functools.partial(_ada_kernel, n_tab),
        grid=(depth, nj),
        in_specs=[
            pl.BlockSpec((rows, d), lambda l, j: (0, 0)),
            pl.BlockSpec((None, d, tn), lambda l, j: (l, 0, j)),
            pl.BlockSpec((None, 1, tn), lambda l, j: (l, 0, j)),
            pl.BlockSpec((rt, 1), lambda l, j: (tab_block(l, j), 0)),
            pl.BlockSpec((1, LANES), lambda l, j: (0, 0)),
        ],
        out_specs=[
            pl.BlockSpec((None, rows, tn), lambda l, j: (l, 0, j)),
            pl.BlockSpec((N_ROPE_TABLES, rt, LANES), lambda l, j: (0, tab_block(l, j), 0)),
        ],
        out_shape=[
            jax.ShapeDtypeStruct((depth, rows, n), F32),
            jax.ShapeDtypeStruct((N_ROPE_TABLES, t, LANES), F32),
        ],
        compiler_params=_cparams(("arbitrary", "arbitrary")),
        name="ada",
    )(c8, ada_w, ada_b.reshape(depth, 1, n), positions.reshape(t, 1), _rope_inv_freq_row())


def _ffn_kernel(sub, final, rs, *refs):
    if final:
        (x_hbm, g_ref, m_ref, wg_ref, wu_ref, wd_ref, fg_ref, o_hbm,
         acc_ref, y_ref, xbuf, in_sem, out_sem) = refs
    else:
        (x_hbm, g_ref, m_ref, wg_ref, wu_ref, wd_ref, o_hbm,
         acc_ref, y_ref, xbuf, in_sem, out_sem) = refs
    k = pl.program_id(1)
    nk = pl.num_programs(1)
    tm, d_out = acc_ref.shape
    n_slabs = tm // rs
    n_slots = xbuf.shape[0]
    n_first = min(n_slots, n_slabs)
    row0 = pl.program_id(0) * tm

    def x_copy(s, tile_row0=row0):
        slot = s % n_slots
        return pltpu.make_async_copy(x_hbm.at[pl.ds(tile_row0 + s * rs, rs), :], xbuf.at[slot],
                                     in_sem.at[slot])

    def o_copy(s):
        return pltpu.make_async_copy(acc_ref.at[pl.ds(s * rs, rs), :],
                                     o_hbm.at[pl.ds(row0 + s * rs, rs), :], out_sem.at[s])

    @pl.when(k == 0)
    def _():
        @pl.when(pl.program_id(0) == 0)
        def _():
            for s in range(n_first):
                x_copy(s).start()

        for s in range(n_slabs):
            x_copy(s).wait()
            pl.when(pl.program_id(0) > 0)(o_copy(s).wait)
            rows = pl.ds(s * rs, rs)
            _write_norm_mod(xbuf.at[s % n_slots], g_ref, m_ref, sub, y_ref.at[rows, :],
                            copy_ref=acc_ref.at[rows, :])
            if s + n_slots < n_slabs:
                x_copy(s + n_slots).start()

    @pl.when(jnp.logical_and(k == nk - 1, pl.program_id(0) + 1 < pl.num_programs(0)))
    def _():
        for s in range(n_first):
            x_copy(s, row0 + tm).start()

    def up_projection():
        y = y_ref[...]
        g = _dot(y, wg_ref[...].astype(BF16))
        u = _dot(y, wu_ref[...].astype(BF16))
        return (_silu(g) * u).astype(BF16)

    half_gate = 0.5 * m_ref[3 * sub + 2:3 * sub + 3, :]
    nc = min(FFN_DOWN_COLS, d_out)

    def down_projection(a, rows):
        for n in range(d_out // nc):
            cols = slice(n * nc, (n + 1) * nc)
            acc_ref[rows, cols] += half_gate[:, cols] * _dot(a, wd_ref[:, cols].astype(BF16))

    @pl.when(k < nk - 1)
    def _():
        down_projection(up_projection(), slice(None))

    @pl.when(k == nk - 1)
    def _():
        a = up_projection()
        group = min(FFN_LAST_ROWS, tm)
        for r in range(tm // group):
            rows = slice(r * group, (r + 1) * group)
            down_projection(a[rows], rows)
            for s in range(r * group // rs, (r + 1) * group // rs):
                if final:
                    _for_row_slabs(
                        rs, min(NORM_SLAB, rs),
                        lambda rr, s=s: _rmsnorm_rows(acc_ref.at[pl.ds(s * rs, rs), :], rr, fg_ref),
                        unroll=NORM_UNROLL)
                o_copy(s).start()

        @pl.when(pl.program_id(0) == pl.num_programs(0) - 1)
        def _():
            for s in range(n_slabs):
                o_copy(s).wait()


def _ffn(h, ada4, norm_g, ffn_wgu, ffn_wd, layer, which, sub, seq, tm, tf, final_g=None):
    t, d = h.shape
    d_ff = ffn_wd.shape[2]
    nk = d_ff // tf
    tpb = seq // tm
    rs = min(FFN_ROW_SLAB, tm)
    final = final_g is not None
    in_specs = [
        pl.BlockSpec(memory_space=pl.ANY),
        pl.BlockSpec((1, d), lambda i, k: (0, 0)),
        pl.BlockSpec((None, None, 3 * N_SUBLAYERS, d), lambda i, k: (layer, i // tpb, 0, 0)),
        pl.BlockSpec((None, None, d, tf), lambda i, k: (layer, which, 0, k)),
        pl.BlockSpec((None, None, d, tf), lambda i, k: (layer, which, 0, nk + k)),
        pl.BlockSpec((None, None, tf, d), lambda i, k: (layer, which, k, 0)),
    ]
    args = [h, norm_g[layer, sub][None, :], ada4, ffn_wgu, ffn_wgu, ffn_wd]
    if final:
        in_specs.append(pl.BlockSpec((1, d), lambda i, k: (0, 0)))
        args.append(final_g[None, :])
    return pl.pallas_call(
        functools.partial(_ffn_kernel, sub, final, rs),
        grid=(t // tm, nk),
        in_specs=in_specs,
        out_specs=pl.BlockSpec(memory_space=pl.ANY),
        out_shape=jax.ShapeDtypeStruct((t, d), F32),
        scratch_shapes=[
            pltpu.VMEM((tm, d), F32),
            pltpu.VMEM((tm, d), BF16),
            pltpu.VMEM((FFN_SLAB_SLOTS, rs, d), F32),
            pltpu.SemaphoreType.DMA((FFN_SLAB_SLOTS,)),
            pltpu.SemaphoreType.DMA((tm // rs,)),
        ],
        compiler_params=_cparams(("arbitrary", "arbitrary")),
        name=f"ffn_l{layer}_{which}",
    )(*args)


def _keep_bf16_weight(w_ref, wbf_ref):
    j = pl.program_id(1)

    @pl.when(pl.program_id(0) == 0)
    def _():
        wbf_ref[j] = w_ref[...].astype(BF16)

    return wbf_ref[j]


def _resident_weight_spec(widx, kdim, tn, nj):
    return pl.BlockSpec((None, kdim, tn), lambda i, j: (widx, 0, jnp.where(i == 0, j, nj - 1)))


def _norm_linear_kernel(sub, x_ref, g_ref, m_ref, w_ref, o_ref, y_ref, wbf_ref):
    @pl.when(pl.program_id(1) == 0)
    def _():
        _write_norm_mod(x_ref, g_ref, m_ref, sub, y_ref)

    o_ref[...] = _dot(y_ref[...], _keep_bf16_weight(w_ref, wbf_ref))


def _norm_linear(h, ada4, g_row, w, widx, layer, sub, seq, tm, tn):
    t, d = h.shape
    n = w.shape[2]
    tpb = seq // tm
    return pl.pallas_call(
        functools.partial(_norm_linear_kernel, sub),
        grid=(t // tm, n // tn),
        in_specs=[
            pl.BlockSpec((tm, d), lambda i, j: (i, 0)),
            pl.BlockSpec((1, d), lambda i, j: (0, 0)),
            pl.BlockSpec((None, None, 3 * N_SUBLAYERS, d), lambda i, j: (layer, i // tpb, 0, 0)),
            _resident_weight_spec(widx, d, tn, n // tn),
        ],
        out_specs=pl.BlockSpec((tm, tn), lambda i, j: (i, j)),
        out_shape=jax.ShapeDtypeStruct((t, n), F32),
        scratch_shapes=[pltpu.VMEM((tm, d), BF16), pltpu.VMEM((n // tn, d, tn), BF16)],
        compiler_params=_cparams(("arbitrary", "arbitrary")),
        name=f"norm_linear_l{layer}",
    )(h, g_row, ada4, w)


def _pool_kernel(u_ref, w_ref, ls_ref, o_ref):
    g = pl.program_id(1)
    u = u_ref[...]
    t = lax.broadcasted_iota(jnp.int32, u.shape, 0)

    def shifted(v, d):
        return jnp.where(t >= d, pltpu.roll(v, d, 0), 0.0)

    for gi, win in enumerate(POOL_WINDOWS):
        @pl.when(g == gi)
        def _(win=win):
            acc = u
            d = 1
            while d < win:
                acc = acc + shifted(acc, d)
                d *= 2
            cnt = jnp.minimum(t + 1, win).astype(F32)
            p = (acc / cnt - u).astype(BF16)
            v = _dot(p, w_ref[...].astype(BF16)) * ls_ref[...]
            o_ref[...] = v.astype(BF16)


def _pool_core(u, w_grp, ls, widx, batch, seq):
    t, d = u.shape
    _, groups, c, _ = w_grp.shape
    return pl.pallas_call(
        _pool_kernel,
        grid=(batch, groups),
        in_specs=[
            pl.BlockSpec((seq, c), lambda b, g: (b, g)),
            pl.BlockSpec((None, None, c, c), lambda b, g: (widx, g, 0, 0)),
            pl.BlockSpec((None, 1, c), lambda b, g: (widx, 0, g)),
        ],
        out_specs=pl.BlockSpec((seq, c), lambda b, g: (b, g)),
        out_shape=jax.ShapeDtypeStruct((t, d), BF16),
        compiler_params=_cparams(("arbitrary", "arbitrary")),
        name="pool_core",
    )(u, w_grp, ls.reshape(ls.shape[0], 1, d))


def _linear_residual_kernel(sub, a_ref, w_ref, h_ref, m_ref, o_ref, wbf_ref):
    gate = m_ref[3 * sub + 2:3 * sub + 3, :]
    o_ref[...] = h_ref[...] + gate * _dot(a_ref[...], _keep_bf16_weight(w_ref, wbf_ref))


def _linear_residual(a, w, widx, h, ada4, layer, sub, seq, tm, tn):
    t, kdim = a.shape
    n = w.shape[2]
    tpb = seq // tm
    return pl.pallas_call(
        functools.partial(_linear_residual_kernel, sub),
        grid=(t // tm, n // tn),
        in_specs=[
            pl.BlockSpec((tm, kdim), lambda i, j: (i, 0)),
            _resident_weight_spec(widx, kdim, tn, n // tn),
            pl.BlockSpec((tm, tn), lambda i, j: (i, j)),
            pl.BlockSpec((None, None, 3 * N_SUBLAYERS, tn), lambda i, j: (layer, i // tpb, 0, j)),
        ],
        out_specs=pl.BlockSpec((tm, tn), lambda i, j: (i, j)),
        out_shape=jax.ShapeDtypeStruct((t, n), F32),
        scratch_shapes=[pltpu.VMEM((n // tn, kdim, tn), BF16)],
        compiler_params=_cparams(("arbitrary", "arbitrary")),
        name=f"linear_residual_l{layer}",
    )(a, w, h, ada4)


ROPE_SET_Q, ROPE_SET_K, ROPE_SET_NONE, ROPE_SET_IDX = 0, 1, 2, 3
HALF_LANES = LANES // 2


def _write_rope_tables(pos_ref, invf_ref, o_ref):
    pos = pos_ref[...].astype(F32)
    lane = lax.broadcasted_iota(jnp.int32, (pos.shape[0], LANES), 1)
    ang = pos * invf_ref[...]
    cos = jnp.cos(ang)
    sin = jnp.where(lane < HALF_LANES, -jnp.sin(ang), jnp.sin(ang))
    rot_big = (lane & (HALF_LANES - 1)) < ROT_DIM // 2
    cb = jnp.where(rot_big, cos, 1.0)
    sb = jnp.where(rot_big, sin, 0.0)
    rot_idx = (lane & (HALF_LANES // 2 - 1)) < IDX_ROT_DIM // 2
    even_quarter = (lane & (HALF_LANES // 2)) == 0
    shift = ROT_DIM // 2

    def idx_table(v, rest):
        from_above = pltpu.roll(v, LANES - shift, 1)
        from_below = pltpu.roll(v, shift, 1)
        return jnp.where(rot_idx, jnp.where(even_quarter, from_above, from_below), rest)

    ci = idx_table(cos, 1.0)
    si = idx_table(sin, 0.0)
    o_ref[2 * ROPE_SET_Q] = cb * Q_PRESCALE
    o_ref[2 * ROPE_SET_Q + 1] = sb * Q_PRESCALE
    o_ref[2 * ROPE_SET_K] = cb
    o_ref[2 * ROPE_SET_K + 1] = sb
    o_ref[2 * ROPE_SET_NONE] = jnp.ones_like(cb)
    o_ref[2 * ROPE_SET_NONE + 1] = jnp.zeros_like(cb)
    o_ref[2 * ROPE_SET_IDX] = ci
    o_ref[2 * ROPE_SET_IDX + 1] = si


def _rope_inv_freq_row():
    def inv_freq(rot_dim):
        half = rot_dim // 2
        return ROPE_THETA ** (-jnp.arange(half, dtype=F32) / half)

    half_row = jnp.concatenate([
        inv_freq(ROT_DIM), inv_freq(IDX_ROT_DIM),
        jnp.zeros((HALF_LANES - ROT_DIM // 2 - IDX_ROT_DIM // 2,), F32)])
    return jnp.tile(half_row, 2)[None, :]


def _rope(x, c, s):
    return x * c + pltpu.roll(x, HALF_LANES, 1) * s


PERM_BIG, PERM_NONE, PERM_IDX = 0, 1, 2


def _lane_permutations():
    h, p = ROT_DIM // 2, (HEAD_DIM - ROT_DIM) // 2
    big = np.concatenate([np.arange(0, h), np.arange(ROT_DIM, ROT_DIM + p),
                          np.arange(h, ROT_DIM), np.arange(ROT_DIM + p, HEAD_DIM)])
    lo, hi = _idx_lo_hi_dims()
    idx = np.concatenate([lo, IDX_DIM + lo, hi, IDX_DIM + hi])
    mats = np.zeros((3, LANES, LANES), np.float32)
    for kind, src in ((PERM_BIG, big), (PERM_NONE, np.arange(LANES)), (PERM_IDX, idx)):
        mats[kind, np.arange(LANES), src] = 1.0
    return jnp.asarray(mats, BF16)


def _idx_lo_hi_dims():
    h, p = IDX_ROT_DIM // 2, (IDX_DIM - IDX_ROT_DIM) // 2
    lo = np.concatenate([np.arange(0, h), np.arange(IDX_ROT_DIM, IDX_ROT_DIM + p)])
    hi = np.concatenate([np.arange(h, IDX_ROT_DIM), np.arange(IDX_ROT_DIM + p, IDX_DIM)])
    return lo, hi


def _dsa_small_weights(w_tail):
    h, p = IDX_ROT_DIM // 2, (IDX_DIM - IDX_ROT_DIM) // 2
    klo = jnp.concatenate([w_tail[:, 0:h], w_tail[:, IDX_ROT_DIM:IDX_ROT_DIM + p]], axis=1)
    khi = jnp.concatenate([w_tail[:, h:IDX_ROT_DIM], w_tail[:, IDX_ROT_DIM + p:IDX_DIM]], axis=1)
    wi = jnp.pad(w_tail[:, IDX_DIM:], ((0, 0), (0, LANES - IDX_DIM - IDX_HEADS)))
    return jnp.concatenate([klo, khi, wi], axis=1).astype(BF16)


def _dsa_proj_kernel(sub, tn, x_hbm, g_ref, m_ref, w_ref, ws_ref, perm_ref, tab_ref,
                     o_ref, ok_ref, ow_ref, y_ref, wbf_ref, x_ref, x_sem):
    i = pl.program_id(0)
    j = pl.program_id(1)
    tm = x_ref.shape[0]
    first_k, first_v, first_qi = Q_W // tn, (Q_W + KV_W) // tn, (Q_W + 2 * KV_W) // tn

    def x_copy(tile):
        return pltpu.make_async_copy(x_hbm.at[pl.ds(tile * tm, tm), :], x_ref, x_sem.at[0])

    @pl.when(j == 0)
    def _():
        pl.when(i == 0)(x_copy(i).start)
        x_copy(i).wait()
        _write_norm_mod(x_ref, g_ref, m_ref, sub, y_ref)
        pl.when(i + 1 < pl.num_programs(0))(x_copy(i + 1).start)
        small = _dot(y_ref[...], ws_ref[...])
        by32 = pltpu.roll(small, HALF_LANES // 2, 1)
        by64 = pltpu.roll(small, HALF_LANES, 1)
        quarter = lax.broadcasted_iota(jnp.int32, small.shape, 1) // (HALF_LANES // 2)
        k_even = jnp.where(quarter == 0, small, jnp.where(quarter == 2, by32, 0.0))
        k_odd = jnp.where(quarter == 1, by32, jnp.where(quarter == 3, by64, 0.0))
        for s, kk in enumerate((k_even, k_odd)):
            ok_ref[:, s * LANES:(s + 1) * LANES] = _rope(
                kk, tab_ref[2 * ROPE_SET_IDX], tab_ref[2 * ROPE_SET_IDX + 1]).astype(BF16)
        ow_ref[...] = jnp.where(quarter == 0, by64, 0.0)

    @pl.when(pl.program_id(0) == 0)
    def _():
        kind = jnp.where(j < first_v, PERM_BIG, jnp.where(j < first_qi, PERM_NONE, PERM_IDX))
        perm = perm_ref[kind]
        for sl in range(tn // LANES):
            rows = slice(sl * LANES, (sl + 1) * LANES)
            wbf_ref[j, rows, :] = _dot(perm, w_ref[rows, :].astype(BF16)).astype(BF16)

    rope_set = jnp.where(j < first_k, ROPE_SET_Q,
                         jnp.where(j < first_v, ROPE_SET_K,
                                   jnp.where(j < first_qi, ROPE_SET_NONE, ROPE_SET_IDX)))
    c = tab_ref[2 * rope_set]
    s = tab_ref[2 * rope_set + 1]
    y = y_ref[...]
    wide = 2 * LANES
    for part in range(tn // wide):
        acc = _dot_t(y, wbf_ref[j, part * wide:(part + 1) * wide, :])
        for sl in range(wide // LANES):
            cols = slice(part * wide + sl * LANES, part * wide + (sl + 1) * LANES)
            o_ref[:, cols] = _rope(acc[:, sl * LANES:(sl + 1) * LANES], c, s).astype(BF16)


def _dsa_proj(h, ada4, g_row, w_in, widx, w_small, tabs, layer, sub, seq, tm, tn):
    t, d = h.shape
    tpb = seq // tm
    return pl.pallas_call(
        functools.partial(_dsa_proj_kernel, sub, tn),
        grid=(t // tm, DSA_MAIN_W // tn),
        in_specs=[
            pl.BlockSpec(memory_space=pl.ANY),
            pl.BlockSpec((1, d), lambda i, j: (0, 0)),
            pl.BlockSpec((None, None, 3 * N_SUBLAYERS, d), lambda i, j: (layer, i // tpb, 0, 0)),
            pl.BlockSpec((None, tn, d),
                         lambda i, j: (widx, jnp.where(i == 0, j, DSA_MAIN_W // tn - 1), 0)),
            pl.BlockSpec((d, LANES), lambda i, j: (0, 0)),
            pl.BlockSpec((3, LANES, LANES), lambda i, j: (0, 0, 0)),
            pl.BlockSpec((N_ROPE_TABLES, tm, LANES), lambda i, j: (0, i, 0)),
        ],
        out_specs=[
            pl.BlockSpec((tm, tn), lambda i, j: (i, j)),
            pl.BlockSpec((tm, 2 * LANES), lambda i, j: (i, 0)),
            pl.BlockSpec((tm, LANES), lambda i, j: (i, 0)),
        ],
        out_shape=[
            jax.ShapeDtypeStruct((t, DSA_MAIN_W), BF16),
            jax.ShapeDtypeStruct((t, 2 * LANES), BF16),
            jax.ShapeDtypeStruct((t, LANES), F32),
        ],
        scratch_shapes=[pltpu.VMEM((tm, d), BF16),
                        pltpu.VMEM((DSA_MAIN_W // tn, tn, d), BF16),
                        pltpu.VMEM((tm, d), F32),
                        pltpu.SemaphoreType.DMA((1,))],
        compiler_params=_cparams(("arbitrary", "arbitrary")),
        name=f"dsa_proj_l{layer}",
    )(h, g_row, ada4, w_in, w_small, _lane_permutations(), tabs)


def _dsa_attn_kernel(tq, tk, top_k, q_ref, qi_ref, wi_ref, k_ref, v_ref, kk_ref, o_ref,
                     score_ref, bias_ref, thr_ref, wib_ref, qg_ref, m_ref, acc_ref):
    i = pl.program_id(1)
    n_chunks = ((i + 1) * tq + tk - 1) // tk
    max_chunks = score_ref.shape[1] // tk
    t_pos = i * tq + lax.broadcasted_iota(jnp.int32, (tq, tk), 0)
    k_iota = lax.broadcasted_iota(jnp.int32, (tq, tk), 1)
    n_cb = tk // LANES

    for hh in range(IDX_HEADS):
        wib_ref[hh] = jnp.broadcast_to(wi_ref[:, hh:hh + 1], (tq, LANES))
    for g in range(N_KV_HEADS):
        for r in range(KV_GROUP):
            hd = g * KV_GROUP + r
            qg_ref[g, r * tq:(r + 1) * tq, :] = q_ref[:, hd * HEAD_DIM:(hd + 1) * HEAD_DIM]

    def score_chunk(c, carry):
        off = pl.multiple_of(c * tk, tk)
        k_even = kk_ref[pl.ds(off, tk), 0:LANES]
        k_odd = kk_ref[pl.ds(off, tk), LANES:2 * LANES]
        acc = [jnp.zeros((tq, LANES), F32) for _ in range(n_cb)]
        for pair in range(IDX_HEADS // 2):
            qp = qi_ref[:, pair * LANES:(pair + 1) * LANES]
            for par, kop in enumerate((k_even, k_odd)):
                r = _dot_t(qp, kop)
                w = wib_ref[2 * pair + par]
                for cb in range(n_cb):
                    acc[cb] = acc[cb] + w * jnp.maximum(r[:, cb * LANES:(cb + 1) * LANES], 0.0)
        sc = jnp.concatenate(acc, axis=1)
        score_ref[:, pl.ds(off, tk)] = jnp.where(k_iota + off <= t_pos, sc, NEG)
        return carry

    lax.fori_loop(0, n_chunks, score_chunk, 0)

    def key_to_f32(key):
        return pltpu.bitcast(key ^ ((key >> 31) & 0x7FFFFFFF), F32)

    k_f = float(top_k)

    def select(n_static):
        def bit_step(n, thr_key):
            cand = thr_key + lax.shift_left(jnp.int32(1), KEY_BITS - 1 - n)
            thr = key_to_f32(cand)
            cnt = jnp.zeros((tq, LANES), F32)
            for cb in range(n_static * n_cb):
                s = score_ref[:, cb * LANES:(cb + 1) * LANES]
                cnt = cnt + jnp.where(s >= thr, 1.0, 0.0)
            total = jnp.sum(cnt, axis=1, keepdims=True)
            return jnp.where(total >= k_f, cand, thr_key)

        thr_key = lax.fori_loop(0, KEY_BITS, bit_step,
                                jnp.full((tq, LANES), INT32_MIN, jnp.int32), unroll=SELECT_UNROLL)
        thr_ref[...] = key_to_f32(thr_key)

    for n_static in range(1, max_chunks + 1):
        pl.when(n_chunks == n_static)(functools.partial(select, n_static))

    def bias_chunk(c, cnt):
        off = pl.multiple_of(c * tk, tk)
        thr = thr_ref[...]
        causal = k_iota + off <= t_pos
        for cb in range(n_cb):
            cols = pl.ds(off + cb * LANES, LANES)
            sel = score_ref[:, cols] >= thr
            bias_ref[:, cols] = jnp.where(
                sel, jnp.where(causal[:, cb * LANES:(cb + 1) * LANES], 0.0, NEG), NEG)
            cnt = cnt + jnp.where(sel, 1.0, 0.0)
        return cnt

    n_ge = jnp.sum(lax.fori_loop(0, n_chunks, bias_chunk, jnp.zeros((tq, LANES), F32)),
                   axis=1, keepdims=True)

    @pl.when(jnp.max(n_ge) > k_f)
    def _():
        def count_greater(c, cnt):
            off = pl.multiple_of(c * tk, tk)
            thr = thr_ref[...]
            for cb in range(n_cb):
                s = score_ref[:, pl.ds(off + cb * LANES, LANES)]
                cnt = cnt + jnp.where(s > thr, 1.0, 0.0)
            return cnt

        n_greater = jnp.sum(lax.fori_loop(0, n_chunks, count_greater,
                                          jnp.zeros((tq, LANES), F32)), axis=1, keepdims=True)
        n_take = k_f - n_greater
        before = jnp.where(
            lax.broadcasted_iota(jnp.int32, (tk, tk), 0) < lax.broadcasted_iota(jnp.int32, (tk, tk), 1),
            1.0, 0.0).astype(BF16)

        def tie_chunk(c, seen):
            off = pl.multiple_of(c * tk, tk)
            sc = score_ref[:, pl.ds(off, tk)]
            thr = jnp.concatenate([thr_ref[...]] * n_cb, axis=1)
            eq = jnp.where(sc == thr, 1.0, 0.0)
            rank = _dot(eq.astype(BF16), before) + seen
            keep = jnp.where(sc > thr, 1.0, jnp.where(rank < n_take, eq, 0.0))
            causal = k_iota + off <= t_pos
            bias_ref[:, pl.ds(off, tk)] = jnp.where(causal, jnp.where(keep > 0.0, 0.0, NEG), NEG)
            return seen + jnp.sum(eq, axis=1, keepdims=True)

        lax.fori_loop(0, n_chunks, tie_chunk, jnp.zeros((tq, 1), F32))

    rows = KV_GROUP * tq
    ones = jnp.ones((tk, LANES), BF16)

    def attend(off, first):
        b = bias_ref[:, pl.ds(off, tk)]
        for g in range(N_KV_HEADS):
            kc = k_ref[pl.ds(off, tk), g * HEAD_DIM:(g + 1) * HEAD_DIM]
            vc = v_ref[pl.ds(off, tk), g * HEAD_DIM:(g + 1) * HEAD_DIM]
            s = _dot_t(qg_ref[g], kc)
            s = (s.reshape(KV_GROUP, tq, tk) + b[None]).reshape(rows, tk)
            m_new = jnp.broadcast_to(jnp.max(s, axis=1, keepdims=True), (rows, LANES))
            if not first:
                m_old = m_ref[g]
                m_new = jnp.maximum(m_old, m_new)
            p = jnp.exp2(s - jnp.concatenate([m_new] * n_cb, axis=1))
            pv = _dot(p.astype(BF16), jnp.concatenate([vc, ones], axis=1))
            if first:
                acc_ref[g] = pv
            else:
                alpha = jnp.exp2(m_old - m_new)
                acc_ref[g] = jnp.concatenate([alpha, alpha], axis=1) * acc_ref[g] + pv
            m_ref[g] = m_new

    attend(0, True)

    def attn_chunk(c, carry):
        attend(pl.multiple_of(c * tk, tk), False)
        return carry

    lax.fori_loop(1, n_chunks, attn_chunk, 0)

    for g in range(N_KV_HEADS):
        out = acc_ref[g, :, 0:HEAD_DIM] / acc_ref[g, :, HEAD_DIM:2 * HEAD_DIM]
        for r in range(KV_GROUP):
            hd = g * KV_GROUP + r
            o_ref[:, hd * HEAD_DIM:(hd + 1) * HEAD_DIM] = out[r * tq:(r + 1) * tq].astype(BF16)


def _dsa_attn(main, kk, wi, batch, seq, tq, tk, top_k):
    t = main.shape[0]
    nq = seq // tq
    rows = KV_GROUP * tq
    return pl.pallas_call(
        functools.partial(_dsa_attn_kernel, tq, tk, top_k),
        grid=(batch, nq),
        in_specs=[
            pl.BlockSpec((tq, Q_W), lambda b, i: (b * nq + i, 0)),
            pl.BlockSpec((tq, QI_W), lambda b, i: (b * nq + i, (Q_W + 2 * KV_W) // QI_W)),
            pl.BlockSpec((tq, LANES), lambda b, i: (b * nq + i, 0)),
            pl.BlockSpec((seq, KV_W), lambda b, i: (b, Q_W // KV_W)),
            pl.BlockSpec((seq, KV_W), lambda b, i: (b, Q_W // KV_W + 1)),
            pl.BlockSpec((seq, 2 * LANES), lambda b, i: (b, 0)),
        ],
        out_specs=pl.BlockSpec((tq, Q_W), lambda b, i: (b * nq + i, 0)),
        out_shape=jax.ShapeDtypeStruct((t, Q_W), BF16),
        scratch_shapes=[
            pltpu.VMEM((tq, seq), F32),
            pltpu.VMEM((tq, seq), F32),
            pltpu.VMEM((tq, LANES), F32),
            pltpu.VMEM((IDX_HEADS, tq, LANES), F32),
            pltpu.VMEM((N_KV_HEADS, rows, HEAD_DIM), BF16),
            pltpu.VMEM((N_KV_HEADS, rows, LANES), F32),
            pltpu.VMEM((N_KV_HEADS, rows, 2 * HEAD_DIM), F32),
        ],
        compiler_params=_cparams(("arbitrary", "arbitrary")),
        name="dsa_attn",
    )(main, main, wi, main, main, kk)


def _tiles(seq, d_ff):
    tm = min(1024, seq)
    tm_ffn = min(2048, seq)
    tf = 256 if d_ff % 256 == 0 else d_ff
    tq = min(256, seq)
    tk = min(256, seq)
    return tm, tm_ffn, tf, tq, tk


def kernel(x, c, positions, ada_w, ada_b, norm_g, final_g, ffn_wgu, ffn_wd,
           pool_w_in, pool_w_grp, pool_scale, pool_w_out, dsa_w_in, dsa_w_out):
    batch, seq, d = x.shape
    depth = ada_w.shape[0]
    d_ff = ffn_wd.shape[2]
    t = batch * seq
    tm, tm_ffn, tf, tq, tk = _tiles(seq, d_ff)
    tn = min(512, d)
    top_k = min(TOPK_MAX, seq // 4)
    assert seq % tm == 0 and seq % tm_ffn == 0 and seq % tq == 0 and seq % tk == 0 and tk >= top_k
    assert dsa_w_in.shape[2] == DSA_MAIN_W + DSA_SMALL_W and DSA_MAIN_W % tn == 0

    rows = -(-batch // SUBLANES) * SUBLANES
    c8 = jnp.pad(c, ((0, rows - batch), (0, 0)))
    ada_n = ada_w.shape[2]
    ada, tabs = _ada(c8, ada_w, ada_b, 1024 if ada_n % 1024 == 0 else d, positions)
    ada4 = ada.reshape(depth, rows, 3 * N_SUBLAYERS, d)

    h = x.reshape(t, d)
    for layer in range(depth):
        j = layer // 2
        h = _ffn(h, ada4, norm_g, ffn_wgu, ffn_wd, layer, 0, 0, seq, tm_ffn, tf)
        g_row = norm_g[layer, 1][None, :]
        if layer % 2 == 0:
            u = _norm_linear(h, ada4, g_row, pool_w_in, j, layer, 1, seq, tm, tn)
            a = _pool_core(u, pool_w_grp, pool_scale, j, batch, seq)
            h = _linear_residual(a, pool_w_out, j, h, ada4, layer, 1, seq, tm, tn)
        else:
            w_small = _dsa_small_weights(dsa_w_in[j, :, DSA_MAIN_W:])
            main, kk, wi = _dsa_proj(h, ada4, g_row, jnp.swapaxes(dsa_w_in, 1, 2), j, w_small,
                                     tabs, layer, 1, seq, tm, tn)
            o = _dsa_attn(main, kk, wi, batch, seq, tq, tk, top_k)
            h = _linear_residual(o, dsa_w_out, j, h, ada4, layer, 1, seq, tm, tn)
        last = layer == depth - 1
        h = _ffn(h, ada4, norm_g, ffn_wgu, ffn_wd, layer, 1, 2, seq, tm_ffn, tf,
                 final_g=final_g if last else None)
    return h.reshape(batch, seq, d)
```

```python
import functools

import jax
import jax.numpy as jnp
import numpy as np
from jax import lax
from jax.experimental import pallas as pl
from jax.experimental.pallas import tpu as pltpu

F32 = jnp.float32
BF16 = jnp.bfloat16

N_SUBLAYERS = 3
POOL_WINDOWS = (2, 4, 8, 16)
N_HEADS = 16
N_KV_HEADS = 4
KV_GROUP = N_HEADS // N_KV_HEADS
HEAD_DIM = 128
ROT_DIM = HEAD_DIM // 4
ROPE_THETA = 500000.0
IDX_HEADS = 16
IDX_DIM = 64
IDX_ROT_DIM = IDX_DIM // 4
TOPK_MAX = 256
EPS = 1e-6
NEG = -1e30
Q_W = N_HEADS * HEAD_DIM
KV_W = N_KV_HEADS * HEAD_DIM
QI_W = IDX_HEADS * IDX_DIM
DSA_MAIN_W = Q_W + 2 * KV_W + QI_W
DSA_SMALL_W = IDX_DIM + IDX_HEADS
Q_PRESCALE = HEAD_DIM ** -0.5 * 1.4426950408889634

LANES = 128
SUBLANES = 8
VMEM_LIMIT_BYTES = 56 * 1024 * 1024
INT32_MIN = -(2 ** 31)
KEY_BITS = 32
SELECT_UNROLL = 4
N_ROPE_TABLES = 8


def _cparams(semantics):
    return pltpu.CompilerParams(dimension_semantics=semantics,
                                vmem_limit_bytes=VMEM_LIMIT_BYTES)


def _rmsnorm(x, g):
    ms = jnp.mean(x * x, axis=-1, keepdims=True)
    return x * lax.rsqrt(ms + EPS) * g


def _rmsnorm_rows(ref, rows, g_ref):
    ref[rows, :] = _rmsnorm(ref[rows, :], g_ref[...])


NORM_SLAB = 16
NORM_UNROLL = 8
FFN_ROW_SLAB = 64
FFN_SLAB_SLOTS = 8
FFN_DOWN_COLS = 256


def _for_row_slabs(n_rows, slab, body, unroll=1):
    def step(r, carry):
        body(pl.ds(pl.multiple_of(r * slab, slab), slab))
        return carry
    lax.fori_loop(0, n_rows // slab, step, 0, unroll=unroll)


def _write_norm_mod(x_ref, g_ref, m_ref, sub, y_ref, copy_ref=None):
    shift = m_ref[3 * sub + 0:3 * sub + 1, :]
    gs = g_ref[...] * (1.0 + m_ref[3 * sub + 1:3 * sub + 2, :])
    slab = min(NORM_SLAB, x_ref.shape[0])

    def norm_rows(rows):
        x = x_ref[rows, :]
        r = lax.rsqrt(jnp.mean(x * x, axis=-1, keepdims=True) + EPS)
        y_ref[rows, :] = ((x * r) * gs + shift).astype(BF16)
        if copy_ref is not None:
            copy_ref[rows, :] = x

    _for_row_slabs(x_ref.shape[0], slab, norm_rows, unroll=NORM_UNROLL)


def _silu(x):
    return x * jax.nn.sigmoid(x)


def _dot(a, b):
    return jnp.dot(a, b, preferred_element_type=F32)


def _dot_t(a, b):
    return lax.dot_general(a, b, (((1,), (1,)), ((), ())), preferred_element_type=F32)


def _ada_kernel(n_tab_blocks, c_ref, w_ref, b_ref, pos_ref, invf_ref, o_ref, tab_ref):
    sc = _silu(c_ref[...]).astype(BF16)
    o_ref[...] = _dot(sc, w_ref[...].astype(BF16)) + b_ref[...]
    step = pl.program_id(0) * pl.num_programs(1) + pl.program_id(1)

    @pl.when(step < n_tab_blocks)
    def _():
        _write_rope_tables(pos_ref, invf_ref, tab_ref)


def _ada(c8, ada_w, ada_b, tn, positions):
    depth, d, n = ada_w.shape
    rows = c8.shape[0]
    n_steps = depth * (n // tn)
    t = positions.size
    rt = SUBLANES
    while t // rt > n_steps or t % rt:
        rt *= 2
    n_tab = t // rt
    nj = n // tn

    def tab_block(l, j):
        return jnp.minimum(l * nj + j, n_tab - 1)

    return pl.pallas_call(
        functools.partial(_ada_kernel, n_tab),
        grid=(depth, nj),
        in_specs=[
            pl.BlockSpec((rows, d), lambda l, j: (0, 0)),
            pl.BlockSpec((None, d, tn), lambda l, j: (l, 0, j)),
            pl.BlockSpec((None, 1, tn), lambda l, j: (l, 0, j)),
            pl.BlockSpec((rt, 1), lambda l, j: (tab_block(l, j), 0)),
            pl.BlockSpec((1, LANES), lambda l, j: (0, 0)),
        ],
        out_specs=[
            pl.BlockSpec((None, rows, tn), lambda l, j: (l, 0, j)),
            pl.BlockSpec((N_ROPE_TABLES, rt, LANES), lambda l, j: (0, tab_block(l, j), 0)),
        ],
        out_shape=[
            jax.ShapeDtypeStruct((depth, rows, n), F32),
            jax.ShapeDtypeStruct((N_ROPE_TABLES, t, LANES), F32),
        ],
        compiler_params=_cparams(("arbitrary", "arbitrary")),
        name="ada",
    )(c8, ada_w, ada_b.reshape(depth, 1, n), positions.reshape(t, 1), _rope_inv_freq_row())


def _ffn_kernel(sub, final, rs, *refs):
    if final:
        (x_hbm, g_ref, m_ref, wg_ref, wu_ref, wd_ref, fg_ref, o_hbm,
         acc_ref, y_ref, xbuf, in_sem, out_sem) = refs
    else:
        (x_hbm, g_ref, m_ref, wg_ref, wu_ref, wd_ref, o_hbm,
         acc_ref, y_ref, xbuf, in_sem, out_sem) = refs
    k = pl.program_id(1)
    nk = pl.num_programs(1)
    tm, d_out = acc_ref.shape
    n_slabs = tm // rs
    n_slots = xbuf.shape[0]
    n_first = min(n_slots, n_slabs)
    row0 = pl.program_id(0) * tm

    def x_copy(s, tile_row0=row0):
        slot = s % n_slots
        return pltpu.make_async_copy(x_hbm.at[pl.ds(tile_row0 + s * rs, rs), :], xbuf.at[slot],
                                     in_sem.at[slot])

    def o_copy(s):
        return pltpu.make_async_copy(acc_ref.at[pl.ds(s * rs, rs), :],
                                     o_hbm.at[pl.ds(row0 + s * rs, rs), :], out_sem.at[s])

    @pl.when(k == 0)
    def _():
        @pl.when(pl.program_id(0) == 0)
        def _():
            for s in range(n_first):
                x_copy(s).start()

        for s in range(n_slabs):
            x_copy(s).wait()
            pl.when(pl.program_id(0) > 0)(o_copy(s).wait)
            rows = pl.ds(s * rs, rs)
            _write_norm_mod(xbuf.at[s % n_slots], g_ref, m_ref, sub, y_ref.at[rows, :],
                            copy_ref=acc_ref.at[rows, :])
            if s + n_slots < n_slabs:
                x_copy(s + n_slots).start()

    @pl.when(jnp.logical_and(k == nk - 1, pl.program_id(0) + 1 < pl.num_programs(0)))
    def _():
        for s in range(n_first):
            x_copy(s, row0 + tm).start()

    y = y_ref[...]
    g = _dot(y, wg_ref[...].astype(BF16))
    u = _dot(y, wu_ref[...].astype(BF16))
    a = (_silu(g) * u).astype(BF16)
    half_gate = 0.5 * m_ref[3 * sub + 2:3 * sub + 3, :]
    nc = min(FFN_DOWN_COLS, d_out)
    for n in range(d_out // nc):
        cols = slice(n * nc, (n + 1) * nc)
        acc_ref[:, cols] += half_gate[:, cols] * _dot(a, wd_ref[:, cols].astype(BF16))

    @pl.when(k == nk - 1)
    def _():
        for s in range(n_slabs):
            if final:
                _for_row_slabs(
                    rs, min(NORM_SLAB, rs),
                    lambda rows, s=s: _rmsnorm_rows(acc_ref.at[pl.ds(s * rs, rs), :], rows, fg_ref),
                    unroll=NORM_UNROLL)
            o_copy(s).start()

        @pl.when(pl.program_id(0) == pl.num_programs(0) - 1)
        def _():
            for s in range(n_slabs):
                o_copy(s).wait()


def _ffn(h, ada4, norm_g, ffn_wgu, ffn_wd, layer, which, sub, seq, tm, tf, final_g=None):
    t, d = h.shape
    d_ff = ffn_wd.shape[2]
    nk = d_ff // tf
    tpb = seq // tm
    rs = min(FFN_ROW_SLAB, tm)
    final = final_g is not None
    in_specs = [
        pl.BlockSpec(memory_space=pl.ANY),
        pl.BlockSpec((1, d), lambda i, k: (0, 0)),
        pl.BlockSpec((None, None, 3 * N_SUBLAYERS, d), lambda i, k: (layer, i // tpb, 0, 0)),
        pl.BlockSpec((None, None, d, tf), lambda i, k: (layer, which, 0, k)),
        pl.BlockSpec((None, None, d, tf), lambda i, k: (layer, which, 0, nk + k)),
        pl.BlockSpec((None, None, tf, d), lambda i, k: (layer, which, k, 0)),
    ]
    args = [h, norm_g[layer, sub][None, :], ada4, ffn_wgu, ffn_wgu, ffn_wd]
    if final:
        in_specs.append(pl.BlockSpec((1, d), lambda i, k: (0, 0)))
        args.append(final_g[None, :])
    return pl.pallas_call(
        functools.partial(_ffn_kernel, sub, final, rs),
        grid=(t // tm, nk),
        in_specs=in_specs,
        out_specs=pl.BlockSpec(memory_space=pl.ANY),
        out_shape=jax.ShapeDtypeStruct((t, d), F32),
        scratch_shapes=[
            pltpu.VMEM((tm, d), F32),
            pltpu.VMEM((tm, d), BF16),
            pltpu.VMEM((FFN_SLAB_SLOTS, rs, d), F32),
            pltpu.SemaphoreType.DMA((FFN_SLAB_SLOTS,)),
            pltpu.SemaphoreType.DMA((tm // rs,)),
        ],
        compiler_params=_cparams(("arbitrary", "arbitrary")),
        name=f"ffn_l{layer}_{which}",
    )(*args)


def _keep_bf16_weight(w_ref, wbf_ref):
    j = pl.program_id(1)

    @pl.when(pl.program_id(0) == 0)
    def _():
        wbf_ref[j] = w_ref[...].astype(BF16)

    return wbf_ref[j]


def _resident_weight_spec(widx, kdim, tn, nj):
    return pl.BlockSpec((None, kdim, tn), lambda i, j: (widx, 0, jnp.where(i == 0, j, nj - 1)))


def _norm_linear_kernel(sub, x_ref, g_ref, m_ref, w_ref, o_ref, y_ref, wbf_ref):
    @pl.when(pl.program_id(1) == 0)
    def _():
        _write_norm_mod(x_ref, g_ref, m_ref, sub, y_ref)

    o_ref[...] = _dot(y_ref[...], _keep_bf16_weight(w_ref, wbf_ref))


def _norm_linear(h, ada4, g_row, w, widx, layer, sub, seq, tm, tn):
    t, d = h.shape
    n = w.shape[2]
    tpb = seq // tm
    return pl.pallas_call(
        functools.partial(_norm_linear_kernel, sub),
        grid=(t // tm, n // tn),
        in_specs=[
            pl.BlockSpec((tm, d), lambda i, j: (i, 0)),
            pl.BlockSpec((1, d), lambda i, j: (0, 0)),
            pl.BlockSpec((None, None, 3 * N_SUBLAYERS, d), lambda i, j: (layer, i // tpb, 0, 0)),
            _resident_weight_spec(widx, d, tn, n // tn),
        ],
        out_specs=pl.BlockSpec((tm, tn), lambda i, j: (i, j)),
        out_shape=jax.ShapeDtypeStruct((t, n), F32),
        scratch_shapes=[pltpu.VMEM((tm, d), BF16), pltpu.VMEM((n // tn, d, tn), BF16)],
        compiler_params=_cparams(("arbitrary", "arbitrary")),
        name=f"norm_linear_l{layer}",
    )(h, g_row, ada4, w)


def _pool_kernel(u_ref, w_ref, ls_ref, o_ref):
    g = pl.program_id(1)
    u = u_ref[...]
    t = lax.broadcasted_iota(jnp.int32, u.shape, 0)

    def shifted(v, d):
        return jnp.where(t >= d, pltpu.roll(v, d, 0), 0.0)

    for gi, win in enumerate(POOL_WINDOWS):
        @pl.when(g == gi)
        def _(win=win):
            acc = u
            d = 1
            while d < win:
                acc = acc + shifted(acc, d)
                d *= 2
            cnt = jnp.minimum(t + 1, win).astype(F32)
            p = (acc / cnt - u).astype(BF16)
            v = _dot(p, w_ref[...].astype(BF16)) * ls_ref[...]
            o_ref[...] = v.astype(BF16)


def _pool_core(u, w_grp, ls, widx, batch, seq):
    t, d = u.shape
    _, groups, c, _ = w_grp.shape
    return pl.pallas_call(
        _pool_kernel,
        grid=(batch, groups),
        in_specs=[
            pl.BlockSpec((seq, c), lambda b, g: (b, g)),
            pl.BlockSpec((None, None, c, c), lambda b, g: (widx, g, 0, 0)),
            pl.BlockSpec((None, 1, c), lambda b, g: (widx, 0, g)),
        ],
        out_specs=pl.BlockSpec((seq, c), lambda b, g: (b, g)),
        out_shape=jax.ShapeDtypeStruct((t, d), BF16),
        compiler_params=_cparams(("arbitrary", "arbitrary")),
        name="pool_core",
    )(u, w_grp, ls.reshape(ls.shape[0], 1, d))


def _linear_residual_kernel(sub, a_ref, w_ref, h_ref, m_ref, o_ref, wbf_ref):
    gate = m_ref[3 * sub + 2:3 * sub + 3, :]
    o_ref[...] = h_ref[...] + gate * _dot(a_ref[...], _keep_bf16_weight(w_ref, wbf_ref))


def _linear_residual(a, w, widx, h, ada4, layer, sub, seq, tm, tn):
    t, kdim = a.shape
    n = w.shape[2]
    tpb = seq // tm
    return pl.pallas_call(
        functools.partial(_linear_residual_kernel, sub),
        grid=(t // tm, n // tn),
        in_specs=[
            pl.BlockSpec((tm, kdim), lambda i, j: (i, 0)),
            _resident_weight_spec(widx, kdim, tn, n // tn),
            pl.BlockSpec((tm, tn), lambda i, j: (i, j)),
            pl.BlockSpec((None, None, 3 * N_SUBLAYERS, tn), lambda i, j: (layer, i // tpb, 0, j)),
        ],
        out_specs=pl.BlockSpec((tm, tn), lambda i, j: (i, j)),
        out_shape=jax.ShapeDtypeStruct((t, n), F32),
        scratch_shapes=[pltpu.VMEM((n // tn, kdim, tn), BF16)],
        compiler_params=_cparams(("arbitrary", "arbitrary")),
        name=f"linear_residual_l{layer}",
    )(a, w, h, ada4)


ROPE_SET_Q, ROPE_SET_K, ROPE_SET_NONE, ROPE_SET_IDX = 0, 1, 2, 3
HALF_LANES = LANES // 2


def _write_rope_tables(pos_ref, invf_ref, o_ref):
    pos = pos_ref[...].astype(F32)
    lane = lax.broadcasted_iota(jnp.int32, (pos.shape[0], LANES), 1)
    ang = pos * invf_ref[...]
    cos = jnp.cos(ang)
    sin = jnp.where(lane < HALF_LANES, -jnp.sin(ang), jnp.sin(ang))
    rot_big = (lane & (HALF_LANES - 1)) < ROT_DIM // 2
    cb = jnp.where(rot_big, cos, 1.0)
    sb = jnp.where(rot_big, sin, 0.0)
    rot_idx = (lane & (HALF_LANES // 2 - 1)) < IDX_ROT_DIM // 2
    even_quarter = (lane & (HALF_LANES // 2)) == 0
    shift = ROT_DIM // 2

    def idx_table(v, rest):
        from_above = pltpu.roll(v, LANES - shift, 1)
        from_below = pltpu.roll(v, shift, 1)
        return jnp.where(rot_idx, jnp.where(even_quarter, from_above, from_below), rest)

    ci = idx_table(cos, 1.0)
    si = idx_table(sin, 0.0)
    o_ref[2 * ROPE_SET_Q] = cb * Q_PRESCALE
    o_ref[2 * ROPE_SET_Q + 1] = sb * Q_PRESCALE
    o_ref[2 * ROPE_SET_K] = cb
    o_ref[2 * ROPE_SET_K + 1] = sb
    o_ref[2 * ROPE_SET_NONE] = jnp.ones_like(cb)
    o_ref[2 * ROPE_SET_NONE + 1] = jnp.zeros_like(cb)
    o_ref[2 * ROPE_SET_IDX] = ci
    o_ref[2 * ROPE_SET_IDX + 1] = si


def _rope_inv_freq_row():
    def inv_freq(rot_dim):
        half = rot_dim // 2
        return ROPE_THETA ** (-jnp.arange(half, dtype=F32) / half)

    half_row = jnp.concatenate([
        inv_freq(ROT_DIM), inv_freq(IDX_ROT_DIM),
        jnp.zeros((HALF_LANES - ROT_DIM // 2 - IDX_ROT_DIM // 2,), F32)])
    return jnp.tile(half_row, 2)[None, :]


def _rope(x, c, s):
    return x * c + pltpu.roll(x, HALF_LANES, 1) * s


PERM_BIG, PERM_NONE, PERM_IDX = 0, 1, 2


def _lane_permutations():
    h, p = ROT_DIM // 2, (HEAD_DIM - ROT_DIM) // 2
    big = np.concatenate([np.arange(0, h), np.arange(ROT_DIM, ROT_DIM + p),
                          np.arange(h, ROT_DIM), np.arange(ROT_DIM + p, HEAD_DIM)])
    lo, hi = _idx_lo_hi_dims()
    idx = np.concatenate([lo, IDX_DIM + lo, hi, IDX_DIM + hi])
    mats = np.zeros((3, LANES, LANES), np.float32)
    for kind, src in ((PERM_BIG, big), (PERM_NONE, np.arange(LANES)), (PERM_IDX, idx)):
        mats[kind, np.arange(LANES), src] = 1.0
    return jnp.asarray(mats, BF16)


def _idx_lo_hi_dims():
    h, p = IDX_ROT_DIM // 2, (IDX_DIM - IDX_ROT_DIM) // 2
    lo = np.concatenate([np.arange(0, h), np.arange(IDX_ROT_DIM, IDX_ROT_DIM + p)])
    hi = np.concatenate([np.arange(h, IDX_ROT_DIM), np.arange(IDX_ROT_DIM + p, IDX_DIM)])
    return lo, hi


def _dsa_small_weights(w_tail):
    h, p = IDX_ROT_DIM // 2, (IDX_DIM - IDX_ROT_DIM) // 2
    klo = jnp.concatenate([w_tail[:, 0:h], w_tail[:, IDX_ROT_DIM:IDX_ROT_DIM + p]], axis=1)
    khi = jnp.concatenate([w_tail[:, h:IDX_ROT_DIM], w_tail[:, IDX_ROT_DIM + p:IDX_DIM]], axis=1)
    wi = jnp.pad(w_tail[:, IDX_DIM:], ((0, 0), (0, LANES - IDX_DIM - IDX_HEADS)))
    return jnp.concatenate([klo, khi, wi], axis=1).astype(BF16)


def _dsa_proj_kernel(sub, tn, x_hbm, g_ref, m_ref, w_ref, ws_ref, perm_ref, tab_ref,
                     o_ref, ok_ref, ow_ref, y_ref, wbf_ref, x_ref, x_sem):
    i = pl.program_id(0)
    j = pl.program_id(1)
    tm = x_ref.shape[0]
    first_k, first_v, first_qi = Q_W // tn, (Q_W + KV_W) // tn, (Q_W + 2 * KV_W) // tn

    def x_copy(tile):
        return pltpu.make_async_copy(x_hbm.at[pl.ds(tile * tm, tm), :], x_ref, x_sem.at[0])

    @pl.when(j == 0)
    def _():
        pl.when(i == 0)(x_copy(i).start)
        x_copy(i).wait()
        _write_norm_mod(x_ref, g_ref, m_ref, sub, y_ref)
        pl.when(i + 1 < pl.num_programs(0))(x_copy(i + 1).start)
        small = _dot(y_ref[...], ws_ref[...])
        by32 = pltpu.roll(small, HALF_LANES // 2, 1)
        by64 = pltpu.roll(small, HALF_LANES, 1)
        quarter = lax.broadcasted_iota(jnp.int32, small.shape, 1) // (HALF_LANES // 2)
        k_even = jnp.where(quarter == 0, small, jnp.where(quarter == 2, by32, 0.0))
        k_odd = jnp.where(quarter == 1, by32, jnp.where(quarter == 3, by64, 0.0))
        for s, kk in enumerate((k_even, k_odd)):
            ok_ref[:, s * LANES:(s + 1) * LANES] = _rope(
                kk, tab_ref[2 * ROPE_SET_IDX], tab_ref[2 * ROPE_SET_IDX + 1]).astype(BF16)
        ow_ref[...] = jnp.where(quarter == 0, by64, 0.0)

    @pl.when(pl.program_id(0) == 0)
    def _():
        kind = jnp.where(j < first_v, PERM_BIG, jnp.where(j < first_qi, PERM_NONE, PERM_IDX))
        perm = perm_ref[kind]
        for sl in range(tn // LANES):
            rows = slice(sl * LANES, (sl + 1) * LANES)
            wbf_ref[j, rows, :] = _dot(perm, w_ref[rows, :].astype(BF16)).astype(BF16)

    rope_set = jnp.where(j < first_k, ROPE_SET_Q,
                         jnp.where(j < first_v, ROPE_SET_K,
                                   jnp.where(j < first_qi, ROPE_SET_NONE, ROPE_SET_IDX)))
    c = tab_ref[2 * rope_set]
    s = tab_ref[2 * rope_set + 1]
    y = y_ref[...]
    wide = 2 * LANES
    for part in range(tn // wide):
        acc = _dot_t(y, wbf_ref[j, part * wide:(part + 1) * wide, :])
        for sl in range(wide // LANES):
            cols = slice(part * wide + sl * LANES, part * wide + (sl + 1) * LANES)
            o_ref[:, cols] = _rope(acc[:, sl * LANES:(sl + 1) * LANES], c, s).astype(BF16)


def _dsa_proj(h, ada4, g_row, w_in, widx, w_small, tabs, layer, sub, seq, tm, tn):
    t, d = h.shape
    tpb = seq // tm
    return pl.pallas_call(
        functools.partial(_dsa_proj_kernel, sub, tn),
        grid=(t // tm, DSA_MAIN_W // tn),
        in_specs=[
            pl.BlockSpec(memory_space=pl.ANY),
            pl.BlockSpec((1, d), lambda i, j: (0, 0)),
            pl.BlockSpec((None, None, 3 * N_SUBLAYERS, d), lambda i, j: (layer, i // tpb, 0, 0)),
            pl.BlockSpec((None, tn, d),
                         lambda i, j: (widx, jnp.where(i == 0, j, DSA_MAIN_W // tn - 1), 0)),
            pl.BlockSpec((d, LANES), lambda i, j: (0, 0)),
            pl.BlockSpec((3, LANES, LANES), lambda i, j: (0, 0, 0)),
            pl.BlockSpec((N_ROPE_TABLES, tm, LANES), lambda i, j: (0, i, 0)),
        ],
        out_specs=[
            pl.BlockSpec((tm, tn), lambda i, j: (i, j)),
            pl.BlockSpec((tm, 2 * LANES), lambda i, j: (i, 0)),
            pl.BlockSpec((tm, LANES), lambda i, j: (i, 0)),
        ],
        out_shape=[
            jax.ShapeDtypeStruct((t, DSA_MAIN_W), BF16),
            jax.ShapeDtypeStruct((t, 2 * LANES), BF16),
            jax.ShapeDtypeStruct((t, LANES), F32),
        ],
        scratch_shapes=[pltpu.VMEM((tm, d), BF16),
                        pltpu.VMEM((DSA_MAIN_W // tn, tn, d), BF16),
                        pltpu.VMEM((tm, d), F32),
                        pltpu.SemaphoreType.DMA((1,))],
        compiler_params=_cparams(("arbitrary", "arbitrary")),
        name=f"dsa_proj_l{layer}",
    )(h, g_row, ada4, w_in, w_small, _lane_permutations(), tabs)


def _dsa_attn_kernel(tq, tk, top_k, q_ref, qi_ref, wi_ref, k_ref, v_ref, kk_ref, o_ref,
                     score_ref, bias_ref, thr_ref, wib_ref, qg_ref, m_ref, acc_ref):
    i = pl.program_id(1)
    n_chunks = ((i + 1) * tq + tk - 1) // tk
    max_chunks = score_ref.shape[1] // tk
    t_pos = i * tq + lax.broadcasted_iota(jnp.int32, (tq, tk), 0)
    k_iota = lax.broadcasted_iota(jnp.int32, (tq, tk), 1)
    n_cb = tk // LANES

    for hh in range(IDX_HEADS):
        wib_ref[hh] = jnp.broadcast_to(wi_ref[:, hh:hh + 1], (tq, LANES))
    for g in range(N_KV_HEADS):
        for r in range(KV_GROUP):
            hd = g * KV_GROUP + r
            qg_ref[g, r * tq:(r + 1) * tq, :] = q_ref[:, hd * HEAD_DIM:(hd + 1) * HEAD_DIM]

    def score_chunk(c, carry):
        off = pl.multiple_of(c * tk, tk)
        k_even = kk_ref[pl.ds(off, tk), 0:LANES]
        k_odd = kk_ref[pl.ds(off, tk), LANES:2 * LANES]
        acc = [jnp.zeros((tq, LANES), F32) for _ in range(n_cb)]
        for pair in range(IDX_HEADS // 2):
            qp = qi_ref[:, pair * LANES:(pair + 1) * LANES]
            for par, kop in enumerate((k_even, k_odd)):
                r = _dot_t(qp, kop)
                w = wib_ref[2 * pair + par]
                for cb in range(n_cb):
                    acc[cb] = acc[cb] + w * jnp.maximum(r[:, cb * LANES:(cb + 1) * LANES], 0.0)
        sc = jnp.concatenate(acc, axis=1)
        score_ref[:, pl.ds(off, tk)] = jnp.where(k_iota + off <= t_pos, sc, NEG)
        return carry

    lax.fori_loop(0, n_chunks, score_chunk, 0)

    def key_to_f32(key):
        return pltpu.bitcast(key ^ ((key >> 31) & 0x7FFFFFFF), F32)

    k_f = float(top_k)

    def select(n_static):
        def bit_step(n, thr_key):
            cand = thr_key + lax.shift_left(jnp.int32(1), KEY_BITS - 1 - n)
            thr = key_to_f32(cand)
            cnt = jnp.zeros((tq, LANES), F32)
            for cb in range(n_static * n_cb):
                s = score_ref[:, cb * LANES:(cb + 1) * LANES]
                cnt = cnt + jnp.where(s >= thr, 1.0, 0.0)
            total = jnp.sum(cnt, axis=1, keepdims=True)
            return jnp.where(total >= k_f, cand, thr_key)

        thr_key = lax.fori_loop(0, KEY_BITS, bit_step,
                                jnp.full((tq, LANES), INT32_MIN, jnp.int32), unroll=SELECT_UNROLL)
        thr_ref[...] = key_to_f32(thr_key)

    for n_static in range(1, max_chunks + 1):
        pl.when(n_chunks == n_static)(functools.partial(select, n_static))

    def bias_chunk(c, cnt):
        off = pl.multiple_of(c * tk, tk)
        thr = thr_ref[...]
        causal = k_iota + off <= t_pos
        for cb in range(n_cb):
            cols = pl.ds(off + cb * LANES, LANES)
            sel = score_ref[:, cols] >= thr
            bias_ref[:, cols] = jnp.where(
                sel, jnp.where(causal[:, cb * LANES:(cb + 1) * LANES], 0.0, NEG), NEG)
            cnt = cnt + jnp.where(sel, 1.0, 0.0)
        return cnt

    n_ge = jnp.sum(lax.fori_loop(0, n_chunks, bias_chunk, jnp.zeros((tq, LANES), F32)),
                   axis=1, keepdims=True)

    @pl.when(jnp.max(n_ge) > k_f)
    def _():
        def count_greater(c, cnt):
            off = pl.multiple_of(c * tk, tk)
            thr = thr_ref[...]
            for cb in range(n_cb):
                s = score_ref[:, pl.ds(off + cb * LANES, LANES)]
                cnt = cnt + jnp.where(s > thr, 1.0, 0.0)
            return cnt

        n_greater = jnp.sum(lax.fori_loop(0, n_chunks, count_greater,
                                          jnp.zeros((tq, LANES), F32)), axis=1, keepdims=True)
        n_take = k_f - n_greater
        before = jnp.where(
            lax.broadcasted_iota(jnp.int32, (tk, tk), 0) < lax.broadcasted_iota(jnp.int32, (tk, tk), 1),
            1.0, 0.0).astype(BF16)

        def tie_chunk(c, seen):
            off = pl.multiple_of(c * tk, tk)
            sc = score_ref[:, pl.ds(off, tk)]
            thr = jnp.concatenate([thr_ref[...]] * n_cb, axis=1)
            eq = jnp.where(sc == thr, 1.0, 0.0)
            rank = _dot(eq.astype(BF16), before) + seen
            keep = jnp.where(sc > thr, 1.0, jnp.where(rank < n_take, eq, 0.0))
            causal = k_iota + off <= t_pos
            bias_ref[:, pl.ds(off, tk)] = jnp.where(causal, jnp.where(keep > 0.0, 0.0, NEG), NEG)
            return seen + jnp.sum(eq, axis=1, keepdims=True)

        lax.fori_loop(0, n_chunks, tie_chunk, jnp.zeros((tq, 1), F32))

    rows = KV_GROUP * tq
    ones = jnp.ones((tk, LANES), BF16)

    def attend(off, first):
        b = bias_ref[:, pl.ds(off, tk)]
        for g in range(N_KV_HEADS):
            kc = k_ref[pl.ds(off, tk), g * HEAD_DIM:(g + 1) * HEAD_DIM]
            vc = v_ref[pl.ds(off, tk), g * HEAD_DIM:(g + 1) * HEAD_DIM]
            s = _dot_t(qg_ref[g], kc)
            s = (s.reshape(KV_GROUP, tq, tk) + b[None]).reshape(rows, tk)
            m_new = jnp.broadcast_to(jnp.max(s, axis=1, keepdims=True), (rows, LANES))
            if not first:
                m_old = m_ref[g]
                m_new = jnp.maximum(m_old, m_new)
            p = jnp.exp2(s - jnp.concatenate([m_new] * n_cb, axis=1))
            pv = _dot(p.astype(BF16), jnp.concatenate([vc, ones], axis=1))
            if first:
                acc_ref[g] = pv
            else:
                alpha = jnp.exp2(m_old - m_new)
                acc_ref[g] = jnp.concatenate([alpha, alpha], axis=1) * acc_ref[g] + pv
            m_ref[g] = m_new

    attend(0, True)

    def attn_chunk(c, carry):
        attend(pl.multiple_of(c * tk, tk), False)
        return carry

    lax.fori_loop(1, n_chunks, attn_chunk, 0)

    for g in range(N_KV_HEADS):
        out = acc_ref[g, :, 0:HEAD_DIM] / acc_ref[g, :, HEAD_DIM:2 * HEAD_DIM]
        for r in range(KV_GROUP):
            hd = g * KV_GROUP + r
            o_ref[:, hd * HEAD_DIM:(hd + 1) * HEAD_DIM] = out[r * tq:(r + 1) * tq].astype(BF16)


def _dsa_attn(main, kk, wi, batch, seq, tq, tk, top_k):
    t = main.shape[0]
    nq = seq // tq
    rows = KV_GROUP * tq
    return pl.pallas_call(
        functools.partial(_dsa_attn_kernel, tq, tk, top_k),
        grid=(batch, nq),
        in_specs=[
            pl.BlockSpec((tq, Q_W), lambda b, i: (b * nq + i, 0)),
            pl.BlockSpec((tq, QI_W), lambda b, i: (b * nq + i, (Q_W + 2 * KV_W) // QI_W)),
            pl.BlockSpec((tq, LANES), lambda b, i: (b * nq + i, 0)),
            pl.BlockSpec((seq, KV_W), lambda b, i: (b, Q_W // KV_W)),
            pl.BlockSpec((seq, KV_W), lambda b, i: (b, Q_W // KV_W + 1)),
            pl.BlockSpec((seq, 2 * LANES), lambda b, i: (b, 0)),
        ],
        out_specs=pl.BlockSpec((tq, Q_W), lambda b, i: (b * nq + i, 0)),
        out_shape=jax.ShapeDtypeStruct((t, Q_W), BF16),
        scratch_shapes=[
            pltpu.VMEM((tq, seq), F32),
            pltpu.VMEM((tq, seq), F32),
            pltpu.VMEM((tq, LANES), F32),
            pltpu.VMEM((IDX_HEADS, tq, LANES), F32),
            pltpu.VMEM((N_KV_HEADS, rows, HEAD_DIM), BF16),
            pltpu.VMEM((N_KV_HEADS, rows, LANES), F32),
            pltpu.VMEM((N_KV_HEADS, rows, 2 * HEAD_DIM), F32),
        ],
        compiler_params=_cparams(("arbitrary", "arbitrary")),
        name="dsa_attn",
    )(main, main, wi, main, main, kk)


def _tiles(seq, d_ff):
    tm = min(1024, seq)
    tm_ffn = min(2048, seq)
    tf = 256 if d_ff % 256 == 0 else d_ff
    tq = min(256, seq)
    tk = min(256, seq)
    return tm, tm_ffn, tf, tq, tk


def kernel(x, c, positions, ada_w, ada_b, norm_g, final_g, ffn_wgu, ffn_wd,
           pool_w_in, pool_w_grp, pool_scale, pool_w_out, dsa_w_in, dsa_w_out):
    batch, seq, d = x.shape
    depth = ada_w.shape[0]
    d_ff = ffn_wd.shape[2]
    t = batch * seq
    tm, tm_ffn, tf, tq, tk = _tiles(seq, d_ff)
    tn = min(512, d)
    top_k = min(TOPK_MAX, seq // 4)
    assert seq % tm == 0 and seq % tm_ffn == 0 and seq % tq == 0 and seq % tk == 0 and tk >= top_k
    assert dsa_w_in.shape[2] == DSA_MAIN_W + DSA_SMALL_W and DSA_MAIN_W % tn == 0

    rows = -(-batch // SUBLANES) * SUBLANES
    c8 = jnp.pad(c, ((0, rows - batch), (0, 0)))
    ada_n = ada_w.shape[2]
    ada, tabs = _ada(c8, ada_w, ada_b, 1024 if ada_n % 1024 == 0 else d, positions)
    ada4 = ada.reshape(depth, rows, 3 * N_SUBLAYERS, d)

    h = x.reshape(t, d)
    for layer in range(depth):
        j = layer // 2
        h = _ffn(h, ada4, norm_g, ffn_wgu, ffn_wd, layer, 0, 0, seq, tm_ffn, tf)
        g_row = norm_g[layer, 1][None, :]
        if layer % 2 == 0:
            u = _norm_linear(h, ada4, g_row, pool_w_in, j, layer, 1, seq, tm, tn)
            a = _pool_core(u, pool_w_grp, pool_scale, j, batch, seq)
            h = _linear_residual(a, pool_w_out, j, h, ada4, layer, 1, seq, tm, tn)
        else:
            w_small = _dsa_small_weights(dsa_w_in[j, :, DSA_MAIN_W:])
            main, kk, wi = _dsa_proj(h, ada4, g_row, jnp.swapaxes(dsa_w_in, 1, 2), j, w_small,
                                     tabs, layer, 1, seq, tm, tn)
            o = _dsa_attn(main, kk, wi, batch, seq, tq, tk, top_k)
            h = _linear_residual(o, dsa_w_out, j, h, ada4, layer, 1, seq, tm, tn)
        last = layer == depth - 1
        h = _ffn(h, ada4, norm_g, ffn_wgu, ffn_wd, layer, 1, 2, seq, tm_ffn, tf,
                 final_g=final_g if last else None)
    return h.reshape(batch, seq, d)
```

```python
import functools

import jax
import jax.numpy as jnp
import numpy as np
from jax import lax
from jax.experimental import pallas as pl
from jax.experimental.pallas import tpu as pltpu

F32 = jnp.float32
BF16 = jnp.bfloat16

N_SUBLAYERS = 3
POOL_WINDOWS = (2, 4, 8, 16)
N_HEADS = 16
N_KV_HEADS = 4
KV_GROUP = N_HEADS // N_KV_HEADS
HEAD_DIM = 128
ROT_DIM = HEAD_DIM // 4
ROPE_THETA = 500000.0
IDX_HEADS = 16
IDX_DIM = 64
IDX_ROT_DIM = IDX_DIM // 4
TOPK_MAX = 256
EPS = 1e-6
NEG = -1e30
Q_W = N_HEADS * HEAD_DIM
KV_W = N_KV_HEADS * HEAD_DIM
QI_W = IDX_HEADS * IDX_DIM
DSA_MAIN_W = Q_W + 2 * KV_W + QI_W
DSA_SMALL_W = IDX_DIM + IDX_HEADS
Q_PRESCALE = HEAD_DIM ** -0.5 * 1.4426950408889634

LANES = 128
SUBLANES = 8
VMEM_LIMIT_BYTES = 56 * 1024 * 1024
INT32_MIN = -(2 ** 31)
KEY_BITS = 32
SELECT_UNROLL = 4
N_ROPE_TABLES = 8


def _cparams(semantics):
    return pltpu.CompilerParams(dimension_semantics=semantics,
                                vmem_limit_bytes=VMEM_LIMIT_BYTES)


def _rmsnorm(x, g):
    ms = jnp.mean(x * x, axis=-1, keepdims=True)
    return x * lax.rsqrt(ms + EPS) * g


def _rmsnorm_rows(ref, rows, g_ref):
    ref[rows, :] = _rmsnorm(ref[rows, :], g_ref[...])


NORM_SLAB = 16
NORM_UNROLL = 8
FFN_ROW_SLAB = 128
FFN_SLAB_SLOTS = 4
FFN_DOWN_COLS = 256


def _for_row_slabs(n_rows, slab, body, unroll=1):
    def step(r, carry):
        body(pl.ds(pl.multiple_of(r * slab, slab), slab))
        return carry
    lax.fori_loop(0, n_rows // slab, step, 0, unroll=unroll)


def _write_norm_mod(x_ref, g_ref, m_ref, sub, y_ref, copy_ref=None):
    shift = m_ref[3 * sub + 0:3 * sub + 1, :]
    gs = g_ref[...] * (1.0 + m_ref[3 * sub + 1:3 * sub + 2, :])
    slab = min(NORM_SLAB, x_ref.shape[0])

    def norm_rows(rows):
        x = x_ref[rows, :]
        r = lax.rsqrt(jnp.mean(x * x, axis=-1, keepdims=True) + EPS)
        y_ref[rows, :] = ((x * r) * gs + shift).astype(BF16)
        if copy_ref is not None:
            copy_ref[rows, :] = x

    _for_row_slabs(x_ref.shape[0], slab, norm_rows, unroll=NORM_UNROLL)


def _silu(x):
    return x * jax.nn.sigmoid(x)


def _dot(a, b):
    return jnp.dot(a, b, preferred_element_type=F32)


def _dot_t(a, b):
    return lax.dot_general(a, b, (((1,), (1,)), ((), ())), preferred_element_type=F32)


def _ada_kernel(n_tab_blocks, c_ref, w_ref, b_ref, pos_ref, invf_ref, o_ref, tab_ref):
    sc = _silu(c_ref[...]).astype(BF16)
    o_ref[...] = _dot(sc, w_ref[...].astype(BF16)) + b_ref[...]
    step = pl.program_id(0) * pl.num_programs(1) + pl.program_id(1)

    @pl.when(step < n_tab_blocks)
    def _():
        _write_rope_tables(pos_ref, invf_ref, tab_ref)


def _ada(c8, ada_w, ada_b, tn, positions):
    depth, d, n = ada_w.shape
    rows = c8.shape[0]
    n_steps = depth * (n // tn)
    t = positions.size
    rt = SUBLANES
    while t // rt > n_steps or t % rt:
        rt *= 2
    n_tab = t // rt
    nj = n // tn

    def tab_block(l, j):
        return jnp.minimum(l * nj + j, n_tab - 1)

    return pl.pallas_call(
        functools.partial(_ada_kernel, n_tab),
        grid=(depth, nj),
        in_specs=[
            pl.BlockSpec((rows, d), lambda l, j: (0, 0)),
            pl.BlockSpec((None, d, tn), lambda l, j: (l, 0, j)),
            pl.BlockSpec((None, 1, tn), lambda l, j: (l, 0, j)),
            pl.BlockSpec((rt, 1), lambda l, j: (tab_block(l, j), 0)),
            pl.BlockSpec((1, LANES), lambda l, j: (0, 0)),
        ],
        out_specs=[
            pl.BlockSpec((None, rows, tn), lambda l, j: (l, 0, j)),
            pl.BlockSpec((N_ROPE_TABLES, rt, LANES), lambda l, j: (0, tab_block(l, j), 0)),
        ],
        out_shape=[
            jax.ShapeDtypeStruct((depth, rows, n), F32),
            jax.ShapeDtypeStruct((N_ROPE_TABLES, t, LANES), F32),
        ],
        compiler_params=_cparams(("arbitrary", "arbitrary")),
        name="ada",
    )(c8, ada_w, ada_b.reshape(depth, 1, n), positions.reshape(t, 1), _rope_inv_freq_row())


def _select_static(values, stage):
    out = values[-1]
    for n in range(len(values) - 2, -1, -1):
        out = jnp.where(stage == n, values[n], out)
    return out


def _ffn_kernel(subs, final, rs, nk, *refs):
    if final:
        (x_hbm, g_ref, m_ref, wg_ref, wu_ref, wd_ref, fg_ref, o_hbm,
         acc_ref, y_ref, xbuf, in_sem, out_sem) = refs
    else:
        (x_hbm, g_ref, m_ref, wg_ref, wu_ref, wd_ref, o_hbm,
         acc_ref, y_ref, xbuf, in_sem, out_sem) = refs
    k = pl.program_id(1)
    last_step = pl.num_programs(1) - 1
    sub = _select_static(subs, k // nk)
    tm, d_out = acc_ref.shape
    n_slabs = tm // rs
    n_slots = xbuf.shape[0]
    n_first = min(n_slots, n_slabs)
    row0 = pl.program_id(0) * tm

    def x_copy(s, tile_row0=row0):
        slot = s % n_slots
        return pltpu.make_async_copy(x_hbm.at[pl.ds(tile_row0 + s * rs, rs), :], xbuf.at[slot],
                                     in_sem.at[slot])

    def o_copy(s):
        return pltpu.make_async_copy(acc_ref.at[pl.ds(s * rs, rs), :],
                                     o_hbm.at[pl.ds(row0 + s * rs, rs), :], out_sem.at[s])

    @pl.when(k == 0)
    def _():
        @pl.when(pl.program_id(0) == 0)
        def _():
            for s in range(n_first):
                x_copy(s).start()

        for s in range(n_slabs):
            x_copy(s).wait()
            pl.when(pl.program_id(0) > 0)(o_copy(s).wait)
            rows = pl.ds(s * rs, rs)
            _write_norm_mod(xbuf.at[s % n_slots], g_ref.at[pl.ds(0, 1), :], m_ref, subs[0],
                            y_ref.at[rows, :], copy_ref=acc_ref.at[rows, :])
            if s + n_slots < n_slabs:
                x_copy(s + n_slots).start()

    for stage in range(1, len(subs)):
        @pl.when(k == stage * nk)
        def _(stage=stage):
            _write_norm_mod(acc_ref, g_ref.at[pl.ds(stage, 1), :], m_ref, subs[stage], y_ref)

    @pl.when(jnp.logical_and(k == last_step, pl.program_id(0) + 1 < pl.num_programs(0)))
    def _():
        for s in range(n_first):
            x_copy(s, row0 + tm).start()

    y = y_ref[...]
    g = _dot(y, wg_ref[...].astype(BF16))
    u = _dot(y, wu_ref[...].astype(BF16))
    a = (_silu(g) * u).astype(BF16)
    half_gate = 0.5 * m_ref[pl.ds(3 * sub + 2, 1), :]
    nc = min(FFN_DOWN_COLS, d_out)
    for n in range(d_out // nc):
        cols = slice(n * nc, (n + 1) * nc)
        acc_ref[:, cols] += half_gate[:, cols] * _dot(a, wd_ref[:, cols].astype(BF16))

    @pl.when(k == last_step)
    def _():
        for s in range(n_slabs):
            if final:
                _for_row_slabs(
                    rs, min(NORM_SLAB, rs),
                    lambda rows, s=s: _rmsnorm_rows(acc_ref.at[pl.ds(s * rs, rs), :], rows, fg_ref),
                    unroll=NORM_UNROLL)
            o_copy(s).start()

        @pl.when(pl.program_id(0) == pl.num_programs(0) - 1)
        def _():
            for s in range(n_slabs):
                o_copy(s).wait()


def _ffn(h, ada4, norm_g, ffn_wgu, ffn_wd, stages, seq, tm, tf, final_g=None):
    t, d = h.shape
    d_ff = ffn_wd.shape[2]
    nk = d_ff // tf
    tpb = seq // tm
    rs = min(FFN_ROW_SLAB, tm)
    final = final_g is not None
    layers, whichs, subs = zip(*stages)

    def layer_of(k):
        return _select_static(layers, k // nk)

    def which_of(k):
        return _select_static(whichs, k // nk)

    in_specs = [
        pl.BlockSpec(memory_space=pl.ANY),
        pl.BlockSpec((len(stages), d), lambda i, k: (0, 0)),
        pl.BlockSpec((None, None, 3 * N_SUBLAYERS, d), lambda i, k: (layer_of(k), i // tpb, 0, 0)),
        pl.BlockSpec((None, None, d, tf), lambda i, k: (layer_of(k), which_of(k), 0, k % nk)),
        pl.BlockSpec((None, None, d, tf), lambda i, k: (layer_of(k), which_of(k), 0, nk + k % nk)),
        pl.BlockSpec((None, None, tf, d), lambda i, k: (layer_of(k), which_of(k), k % nk, 0)),
    ]
    g_rows = jnp.stack([norm_g[layer, sub] for layer, _, sub in stages])
    args = [h, g_rows, ada4, ffn_wgu, ffn_wgu, ffn_wd]
    if final:
        in_specs.append(pl.BlockSpec((1, d), lambda i, k: (0, 0)))
        args.append(final_g[None, :])
    layer, which = layers[0], whichs[0]
    return pl.pallas_call(
        functools.partial(_ffn_kernel, subs, final, rs, nk),
        grid=(t // tm, len(stages) * nk),
        in_specs=in_specs,
        out_specs=pl.BlockSpec(memory_space=pl.ANY),
        out_shape=jax.ShapeDtypeStruct((t, d), F32),
        scratch_shapes=[
            pltpu.VMEM((tm, d), F32),
            pltpu.VMEM((tm, d), BF16),
            pltpu.VMEM((FFN_SLAB_SLOTS, rs, d), F32),
            pltpu.SemaphoreType.DMA((FFN_SLAB_SLOTS,)),
            pltpu.SemaphoreType.DMA((tm // rs,)),
        ],
        compiler_params=_cparams(("arbitrary", "arbitrary")),
        name=f"ffn_l{layer}_{which}",
    )(*args)


def _keep_bf16_weight(w_ref, wbf_ref):
    j = pl.program_id(1)

    @pl.when(pl.program_id(0) == 0)
    def _():
        wbf_ref[j] = w_ref[...].astype(BF16)

    return wbf_ref[j]


def _resident_weight_spec(widx, kdim, tn, nj):
    return pl.BlockSpec((None, kdim, tn), lambda i, j: (widx, 0, jnp.where(i == 0, j, nj - 1)))


def _norm_linear_kernel(sub, x_ref, g_ref, m_ref, w_ref, o_ref, y_ref, wbf_ref):
    @pl.when(pl.program_id(1) == 0)
    def _():
        _write_norm_mod(x_ref, g_ref, m_ref, sub, y_ref)

    o_ref[...] = _dot(y_ref[...], _keep_bf16_weight(w_ref, wbf_ref))


def _norm_linear(h, ada4, g_row, w, widx, layer, sub, seq, tm, tn):
    t, d = h.shape
    n = w.shape[2]
    tpb = seq // tm
    return pl.pallas_call(
        functools.partial(_norm_linear_kernel, sub),
        grid=(t // tm, n // tn),
        in_specs=[
            pl.BlockSpec((tm, d), lambda i, j: (i, 0)),
            pl.BlockSpec((1, d), lambda i, j: (0, 0)),
            pl.BlockSpec((None, None, 3 * N_SUBLAYERS, d), lambda i, j: (layer, i // tpb, 0, 0)),
            _resident_weight_spec(widx, d, tn, n // tn),
        ],
        out_specs=pl.BlockSpec((tm, tn), lambda i, j: (i, j)),
        out_shape=jax.ShapeDtypeStruct((t, n), F32),
        scratch_shapes=[pltpu.VMEM((tm, d), BF16), pltpu.VMEM((n // tn, d, tn), BF16)],
        compiler_params=_cparams(("arbitrary", "arbitrary")),
        name=f"norm_linear_l{layer}",
    )(h, g_row, ada4, w)


def _pool_kernel(u_ref, w_ref, ls_ref, o_ref):
    g = pl.program_id(1)
    u = u_ref[...]
    t = lax.broadcasted_iota(jnp.int32, u.shape, 0)

    def shifted(v, d):
        return jnp.where(t >= d, pltpu.roll(v, d, 0), 0.0)

    for gi, win in enumerate(POOL_WINDOWS):
        @pl.when(g == gi)
        def _(win=win):
            acc = u
            d = 1
            while d < win:
                acc = acc + shifted(acc, d)
                d *= 2
            cnt = jnp.minimum(t + 1, win).astype(F32)
            p = (acc / cnt - u).astype(BF16)
            v = _dot(p, w_ref[...].astype(BF16)) * ls_ref[...]
            o_ref[...] = v.astype(BF16)


def _pool_core(u, w_grp, ls, widx, batch, seq):
    t, d = u.shape
    _, groups, c, _ = w_grp.shape
    return pl.pallas_call(
        _pool_kernel,
        grid=(batch, groups),
        in_specs=[
            pl.BlockSpec((seq, c), lambda b, g: (b, g)),
            pl.BlockSpec((None, None, c, c), lambda b, g: (widx, g, 0, 0)),
            pl.BlockSpec((None, 1, c), lambda b, g: (widx, 0, g)),
        ],
        out_specs=pl.BlockSpec((seq, c), lambda b, g: (b, g)),
        out_shape=jax.ShapeDtypeStruct((t, d), BF16),
        compiler_params=_cparams(("arbitrary", "arbitrary")),
        name="pool_core",
    )(u, w_grp, ls.reshape(ls.shape[0], 1, d))


def _linear_residual_kernel(sub, a_ref, w_ref, h_ref, m_ref, o_ref, wbf_ref):
    gate = m_ref[3 * sub + 2:3 * sub + 3, :]
    o_ref[...] = h_ref[...] + gate * _dot(a_ref[...], _keep_bf16_weight(w_ref, wbf_ref))


def _linear_residual(a, w, widx, h, ada4, layer, sub, seq, tm, tn):
    t, kdim = a.shape
    n = w.shape[2]
    tpb = seq // tm
    return pl.pallas_call(
        functools.partial(_linear_residual_kernel, sub),
        grid=(t // tm, n // tn),
        in_specs=[
            pl.BlockSpec((tm, kdim), lambda i, j: (i, 0)),
            _resident_weight_spec(widx, kdim, tn, n // tn),
            pl.BlockSpec((tm, tn), lambda i, j: (i, j)),
            pl.BlockSpec((None, None, 3 * N_SUBLAYERS, tn), lambda i, j: (layer, i // tpb, 0, j)),
        ],
        out_specs=pl.BlockSpec((tm, tn), lambda i, j: (i, j)),
        out_shape=jax.ShapeDtypeStruct((t, n), F32),
        scratch_shapes=[pltpu.VMEM((n // tn, kdim, tn), BF16)],
        compiler_params=_cparams(("arbitrary", "arbitrary")),
        name=f"linear_residual_l{layer}",
    )(a, w, h, ada4)


ROPE_SET_Q, ROPE_SET_K, ROPE_SET_NONE, ROPE_SET_IDX = 0, 1, 2, 3
HALF_LANES = LANES // 2


def _write_rope_tables(pos_ref, invf_ref, o_ref):
    pos = pos_ref[...].astype(F32)
    lane = lax.broadcasted_iota(jnp.int32, (pos.shape[0], LANES), 1)
    ang = pos * invf_ref[...]
    cos = jnp.cos(ang)
    sin = jnp.where(lane < HALF_LANES, -jnp.sin(ang), jnp.sin(ang))
    rot_big = (lane & (HALF_LANES - 1)) < ROT_DIM // 2
    cb = jnp.where(rot_big, cos, 1.0)
    sb = jnp.where(rot_big, sin, 0.0)
    rot_idx = (lane & (HALF_LANES // 2 - 1)) < IDX_ROT_DIM // 2
    even_quarter = (lane & (HALF_LANES // 2)) == 0
    shift = ROT_DIM // 2

    def idx_table(v, rest):
        from_above = pltpu.roll(v, LANES - shift, 1)
        from_below = pltpu.roll(v, shift, 1)
        return jnp.where(rot_idx, jnp.where(even_quarter, from_above, from_below), rest)

    ci = idx_table(cos, 1.0)
    si = idx_table(sin, 0.0)
    o_ref[2 * ROPE_SET_Q] = cb * Q_PRESCALE
    o_ref[2 * ROPE_SET_Q + 1] = sb * Q_PRESCALE
    o_ref[2 * ROPE_SET_K] = cb
    o_ref[2 * ROPE_SET_K + 1] = sb
    o_ref[2 * ROPE_SET_NONE] = jnp.ones_like(cb)
    o_ref[2 * ROPE_SET_NONE + 1] = jnp.zeros_like(cb)
    o_ref[2 * ROPE_SET_IDX] = ci
    o_ref[2 * ROPE_SET_IDX + 1] = si


def _rope_inv_freq_row():
    def inv_freq(rot_dim):
        half = rot_dim // 2
        return ROPE_THETA ** (-jnp.arange(half, dtype=F32) / half)

    half_row = jnp.concatenate([
        inv_freq(ROT_DIM), inv_freq(IDX_ROT_DIM),
        jnp.zeros((HALF_LANES - ROT_DIM // 2 - IDX_ROT_DIM // 2,), F32)])
    return jnp.tile(half_row, 2)[None, :]


def _rope(x, c, s):
    return x * c + pltpu.roll(x, HALF_LANES, 1) * s


PERM_BIG, PERM_NONE, PERM_IDX = 0, 1, 2


def _lane_permutations():
    h, p = ROT_DIM // 2, (HEAD_DIM - ROT_DIM) // 2
    big = np.concatenate([np.arange(0, h), np.arange(ROT_DIM, ROT_DIM + p),
                          np.arange(h, ROT_DIM), np.arange(ROT_DIM + p, HEAD_DIM)])
    lo, hi = _idx_lo_hi_dims()
    idx = np.concatenate([lo, IDX_DIM + lo, hi, IDX_DIM + hi])
    mats = np.zeros((3, LANES, LANES), np.float32)
    for kind, src in ((PERM_BIG, big), (PERM_NONE, np.arange(LANES)), (PERM_IDX, idx)):
        mats[kind, np.arange(LANES), src] = 1.0
    return jnp.asarray(mats, BF16)


def _idx_lo_hi_dims():
    h, p = IDX_ROT_DIM // 2, (IDX_DIM - IDX_ROT_DIM) // 2
    lo = np.concatenate([np.arange(0, h), np.arange(IDX_ROT_DIM, IDX_ROT_DIM + p)])
    hi = np.concatenate([np.arange(h, IDX_ROT_DIM), np.arange(IDX_ROT_DIM + p, IDX_DIM)])
    return lo, hi


def _dsa_small_weights(w_tail):
    h, p = IDX_ROT_DIM // 2, (IDX_DIM - IDX_ROT_DIM) // 2
    klo = jnp.concatenate([w_tail[:, 0:h], w_tail[:, IDX_ROT_DIM:IDX_ROT_DIM + p]], axis=1)
    khi = jnp.concatenate([w_tail[:, h:IDX_ROT_DIM], w_tail[:, IDX_ROT_DIM + p:IDX_DIM]], axis=1)
    wi = jnp.pad(w_tail[:, IDX_DIM:], ((0, 0), (0, LANES - IDX_DIM - IDX_HEADS)))
    return jnp.concatenate([klo, khi, wi], axis=1).astype(BF16)


def _dsa_proj_kernel(sub, tn, x_hbm, g_ref, m_ref, w_ref, ws_ref, perm_ref, tab_ref,
                     o_ref, ok_ref, ow_ref, y_ref, wbf_ref, x_ref, x_sem):
    i = pl.program_id(0)
    j = pl.program_id(1)
    tm = x_ref.shape[0]
    first_k, first_v, first_qi = Q_W // tn, (Q_W + KV_W) // tn, (Q_W + 2 * KV_W) // tn

    def x_copy(tile):
        return pltpu.make_async_copy(x_hbm.at[pl.ds(tile * tm, tm), :], x_ref, x_sem.at[0])

    @pl.when(j == 0)
    def _():
        pl.when(i == 0)(x_copy(i).start)
        x_copy(i).wait()
        _write_norm_mod(x_ref, g_ref, m_ref, sub, y_ref)
        pl.when(i + 1 < pl.num_programs(0))(x_copy(i + 1).start)
        small = _dot(y_ref[...], ws_ref[...])
        by32 = pltpu.roll(small, HALF_LANES // 2, 1)
        by64 = pltpu.roll(small, HALF_LANES, 1)
        quarter = lax.broadcasted_iota(jnp.int32, small.shape, 1) // (HALF_LANES // 2)
        k_even = jnp.where(quarter == 0, small, jnp.where(quarter == 2, by32, 0.0))
        k_odd = jnp.where(quarter == 1, by32, jnp.where(quarter == 3, by64, 0.0))
        for s, kk in enumerate((k_even, k_odd)):
            ok_ref[:, s * LANES:(s + 1) * LANES] = _rope(
                kk, tab_ref[2 * ROPE_SET_IDX], tab_ref[2 * ROPE_SET_IDX + 1]).astype(BF16)
        ow_ref[...] = jnp.where(quarter == 0, by64, 0.0)

    @pl.when(pl.program_id(0) == 0)
    def _():
        kind = jnp.where(j < first_v, PERM_BIG, jnp.where(j < first_qi, PERM_NONE, PERM_IDX))
        perm = perm_ref[kind]
        for sl in range(tn // LANES):
            rows = slice(sl * LANES, (sl + 1) * LANES)
            wbf_ref[j, rows, :] = _dot(perm, w_ref[rows, :].astype(BF16)).astype(BF16)

    rope_set = jnp.where(j < first_k, ROPE_SET_Q,
                         jnp.where(j < first_v, ROPE_SET_K,
                                   jnp.where(j < first_qi, ROPE_SET_NONE, ROPE_SET_IDX)))
    c = tab_ref[2 * rope_set]
    s = tab_ref[2 * rope_set + 1]
    y = y_ref[...]
    wide = 2 * LANES
    for part in range(tn // wide):
        acc = _dot_t(y, wbf_ref[j, part * wide:(part + 1) * wide, :])
        for sl in range(wide // LANES):
            cols = slice(part * wide + sl * LANES, part * wide + (sl + 1) * LANES)
            o_ref[:, cols] = _rope(acc[:, sl * LANES:(sl + 1) * LANES], c, s).astype(BF16)


def _dsa_proj(h, ada4, g_row, w_in, widx, w_small, tabs, layer, sub, seq, tm, tn):
    t, d = h.shape
    tpb = seq // tm
    return pl.pallas_call(
        functools.partial(_dsa_proj_kernel, sub, tn),
        grid=(t // tm, DSA_MAIN_W // tn),
        in_specs=[
            pl.BlockSpec(memory_space=pl.ANY),
            pl.BlockSpec((1, d), lambda i, j: (0, 0)),
            pl.BlockSpec((None, None, 3 * N_SUBLAYERS, d), lambda i, j: (layer, i // tpb, 0, 0)),
            pl.BlockSpec((None, tn, d),
                         lambda i, j: (widx, jnp.where(i == 0, j, DSA_MAIN_W // tn - 1), 0)),
            pl.BlockSpec((d, LANES), lambda i, j: (0, 0)),
            pl.BlockSpec((3, LANES, LANES), lambda i, j: (0, 0, 0)),
            pl.BlockSpec((N_ROPE_TABLES, tm, LANES), lambda i, j: (0, i, 0)),
        ],
        out_specs=[
            pl.BlockSpec((tm, tn), lambda i, j: (i, j)),
            pl.BlockSpec((tm, 2 * LANES), lambda i, j: (i, 0)),
            pl.BlockSpec((tm, LANES), lambda i, j: (i, 0)),
        ],
        out_shape=[
            jax.ShapeDtypeStruct((t, DSA_MAIN_W), BF16),
            jax.ShapeDtypeStruct((t, 2 * LANES), BF16),
            jax.ShapeDtypeStruct((t, LANES), F32),
        ],
        scratch_shapes=[pltpu.VMEM((tm, d), BF16),
                        pltpu.VMEM((DSA_MAIN_W // tn, tn, d), BF16),
                        pltpu.VMEM((tm, d), F32),
                        pltpu.SemaphoreType.DMA((1,))],
        compiler_params=_cparams(("arbitrary", "arbitrary")),
        name=f"dsa_proj_l{layer}",
    )(h, g_row, ada4, w_in, w_small, _lane_permutations(), tabs)


def _dsa_attn_kernel(tq, tk, top_k, q_ref, qi_ref, wi_ref, k_ref, v_ref, kk_ref, o_ref,
                     score_ref, bias_ref, thr_ref, wib_ref, qg_ref, m_ref, acc_ref):
    i = pl.program_id(1)
    n_chunks = ((i + 1) * tq + tk - 1) // tk
    max_chunks = score_ref.shape[1] // tk
    t_pos = i * tq + lax.broadcasted_iota(jnp.int32, (tq, tk), 0)
    k_iota = lax.broadcasted_iota(jnp.int32, (tq, tk), 1)
    n_cb = tk // LANES

    for hh in range(IDX_HEADS):
        wib_ref[hh] = jnp.broadcast_to(wi_ref[:, hh:hh + 1], (tq, LANES))
    for g in range(N_KV_HEADS):
        for r in range(KV_GROUP):
            hd = g * KV_GROUP + r
            qg_ref[g, r * tq:(r + 1) * tq, :] = q_ref[:, hd * HEAD_DIM:(hd + 1) * HEAD_DIM]

    def score_chunk(c, carry):
        off = pl.multiple_of(c * tk, tk)
        k_even = kk_ref[pl.ds(off, tk), 0:LANES]
        k_odd = kk_ref[pl.ds(off, tk), LANES:2 * LANES]
        acc = [jnp.zeros((tq, LANES), F32) for _ in range(n_cb)]
        for pair in range(IDX_HEADS // 2):
            qp = qi_ref[:, pair * LANES:(pair + 1) * LANES]
            for par, kop in enumerate((k_even, k_odd)):
                r = _dot_t(qp, kop)
                w = wib_ref[2 * pair + par]
                for cb in range(n_cb):
                    acc[cb] = acc[cb] + w * jnp.maximum(r[:, cb * LANES:(cb + 1) * LANES], 0.0)
        sc = jnp.concatenate(acc, axis=1)
        score_ref[:, pl.ds(off, tk)] = jnp.where(k_iota + off <= t_pos, sc, NEG)
        return carry

    lax.fori_loop(0, n_chunks, score_chunk, 0)

    def key_to_f32(key):
        return pltpu.bitcast(key ^ ((key >> 31) & 0x7FFFFFFF), F32)

    k_f = float(top_k)

    def select(n_static):
        def bit_step(n, thr_key):
            cand = thr_key + lax.shift_left(jnp.int32(1), KEY_BITS - 1 - n)
            thr = key_to_f32(cand)
            cnt = jnp.zeros((tq, LANES), F32)
            for cb in range(n_static * n_cb):
                s = score_ref[:, cb * LANES:(cb + 1) * LANES]
                cnt = cnt + jnp.where(s >= thr, 1.0, 0.0)
            total = jnp.sum(cnt, axis=1, keepdims=True)
            return jnp.where(total >= k_f, cand, thr_key)

        thr_key = lax.fori_loop(0, KEY_BITS, bit_step,
                                jnp.full((tq, LANES), INT32_MIN, jnp.int32), unroll=SELECT_UNROLL)
        thr_ref[...] = key_to_f32(thr_key)

    for n_static in range(1, max_chunks + 1):
        pl.when(n_chunks == n_static)(functools.partial(select, n_static))

    def bias_chunk(c, cnt):
        off = pl.multiple_of(c * tk, tk)
        thr = thr_ref[...]
        causal = k_iota + off <= t_pos
        for cb in range(n_cb):
            cols = pl.ds(off + cb * LANES, LANES)
            sel = score_ref[:, cols] >= thr
            bias_ref[:, cols] = jnp.where(
                sel, jnp.where(causal[:, cb * LANES:(cb + 1) * LANES], 0.0, NEG), NEG)
            cnt = cnt + jnp.where(sel, 1.0, 0.0)
        return cnt

    n_ge = jnp.sum(lax.fori_loop(0, n_chunks, bias_chunk, jnp.zeros((tq, LANES), F32)),
                   axis=1, keepdims=True)

    @pl.when(jnp.max(n_ge) > k_f)
    def _():
        def count_greater(c, cnt):
            off = pl.multiple_of(c * tk, tk)
            thr = thr_ref[...]
            for cb in range(n_cb):
                s = score_ref[:, pl.ds(off + cb * LANES, LANES)]
                cnt = cnt + jnp.where(s > thr, 1.0, 0.0)
            return cnt

        n_greater = jnp.sum(lax.fori_loop(0, n_chunks, count_greater,
                                          jnp.zeros((tq, LANES), F32)), axis=1, keepdims=True)
        n_take = k_f - n_greater
        before = jnp.where(
            lax.broadcasted_iota(jnp.int32, (tk, tk), 0) < lax.broadcasted_iota(jnp.int32, (tk, tk), 1),
            1.0, 0.0).astype(BF16)

        def tie_chunk(c, seen):
            off = pl.multiple_of(c * tk, tk)
            sc = score_ref[:, pl.ds(off, tk)]
            thr = jnp.concatenate([thr_ref[...]] * n_cb, axis=1)
            eq = jnp.where(sc == thr, 1.0, 0.0)
            rank = _dot(eq.astype(BF16), before) + seen
            keep = jnp.where(sc > thr, 1.0, jnp.where(rank < n_take, eq, 0.0))
            causal = k_iota + off <= t_pos
            bias_ref[:, pl.ds(off, tk)] = jnp.where(causal, jnp.where(keep > 0.0, 0.0, NEG), NEG)
            return seen + jnp.sum(eq, axis=1, keepdims=True)

        lax.fori_loop(0, n_chunks, tie_chunk, jnp.zeros((tq, 1), F32))

    rows = KV_GROUP * tq
    ones = jnp.ones((tk, LANES), BF16)

    def attend(off, first):
        b = bias_ref[:, pl.ds(off, tk)]
        for g in range(N_KV_HEADS):
            kc = k_ref[pl.ds(off, tk), g * HEAD_DIM:(g + 1) * HEAD_DIM]
            vc = v_ref[pl.ds(off, tk), g * HEAD_DIM:(g + 1) * HEAD_DIM]
            s = _dot_t(qg_ref[g], kc)
            s = (s.reshape(KV_GROUP, tq, tk) + b[None]).reshape(rows, tk)
            m_new = jnp.broadcast_to(jnp.max(s, axis=1, keepdims=True), (rows, LANES))
            if not first:
                m_old = m_ref[g]
                m_new = jnp.maximum(m_old, m_new)
            p = jnp.exp2(s - jnp.concatenate([m_new] * n_cb, axis=1))
            pv = _dot(p.astype(BF16), jnp.concatenate([vc, ones], axis=1))
            if first:
                acc_ref[g] = pv
            else:
                alpha = jnp.exp2(m_old - m_new)
                acc_ref[g] = jnp.concatenate([alpha, alpha], axis=1) * acc_ref[g] + pv
            m_ref[g] = m_new

    attend(0, True)

    def attn_chunk(c, carry):
        attend(pl.multiple_of(c * tk, tk), False)
        return carry

    lax.fori_loop(1, n_chunks, attn_chunk, 0)

    for g in range(N_KV_HEADS):
        out = acc_ref[g, :, 0:HEAD_DIM] / acc_ref[g, :, HEAD_DIM:2 * HEAD_DIM]
        for r in range(KV_GROUP):
            hd = g * KV_GROUP + r
            o_ref[:, hd * HEAD_DIM:(hd + 1) * HEAD_DIM] = out[r * tq:(r + 1) * tq].astype(BF16)


def _dsa_attn(main, kk, wi, batch, seq, tq, tk, top_k):
    t = main.shape[0]
    nq = seq // tq
    rows = KV_GROUP * tq
    return pl.pallas_call(
        functools.partial(_dsa_attn_kernel, tq, tk, top_k),
        grid=(batch, nq),
        in_specs=[
            pl.BlockSpec((tq, Q_W), lambda b, i: (b * nq + i, 0)),
            pl.BlockSpec((tq, QI_W), lambda b, i: (b * nq + i, (Q_W + 2 * KV_W) // QI_W)),
            pl.BlockSpec((tq, LANES), lambda b, i: (b * nq + i, 0)),
            pl.BlockSpec((seq, KV_W), lambda b, i: (b, Q_W // KV_W)),
            pl.BlockSpec((seq, KV_W), lambda b, i: (b, Q_W // KV_W + 1)),
            pl.BlockSpec((seq, 2 * LANES), lambda b, i: (b, 0)),
        ],
        out_specs=pl.BlockSpec((tq, Q_W), lambda b, i: (b * nq + i, 0)),
        out_shape=jax.ShapeDtypeStruct((t, Q_W), BF16),
        scratch_shapes=[
            pltpu.VMEM((tq, seq), F32),
            pltpu.VMEM((tq, seq), F32),
            pltpu.VMEM((tq, LANES), F32),
            pltpu.VMEM((IDX_HEADS, tq, LANES), F32),
            pltpu.VMEM((N_KV_HEADS, rows, HEAD_DIM), BF16),
            pltpu.VMEM((N_KV_HEADS, rows, LANES), F32),
            pltpu.VMEM((N_KV_HEADS, rows, 2 * HEAD_DIM), F32),
        ],
        compiler_params=_cparams(("arbitrary", "arbitrary")),
        name="dsa_attn",
    )(main, main, wi, main, main, kk)


def _tiles(seq, d_ff):
    tm = min(1024, seq)
    tm_ffn = min(2048, seq)
    tf = 256 if d_ff % 256 == 0 else d_ff
    tq = min(256, seq)
    tk = min(256, seq)
    return tm, tm_ffn, tf, tq, tk


def kernel(x, c, positions, ada_w, ada_b, norm_g, final_g, ffn_wgu, ffn_wd,
           pool_w_in, pool_w_grp, pool_scale, pool_w_out, dsa_w_in, dsa_w_out):
    batch, seq, d = x.shape
    depth = ada_w.shape[0]
    d_ff = ffn_wd.shape[2]
    t = batch * seq
    tm, tm_ffn, tf, tq, tk = _tiles(seq, d_ff)
    tn = min(512, d)
    top_k = min(TOPK_MAX, seq // 4)
    assert seq % tm == 0 and seq % tm_ffn == 0 and seq % tq == 0 and seq % tk == 0 and tk >= top_k
    assert dsa_w_in.shape[2] == DSA_MAIN_W + DSA_SMALL_W and DSA_MAIN_W % tn == 0

    rows = -(-batch // SUBLANES) * SUBLANES
    c8 = jnp.pad(c, ((0, rows - batch), (0, 0)))
    ada_n = ada_w.shape[2]
    ada, tabs = _ada(c8, ada_w, ada_b, 1024 if ada_n % 1024 == 0 else d, positions)
    ada4 = ada.reshape(depth, rows, 3 * N_SUBLAYERS, d)

    h = x.reshape(t, d)
    h = _ffn(h, ada4, norm_g, ffn_wgu, ffn_wd, [(0, 0, 0)], seq, tm_ffn, tf)
    for layer in range(depth):
        j = layer // 2
        g_row = norm_g[layer, 1][None, :]
        if layer % 2 == 0:
            u = _norm_linear(h, ada4, g_row, pool_w_in, j, layer, 1, seq, tm, tn)
            a = _pool_core(u, pool_w_grp, pool_scale, j, batch, seq)
            h = _linear_residual(a, pool_w_out, j, h, ada4, layer, 1, seq, tm, tn)
        else:
            w_small = _dsa_small_weights(dsa_w_in[j, :, DSA_MAIN_W:])
            main, kk, wi = _dsa_proj(h, ada4, g_row, jnp.swapaxes(dsa_w_in, 1, 2), j, w_small,
                                     tabs, layer, 1, seq, tm, tn)
            o = _dsa_attn(main, kk, wi, batch, seq, tq, tk, top_k)
            h = _linear_residual(o, dsa_w_out, j, h, ada4, layer, 1, seq, tm, tn)
        if layer == depth - 1:
            h = _ffn(h, ada4, norm_g, ffn_wgu, ffn_wd, [(layer, 1, 2)], seq, tm_ffn, tf,
                     final_g=final_g)
        else:
            h = _ffn(h, ada4, norm_g, ffn_wgu, ffn_wd, [(layer, 1, 2), (layer + 1, 0, 0)],
                     seq, tm_ffn, tf)
    return h.reshape(batch, seq, d)
```

```python
import functools

import jax
import jax.numpy as jnp
import numpy as np
from jax import lax
from jax.experimental import pallas as pl
from jax.experimental.pallas import tpu as pltpu

F32 = jnp.float32
BF16 = jnp.bfloat16

N_SUBLAYERS = 3
POOL_WINDOWS = (2, 4, 8, 16)
N_HEADS = 16
N_KV_HEADS = 4
KV_GROUP = N_HEADS // N_KV_HEADS
HEAD_DIM = 128
ROT_DIM = HEAD_DIM // 4
ROPE_THETA = 500000.0
IDX_HEADS = 16
IDX_DIM = 64
IDX_ROT_DIM = IDX_DIM // 4
TOPK_MAX = 256
EPS = 1e-6
NEG = -1e30
Q_W = N_HEADS * HEAD_DIM
KV_W = N_KV_HEADS * HEAD_DIM
QI_W = IDX_HEADS * IDX_DIM
DSA_MAIN_W = Q_W + 2 * KV_W + QI_W
DSA_SMALL_W = IDX_DIM + IDX_HEADS
Q_PRESCALE = HEAD_DIM ** -0.5 * 1.4426950408889634

LANES = 128
SUBLANES = 8
VMEM_LIMIT_BYTES = 56 * 1024 * 1024
INT32_MIN = -(2 ** 31)
KEY_BITS = 32
SELECT_UNROLL = 4
N_ROPE_TABLES = 8


def _cparams(semantics):
    return pltpu.CompilerParams(dimension_semantics=semantics,
                                vmem_limit_bytes=VMEM_LIMIT_BYTES)


def _rmsnorm(x, g):
    ms = jnp.mean(x * x, axis=-1, keepdims=True)
    return x * lax.rsqrt(ms + EPS) * g


def _rmsnorm_rows(ref, rows, g_ref):
    ref[rows, :] = _rmsnorm(ref[rows, :], g_ref[...])


NORM_SLAB = 16
NORM_UNROLL = 8
FFN_ROW_SLAB = 128
FFN_SLAB_SLOTS = 4
FFN_DOWN_COLS = 256


def _for_row_slabs(n_rows, slab, body, unroll=1):
    def step(r, carry):
        body(pl.ds(pl.multiple_of(r * slab, slab), slab))
        return carry
    lax.fori_loop(0, n_rows // slab, step, 0, unroll=unroll)


def _write_norm_mod(x_ref, g_ref, m_ref, sub, y_ref, copy_ref=None):
    shift = m_ref[3 * sub + 0:3 * sub + 1, :]
    gs = g_ref[...] * (1.0 + m_ref[3 * sub + 1:3 * sub + 2, :])
    slab = min(NORM_SLAB, x_ref.shape[0])

    def norm_rows(rows):
        x = x_ref[rows, :]
        r = lax.rsqrt(jnp.mean(x * x, axis=-1, keepdims=True) + EPS)
        y_ref[rows, :] = ((x * r) * gs + shift).astype(BF16)
        if copy_ref is not None:
            copy_ref[rows, :] = x

    _for_row_slabs(x_ref.shape[0], slab, norm_rows, unroll=NORM_UNROLL)


def _silu(x):
    return x * jax.nn.sigmoid(x)


def _dot(a, b):
    return jnp.dot(a, b, preferred_element_type=F32)


def _dot_t(a, b):
    return lax.dot_general(a, b, (((1,), (1,)), ((), ())), preferred_element_type=F32)


def _ada_kernel(n_tab_blocks, c_ref, w_ref, b_ref, pos_ref, invf_ref, o_ref, tab_ref):
    sc = _silu(c_ref[...]).astype(BF16)
    o_ref[...] = _dot(sc, w_ref[...].astype(BF16)) + b_ref[...]
    step = pl.program_id(0) * pl.num_programs(1) + pl.program_id(1)

    @pl.when(step < n_tab_blocks)
    def _():
        _write_rope_tables(pos_ref, invf_ref, tab_ref)


def _ada(c8, ada_w, ada_b, tn, positions):
    depth, d, n = ada_w.shape
    rows = c8.shape[0]
    n_steps = depth * (n // tn)
    t = positions.size
    rt = SUBLANES
    while t // rt > n_steps or t % rt:
        rt *= 2
    n_tab = t // rt
    nj = n // tn

    def tab_block(l, j):
        return jnp.minimum(l * nj + j, n_tab - 1)

    return pl.pallas_call(
        functools.partial(_ada_kernel, n_tab),
        grid=(depth, nj),
        in_specs=[
            pl.BlockSpec((rows, d), lambda l, j: (0, 0)),
            pl.BlockSpec((None, d, tn), lambda l, j: (l, 0, j)),
            pl.BlockSpec((None, 1, tn), lambda l, j: (l, 0, j)),
            pl.BlockSpec((rt, 1), lambda l, j: (tab_block(l, j), 0)),
            pl.BlockSpec((1, LANES), lambda l, j: (0, 0)),
        ],
        out_specs=[
            pl.BlockSpec((None, rows, tn), lambda l, j: (l, 0, j)),
            pl.BlockSpec((N_ROPE_TABLES, rt, LANES), lambda l, j: (0, tab_block(l, j), 0)),
        ],
        out_shape=[
            jax.ShapeDtypeStruct((depth, rows, n), F32),
            jax.ShapeDtypeStruct((N_ROPE_TABLES, t, LANES), F32),
        ],
        compiler_params=_cparams(("arbitrary", "arbitrary")),
        name="ada",
    )(c8, ada_w, ada_b.reshape(depth, 1, n), positions.reshape(t, 1), _rope_inv_freq_row())


def _select_static(values, stage):
    out = values[-1]
    for n in range(len(values) - 2, -1, -1):
        out = jnp.where(stage == n, values[n], out)
    return out


def _ffn_kernel(subs, final, rs, nk, *refs):
    if final:
        (x_hbm, g_ref, m_ref, wg_ref, wu_ref, wd_ref, fg_ref, o_hbm,
         acc_ref, y_ref, xbuf, in_sem, out_sem) = refs
    else:
        (x_hbm, g_ref, m_ref, wg_ref, wu_ref, wd_ref, o_hbm,
         acc_ref, y_ref, xbuf, in_sem, out_sem) = refs
    k = pl.program_id(1)
    last_step = pl.num_programs(1) - 1
    sub = _select_static(subs, k // nk)
    tm, d_out = acc_ref.shape
    n_slabs = tm // rs
    n_slots = xbuf.shape[0]
    n_first = min(n_slots, n_slabs)
    row0 = pl.program_id(0) * tm

    def x_copy(s, tile_row0=row0):
        slot = s % n_slots
        return pltpu.make_async_copy(x_hbm.at[pl.ds(tile_row0 + s * rs, rs), :], xbuf.at[slot],
                                     in_sem.at[slot])

    def o_copy(s):
        return pltpu.make_async_copy(acc_ref.at[pl.ds(s * rs, rs), :],
                                     o_hbm.at[pl.ds(row0 + s * rs, rs), :], out_sem.at[s])

    @pl.when(k == 0)
    def _():
        @pl.when(pl.program_id(0) == 0)
        def _():
            for s in range(n_first):
                x_copy(s).start()

        for s in range(n_slabs):
            x_copy(s).wait()
            pl.when(pl.program_id(0) > 0)(o_copy(s).wait)
            rows = pl.ds(s * rs, rs)
            _write_norm_mod(xbuf.at[s % n_slots], g_ref.at[pl.ds(0, 1), :], m_ref, subs[0],
                            y_ref.at[rows, :], copy_ref=acc_ref.at[rows, :])
            if s + n_slots < n_slabs:
                x_copy(s + n_slots).start()

    for stage in range(1, len(subs)):
        @pl.when(k == stage * nk)
        def _(stage=stage):
            _write_norm_mod(acc_ref, g_ref.at[pl.ds(stage, 1), :], m_ref, subs[stage], y_ref)

    @pl.when(jnp.logical_and(k == last_step, pl.program_id(0) + 1 < pl.num_programs(0)))
    def _():
        for s in range(n_first):
            x_copy(s, row0 + tm).start()

    y = y_ref[...]
    g = _dot(y, wg_ref[...].astype(BF16))
    u = _dot(y, wu_ref[...].astype(BF16))
    a = (_silu(g) * u).astype(BF16)
    half_gate = 0.5 * m_ref[pl.ds(3 * sub + 2, 1), :]
    nc = min(FFN_DOWN_COLS, d_out)
    for n in range(d_out // nc):
        cols = slice(n * nc, (n + 1) * nc)
        acc_ref[:, cols] += half_gate[:, cols] * _dot(a, wd_ref[:, cols].astype(BF16))

    @pl.when(k == last_step)
    def _():
        for s in range(n_slabs):
            if final:
                _for_row_slabs(
                    rs, min(NORM_SLAB, rs),
                    lambda rows, s=s: _rmsnorm_rows(acc_ref.at[pl.ds(s * rs, rs), :], rows, fg_ref),
                    unroll=NORM_UNROLL)
            o_copy(s).start()

        @pl.when(pl.program_id(0) == pl.num_programs(0) - 1)
        def _():
            for s in range(n_slabs):
                o_copy(s).wait()


def _ffn(h, ada4, norm_g, ffn_wgu, ffn_wd, stages, seq, tm, tf, final_g=None):
    t, d = h.shape
    d_ff = ffn_wd.shape[2]
    nk = d_ff // tf
    tpb = seq // tm
    rs = min(FFN_ROW_SLAB, tm)
    final = final_g is not None
    layers, whichs, subs = zip(*stages)

    def layer_of(k):
        return _select_static(layers, k // nk)

    def which_of(k):
        return _select_static(whichs, k // nk)

    in_specs = [
        pl.BlockSpec(memory_space=pl.ANY),
        pl.BlockSpec((len(stages), d), lambda i, k: (0, 0)),
        pl.BlockSpec((None, None, 3 * N_SUBLAYERS, d), lambda i, k: (layer_of(k), i // tpb, 0, 0)),
        pl.BlockSpec((None, None, d, tf), lambda i, k: (layer_of(k), which_of(k), 0, k % nk)),
        pl.BlockSpec((None, None, d, tf), lambda i, k: (layer_of(k), which_of(k), 0, nk + k % nk)),
        pl.BlockSpec((None, None, tf, d), lambda i, k: (layer_of(k), which_of(k), k % nk, 0)),
    ]
    g_rows = jnp.stack([norm_g[layer, sub] for layer, _, sub in stages])
    args = [h, g_rows, ada4, ffn_wgu, ffn_wgu, ffn_wd]
    if final:
        in_specs.append(pl.BlockSpec((1, d), lambda i, k: (0, 0)))
        args.append(final_g[None, :])
    layer, which = layers[0], whichs[0]
    return pl.pallas_call(
        functools.partial(_ffn_kernel, subs, final, rs, nk),
        grid=(t // tm, len(stages) * nk),
        in_specs=in_specs,
        out_specs=pl.BlockSpec(memory_space=pl.ANY),
        out_shape=jax.ShapeDtypeStruct((t, d), F32),
        scratch_shapes=[
            pltpu.VMEM((tm, d), F32),
            pltpu.VMEM((tm, d), BF16),
            pltpu.VMEM((FFN_SLAB_SLOTS, rs, d), F32),
            pltpu.SemaphoreType.DMA((FFN_SLAB_SLOTS,)),
            pltpu.SemaphoreType.DMA((tm // rs,)),
        ],
        compiler_params=_cparams(("arbitrary", "arbitrary")),
        name=f"ffn_l{layer}_{which}",
    )(*args)


def _keep_bf16_weight(w_ref, wbf_ref):
    j = pl.program_id(1)

    @pl.when(pl.program_id(0) == 0)
    def _():
        wbf_ref[j] = w_ref[...].astype(BF16)

    return wbf_ref[j]


def _resident_weight_spec(widx, kdim, tn, nj):
    return pl.BlockSpec((None, kdim, tn), lambda i, j: (widx, 0, jnp.where(i == 0, j, nj - 1)))


def _norm_linear_kernel(sub, x_ref, g_ref, m_ref, w_ref, o_ref, y_ref, wbf_ref):
    @pl.when(pl.program_id(1) == 0)
    def _():
        _write_norm_mod(x_ref, g_ref, m_ref, sub, y_ref)

    o_ref[...] = _dot(y_ref[...], _keep_bf16_weight(w_ref, wbf_ref))


def _norm_linear(h, ada4, g_row, w, widx, layer, sub, seq, tm, tn):
    t, d = h.shape
    n = w.shape[2]
    tpb = seq // tm
    return pl.pallas_call(
        functools.partial(_norm_linear_kernel, sub),
        grid=(t // tm, n // tn),
        in_specs=[
            pl.BlockSpec((tm, d), lambda i, j: (i, 0)),
            pl.BlockSpec((1, d), lambda i, j: (0, 0)),
            pl.BlockSpec((None, None, 3 * N_SUBLAYERS, d), lambda i, j: (layer, i // tpb, 0, 0)),
            _resident_weight_spec(widx, d, tn, n // tn),
        ],
        out_specs=pl.BlockSpec((tm, tn), lambda i, j: (i, j)),
        out_shape=jax.ShapeDtypeStruct((t, n), F32),
        scratch_shapes=[pltpu.VMEM((tm, d), BF16), pltpu.VMEM((n // tn, d, tn), BF16)],
        compiler_params=_cparams(("arbitrary", "arbitrary")),
        name=f"norm_linear_l{layer}",
    )(h, g_row, ada4, w)


def _pool_kernel(u_ref, w_ref, ls_ref, o_ref):
    g = pl.program_id(1)
    u = u_ref[...]
    t = lax.broadcasted_iota(jnp.int32, u.shape, 0)

    def shifted(v, d):
        return jnp.where(t >= d, pltpu.roll(v, d, 0), 0.0)

    for gi, win in enumerate(POOL_WINDOWS):
        @pl.when(g == gi)
        def _(win=win):
            acc = u
            d = 1
            while d < win:
                acc = acc + shifted(acc, d)
                d *= 2
            cnt = jnp.minimum(t + 1, win).astype(F32)
            p = (acc / cnt - u).astype(BF16)
            v = _dot(p, w_ref[...].astype(BF16)) * ls_ref[...]
            o_ref[...] = v.astype(BF16)


def _pool_core(u, w_grp, ls, widx, batch, seq):
    t, d = u.shape
    _, groups, c, _ = w_grp.shape
    return pl.pallas_call(
        _pool_kernel,
        grid=(batch, groups),
        in_specs=[
            pl.BlockSpec((seq, c), lambda b, g: (b, g)),
            pl.BlockSpec((None, None, c, c), lambda b, g: (widx, g, 0, 0)),
            pl.BlockSpec((None, 1, c), lambda b, g: (widx, 0, g)),
        ],
        out_specs=pl.BlockSpec((seq, c), lambda b, g: (b, g)),
        out_shape=jax.ShapeDtypeStruct((t, d), BF16),
        compiler_params=_cparams(("arbitrary", "arbitrary")),
        name="pool_core",
    )(u, w_grp, ls.reshape(ls.shape[0], 1, d))


def _linear_residual_kernel(sub, a_ref, w_ref, h_ref, m_ref, o_ref, wbf_ref):
    gate = m_ref[3 * sub + 2:3 * sub + 3, :]
    o_ref[...] = h_ref[...] + gate * _dot(a_ref[...], _keep_bf16_weight(w_ref, wbf_ref))


def _linear_residual(a, w, widx, h, ada4, layer, sub, seq, tm, tn):
    t, kdim = a.shape
    n = w.shape[2]
    tpb = seq // tm
    return pl.pallas_call(
        functools.partial(_linear_residual_kernel, sub),
        grid=(t // tm, n // tn),
        in_specs=[
            pl.BlockSpec((tm, kdim), lambda i, j: (i, 0)),
            _resident_weight_spec(widx, kdim, tn, n // tn),
            pl.BlockSpec((tm, tn), lambda i, j: (i, j)),
            pl.BlockSpec((None, None, 3 * N_SUBLAYERS, tn), lambda i, j: (layer, i // tpb, 0, j)),
        ],
        out_specs=pl.BlockSpec((tm, tn), lambda i, j: (i, j)),
        out_shape=jax.ShapeDtypeStruct((t, n), F32),
        scratch_shapes=[pltpu.VMEM((n // tn, kdim, tn), BF16)],
        compiler_params=_cparams(("arbitrary", "arbitrary")),
        name=f"linear_residual_l{layer}",
    )(a, w, h, ada4)


ROPE_SET_Q, ROPE_SET_K, ROPE_SET_NONE, ROPE_SET_IDX = 0, 1, 2, 3
HALF_LANES = LANES // 2


def _write_rope_tables(pos_ref, invf_ref, o_ref):
    pos = pos_ref[...].astype(F32)
    lane = lax.broadcasted_iota(jnp.int32, (pos.shape[0], LANES), 1)
    ang = pos * invf_ref[...]
    cos = jnp.cos(ang)
    sin = jnp.where(lane < HALF_LANES, -jnp.sin(ang), jnp.sin(ang))
    rot_big = (lane & (HALF_LANES - 1)) < ROT_DIM // 2
    cb = jnp.where(rot_big, cos, 1.0)
    sb = jnp.where(rot_big, sin, 0.0)
    rot_idx = (lane & (HALF_LANES // 2 - 1)) < IDX_ROT_DIM // 2
    even_quarter = (lane & (HALF_LANES // 2)) == 0
    shift = ROT_DIM // 2

    def idx_table(v, rest):
        from_above = pltpu.roll(v, LANES - shift, 1)
        from_below = pltpu.roll(v, shift, 1)
        return jnp.where(rot_idx, jnp.where(even_quarter, from_above, from_below), rest)

    ci = idx_table(cos, 1.0)
    si = idx_table(sin, 0.0)
    o_ref[2 * ROPE_SET_Q] = cb * Q_PRESCALE
    o_ref[2 * ROPE_SET_Q + 1] = sb * Q_PRESCALE
    o_ref[2 * ROPE_SET_K] = cb
    o_ref[2 * ROPE_SET_K + 1] = sb
    o_ref[2 * ROPE_SET_NONE] = jnp.ones_like(cb)
    o_ref[2 * ROPE_SET_NONE + 1] = jnp.zeros_like(cb)
    o_ref[2 * ROPE_SET_IDX] = ci
    o_ref[2 * ROPE_SET_IDX + 1] = si


def _rope_inv_freq_row():
    def inv_freq(rot_dim):
        half = rot_dim // 2
        return ROPE_THETA ** (-jnp.arange(half, dtype=F32) / half)

    half_row = jnp.concatenate([
        inv_freq(ROT_DIM), inv_freq(IDX_ROT_DIM),
        jnp.zeros((HALF_LANES - ROT_DIM // 2 - IDX_ROT_DIM // 2,), F32)])
    return jnp.tile(half_row, 2)[None, :]


def _rope(x, c, s):
    return x * c + pltpu.roll(x, HALF_LANES, 1) * s


PERM_BIG, PERM_NONE, PERM_IDX = 0, 1, 2


def _lane_permutations():
    h, p = ROT_DIM // 2, (HEAD_DIM - ROT_DIM) // 2
    big = np.concatenate([np.arange(0, h), np.arange(ROT_DIM, ROT_DIM + p),
                          np.arange(h, ROT_DIM), np.arange(ROT_DIM + p, HEAD_DIM)])
    lo, hi = _idx_lo_hi_dims()
    idx = np.concatenate([lo, IDX_DIM + lo, hi, IDX_DIM + hi])
    mats = np.zeros((3, LANES, LANES), np.float32)
    for kind, src in ((PERM_BIG, big), (PERM_NONE, np.arange(LANES)), (PERM_IDX, idx)):
        mats[kind, np.arange(LANES), src] = 1.0
    return jnp.asarray(mats, BF16)


def _idx_lo_hi_dims():
    h, p = IDX_ROT_DIM // 2, (IDX_DIM - IDX_ROT_DIM) // 2
    lo = np.concatenate([np.arange(0, h), np.arange(IDX_ROT_DIM, IDX_ROT_DIM + p)])
    hi = np.concatenate([np.arange(h, IDX_ROT_DIM), np.arange(IDX_ROT_DIM + p, IDX_DIM)])
    return lo, hi


def _dsa_small_weights(w_tail):
    h, p = IDX_ROT_DIM // 2, (IDX_DIM - IDX_ROT_DIM) // 2
    klo = jnp.concatenate([w_tail[:, 0:h], w_tail[:, IDX_ROT_DIM:IDX_ROT_DIM + p]], axis=1)
    khi = jnp.concatenate([w_tail[:, h:IDX_ROT_DIM], w_tail[:, IDX_ROT_DIM + p:IDX_DIM]], axis=1)
    wi = jnp.pad(w_tail[:, IDX_DIM:], ((0, 0), (0, LANES - IDX_DIM - IDX_HEADS)))
    return jnp.concatenate([klo, khi, wi], axis=1).astype(BF16)


def _dsa_proj_kernel(sub, tn, x_hbm, g_ref, m_ref, w_ref, ws_ref, perm_ref, tab_ref,
                     o_ref, ok_ref, ow_ref, y_ref, wbf_ref, x_ref, x_sem):
    i = pl.program_id(0)
    j = pl.program_id(1)
    tm = x_ref.shape[0]
    first_k, first_v, first_qi = Q_W // tn, (Q_W + KV_W) // tn, (Q_W + 2 * KV_W) // tn

    def x_copy(tile):
        return pltpu.make_async_copy(x_hbm.at[pl.ds(tile * tm, tm), :], x_ref, x_sem.at[0])

    @pl.when(j == 0)
    def _():
        pl.when(i == 0)(x_copy(i).start)
        x_copy(i).wait()
        _write_norm_mod(x_ref, g_ref, m_ref, sub, y_ref)
        pl.when(i + 1 < pl.num_programs(0))(x_copy(i + 1).start)
        small = _dot(y_ref[...], ws_ref[...])
        by32 = pltpu.roll(small, HALF_LANES // 2, 1)
        by64 = pltpu.roll(small, HALF_LANES, 1)
        quarter = lax.broadcasted_iota(jnp.int32, small.shape, 1) // (HALF_LANES // 2)
        k_even = jnp.where(quarter == 0, small, jnp.where(quarter == 2, by32, 0.0))
        k_odd = jnp.where(quarter == 1, by32, jnp.where(quarter == 3, by64, 0.0))
        for s, kk in enumerate((k_even, k_odd)):
            ok_ref[:, s * LANES:(s + 1) * LANES] = _rope(
                kk, tab_ref[2 * ROPE_SET_IDX], tab_ref[2 * ROPE_SET_IDX + 1]).astype(BF16)
        ow_ref[...] = jnp.where(quarter == 0, by64, 0.0)

    @pl.when(pl.program_id(0) == 0)
    def _():
        kind = jnp.where(j < first_v, PERM_BIG, jnp.where(j < first_qi, PERM_NONE, PERM_IDX))
        perm = perm_ref[kind]
        for sl in range(tn // LANES):
            rows = slice(sl * LANES, (sl + 1) * LANES)
            wbf_ref[j, rows, :] = _dot(perm, w_ref[rows, :].astype(BF16)).astype(BF16)

    rope_set = jnp.where(j < first_k, ROPE_SET_Q,
                         jnp.where(j < first_v, ROPE_SET_K,
                                   jnp.where(j < first_qi, ROPE_SET_NONE, ROPE_SET_IDX)))
    c = tab_ref[2 * rope_set]
    s = tab_ref[2 * rope_set + 1]
    y = y_ref[...]
    wide = 2 * LANES
    for part in range(tn // wide):
        acc = _dot_t(y, wbf_ref[j, part * wide:(part + 1) * wide, :])
        for sl in range(wide // LANES):
            cols = slice(part * wide + sl * LANES, part * wide + (sl + 1) * LANES)
            o_ref[:, cols] = _rope(acc[:, sl * LANES:(sl + 1) * LANES], c, s).astype(BF16)


def _dsa_proj(h, ada4, g_row, w_in, widx, w_small, tabs, layer, sub, seq, tm, tn):
    t, d = h.shape
    tpb = seq // tm
    return pl.pallas_call(
        functools.partial(_dsa_proj_kernel, sub, tn),
        grid=(t // tm, DSA_MAIN_W // tn),
        in_specs=[
            pl.BlockSpec(memory_space=pl.ANY),
            pl.BlockSpec((1, d), lambda i, j: (0, 0)),
            pl.BlockSpec((None, None, 3 * N_SUBLAYERS, d), lambda i, j: (layer, i // tpb, 0, 0)),
            pl.BlockSpec((None, tn, d),
                         lambda i, j: (widx, jnp.where(i == 0, j, DSA_MAIN_W // tn - 1), 0)),
            pl.BlockSpec((d, LANES), lambda i, j: (0, 0)),
            pl.BlockSpec((3, LANES, LANES), lambda i, j: (0, 0, 0)),
            pl.BlockSpec((N_ROPE_TABLES, tm, LANES), lambda i, j: (0, i, 0)),
        ],
        out_specs=[
            pl.BlockSpec((tm, tn), lambda i, j: (i, j)),
            pl.BlockSpec((tm, 2 * LANES), lambda i, j: (i, 0)),
            pl.BlockSpec((tm, LANES), lambda i, j: (i, 0)),
        ],
        out_shape=[
            jax.ShapeDtypeStruct((t, DSA_MAIN_W), BF16),
            jax.ShapeDtypeStruct((t, 2 * LANES), BF16),
            jax.ShapeDtypeStruct((t, LANES), F32),
        ],
        scratch_shapes=[pltpu.VMEM((tm, d), BF16),
                        pltpu.VMEM((DSA_MAIN_W // tn, tn, d), BF16),
                        pltpu.VMEM((tm, d), F32),
                        pltpu.SemaphoreType.DMA((1,))],
        compiler_params=_cparams(("arbitrary", "arbitrary")),
        name=f"dsa_proj_l{layer}",
    )(h, g_row, ada4, w_in, w_small, _lane_permutations(), tabs)


def _dsa_attn_kernel(tq, tk, top_k, sub, widx, q_ref, qi_ref, wi_ref, k_ref, v_ref, kk_ref,
                     h_ref, ada_ref, wo_ref, o_ref,
                     score_ref, bias_ref, thr_ref, wib_ref, qg_ref, m_ref, acc_ref,
                     attn_ref, wobf_ref, wo_stage, wo_sem):
    i = pl.program_id(1)

    @pl.when(jnp.logical_and(pl.program_id(0) == 0, i == 0))
    def _():
        slab = wo_stage.shape[1]
        n_slabs = wobf_ref.shape[0] // slab

        def wo_copy(r):
            return pltpu.make_async_copy(wo_ref.at[widx, pl.ds(r * slab, slab), :],
                                         wo_stage.at[r % 2], wo_sem.at[r % 2])

        wo_copy(0).start()
        for r in range(n_slabs):
            if r + 1 < n_slabs:
                wo_copy(r + 1).start()
            wo_copy(r).wait()
            wobf_ref[r * slab:(r + 1) * slab, :] = wo_stage[r % 2].astype(BF16)

    n_chunks = ((i + 1) * tq + tk - 1) // tk
    max_chunks = score_ref.shape[1] // tk
    t_pos = i * tq + lax.broadcasted_iota(jnp.int32, (tq, tk), 0)
    k_iota = lax.broadcasted_iota(jnp.int32, (tq, tk), 1)
    n_cb = tk // LANES

    for hh in range(IDX_HEADS):
        wib_ref[hh] = jnp.broadcast_to(wi_ref[:, hh:hh + 1], (tq, LANES))
    for g in range(N_KV_HEADS):
        for r in range(KV_GROUP):
            hd = g * KV_GROUP + r
            qg_ref[g, r * tq:(r + 1) * tq, :] = q_ref[:, hd * HEAD_DIM:(hd + 1) * HEAD_DIM]

    def score_chunk(c, carry):
        off = pl.multiple_of(c * tk, tk)
        k_even = kk_ref[pl.ds(off, tk), 0:LANES]
        k_odd = kk_ref[pl.ds(off, tk), LANES:2 * LANES]
        acc = [jnp.zeros((tq, LANES), F32) for _ in range(n_cb)]
        for pair in range(IDX_HEADS // 2):
            qp = qi_ref[:, pair * LANES:(pair + 1) * LANES]
            for par, kop in enumerate((k_even, k_odd)):
                r = _dot_t(qp, kop)
                w = wib_ref[2 * pair + par]
                for cb in range(n_cb):
                    acc[cb] = acc[cb] + w * jnp.maximum(r[:, cb * LANES:(cb + 1) * LANES], 0.0)
        sc = jnp.concatenate(acc, axis=1)
        score_ref[:, pl.ds(off, tk)] = jnp.where(k_iota + off <= t_pos, sc, NEG)
        return carry

    lax.fori_loop(0, n_chunks, score_chunk, 0)

    def key_to_f32(key):
        return pltpu.bitcast(key ^ ((key >> 31) & 0x7FFFFFFF), F32)

    k_f = float(top_k)

    def select(n_static):
        def bit_step(n, thr_key):
            cand = thr_key + lax.shift_left(jnp.int32(1), KEY_BITS - 1 - n)
            thr = key_to_f32(cand)
            cnt = jnp.zeros((tq, LANES), F32)
            for cb in range(n_static * n_cb):
                s = score_ref[:, cb * LANES:(cb + 1) * LANES]
                cnt = cnt + jnp.where(s >= thr, 1.0, 0.0)
            total = jnp.sum(cnt, axis=1, keepdims=True)
            return jnp.where(total >= k_f, cand, thr_key)

        thr_key = lax.fori_loop(0, KEY_BITS, bit_step,
                                jnp.full((tq, LANES), INT32_MIN, jnp.int32), unroll=SELECT_UNROLL)
        thr_ref[...] = key_to_f32(thr_key)

    for n_static in range(1, max_chunks + 1):
        pl.when(n_chunks == n_static)(functools.partial(select, n_static))

    def bias_chunk(c, cnt):
        off = pl.multiple_of(c * tk, tk)
        thr = thr_ref[...]
        causal = k_iota + off <= t_pos
        for cb in range(n_cb):
            cols = pl.ds(off + cb * LANES, LANES)
            sel = score_ref[:, cols] >= thr
            bias_ref[:, cols] = jnp.where(
                sel, jnp.where(causal[:, cb * LANES:(cb + 1) * LANES], 0.0, NEG), NEG)
            cnt = cnt + jnp.where(sel, 1.0, 0.0)
        return cnt

    n_ge = jnp.sum(lax.fori_loop(0, n_chunks, bias_chunk, jnp.zeros((tq, LANES), F32)),
                   axis=1, keepdims=True)

    @pl.when(jnp.max(n_ge) > k_f)
    def _():
        def count_greater(c, cnt):
            off = pl.multiple_of(c * tk, tk)
            thr = thr_ref[...]
            for cb in range(n_cb):
                s = score_ref[:, pl.ds(off + cb * LANES, LANES)]
                cnt = cnt + jnp.where(s > thr, 1.0, 0.0)
            return cnt

        n_greater = jnp.sum(lax.fori_loop(0, n_chunks, count_greater,
                                          jnp.zeros((tq, LANES), F32)), axis=1, keepdims=True)
        n_take = k_f - n_greater
        before = jnp.where(
            lax.broadcasted_iota(jnp.int32, (tk, tk), 0) < lax.broadcasted_iota(jnp.int32, (tk, tk), 1),
            1.0, 0.0).astype(BF16)

        def tie_chunk(c, seen):
            off = pl.multiple_of(c * tk, tk)
            sc = score_ref[:, pl.ds(off, tk)]
            thr = jnp.concatenate([thr_ref[...]] * n_cb, axis=1)
            eq = jnp.where(sc == thr, 1.0, 0.0)
            rank = _dot(eq.astype(BF16), before) + seen
            keep = jnp.where(sc > thr, 1.0, jnp.where(rank < n_take, eq, 0.0))
            causal = k_iota + off <= t_pos
            bias_ref[:, pl.ds(off, tk)] = jnp.where(causal, jnp.where(keep > 0.0, 0.0, NEG), NEG)
            return seen + jnp.sum(eq, axis=1, keepdims=True)

        lax.fori_loop(0, n_chunks, tie_chunk, jnp.zeros((tq, 1), F32))

    rows = KV_GROUP * tq
    ones = jnp.ones((tk, LANES), BF16)

    def attend(off, first):
        b = bias_ref[:, pl.ds(off, tk)]
        for g in range(N_KV_HEADS):
            kc = k_ref[pl.ds(off, tk), g * HEAD_DIM:(g + 1) * HEAD_DIM]
            vc = v_ref[pl.ds(off, tk), g * HEAD_DIM:(g + 1) * HEAD_DIM]
            s = _dot_t(qg_ref[g], kc)
            s = (s.reshape(KV_GROUP, tq, tk) + b[None]).reshape(rows, tk)
            m_new = jnp.broadcast_to(jnp.max(s, axis=1, keepdims=True), (rows, LANES))
            if not first:
                m_old = m_ref[g]
                m_new = jnp.maximum(m_old, m_new)
            p = jnp.exp2(s - jnp.concatenate([m_new] * n_cb, axis=1))
            pv = _dot(p.astype(BF16), jnp.concatenate([vc, ones], axis=1))
            if first:
                acc_ref[g] = pv
            else:
                alpha = jnp.exp2(m_old - m_new)
                acc_ref[g] = jnp.concatenate([alpha, alpha], axis=1) * acc_ref[g] + pv
            m_ref[g] = m_new

    attend(0, True)

    def attn_chunk(c, carry):
        attend(pl.multiple_of(c * tk, tk), False)
        return carry

    lax.fori_loop(1, n_chunks, attn_chunk, 0)

    for g in range(N_KV_HEADS):
        out = acc_ref[g, :, 0:HEAD_DIM] / acc_ref[g, :, HEAD_DIM:2 * HEAD_DIM]
        for r in range(KV_GROUP):
            hd = g * KV_GROUP + r
            attn_ref[:, hd * HEAD_DIM:(hd + 1) * HEAD_DIM] = out[r * tq:(r + 1) * tq].astype(BF16)

    gate = ada_ref[3 * sub + 2:3 * sub + 3, :]
    attn = attn_ref[...]
    wide = 4 * LANES
    for n in range(o_ref.shape[1] // wide):
        cols = slice(n * wide, (n + 1) * wide)
        o_ref[:, cols] = h_ref[:, cols] + gate[:, cols] * _dot(attn, wobf_ref[:, cols])


def _dsa_attn(main, kk, wi, h, ada4, w_out, widx, layer, sub, batch, seq, tq, tk, top_k):
    t, d = h.shape
    nq = seq // tq
    rows = KV_GROUP * tq
    return pl.pallas_call(
        functools.partial(_dsa_attn_kernel, tq, tk, top_k, sub, widx),
        grid=(batch, nq),
        in_specs=[
            pl.BlockSpec((tq, Q_W), lambda b, i: (b * nq + i, 0)),
            pl.BlockSpec((tq, QI_W), lambda b, i: (b * nq + i, (Q_W + 2 * KV_W) // QI_W)),
            pl.BlockSpec((tq, LANES), lambda b, i: (b * nq + i, 0)),
            pl.BlockSpec((seq, KV_W), lambda b, i: (b, Q_W // KV_W)),
            pl.BlockSpec((seq, KV_W), lambda b, i: (b, Q_W // KV_W + 1)),
            pl.BlockSpec((seq, 2 * LANES), lambda b, i: (b, 0)),
            pl.BlockSpec((tq, d), lambda b, i: (b * nq + i, 0)),
            pl.BlockSpec((None, None, 3 * N_SUBLAYERS, d), lambda b, i: (layer, b, 0, 0)),
            pl.BlockSpec(memory_space=pl.ANY),
        ],
        out_specs=pl.BlockSpec((tq, d), lambda b, i: (b * nq + i, 0)),
        out_shape=jax.ShapeDtypeStruct((t, d), F32),
        scratch_shapes=[
            pltpu.VMEM((tq, seq), F32),
            pltpu.VMEM((tq, seq), F32),
            pltpu.VMEM((tq, LANES), F32),
            pltpu.VMEM((IDX_HEADS, tq, LANES), F32),
            pltpu.VMEM((N_KV_HEADS, rows, HEAD_DIM), BF16),
            pltpu.VMEM((N_KV_HEADS, rows, LANES), F32),
            pltpu.VMEM((N_KV_HEADS, rows, 2 * HEAD_DIM), F32),
            pltpu.VMEM((tq, Q_W), BF16),
            pltpu.VMEM((Q_W, d), BF16),
            pltpu.VMEM((2, 2 * LANES, d), F32),
            pltpu.SemaphoreType.DMA((2,)),
        ],
        compiler_params=_cparams(("arbitrary", "arbitrary")),
        name="dsa_attn",
    )(main, main, wi, main, main, kk, h, ada4, w_out)


def _tiles(seq, d_ff):
    tm = min(1024, seq)
    tm_ffn = min(2048, seq)
    tf = 256 if d_ff % 256 == 0 else d_ff
    tq = min(256, seq)
    tk = min(256, seq)
    return tm, tm_ffn, tf, tq, tk


def kernel(x, c, positions, ada_w, ada_b, norm_g, final_g, ffn_wgu, ffn_wd,
           pool_w_in, pool_w_grp, pool_scale, pool_w_out, dsa_w_in, dsa_w_out):
    batch, seq, d = x.shape
    depth = ada_w.shape[0]
    d_ff = ffn_wd.shape[2]
    t = batch * seq
    tm, tm_ffn, tf, tq, tk = _tiles(seq, d_ff)
    tn = min(512, d)
    top_k = min(TOPK_MAX, seq // 4)
    assert seq % tm == 0 and seq % tm_ffn == 0 and seq % tq == 0 and seq % tk == 0 and tk >= top_k
    assert dsa_w_in.shape[2] == DSA_MAIN_W + DSA_SMALL_W and DSA_MAIN_W % tn == 0

    rows = -(-batch // SUBLANES) * SUBLANES
    c8 = jnp.pad(c, ((0, rows - batch), (0, 0)))
    ada_n = ada_w.shape[2]
    ada, tabs = _ada(c8, ada_w, ada_b, 1024 if ada_n % 1024 == 0 else d, positions)
    ada4 = ada.reshape(depth, rows, 3 * N_SUBLAYERS, d)

    h = x.reshape(t, d)
    h = _ffn(h, ada4, norm_g, ffn_wgu, ffn_wd, [(0, 0, 0)], seq, tm_ffn, tf)
    for layer in range(depth):
        j = layer // 2
        g_row = norm_g[layer, 1][None, :]
        if layer % 2 == 0:
            u = _norm_linear(h, ada4, g_row, pool_w_in, j, layer, 1, seq, tm, tn)
            a = _pool_core(u, pool_w_grp, pool_scale, j, batch, seq)
            h = _linear_residual(a, pool_w_out, j, h, ada4, layer, 1, seq, tm, tn)
        else:
            w_small = _dsa_small_weights(dsa_w_in[j, :, DSA_MAIN_W:])
            main, kk, wi = _dsa_proj(h, ada4, g_row, jnp.swapaxes(dsa_w_in, 1, 2), j, w_small,
                                     tabs, layer, 1, seq, tm, tn)
            h = _dsa_attn(main, kk, wi, h, ada4, dsa_w_out, j, layer, 1, batch, seq, tq, tk, top_k)
        if layer == depth - 1:
            h = _ffn(h, ada4, norm_g, ffn_wgu, ffn_wd, [(layer, 1, 2)], seq, tm_ffn, tf,
                     final_g=final_g)
        else:
            h = _ffn(h, ada4, norm_g, ffn_wgu, ffn_wd, [(layer, 1, 2), (layer + 1, 0, 0)],
                     seq, tm_ffn, tf)
    return h.reshape(batch, seq, d)
```

```python
import functools

import jax
import jax.numpy as jnp
import numpy as np
from jax import lax
from jax.experimental import pallas as pl
from jax.experimental.pallas import tpu as pltpu

F32 = jnp.float32
BF16 = jnp.bfloat16

N_SUBLAYERS = 3
POOL_WINDOWS = (2, 4, 8, 16)
N_HEADS = 16
N_KV_HEADS = 4
KV_GROUP = N_HEADS // N_KV_HEADS
HEAD_DIM = 128
ROT_DIM = HEAD_DIM // 4
ROPE_THETA = 500000.0
IDX_HEADS = 16
IDX_DIM = 64
IDX_ROT_DIM = IDX_DIM // 4
TOPK_MAX = 256
EPS = 1e-6
NEG = -1e30
Q_W = N_HEADS * HEAD_DIM
KV_W = N_KV_HEADS * HEAD_DIM
QI_W = IDX_HEADS * IDX_DIM
DSA_MAIN_W = Q_W + 2 * KV_W + QI_W
DSA_SMALL_W = IDX_DIM + IDX_HEADS
Q_PRESCALE = HEAD_DIM ** -0.5 * 1.4426950408889634

LANES = 128
SUBLANES = 8
VMEM_LIMIT_BYTES = 56 * 1024 * 1024
INT32_MIN = -(2 ** 31)
KEY_BITS = 32
SELECT_UNROLL = 8
N_ROPE_TABLES = 8


def _cparams(semantics):
    return pltpu.CompilerParams(dimension_semantics=semantics,
                                vmem_limit_bytes=VMEM_LIMIT_BYTES)


def _rmsnorm(x, g):
    ms = jnp.mean(x * x, axis=-1, keepdims=True)
    return x * lax.rsqrt(ms + EPS) * g


def _rmsnorm_rows(ref, rows, g_ref):
    ref[rows, :] = _rmsnorm(ref[rows, :], g_ref[...])


NORM_SLAB = 16
NORM_UNROLL = 8
FFN_ROW_SLAB = 128
FFN_SLAB_SLOTS = 4
FFN_DOWN_COLS = 256


def _for_row_slabs(n_rows, slab, body, unroll=1):
    def step(r, carry):
        body(pl.ds(pl.multiple_of(r * slab, slab), slab))
        return carry
    lax.fori_loop(0, n_rows // slab, step, 0, unroll=unroll)


def _write_norm_mod(x_ref, g_ref, m_ref, sub, y_ref, copy_ref=None):
    shift = m_ref[3 * sub + 0:3 * sub + 1, :]
    gs = g_ref[...] * (1.0 + m_ref[3 * sub + 1:3 * sub + 2, :])
    slab = min(NORM_SLAB, x_ref.shape[0])

    def norm_rows(rows):
        x = x_ref[rows, :]
        r = lax.rsqrt(jnp.mean(x * x, axis=-1, keepdims=True) + EPS)
        y_ref[rows, :] = ((x * r) * gs + shift).astype(BF16)
        if copy_ref is not None:
            copy_ref[rows, :] = x

    _for_row_slabs(x_ref.shape[0], slab, norm_rows, unroll=NORM_UNROLL)


def _silu(x):
    return x * jax.nn.sigmoid(x)


def _dot(a, b):
    return jnp.dot(a, b, preferred_element_type=F32)


def _dot_t(a, b):
    return lax.dot_general(a, b, (((1,), (1,)), ((), ())), preferred_element_type=F32)


def _ada_kernel(n_tab_blocks, c_ref, w_ref, b_ref, pos_ref, invf_ref, o_ref, tab_ref):
    sc = _silu(c_ref[...]).astype(BF16)
    o_ref[...] = _dot(sc, w_ref[...].astype(BF16)) + b_ref[...]
    step = pl.program_id(0) * pl.num_programs(1) + pl.program_id(1)

    @pl.when(step < n_tab_blocks)
    def _():
        _write_rope_tables(pos_ref, invf_ref, tab_ref)


def _ada(c8, ada_w, ada_b, tn, positions):
    depth, d, n = ada_w.shape
    rows = c8.shape[0]
    n_steps = depth * (n // tn)
    t = positions.size
    rt = SUBLANES
    while t // rt > n_steps or t % rt:
        rt *= 2
    n_tab = t // rt
    nj = n // tn

    def tab_block(l, j):
        return jnp.minimum(l * nj + j, n_tab - 1)

    return pl.pallas_call(
        functools.partial(_ada_kernel, n_tab),
        grid=(depth, nj),
        in_specs=[
            pl.BlockSpec((rows, d), lambda l, j: (0, 0)),
            pl.BlockSpec((None, d, tn), lambda l, j: (l, 0, j)),
            pl.BlockSpec((None, 1, tn), lambda l, j: (l, 0, j)),
            pl.BlockSpec((rt, 1), lambda l, j: (tab_block(l, j), 0)),
            pl.BlockSpec((1, LANES), lambda l, j: (0, 0)),
        ],
        out_specs=[
            pl.BlockSpec((None, rows, tn), lambda l, j: (l, 0, j)),
            pl.BlockSpec((N_ROPE_TABLES, rt, LANES), lambda l, j: (0, tab_block(l, j), 0)),
        ],
        out_shape=[
            jax.ShapeDtypeStruct((depth, rows, n), F32),
            jax.ShapeDtypeStruct((N_ROPE_TABLES, t, LANES), F32),
        ],
        compiler_params=_cparams(("arbitrary", "arbitrary")),
        name="ada",
    )(c8, ada_w, ada_b.reshape(depth, 1, n), positions.reshape(t, 1), _rope_inv_freq_row())


def _select_static(values, stage):
    out = values[-1]
    for n in range(len(values) - 2, -1, -1):
        out = jnp.where(stage == n, values[n], out)
    return out


def _ffn_kernel(subs, final, rs, nk, *refs):
    if final:
        (x_hbm, g_ref, m_ref, wg_ref, wu_ref, wd_ref, fg_ref, o_hbm,
         acc_ref, y_ref, xbuf, in_sem, out_sem) = refs
    else:
        (x_hbm, g_ref, m_ref, wg_ref, wu_ref, wd_ref, o_hbm,
         acc_ref, y_ref, xbuf, in_sem, out_sem) = refs
    k = pl.program_id(1)
    last_step = pl.num_programs(1) - 1
    sub = _select_static(subs, k // nk)
    tm, d_out = acc_ref.shape
    n_slabs = tm // rs
    n_slots = xbuf.shape[0]
    n_first = min(n_slots, n_slabs)
    row0 = pl.program_id(0) * tm

    def x_copy(s, tile_row0=row0):
        slot = s % n_slots
        return pltpu.make_async_copy(x_hbm.at[pl.ds(tile_row0 + s * rs, rs), :], xbuf.at[slot],
                                     in_sem.at[slot])

    def o_copy(s):
        return pltpu.make_async_copy(acc_ref.at[pl.ds(s * rs, rs), :],
                                     o_hbm.at[pl.ds(row0 + s * rs, rs), :], out_sem.at[s])

    @pl.when(k == 0)
    def _():
        @pl.when(pl.program_id(0) == 0)
        def _():
            for s in range(n_first):
                x_copy(s).start()

        for s in range(n_slabs):
            x_copy(s).wait()
            pl.when(pl.program_id(0) > 0)(o_copy(s).wait)
            rows = pl.ds(s * rs, rs)
            _write_norm_mod(xbuf.at[s % n_slots], g_ref.at[pl.ds(0, 1), :], m_ref, subs[0],
                            y_ref.at[rows, :], copy_ref=acc_ref.at[rows, :])
            if s + n_slots < n_slabs:
                x_copy(s + n_slots).start()

    for stage in range(1, len(subs)):
        @pl.when(k == stage * nk)
        def _(stage=stage):
            _write_norm_mod(acc_ref, g_ref.at[pl.ds(stage, 1), :], m_ref, subs[stage], y_ref)

    @pl.when(jnp.logical_and(k == last_step, pl.program_id(0) + 1 < pl.num_programs(0)))
    def _():
        for s in range(n_first):
            x_copy(s, row0 + tm).start()

    y = y_ref[...]
    g = _dot(y, wg_ref[...].astype(BF16))
    u = _dot(y, wu_ref[...].astype(BF16))
    a = (_silu(g) * u).astype(BF16)
    half_gate = 0.5 * m_ref[pl.ds(3 * sub + 2, 1), :]
    nc = min(FFN_DOWN_COLS, d_out)
    for n in range(d_out // nc):
        cols = slice(n * nc, (n + 1) * nc)
        acc_ref[:, cols] += half_gate[:, cols] * _dot(a, wd_ref[:, cols].astype(BF16))

    @pl.when(k == last_step)
    def _():
        for s in range(n_slabs):
            if final:
                _for_row_slabs(
                    rs, min(NORM_SLAB, rs),
                    lambda rows, s=s: _rmsnorm_rows(acc_ref.at[pl.ds(s * rs, rs), :], rows, fg_ref),
                    unroll=NORM_UNROLL)
            o_copy(s).start()

        @pl.when(pl.program_id(0) == pl.num_programs(0) - 1)
        def _():
            for s in range(n_slabs):
                o_copy(s).wait()


def _ffn(h, ada4, norm_g, ffn_wgu, ffn_wd, stages, seq, tm, tf, final_g=None):
    t, d = h.shape
    d_ff = ffn_wd.shape[2]
    nk = d_ff // tf
    tpb = seq // tm
    rs = min(FFN_ROW_SLAB, tm)
    final = final_g is not None
    layers, whichs, subs = zip(*stages)

    def layer_of(k):
        return _select_static(layers, k // nk)

    def which_of(k):
        return _select_static(whichs, k // nk)

    in_specs = [
        pl.BlockSpec(memory_space=pl.ANY),
        pl.BlockSpec((len(stages), d), lambda i, k: (0, 0)),
        pl.BlockSpec((None, None, 3 * N_SUBLAYERS, d), lambda i, k: (layer_of(k), i // tpb, 0, 0)),
        pl.BlockSpec((None, None, d, tf), lambda i, k: (layer_of(k), which_of(k), 0, k % nk)),
        pl.BlockSpec((None, None, d, tf), lambda i, k: (layer_of(k), which_of(k), 0, nk + k % nk)),
        pl.BlockSpec((None, None, tf, d), lambda i, k: (layer_of(k), which_of(k), k % nk, 0)),
    ]
    g_rows = jnp.stack([norm_g[layer, sub] for layer, _, sub in stages])
    args = [h, g_rows, ada4, ffn_wgu, ffn_wgu, ffn_wd]
    if final:
        in_specs.append(pl.BlockSpec((1, d), lambda i, k: (0, 0)))
        args.append(final_g[None, :])
    layer, which = layers[0], whichs[0]
    return pl.pallas_call(
        functools.partial(_ffn_kernel, subs, final, rs, nk),
        grid=(t // tm, len(stages) * nk),
        in_specs=in_specs,
        out_specs=pl.BlockSpec(memory_space=pl.ANY),
        out_shape=jax.ShapeDtypeStruct((t, d), F32),
        scratch_shapes=[
            pltpu.VMEM((tm, d), F32),
            pltpu.VMEM((tm, d), BF16),
            pltpu.VMEM((FFN_SLAB_SLOTS, rs, d), F32),
            pltpu.SemaphoreType.DMA((FFN_SLAB_SLOTS,)),
            pltpu.SemaphoreType.DMA((tm // rs,)),
        ],
        compiler_params=_cparams(("arbitrary", "arbitrary")),
        name=f"ffn_l{layer}_{which}",
    )(*args)


def _keep_bf16_weight(w_ref, wbf_ref):
    j = pl.program_id(1)

    @pl.when(pl.program_id(0) == 0)
    def _():
        wbf_ref[j] = w_ref[...].astype(BF16)

    return wbf_ref[j]


def _resident_weight_spec(widx, kdim, tn, nj):
    return pl.BlockSpec((None, kdim, tn), lambda i, j: (widx, 0, jnp.where(i == 0, j, nj - 1)))


def _norm_linear_kernel(sub, x_ref, g_ref, m_ref, w_ref, o_ref, y_ref, wbf_ref):
    @pl.when(pl.program_id(1) == 0)
    def _():
        _write_norm_mod(x_ref, g_ref, m_ref, sub, y_ref)

    o_ref[...] = _dot(y_ref[...], _keep_bf16_weight(w_ref, wbf_ref))


def _norm_linear(h, ada4, g_row, w, widx, layer, sub, seq, tm, tn):
    t, d = h.shape
    n = w.shape[2]
    tpb = seq // tm
    return pl.pallas_call(
        functools.partial(_norm_linear_kernel, sub),
        grid=(t // tm, n // tn),
        in_specs=[
            pl.BlockSpec((tm, d), lambda i, j: (i, 0)),
            pl.BlockSpec((1, d), lambda i, j: (0, 0)),
            pl.BlockSpec((None, None, 3 * N_SUBLAYERS, d), lambda i, j: (layer, i // tpb, 0, 0)),
            _resident_weight_spec(widx, d, tn, n // tn),
        ],
        out_specs=pl.BlockSpec((tm, tn), lambda i, j: (i, j)),
        out_shape=jax.ShapeDtypeStruct((t, n), F32),
        scratch_shapes=[pltpu.VMEM((tm, d), BF16), pltpu.VMEM((n // tn, d, tn), BF16)],
        compiler_params=_cparams(("arbitrary", "arbitrary")),
        name=f"norm_linear_l{layer}",
    )(h, g_row, ada4, w)


def _pool_kernel(u_ref, w_ref, ls_ref, o_ref):
    g = pl.program_id(1)
    u = u_ref[...]
    t = lax.broadcasted_iota(jnp.int32, u.shape, 0)

    def shifted(v, d):
        return jnp.where(t >= d, pltpu.roll(v, d, 0), 0.0)

    for gi, win in enumerate(POOL_WINDOWS):
        @pl.when(g == gi)
        def _(win=win):
            acc = u
            d = 1
            while d < win:
                acc = acc + shifted(acc, d)
                d *= 2
            cnt = jnp.minimum(t + 1, win).astype(F32)
            p = (acc / cnt - u).astype(BF16)
            v = _dot(p, w_ref[...].astype(BF16)) * ls_ref[...]
            o_ref[...] = v.astype(BF16)


def _pool_core(u, w_grp, ls, widx, batch, seq):
    t, d = u.shape
    _, groups, c, _ = w_grp.shape
    return pl.pallas_call(
        _pool_kernel,
        grid=(batch, groups),
        in_specs=[
            pl.BlockSpec((seq, c), lambda b, g: (b, g)),
            pl.BlockSpec((None, None, c, c), lambda b, g: (widx, g, 0, 0)),
            pl.BlockSpec((None, 1, c), lambda b, g: (widx, 0, g)),
        ],
        out_specs=pl.BlockSpec((seq, c), lambda b, g: (b, g)),
        out_shape=jax.ShapeDtypeStruct((t, d), BF16),
        compiler_params=_cparams(("arbitrary", "arbitrary")),
        name="pool_core",
    )(u, w_grp, ls.reshape(ls.shape[0], 1, d))


def _linear_residual_kernel(sub, a_ref, w_ref, h_ref, m_ref, o_ref, wbf_ref):
    gate = m_ref[3 * sub + 2:3 * sub + 3, :]
    o_ref[...] = h_ref[...] + gate * _dot(a_ref[...], _keep_bf16_weight(w_ref, wbf_ref))


def _linear_residual(a, w, widx, h, ada4, layer, sub, seq, tm, tn):
    t, kdim = a.shape
    n = w.shape[2]
    tpb = seq // tm
    return pl.pallas_call(
        functools.partial(_linear_residual_kernel, sub),
        grid=(t // tm, n // tn),
        in_specs=[
            pl.BlockSpec((tm, kdim), lambda i, j: (i, 0)),
            _resident_weight_spec(widx, kdim, tn, n // tn),
            pl.BlockSpec((tm, tn), lambda i, j: (i, j)),
            pl.BlockSpec((None, None, 3 * N_SUBLAYERS, tn), lambda i, j: (layer, i // tpb, 0, j)),
        ],
        out_specs=pl.BlockSpec((tm, tn), lambda i, j: (i, j)),
        out_shape=jax.ShapeDtypeStruct((t, n), F32),
        scratch_shapes=[pltpu.VMEM((n // tn, kdim, tn), BF16)],
        compiler_params=_cparams(("arbitrary", "arbitrary")),
        name=f"linear_residual_l{layer}",
    )(a, w, h, ada4)


ROPE_SET_Q, ROPE_SET_K, ROPE_SET_NONE, ROPE_SET_IDX = 0, 1, 2, 3
HALF_LANES = LANES // 2


def _write_rope_tables(pos_ref, invf_ref, o_ref):
    pos = pos_ref[...].astype(F32)
    lane = lax.broadcasted_iota(jnp.int32, (pos.shape[0], LANES), 1)
    ang = pos * invf_ref[...]
    cos = jnp.cos(ang)
    sin = jnp.where(lane < HALF_LANES, -jnp.sin(ang), jnp.sin(ang))
    rot_big = (lane & (HALF_LANES - 1)) < ROT_DIM // 2
    cb = jnp.where(rot_big, cos, 1.0)
    sb = jnp.where(rot_big, sin, 0.0)
    rot_idx = (lane & (HALF_LANES // 2 - 1)) < IDX_ROT_DIM // 2
    even_quarter = (lane & (HALF_LANES // 2)) == 0
    shift = ROT_DIM // 2

    def idx_table(v, rest):
        from_above = pltpu.roll(v, LANES - shift, 1)
        from_below = pltpu.roll(v, shift, 1)
        return jnp.where(rot_idx, jnp.where(even_quarter, from_above, from_below), rest)

    ci = idx_table(cos, 1.0)
    si = idx_table(sin, 0.0)
    o_ref[2 * ROPE_SET_Q] = cb * Q_PRESCALE
    o_ref[2 * ROPE_SET_Q + 1] = sb * Q_PRESCALE
    o_ref[2 * ROPE_SET_K] = cb
    o_ref[2 * ROPE_SET_K + 1] = sb
    o_ref[2 * ROPE_SET_NONE] = jnp.ones_like(cb)
    o_ref[2 * ROPE_SET_NONE + 1] = jnp.zeros_like(cb)
    o_ref[2 * ROPE_SET_IDX] = ci
    o_ref[2 * ROPE_SET_IDX + 1] = si


def _rope_inv_freq_row():
    def inv_freq(rot_dim):
        half = rot_dim // 2
        return ROPE_THETA ** (-jnp.arange(half, dtype=F32) / half)

    half_row = jnp.concatenate([
        inv_freq(ROT_DIM), inv_freq(IDX_ROT_DIM),
        jnp.zeros((HALF_LANES - ROT_DIM // 2 - IDX_ROT_DIM // 2,), F32)])
    return jnp.tile(half_row, 2)[None, :]


def _rope(x, c, s):
    return x * c + pltpu.roll(x, HALF_LANES, 1) * s


PERM_BIG, PERM_NONE, PERM_IDX = 0, 1, 2


def _lane_permutations():
    h, p = ROT_DIM // 2, (HEAD_DIM - ROT_DIM) // 2
    big = np.concatenate([np.arange(0, h), np.arange(ROT_DIM, ROT_DIM + p),
                          np.arange(h, ROT_DIM), np.arange(ROT_DIM + p, HEAD_DIM)])
    lo, hi = _idx_lo_hi_dims()
    idx = np.concatenate([lo, IDX_DIM + lo, hi, IDX_DIM + hi])
    mats = np.zeros((3, LANES, LANES), np.float32)
    for kind, src in ((PERM_BIG, big), (PERM_NONE, np.arange(LANES)), (PERM_IDX, idx)):
        mats[kind, np.arange(LANES), src] = 1.0
    return jnp.asarray(mats, BF16)


def _idx_lo_hi_dims():
    h, p = IDX_ROT_DIM // 2, (IDX_DIM - IDX_ROT_DIM) // 2
    lo = np.concatenate([np.arange(0, h), np.arange(IDX_ROT_DIM, IDX_ROT_DIM + p)])
    hi = np.concatenate([np.arange(h, IDX_ROT_DIM), np.arange(IDX_ROT_DIM + p, IDX_DIM)])
    return lo, hi


def _dsa_small_weights(w_tail):
    h, p = IDX_ROT_DIM // 2, (IDX_DIM - IDX_ROT_DIM) // 2
    klo = jnp.concatenate([w_tail[:, 0:h], w_tail[:, IDX_ROT_DIM:IDX_ROT_DIM + p]], axis=1)
    khi = jnp.concatenate([w_tail[:, h:IDX_ROT_DIM], w_tail[:, IDX_ROT_DIM + p:IDX_DIM]], axis=1)
    wi = jnp.pad(w_tail[:, IDX_DIM:], ((0, 0), (0, LANES - IDX_DIM - IDX_HEADS)))
    return jnp.concatenate([klo, khi, wi], axis=1).astype(BF16)


def _dsa_proj_kernel(sub, tn, x_hbm, g_ref, m_ref, w_ref, ws_ref, perm_ref, tab_ref,
                     o_ref, ok_ref, ow_ref, y_ref, wbf_ref, x_ref, x_sem):
    i = pl.program_id(0)
    j = pl.program_id(1)
    tm = x_ref.shape[0]
    first_k, first_v, first_qi = Q_W // tn, (Q_W + KV_W) // tn, (Q_W + 2 * KV_W) // tn

    def x_copy(tile):
        return pltpu.make_async_copy(x_hbm.at[pl.ds(tile * tm, tm), :], x_ref, x_sem.at[0])

    @pl.when(j == 0)
    def _():
        pl.when(i == 0)(x_copy(i).start)
        x_copy(i).wait()
        _write_norm_mod(x_ref, g_ref, m_ref, sub, y_ref)
        pl.when(i + 1 < pl.num_programs(0))(x_copy(i + 1).start)
        small = _dot(y_ref[...], ws_ref[...])
        by32 = pltpu.roll(small, HALF_LANES // 2, 1)
        by64 = pltpu.roll(small, HALF_LANES, 1)
        quarter = lax.broadcasted_iota(jnp.int32, small.shape, 1) // (HALF_LANES // 2)
        k_even = jnp.where(quarter == 0, small, jnp.where(quarter == 2, by32, 0.0))
        k_odd = jnp.where(quarter == 1, by32, jnp.where(quarter == 3, by64, 0.0))
        for s, kk in enumerate((k_even, k_odd)):
            ok_ref[:, s * LANES:(s + 1) * LANES] = _rope(
                kk, tab_ref[2 * ROPE_SET_IDX], tab_ref[2 * ROPE_SET_IDX + 1]).astype(BF16)
        ow_ref[...] = jnp.where(quarter == 0, by64, 0.0)

    @pl.when(pl.program_id(0) == 0)
    def _():
        kind = jnp.where(j < first_v, PERM_BIG, jnp.where(j < first_qi, PERM_NONE, PERM_IDX))
        perm = perm_ref[kind]
        for sl in range(tn // LANES):
            rows = slice(sl * LANES, (sl + 1) * LANES)
            wbf_ref[j, rows, :] = _dot(perm, w_ref[rows, :].astype(BF16)).astype(BF16)

    rope_set = jnp.where(j < first_k, ROPE_SET_Q,
                         jnp.where(j < first_v, ROPE_SET_K,
                                   jnp.where(j < first_qi, ROPE_SET_NONE, ROPE_SET_IDX)))
    c = tab_ref[2 * rope_set]
    s = tab_ref[2 * rope_set + 1]
    y = y_ref[...]
    wide = 2 * LANES
    for part in range(tn // wide):
        acc = _dot_t(y, wbf_ref[j, part * wide:(part + 1) * wide, :])
        for sl in range(wide // LANES):
            cols = slice(part * wide + sl * LANES, part * wide + (sl + 1) * LANES)
            o_ref[:, cols] = _rope(acc[:, sl * LANES:(sl + 1) * LANES], c, s).astype(BF16)


def _dsa_proj(h, ada4, g_row, w_in, widx, w_small, tabs, layer, sub, seq, tm, tn):
    t, d = h.shape
    tpb = seq // tm
    return pl.pallas_call(
        functools.partial(_dsa_proj_kernel, sub, tn),
        grid=(t // tm, DSA_MAIN_W // tn),
        in_specs=[
            pl.BlockSpec(memory_space=pl.ANY),
            pl.BlockSpec((1, d), lambda i, j: (0, 0)),
            pl.BlockSpec((None, None, 3 * N_SUBLAYERS, d), lambda i, j: (layer, i // tpb, 0, 0)),
            pl.BlockSpec((None, tn, d),
                         lambda i, j: (widx, jnp.where(i == 0, j, DSA_MAIN_W // tn - 1), 0)),
            pl.BlockSpec((d, LANES), lambda i, j: (0, 0)),
            pl.BlockSpec((3, LANES, LANES), lambda i, j: (0, 0, 0)),
            pl.BlockSpec((N_ROPE_TABLES, tm, LANES), lambda i, j: (0, i, 0)),
        ],
        out_specs=[
            pl.BlockSpec((tm, tn), lambda i, j: (i, j)),
            pl.BlockSpec((tm, 2 * LANES), lambda i, j: (i, 0)),
            pl.BlockSpec((tm, LANES), lambda i, j: (i, 0)),
        ],
        out_shape=[
            jax.ShapeDtypeStruct((t, DSA_MAIN_W), BF16),
            jax.ShapeDtypeStruct((t, 2 * LANES), BF16),
            jax.ShapeDtypeStruct((t, LANES), F32),
        ],
        scratch_shapes=[pltpu.VMEM((tm, d), BF16),
                        pltpu.VMEM((DSA_MAIN_W // tn, tn, d), BF16),
                        pltpu.VMEM((tm, d), F32),
                        pltpu.SemaphoreType.DMA((1,))],
        compiler_params=_cparams(("arbitrary", "arbitrary")),
        name=f"dsa_proj_l{layer}",
    )(h, g_row, ada4, w_in, w_small, _lane_permutations(), tabs)


def _dsa_attn_kernel(tq, tk, top_k, sub, widx, q_ref, qi_ref, wi_ref, k_ref, v_ref, kk_ref,
                     h_ref, ada_ref, wo_ref, o_ref,
                     score_ref, bias_ref, thr_ref, wib_ref, qg_ref, m_ref, acc_ref,
                     attn_ref, wobf_ref, wo_stage, wo_sem):
    i = pl.program_id(1)

    @pl.when(jnp.logical_and(pl.program_id(0) == 0, i == 0))
    def _():
        slab = wo_stage.shape[1]
        n_slabs = wobf_ref.shape[0] // slab

        def wo_copy(r):
            return pltpu.make_async_copy(wo_ref.at[widx, pl.ds(r * slab, slab), :],
                                         wo_stage.at[r % 2], wo_sem.at[r % 2])

        wo_copy(0).start()
        for r in range(n_slabs):
            if r + 1 < n_slabs:
                wo_copy(r + 1).start()
            wo_copy(r).wait()
            wobf_ref[r * slab:(r + 1) * slab, :] = wo_stage[r % 2].astype(BF16)

    n_chunks = ((i + 1) * tq + tk - 1) // tk
    max_chunks = score_ref.shape[1] // tk
    t_pos = i * tq + lax.broadcasted_iota(jnp.int32, (tq, tk), 0)
    k_iota = lax.broadcasted_iota(jnp.int32, (tq, tk), 1)
    n_cb = tk // LANES

    for hh in range(IDX_HEADS):
        wib_ref[hh] = jnp.broadcast_to(wi_ref[:, hh:hh + 1], (tq, LANES))
    for g in range(N_KV_HEADS):
        for r in range(KV_GROUP):
            hd = g * KV_GROUP + r
            qg_ref[g, r * tq:(r + 1) * tq, :] = q_ref[:, hd * HEAD_DIM:(hd + 1) * HEAD_DIM]

    def score_chunk(c, carry):
        off = pl.multiple_of(c * tk, tk)
        k_even = kk_ref[pl.ds(off, tk), 0:LANES]
        k_odd = kk_ref[pl.ds(off, tk), LANES:2 * LANES]
        acc = [jnp.zeros((tq, LANES), F32) for _ in range(n_cb)]
        for pair in range(IDX_HEADS // 2):
            qp = qi_ref[:, pair * LANES:(pair + 1) * LANES]
            for par, kop in enumerate((k_even, k_odd)):
                r = _dot_t(qp, kop)
                w = wib_ref[2 * pair + par]
                for cb in range(n_cb):
                    acc[cb] = acc[cb] + w * jnp.maximum(r[:, cb * LANES:(cb + 1) * LANES], 0.0)
        sc = jnp.concatenate(acc, axis=1)
        score_ref[:, pl.ds(off, tk)] = jnp.where(k_iota + off <= t_pos, sc, NEG)
        return carry

    lax.fori_loop(0, n_chunks, score_chunk, 0)

    def key_to_f32(key):
        return pltpu.bitcast(key ^ ((key >> 31) & 0x7FFFFFFF), F32)

    k_f = float(top_k)

    def select(n_static):
        def bit_step(n, thr_key):
            cand = thr_key + lax.shift_left(jnp.int32(1), KEY_BITS - 1 - n)
            thr = key_to_f32(cand)
            cnt = jnp.zeros((tq, LANES), F32)
            for cb in range(n_static * n_cb):
                s = score_ref[:, cb * LANES:(cb + 1) * LANES]
                cnt = cnt + jnp.where(s >= thr, 1.0, 0.0)
            total = jnp.sum(cnt, axis=1, keepdims=True)
            return jnp.where(total >= k_f, cand, thr_key)

        thr_key = lax.fori_loop(0, KEY_BITS, bit_step,
                                jnp.full((tq, LANES), INT32_MIN, jnp.int32), unroll=SELECT_UNROLL)
        thr_ref[...] = key_to_f32(thr_key)

    for n_static in range(1, max_chunks + 1):
        pl.when(n_chunks == n_static)(functools.partial(select, n_static))

    def bias_chunk(c, cnt):
        off = pl.multiple_of(c * tk, tk)
        thr = thr_ref[...]
        causal = k_iota + off <= t_pos
        for cb in range(n_cb):
            cols = pl.ds(off + cb * LANES, LANES)
            sel = score_ref[:, cols] >= thr
            bias_ref[:, cols] = jnp.where(
                sel, jnp.where(causal[:, cb * LANES:(cb + 1) * LANES], 0.0, NEG), NEG)
            cnt = cnt + jnp.where(sel, 1.0, 0.0)
        return cnt

    n_ge = jnp.sum(lax.fori_loop(0, n_chunks, bias_chunk, jnp.zeros((tq, LANES), F32)),
                   axis=1, keepdims=True)

    @pl.when(jnp.max(n_ge) > k_f)
    def _():
        def count_greater(c, cnt):
            off = pl.multiple_of(c * tk, tk)
            thr = thr_ref[...]
            for cb in range(n_cb):
                s = score_ref[:, pl.ds(off + cb * LANES, LANES)]
                cnt = cnt + jnp.where(s > thr, 1.0, 0.0)
            return cnt

        n_greater = jnp.sum(lax.fori_loop(0, n_chunks, count_greater,
                                          jnp.zeros((tq, LANES), F32)), axis=1, keepdims=True)
        n_take = k_f - n_greater
        before = jnp.where(
            lax.broadcasted_iota(jnp.int32, (tk, tk), 0) < lax.broadcasted_iota(jnp.int32, (tk, tk), 1),
            1.0, 0.0).astype(BF16)

        def tie_chunk(c, seen):
            off = pl.multiple_of(c * tk, tk)
            sc = score_ref[:, pl.ds(off, tk)]
            thr = jnp.concatenate([thr_ref[...]] * n_cb, axis=1)
            eq = jnp.where(sc == thr, 1.0, 0.0)
            rank = _dot(eq.astype(BF16), before) + seen
            keep = jnp.where(sc > thr, 1.0, jnp.where(rank < n_take, eq, 0.0))
            causal = k_iota + off <= t_pos
            bias_ref[:, pl.ds(off, tk)] = jnp.where(causal, jnp.where(keep > 0.0, 0.0, NEG), NEG)
            return seen + jnp.sum(eq, axis=1, keepdims=True)

        lax.fori_loop(0, n_chunks, tie_chunk, jnp.zeros((tq, 1), F32))

    rows = KV_GROUP * tq
    ones = jnp.ones((tk, LANES), BF16)

    def attend(off, first):
        b = bias_ref[:, pl.ds(off, tk)]
        for g in range(N_KV_HEADS):
            kc = k_ref[pl.ds(off, tk), g * HEAD_DIM:(g + 1) * HEAD_DIM]
            vc = v_ref[pl.ds(off, tk), g * HEAD_DIM:(g + 1) * HEAD_DIM]
            s = _dot_t(qg_ref[g], kc)
            s = (s.reshape(KV_GROUP, tq, tk) + b[None]).reshape(rows, tk)
            m_new = jnp.broadcast_to(jnp.max(s, axis=1, keepdims=True), (rows, LANES))
            if not first:
                m_old = m_ref[g]
                m_new = jnp.maximum(m_old, m_new)
            p = jnp.exp2(s - jnp.concatenate([m_new] * n_cb, axis=1))
            pv = _dot(p.astype(BF16), jnp.concatenate([vc, ones], axis=1))
            if first:
                acc_ref[g] = pv
            else:
                alpha = jnp.exp2(m_old - m_new)
                acc_ref[g] = jnp.concatenate([alpha, alpha], axis=1) * acc_ref[g] + pv
            m_ref[g] = m_new

    attend(0, True)

    def attn_chunk(c, carry):
        attend(pl.multiple_of(c * tk, tk), False)
        return carry

    lax.fori_loop(1, n_chunks, attn_chunk, 0)

    for g in range(N_KV_HEADS):
        out = acc_ref[g, :, 0:HEAD_DIM] / acc_ref[g, :, HEAD_DIM:2 * HEAD_DIM]
        for r in range(KV_GROUP):
            hd = g * KV_GROUP + r
            attn_ref[:, hd * HEAD_DIM:(hd + 1) * HEAD_DIM] = out[r * tq:(r + 1) * tq].astype(BF16)

    gate = ada_ref[3 * sub + 2:3 * sub + 3, :]
    attn = attn_ref[...]
    wide = 4 * LANES
    for n in range(o_ref.shape[1] // wide):
        cols = slice(n * wide, (n + 1) * wide)
        o_ref[:, cols] = h_ref[:, cols] + gate[:, cols] * _dot(attn, wobf_ref[:, cols])


def _dsa_attn(main, kk, wi, h, ada4, w_out, widx, layer, sub, batch, seq, tq, tk, top_k):
    t, d = h.shape
    nq = seq // tq
    rows = KV_GROUP * tq
    return pl.pallas_call(
        functools.partial(_dsa_attn_kernel, tq, tk, top_k, sub, widx),
        grid=(batch, nq),
        in_specs=[
            pl.BlockSpec((tq, Q_W), lambda b, i: (b * nq + i, 0)),
            pl.BlockSpec((tq, QI_W), lambda b, i: (b * nq + i, (Q_W + 2 * KV_W) // QI_W)),
            pl.BlockSpec((tq, LANES), lambda b, i: (b * nq + i, 0)),
            pl.BlockSpec((seq, KV_W), lambda b, i: (b, Q_W // KV_W)),
            pl.BlockSpec((seq, KV_W), lambda b, i: (b, Q_W // KV_W + 1)),
            pl.BlockSpec((seq, 2 * LANES), lambda b, i: (b, 0)),
            pl.BlockSpec((tq, d), lambda b, i: (b * nq + i, 0)),
            pl.BlockSpec((None, None, 3 * N_SUBLAYERS, d), lambda b, i: (layer, b, 0, 0)),
            pl.BlockSpec(memory_space=pl.ANY),
        ],
        out_specs=pl.BlockSpec((tq, d), lambda b, i: (b * nq + i, 0)),
        out_shape=jax.ShapeDtypeStruct((t, d), F32),
        scratch_shapes=[
            pltpu.VMEM((tq, seq), F32),
            pltpu.VMEM((tq, seq), F32),
            pltpu.VMEM((tq, LANES), F32),
            pltpu.VMEM((IDX_HEADS, tq, LANES), F32),
            pltpu.VMEM((N_KV_HEADS, rows, HEAD_DIM), BF16),
            pltpu.VMEM((N_KV_HEADS, rows, LANES), F32),
            pltpu.VMEM((N_KV_HEADS, rows, 2 * HEAD_DIM), F32),
            pltpu.VMEM((tq, Q_W), BF16),
            pltpu.VMEM((Q_W, d), BF16),
            pltpu.VMEM((2, 2 * LANES, d), F32),
            pltpu.SemaphoreType.DMA((2,)),
        ],
        compiler_params=_cparams(("arbitrary", "arbitrary")),
        name="dsa_attn",
    )(main, main, wi, main, main, kk, h, ada4, w_out)


def _tiles(seq, d_ff):
    tm = min(1024, seq)
    tm_ffn = min(2048, seq)
    tf = 256 if d_ff % 256 == 0 else d_ff
    tq = min(256, seq)
    tk = min(256, seq)
    return tm, tm_ffn, tf, tq, tk


def kernel(x, c, positions, ada_w, ada_b, norm_g, final_g, ffn_wgu, ffn_wd,
           pool_w_in, pool_w_grp, pool_scale, pool_w_out, dsa_w_in, dsa_w_out):
    batch, seq, d = x.shape
    depth = ada_w.shape[0]
    d_ff = ffn_wd.shape[2]
    t = batch * seq
    tm, tm_ffn, tf, tq, tk = _tiles(seq, d_ff)
    tn = min(512, d)
    top_k = min(TOPK_MAX, seq // 4)
    assert seq % tm == 0 and seq % tm_ffn == 0 and seq % tq == 0 and seq % tk == 0 and tk >= top_k
    assert dsa_w_in.shape[2] == DSA_MAIN_W + DSA_SMALL_W and DSA_MAIN_W % tn == 0

    rows = -(-batch // SUBLANES) * SUBLANES
    c8 = jnp.pad(c, ((0, rows - batch), (0, 0)))
    ada_n = ada_w.shape[2]
    ada, tabs = _ada(c8, ada_w, ada_b, 1024 if ada_n % 1024 == 0 else d, positions)
    ada4 = ada.reshape(depth, rows, 3 * N_SUBLAYERS, d)

    h = x.reshape(t, d)
    h = _ffn(h, ada4, norm_g, ffn_wgu, ffn_wd, [(0, 0, 0)], seq, tm_ffn, tf)
    for layer in range(depth):
        j = layer // 2
        g_row = norm_g[layer, 1][None, :]
        if layer % 2 == 0:
            u = _norm_linear(h, ada4, g_row, pool_w_in, j, layer, 1, seq, tm, tn)
            a = _pool_core(u, pool_w_grp, pool_scale, j, batch, seq)
            h = _linear_residual(a, pool_w_out, j, h, ada4, layer, 1, seq, tm, tn)
        else:
            w_small = _dsa_small_weights(dsa_w_in[j, :, DSA_MAIN_W:])
            main, kk, wi = _dsa_proj(h, ada4, g_row, jnp.swapaxes(dsa_w_in, 1, 2), j, w_small,
                                     tabs, layer, 1, seq, tm, tn)
            h = _dsa_attn(main, kk, wi, h, ada4, dsa_w_out, j, layer, 1, batch, seq, tq, tk, top_k)
        if layer == depth - 1:
            h = _ffn(h, ada4, norm_g, ffn_wgu, ffn_wd, [(layer, 1, 2)], seq, tm_ffn, tf,
                     final_g=final_g)
        else:
            h = _ffn(h, ada4, norm_g, ffn_wgu, ffn_wd, [(layer, 1, 2), (layer + 1, 0, 0)],
                     seq, tm_ffn, tf)
    return h.reshape(batch, seq, d)
```

```python
import functools

import jax
import jax.numpy as jnp
import numpy as np
from jax import lax
from jax.experimental import pallas as pl
from jax.experimental.pallas import tpu as pltpu

F32 = jnp.float32
BF16 = jnp.bfloat16

N_SUBLAYERS = 3
POOL_WINDOWS = (2, 4, 8, 16)
N_HEADS = 16
N_KV_HEADS = 4
KV_GROUP = N_HEADS // N_KV_HEADS
HEAD_DIM = 128
ROT_DIM = HEAD_DIM // 4
ROPE_THETA = 500000.0
IDX_HEADS = 16
IDX_DIM = 64
IDX_ROT_DIM = IDX_DIM // 4
TOPK_MAX = 256
EPS = 1e-6
NEG = -1e30
Q_W = N_HEADS * HEAD_DIM
KV_W = N_KV_HEADS * HEAD_DIM
QI_W = IDX_HEADS * IDX_DIM
DSA_MAIN_W = Q_W + 2 * KV_W + QI_W
DSA_SMALL_W = IDX_DIM + IDX_HEADS
Q_PRESCALE = HEAD_DIM ** -0.5 * 1.4426950408889634

LANES = 128
SUBLANES = 8
VMEM_LIMIT_BYTES = 56 * 1024 * 1024
INT32_MIN = -(2 ** 31)
KEY_BITS = 32
SELECT_UNROLL = 8
N_ROPE_TABLES = 8


def _cparams(semantics):
    return pltpu.CompilerParams(dimension_semantics=semantics,
                                vmem_limit_bytes=VMEM_LIMIT_BYTES)


def _rmsnorm(x, g):
    ms = jnp.mean(x * x, axis=-1, keepdims=True)
    return x * lax.rsqrt(ms + EPS) * g


def _rmsnorm_rows(ref, rows, g_ref):
    ref[rows, :] = _rmsnorm(ref[rows, :], g_ref[...])


NORM_SLAB = 16
NORM_UNROLL = 8
FFN_ROW_SLAB = 128
FFN_SLAB_SLOTS = 4
FFN_DOWN_COLS = 256


def _for_row_slabs(n_rows, slab, body, unroll=1):
    def step(r, carry):
        body(pl.ds(pl.multiple_of(r * slab, slab), slab))
        return carry
    lax.fori_loop(0, n_rows // slab, step, 0, unroll=unroll)


def _write_norm_mod(x_ref, g_ref, m_ref, sub, y_ref, copy_ref=None):
    shift = m_ref[3 * sub + 0:3 * sub + 1, :]
    gs = g_ref[...] * (1.0 + m_ref[3 * sub + 1:3 * sub + 2, :])
    slab = min(NORM_SLAB, x_ref.shape[0])

    def norm_rows(rows):
        x = x_ref[rows, :]
        r = lax.rsqrt(jnp.mean(x * x, axis=-1, keepdims=True) + EPS)
        y_ref[rows, :] = ((x * r) * gs + shift).astype(BF16)
        if copy_ref is not None:
            copy_ref[rows, :] = x

    _for_row_slabs(x_ref.shape[0], slab, norm_rows, unroll=NORM_UNROLL)


def _silu(x):
    return x * jax.nn.sigmoid(x)


def _dot(a, b):
    return jnp.dot(a, b, preferred_element_type=F32)


def _dot_t(a, b):
    return lax.dot_general(a, b, (((1,), (1,)), ((), ())), preferred_element_type=F32)


def _ada_kernel(n_tab_blocks, c_ref, w_ref, b_ref, pos_ref, invf_ref, o_ref, tab_ref):
    sc = _silu(c_ref[...]).astype(BF16)
    o_ref[...] = _dot(sc, w_ref[...].astype(BF16)) + b_ref[...]
    step = pl.program_id(0) * pl.num_programs(1) + pl.program_id(1)

    @pl.when(step < n_tab_blocks)
    def _():
        _write_rope_tables(pos_ref, invf_ref, tab_ref)


def _ada(c8, ada_w, ada_b, tn, positions):
    depth, d, n = ada_w.shape
    rows = c8.shape[0]
    n_steps = depth * (n // tn)
    t = positions.size
    rt = SUBLANES
    while t // rt > n_steps or t % rt:
        rt *= 2
    n_tab = t // rt
    nj = n // tn

    def tab_block(l, j):
        return jnp.minimum(l * nj + j, n_tab - 1)

    return pl.pallas_call(
        functools.partial(_ada_kernel, n_tab),
        grid=(depth, nj),
        in_specs=[
            pl.BlockSpec((rows, d), lambda l, j: (0, 0)),
            pl.BlockSpec((None, d, tn), lambda l, j: (l, 0, j)),
            pl.BlockSpec((None, 1, tn), lambda l, j: (l, 0, j)),
            pl.BlockSpec((rt, 1), lambda l, j: (tab_block(l, j), 0)),
            pl.BlockSpec((1, LANES), lambda l, j: (0, 0)),
        ],
        out_specs=[
            pl.BlockSpec((None, rows, tn), lambda l, j: (l, 0, j)),
            pl.BlockSpec((N_ROPE_TABLES, rt, LANES), lambda l, j: (0, tab_block(l, j), 0)),
        ],
        out_shape=[
            jax.ShapeDtypeStruct((depth, rows, n), F32),
            jax.ShapeDtypeStruct((N_ROPE_TABLES, t, LANES), F32),
        ],
        compiler_params=_cparams(("arbitrary", "arbitrary")),
        name="ada",
    )(c8, ada_w, ada_b.reshape(depth, 1, n), positions.reshape(t, 1), _rope_inv_freq_row())


def _select_static(values, stage):
    out = values[-1]
    for n in range(len(values) - 2, -1, -1):
        out = jnp.where(stage == n, values[n], out)
    return out


def _ffn_kernel(subs, final, rs, nk, *refs):
    if final:
        (x_hbm, g_ref, m_ref, wg_ref, wu_ref, wd_ref, fg_ref, o_hbm,
         acc_ref, y_ref, xbuf, in_sem, out_sem) = refs
    else:
        (x_hbm, g_ref, m_ref, wg_ref, wu_ref, wd_ref, o_hbm,
         acc_ref, y_ref, xbuf, in_sem, out_sem) = refs
    k = pl.program_id(1)
    last_step = pl.num_programs(1) - 1
    sub = _select_static(subs, k // nk)
    tm, d_out = acc_ref.shape
    n_slabs = tm // rs
    n_slots = xbuf.shape[0]
    n_first = min(n_slots, n_slabs)
    row0 = pl.program_id(0) * tm

    def x_copy(s, tile_row0=row0):
        slot = s % n_slots
        return pltpu.make_async_copy(x_hbm.at[pl.ds(tile_row0 + s * rs, rs), :], xbuf.at[slot],
                                     in_sem.at[slot])

    def o_copy(s):
        return pltpu.make_async_copy(acc_ref.at[pl.ds(s * rs, rs), :],
                                     o_hbm.at[pl.ds(row0 + s * rs, rs), :], out_sem.at[s])

    @pl.when(k == 0)
    def _():
        @pl.when(pl.program_id(0) == 0)
        def _():
            for s in range(n_first):
                x_copy(s).start()

        for s in range(n_slabs):
            x_copy(s).wait()
            pl.when(pl.program_id(0) > 0)(o_copy(s).wait)
            rows = pl.ds(s * rs, rs)
            _write_norm_mod(xbuf.at[s % n_slots], g_ref.at[pl.ds(0, 1), :], m_ref, subs[0],
                            y_ref.at[rows, :], copy_ref=acc_ref.at[rows, :])
            if s + n_slots < n_slabs:
                x_copy(s + n_slots).start()

    for stage in range(1, len(subs)):
        @pl.when(k == stage * nk)
        def _(stage=stage):
            _write_norm_mod(acc_ref, g_ref.at[pl.ds(stage, 1), :], m_ref, subs[stage], y_ref)

    @pl.when(jnp.logical_and(k == last_step, pl.program_id(0) + 1 < pl.num_programs(0)))
    def _():
        for s in range(n_first):
            x_copy(s, row0 + tm).start()

    y = y_ref[...]
    g = _dot(y, wg_ref[...].astype(BF16))
    u = _dot(y, wu_ref[...].astype(BF16))
    a = (_silu(g) * u).astype(BF16)
    half_gate = 0.5 * m_ref[pl.ds(3 * sub + 2, 1), :]
    nc = min(FFN_DOWN_COLS, d_out)
    for n in range(d_out // nc):
        cols = slice(n * nc, (n + 1) * nc)
        acc_ref[:, cols] += half_gate[:, cols] * _dot(a, wd_ref[:, cols].astype(BF16))

    @pl.when(k == last_step)
    def _():
        for s in range(n_slabs):
            if final:
                _for_row_slabs(
                    rs, min(NORM_SLAB, rs),
                    lambda rows, s=s: _rmsnorm_rows(acc_ref.at[pl.ds(s * rs, rs), :], rows, fg_ref),
                    unroll=NORM_UNROLL)
            o_copy(s).start(priority=1)

        @pl.when(pl.program_id(0) == pl.num_programs(0) - 1)
        def _():
            for s in range(n_slabs):
                o_copy(s).wait()


def _ffn(h, ada4, norm_g, ffn_wgu, ffn_wd, stages, seq, tm, tf, final_g=None):
    t, d = h.shape
    d_ff = ffn_wd.shape[2]
    nk = d_ff // tf
    tpb = seq // tm
    rs = min(FFN_ROW_SLAB, tm)
    final = final_g is not None
    layers, whichs, subs = zip(*stages)

    def layer_of(k):
        return _select_static(layers, k // nk)

    def which_of(k):
        return _select_static(whichs, k // nk)

    in_specs = [
        pl.BlockSpec(memory_space=pl.ANY),
        pl.BlockSpec((len(stages), d), lambda i, k: (0, 0)),
        pl.BlockSpec((None, None, 3 * N_SUBLAYERS, d), lambda i, k: (layer_of(k), i // tpb, 0, 0)),
        pl.BlockSpec((None, None, d, tf), lambda i, k: (layer_of(k), which_of(k), 0, k % nk)),
        pl.BlockSpec((None, None, d, tf), lambda i, k: (layer_of(k), which_of(k), 0, nk + k % nk)),
        pl.BlockSpec((None, None, tf, d), lambda i, k: (layer_of(k), which_of(k), k % nk, 0)),
    ]
    g_rows = jnp.stack([norm_g[layer, sub] for layer, _, sub in stages])
    args = [h, g_rows, ada4, ffn_wgu, ffn_wgu, ffn_wd]
    if final:
        in_specs.append(pl.BlockSpec((1, d), lambda i, k: (0, 0)))
        args.append(final_g[None, :])
    layer, which = layers[0], whichs[0]
    return pl.pallas_call(
        functools.partial(_ffn_kernel, subs, final, rs, nk),
        grid=(t // tm, len(stages) * nk),
        in_specs=in_specs,
        out_specs=pl.BlockSpec(memory_space=pl.ANY),
        out_shape=jax.ShapeDtypeStruct((t, d), F32),
        scratch_shapes=[
            pltpu.VMEM((tm, d), F32),
            pltpu.VMEM((tm, d), BF16),
            pltpu.VMEM((FFN_SLAB_SLOTS, rs, d), F32),
            pltpu.SemaphoreType.DMA((FFN_SLAB_SLOTS,)),
            pltpu.SemaphoreType.DMA((tm // rs,)),
        ],
        compiler_params=_cparams(("arbitrary", "arbitrary")),
        name=f"ffn_l{layer}_{which}",
    )(*args)


def _keep_bf16_weight(w_ref, wbf_ref):
    j = pl.program_id(1)

    @pl.when(pl.program_id(0) == 0)
    def _():
        wbf_ref[j] = w_ref[...].astype(BF16)

    return wbf_ref[j]


def _resident_weight_spec(widx, kdim, tn, nj):
    return pl.BlockSpec((None, kdim, tn), lambda i, j: (widx, 0, jnp.where(i == 0, j, nj - 1)))


def _norm_linear_kernel(sub, x_ref, g_ref, m_ref, w_ref, o_ref, y_ref, wbf_ref):
    @pl.when(pl.program_id(1) == 0)
    def _():
        _write_norm_mod(x_ref, g_ref, m_ref, sub, y_ref)

    o_ref[...] = _dot(y_ref[...], _keep_bf16_weight(w_ref, wbf_ref))


def _norm_linear(h, ada4, g_row, w, widx, layer, sub, seq, tm, tn):
    t, d = h.shape
    n = w.shape[2]
    tpb = seq // tm
    return pl.pallas_call(
        functools.partial(_norm_linear_kernel, sub),
        grid=(t // tm, n // tn),
        in_specs=[
            pl.BlockSpec((tm, d), lambda i, j: (i, 0)),
            pl.BlockSpec((1, d), lambda i, j: (0, 0)),
            pl.BlockSpec((None, None, 3 * N_SUBLAYERS, d), lambda i, j: (layer, i // tpb, 0, 0)),
            _resident_weight_spec(widx, d, tn, n // tn),
        ],
        out_specs=pl.BlockSpec((tm, tn), lambda i, j: (i, j)),
        out_shape=jax.ShapeDtypeStruct((t, n), F32),
        scratch_shapes=[pltpu.VMEM((tm, d), BF16), pltpu.VMEM((n // tn, d, tn), BF16)],
        compiler_params=_cparams(("arbitrary", "arbitrary")),
        name=f"norm_linear_l{layer}",
    )(h, g_row, ada4, w)


def _pool_kernel(u_ref, w_ref, ls_ref, o_ref):
    g = pl.program_id(1)
    u = u_ref[...]
    t = lax.broadcasted_iota(jnp.int32, u.shape, 0)

    def shifted(v, d):
        return jnp.where(t >= d, pltpu.roll(v, d, 0), 0.0)

    for gi, win in enumerate(POOL_WINDOWS):
        @pl.when(g == gi)
        def _(win=win):
            acc = u
            d = 1
            while d < win:
                acc = acc + shifted(acc, d)
                d *= 2
            cnt = jnp.minimum(t + 1, win).astype(F32)
            p = (acc / cnt - u).astype(BF16)
            v = _dot(p, w_ref[...].astype(BF16)) * ls_ref[...]
            o_ref[...] = v.astype(BF16)


def _pool_core(u, w_grp, ls, widx, batch, seq):
    t, d = u.shape
    _, groups, c, _ = w_grp.shape
    return pl.pallas_call(
        _pool_kernel,
        grid=(batch, groups),
        in_specs=[
            pl.BlockSpec((seq, c), lambda b, g: (b, g)),
            pl.BlockSpec((None, None, c, c), lambda b, g: (widx, g, 0, 0)),
            pl.BlockSpec((None, 1, c), lambda b, g: (widx, 0, g)),
        ],
        out_specs=pl.BlockSpec((seq, c), lambda b, g: (b, g)),
        out_shape=jax.ShapeDtypeStruct((t, d), BF16),
        compiler_params=_cparams(("arbitrary", "arbitrary")),
        name="pool_core",
    )(u, w_grp, ls.reshape(ls.shape[0], 1, d))


def _linear_residual_kernel(sub, a_ref, w_ref, h_ref, m_ref, o_ref, wbf_ref):
    gate = m_ref[3 * sub + 2:3 * sub + 3, :]
    o_ref[...] = h_ref[...] + gate * _dot(a_ref[...], _keep_bf16_weight(w_ref, wbf_ref))


def _linear_residual(a, w, widx, h, ada4, layer, sub, seq, tm, tn):
    t, kdim = a.shape
    n = w.shape[2]
    tpb = seq // tm
    return pl.pallas_call(
        functools.partial(_linear_residual_kernel, sub),
        grid=(t // tm, n // tn),
        in_specs=[
            pl.BlockSpec((tm, kdim), lambda i, j: (i, 0)),
            _resident_weight_spec(widx, kdim, tn, n // tn),
            pl.BlockSpec((tm, tn), lambda i, j: (i, j)),
            pl.BlockSpec((None, None, 3 * N_SUBLAYERS, tn), lambda i, j: (layer, i // tpb, 0, j)),
        ],
        out_specs=pl.BlockSpec((tm, tn), lambda i, j: (i, j)),
        out_shape=jax.ShapeDtypeStruct((t, n), F32),
        scratch_shapes=[pltpu.VMEM((n // tn, kdim, tn), BF16)],
        compiler_params=_cparams(("arbitrary", "arbitrary")),
        name=f"linear_residual_l{layer}",
    )(a, w, h, ada4)


ROPE_SET_Q, ROPE_SET_K, ROPE_SET_NONE, ROPE_SET_IDX = 0, 1, 2, 3
HALF_LANES = LANES // 2


def _write_rope_tables(pos_ref, invf_ref, o_ref):
    pos = pos_ref[...].astype(F32)
    lane = lax.broadcasted_iota(jnp.int32, (pos.shape[0], LANES), 1)
    ang = pos * invf_ref[...]
    cos = jnp.cos(ang)
    sin = jnp.where(lane < HALF_LANES, -jnp.sin(ang), jnp.sin(ang))
    rot_big = (lane & (HALF_LANES - 1)) < ROT_DIM // 2
    cb = jnp.where(rot_big, cos, 1.0)
    sb = jnp.where(rot_big, sin, 0.0)
    rot_idx = (lane & (HALF_LANES // 2 - 1)) < IDX_ROT_DIM // 2
    even_quarter = (lane & (HALF_LANES // 2)) == 0
    shift = ROT_DIM // 2

    def idx_table(v, rest):
        from_above = pltpu.roll(v, LANES - shift, 1)
        from_below = pltpu.roll(v, shift, 1)
        return jnp.where(rot_idx, jnp.where(even_quarter, from_above, from_below), rest)

    ci = idx_table(cos, 1.0)
    si = idx_table(sin, 0.0)
    o_ref[2 * ROPE_SET_Q] = cb * Q_PRESCALE
    o_ref[2 * ROPE_SET_Q + 1] = sb * Q_PRESCALE
    o_ref[2 * ROPE_SET_K] = cb
    o_ref[2 * ROPE_SET_K + 1] = sb
    o_ref[2 * ROPE_SET_NONE] = jnp.ones_like(cb)
    o_ref[2 * ROPE_SET_NONE + 1] = jnp.zeros_like(cb)
    o_ref[2 * ROPE_SET_IDX] = ci
    o_ref[2 * ROPE_SET_IDX + 1] = si


def _rope_inv_freq_row():
    def inv_freq(rot_dim):
        half = rot_dim // 2
        return ROPE_THETA ** (-jnp.arange(half, dtype=F32) / half)

    half_row = jnp.concatenate([
        inv_freq(ROT_DIM), inv_freq(IDX_ROT_DIM),
        jnp.zeros((HALF_LANES - ROT_DIM // 2 - IDX_ROT_DIM // 2,), F32)])
    return jnp.tile(half_row, 2)[None, :]


def _rope(x, c, s):
    return x * c + pltpu.roll(x, HALF_LANES, 1) * s


PERM_BIG, PERM_NONE, PERM_IDX = 0, 1, 2


def _lane_permutations():
    h, p = ROT_DIM // 2, (HEAD_DIM - ROT_DIM) // 2
    big = np.concatenate([np.arange(0, h), np.arange(ROT_DIM, ROT_DIM + p),
                          np.arange(h, ROT_DIM), np.arange(ROT_DIM + p, HEAD_DIM)])
    lo, hi = _idx_lo_hi_dims()
    idx = np.concatenate([lo, IDX_DIM + lo, hi, IDX_DIM + hi])
    mats = np.zeros((3, LANES, LANES), np.float32)
    for kind, src in ((PERM_BIG, big), (PERM_NONE, np.arange(LANES)), (PERM_IDX, idx)):
        mats[kind, np.arange(LANES), src] = 1.0
    return jnp.asarray(mats, BF16)


def _idx_lo_hi_dims():
    h, p = IDX_ROT_DIM // 2, (IDX_DIM - IDX_ROT_DIM) // 2
    lo = np.concatenate([np.arange(0, h), np.arange(IDX_ROT_DIM, IDX_ROT_DIM + p)])
    hi = np.concatenate([np.arange(h, IDX_ROT_DIM), np.arange(IDX_ROT_DIM + p, IDX_DIM)])
    return lo, hi


def _dsa_small_weights(w_tail):
    h, p = IDX_ROT_DIM // 2, (IDX_DIM - IDX_ROT_DIM) // 2
    klo = jnp.concatenate([w_tail[:, 0:h], w_tail[:, IDX_ROT_DIM:IDX_ROT_DIM + p]], axis=1)
    khi = jnp.concatenate([w_tail[:, h:IDX_ROT_DIM], w_tail[:, IDX_ROT_DIM + p:IDX_DIM]], axis=1)
    wi = jnp.pad(w_tail[:, IDX_DIM:], ((0, 0), (0, LANES - IDX_DIM - IDX_HEADS)))
    return jnp.concatenate([klo, khi, wi], axis=1).astype(BF16)


def _dsa_proj_kernel(sub, tn, x_hbm, g_ref, m_ref, w_ref, ws_ref, perm_ref, tab_ref,
                     o_ref, ok_ref, ow_ref, y_ref, wbf_ref, x_ref, x_sem):
    i = pl.program_id(0)
    j = pl.program_id(1)
    tm = x_ref.shape[0]
    first_k, first_v, first_qi = Q_W // tn, (Q_W + KV_W) // tn, (Q_W + 2 * KV_W) // tn

    def x_copy(tile):
        return pltpu.make_async_copy(x_hbm.at[pl.ds(tile * tm, tm), :], x_ref, x_sem.at[0])

    @pl.when(j == 0)
    def _():
        pl.when(i == 0)(x_copy(i).start)
        x_copy(i).wait()
        _write_norm_mod(x_ref, g_ref, m_ref, sub, y_ref)
        pl.when(i + 1 < pl.num_programs(0))(x_copy(i + 1).start)
        small = _dot(y_ref[...], ws_ref[...])
        by32 = pltpu.roll(small, HALF_LANES // 2, 1)
        by64 = pltpu.roll(small, HALF_LANES, 1)
        quarter = lax.broadcasted_iota(jnp.int32, small.shape, 1) // (HALF_LANES // 2)
        k_even = jnp.where(quarter == 0, small, jnp.where(quarter == 2, by32, 0.0))
        k_odd = jnp.where(quarter == 1, by32, jnp.where(quarter == 3, by64, 0.0))
        for s, kk in enumerate((k_even, k_odd)):
            ok_ref[:, s * LANES:(s + 1) * LANES] = _rope(
                kk, tab_ref[2 * ROPE_SET_IDX], tab_ref[2 * ROPE_SET_IDX + 1]).astype(BF16)
        ow_ref[...] = jnp.where(quarter == 0, by64, 0.0)

    @pl.when(pl.program_id(0) == 0)
    def _():
        kind = jnp.where(j < first_v, PERM_BIG, jnp.where(j < first_qi, PERM_NONE, PERM_IDX))
        perm = perm_ref[kind]
        for sl in range(tn // LANES):
            rows = slice(sl * LANES, (sl + 1) * LANES)
            wbf_ref[j, rows, :] = _dot(perm, w_ref[rows, :].astype(BF16)).astype(BF16)

    rope_set = jnp.where(j < first_k, ROPE_SET_Q,
                         jnp.where(j < first_v, ROPE_SET_K,
                                   jnp.where(j < first_qi, ROPE_SET_NONE, ROPE_SET_IDX)))
    c = tab_ref[2 * rope_set]
    s = tab_ref[2 * rope_set + 1]
    y = y_ref[...]
    wide = 2 * LANES
    for part in range(tn // wide):
        acc = _dot_t(y, wbf_ref[j, part * wide:(part + 1) * wide, :])
        for sl in range(wide // LANES):
            cols = slice(part * wide + sl * LANES, part * wide + (sl + 1) * LANES)
            o_ref[:, cols] = _rope(acc[:, sl * LANES:(sl + 1) * LANES], c, s).astype(BF16)


def _dsa_proj(h, ada4, g_row, w_in, widx, w_small, tabs, layer, sub, seq, tm, tn):
    t, d = h.shape
    tpb = seq // tm
    return pl.pallas_call(
        functools.partial(_dsa_proj_kernel, sub, tn),
        grid=(t // tm, DSA_MAIN_W // tn),
        in_specs=[
            pl.BlockSpec(memory_space=pl.ANY),
            pl.BlockSpec((1, d), lambda i, j: (0, 0)),
            pl.BlockSpec((None, None, 3 * N_SUBLAYERS, d), lambda i, j: (layer, i // tpb, 0, 0)),
            pl.BlockSpec((None, tn, d),
                         lambda i, j: (widx, jnp.where(i == 0, j, DSA_MAIN_W // tn - 1), 0)),
            pl.BlockSpec((d, LANES), lambda i, j: (0, 0)),
            pl.BlockSpec((3, LANES, LANES), lambda i, j: (0, 0, 0)),
            pl.BlockSpec((N_ROPE_TABLES, tm, LANES), lambda i, j: (0, i, 0)),
        ],
        out_specs=[
            pl.BlockSpec((tm, tn), lambda i, j: (i, j)),
            pl.BlockSpec((tm, 2 * LANES), lambda i, j: (i, 0)),
            pl.BlockSpec((tm, LANES), lambda i, j: (i, 0)),
        ],
        out_shape=[
            jax.ShapeDtypeStruct((t, DSA_MAIN_W), BF16),
            jax.ShapeDtypeStruct((t, 2 * LANES), BF16),
            jax.ShapeDtypeStruct((t, LANES), F32),
        ],
        scratch_shapes=[pltpu.VMEM((tm, d), BF16),
                        pltpu.VMEM((DSA_MAIN_W // tn, tn, d), BF16),
                        pltpu.VMEM((tm, d), F32),
                        pltpu.SemaphoreType.DMA((1,))],
        compiler_params=_cparams(("arbitrary", "arbitrary")),
        name=f"dsa_proj_l{layer}",
    )(h, g_row, ada4, w_in, w_small, _lane_permutations(), tabs)


def _dsa_attn_kernel(tq, tk, top_k, sub, widx, q_ref, qi_ref, wi_ref, k_ref, v_ref, kk_ref,
                     h_ref, ada_ref, wo_ref, o_ref,
                     score_ref, bias_ref, thr_ref, wib_ref, qg_ref, m_ref, acc_ref,
                     attn_ref, wobf_ref, wo_stage, wo_sem):
    i = pl.program_id(1)

    @pl.when(jnp.logical_and(pl.program_id(0) == 0, i == 0))
    def _():
        slab = wo_stage.shape[1]
        n_slabs = wobf_ref.shape[0] // slab

        def wo_copy(r):
            return pltpu.make_async_copy(wo_ref.at[widx, pl.ds(r * slab, slab), :],
                                         wo_stage.at[r % 2], wo_sem.at[r % 2])

        wo_copy(0).start()
        for r in range(n_slabs):
            if r + 1 < n_slabs:
                wo_copy(r + 1).start()
            wo_copy(r).wait()
            wobf_ref[r * slab:(r + 1) * slab, :] = wo_stage[r % 2].astype(BF16)

    n_chunks = ((i + 1) * tq + tk - 1) // tk
    max_chunks = score_ref.shape[1] // tk
    t_pos = i * tq + lax.broadcasted_iota(jnp.int32, (tq, tk), 0)
    k_iota = lax.broadcasted_iota(jnp.int32, (tq, tk), 1)
    n_cb = tk // LANES

    for hh in range(IDX_HEADS):
        wib_ref[hh] = jnp.broadcast_to(wi_ref[:, hh:hh + 1], (tq, LANES))
    for g in range(N_KV_HEADS):
        for r in range(KV_GROUP):
            hd = g * KV_GROUP + r
            qg_ref[g, r * tq:(r + 1) * tq, :] = q_ref[:, hd * HEAD_DIM:(hd + 1) * HEAD_DIM]

    def score_chunk(c, carry):
        off = pl.multiple_of(c * tk, tk)
        k_even = kk_ref[pl.ds(off, tk), 0:LANES]
        k_odd = kk_ref[pl.ds(off, tk), LANES:2 * LANES]
        acc = [jnp.zeros((tq, LANES), F32) for _ in range(n_cb)]
        for pair in range(IDX_HEADS // 2):
            qp = qi_ref[:, pair * LANES:(pair + 1) * LANES]
            for par, kop in enumerate((k_even, k_odd)):
                r = _dot_t(qp, kop)
                w = wib_ref[2 * pair + par]
                for cb in range(n_cb):
                    acc[cb] = acc[cb] + w * jnp.maximum(r[:, cb * LANES:(cb + 1) * LANES], 0.0)
        sc = jnp.concatenate(acc, axis=1)
        score_ref[:, pl.ds(off, tk)] = jnp.where(k_iota + off <= t_pos, sc, NEG)
        return carry

    lax.fori_loop(0, n_chunks, score_chunk, 0)

    def key_to_f32(key):
        return pltpu.bitcast(key ^ ((key >> 31) & 0x7FFFFFFF), F32)

    k_f = float(top_k)

    def select(n_static):
        def bit_step(n, thr_key):
            cand = thr_key + lax.shift_left(jnp.int32(1), KEY_BITS - 1 - n)
            thr = key_to_f32(cand)
            cnt = jnp.zeros((tq, LANES), F32)
            for cb in range(n_static * n_cb):
                s = score_ref[:, cb * LANES:(cb + 1) * LANES]
                cnt = cnt + jnp.where(s >= thr, 1.0, 0.0)
            total = jnp.sum(cnt, axis=1, keepdims=True)
            return jnp.where(total >= k_f, cand, thr_key)

        thr_key = lax.fori_loop(0, KEY_BITS, bit_step,
                                jnp.full((tq, LANES), INT32_MIN, jnp.int32), unroll=SELECT_UNROLL)
        thr_ref[...] = key_to_f32(thr_key)

    for n_static in range(1, max_chunks + 1):
        pl.when(n_chunks == n_static)(functools.partial(select, n_static))

    def bias_chunk(c, cnt):
        off = pl.multiple_of(c * tk, tk)
        thr = thr_ref[...]
        causal = k_iota + off <= t_pos
        for cb in range(n_cb):
            cols = pl.ds(off + cb * LANES, LANES)
            sel = score_ref[:, cols] >= thr
            bias_ref[:, cols] = jnp.where(
                sel, jnp.where(causal[:, cb * LANES:(cb + 1) * LANES], 0.0, NEG), NEG)
            cnt = cnt + jnp.where(sel, 1.0, 0.0)
        return cnt

    n_ge = jnp.sum(lax.fori_loop(0, n_chunks, bias_chunk, jnp.zeros((tq, LANES), F32)),
                   axis=1, keepdims=True)

    @pl.when(jnp.max(n_ge) > k_f)
    def _():
        def count_greater(c, cnt):
            off = pl.multiple_of(c * tk, tk)
            thr = thr_ref[...]
            for cb in range(n_cb):
                s = score_ref[:, pl.ds(off + cb * LANES, LANES)]
                cnt = cnt + jnp.where(s > thr, 1.0, 0.0)
            return cnt

        n_greater = jnp.sum(lax.fori_loop(0, n_chunks, count_greater,
                                          jnp.zeros((tq, LANES), F32)), axis=1, keepdims=True)
        n_take = k_f - n_greater
        before = jnp.where(
            lax.broadcasted_iota(jnp.int32, (tk, tk), 0) < lax.broadcasted_iota(jnp.int32, (tk, tk), 1),
            1.0, 0.0).astype(BF16)

        def tie_chunk(c, seen):
            off = pl.multiple_of(c * tk, tk)
            sc = score_ref[:, pl.ds(off, tk)]
            thr = jnp.concatenate([thr_ref[...]] * n_cb, axis=1)
            eq = jnp.where(sc == thr, 1.0, 0.0)
            rank = _dot(eq.astype(BF16), before) + seen
            keep = jnp.where(sc > thr, 1.0, jnp.where(rank < n_take, eq, 0.0))
            causal = k_iota + off <= t_pos
            bias_ref[:, pl.ds(off, tk)] = jnp.where(causal, jnp.where(keep > 0.0, 0.0, NEG), NEG)
            return seen + jnp.sum(eq, axis=1, keepdims=True)

        lax.fori_loop(0, n_chunks, tie_chunk, jnp.zeros((tq, 1), F32))

    rows = KV_GROUP * tq
    ones = jnp.ones((tk, LANES), BF16)

    def attend(off, first):
        b = bias_ref[:, pl.ds(off, tk)]
        for g in range(N_KV_HEADS):
            kc = k_ref[pl.ds(off, tk), g * HEAD_DIM:(g + 1) * HEAD_DIM]
            vc = v_ref[pl.ds(off, tk), g * HEAD_DIM:(g + 1) * HEAD_DIM]
            s = _dot_t(qg_ref[g], kc)
            s = (s.reshape(KV_GROUP, tq, tk) + b[None]).reshape(rows, tk)
            m_new = jnp.broadcast_to(jnp.max(s, axis=1, keepdims=True), (rows, LANES))
            if not first:
                m_old = m_ref[g]
                m_new = jnp.maximum(m_old, m_new)
            p = jnp.exp2(s - jnp.concatenate([m_new] * n_cb, axis=1))
            pv = _dot(p.astype(BF16), jnp.concatenate([vc, ones], axis=1))
            if first:
                acc_ref[g] = pv
            else:
                alpha = jnp.exp2(m_old - m_new)
                acc_ref[g] = jnp.concatenate([alpha, alpha], axis=1) * acc_ref[g] + pv
            m_ref[g] = m_new

    attend(0, True)

    def attn_chunk(c, carry):
        attend(pl.multiple_of(c * tk, tk), False)
        return carry

    lax.fori_loop(1, n_chunks, attn_chunk, 0)

    for g in range(N_KV_HEADS):
        out = acc_ref[g, :, 0:HEAD_DIM] / acc_ref[g, :, HEAD_DIM:2 * HEAD_DIM]
        for r in range(KV_GROUP):
            hd = g * KV_GROUP + r
            attn_ref[:, hd * HEAD_DIM:(hd + 1) * HEAD_DIM] = out[r * tq:(r + 1) * tq].astype(BF16)

    gate = ada_ref[3 * sub + 2:3 * sub + 3, :]
    attn = attn_ref[...]
    wide = 4 * LANES
    for n in range(o_ref.shape[1] // wide):
        cols = slice(n * wide, (n + 1) * wide)
        o_ref[:, cols] = h_ref[:, cols] + gate[:, cols] * _dot(attn, wobf_ref[:, cols])


def _dsa_attn(main, kk, wi, h, ada4, w_out, widx, layer, sub, batch, seq, tq, tk, top_k):
    t, d = h.shape
    nq = seq // tq
    rows = KV_GROUP * tq
    return pl.pallas_call(
        functools.partial(_dsa_attn_kernel, tq, tk, top_k, sub, widx),
        grid=(batch, nq),
        in_specs=[
            pl.BlockSpec((tq, Q_W), lambda b, i: (b * nq + i, 0)),
            pl.BlockSpec((tq, QI_W), lambda b, i: (b * nq + i, (Q_W + 2 * KV_W) // QI_W)),
            pl.BlockSpec((tq, LANES), lambda b, i: (b * nq + i, 0)),
            pl.BlockSpec((seq, KV_W), lambda b, i: (b, Q_W // KV_W)),
            pl.BlockSpec((seq, KV_W), lambda b, i: (b, Q_W // KV_W + 1)),
            pl.BlockSpec((seq, 2 * LANES), lambda b, i: (b, 0)),
            pl.BlockSpec((tq, d), lambda b, i: (b * nq + i, 0)),
            pl.BlockSpec((None, None, 3 * N_SUBLAYERS, d), lambda b, i: (layer, b, 0, 0)),
            pl.BlockSpec(memory_space=pl.ANY),
        ],
        out_specs=pl.BlockSpec((tq, d), lambda b, i: (b * nq + i, 0)),
        out_shape=jax.ShapeDtypeStruct((t, d), F32),
        scratch_shapes=[
            pltpu.VMEM((tq, seq), F32),
            pltpu.VMEM((tq, seq), F32),
            pltpu.VMEM((tq, LANES), F32),
            pltpu.VMEM((IDX_HEADS, tq, LANES), F32),
            pltpu.VMEM((N_KV_HEADS, rows, HEAD_DIM), BF16),
            pltpu.VMEM((N_KV_HEADS, rows, LANES), F32),
            pltpu.VMEM((N_KV_HEADS, rows, 2 * HEAD_DIM), F32),
            pltpu.VMEM((tq, Q_W), BF16),
            pltpu.VMEM((Q_W, d), BF16),
            pltpu.VMEM((2, 2 * LANES, d), F32),
            pltpu.SemaphoreType.DMA((2,)),
        ],
        compiler_params=_cparams(("arbitrary", "arbitrary")),
        name="dsa_attn",
    )(main, main, wi, main, main, kk, h, ada4, w_out)


def _tiles(seq, d_ff):
    tm = min(1024, seq)
    tm_ffn = min(2048, seq)
    tf = 256 if d_ff % 256 == 0 else d_ff
    tq = min(256, seq)
    tk = min(256, seq)
    return tm, tm_ffn, tf, tq, tk


def kernel(x, c, positions, ada_w, ada_b, norm_g, final_g, ffn_wgu, ffn_wd,
           pool_w_in, pool_w_grp, pool_scale, pool_w_out, dsa_w_in, dsa_w_out):
    batch, seq, d = x.shape
    depth = ada_w.shape[0]
    d_ff = ffn_wd.shape[2]
    t = batch * seq
    tm, tm_ffn, tf, tq, tk = _tiles(seq, d_ff)
    tn = min(512, d)
    top_k = min(TOPK_MAX, seq // 4)
    assert seq % tm == 0 and seq % tm_ffn == 0 and seq % tq == 0 and seq % tk == 0 and tk >= top_k
    assert dsa_w_in.shape[2] == DSA_MAIN_W + DSA_SMALL_W and DSA_MAIN_W % tn == 0

    rows = -(-batch // SUBLANES) * SUBLANES
    c8 = jnp.pad(c, ((0, rows - batch), (0, 0)))
    ada_n = ada_w.shape[2]
    ada, tabs = _ada(c8, ada_w, ada_b, 1024 if ada_n % 1024 == 0 else d, positions)
    ada4 = ada.reshape(depth, rows, 3 * N_SUBLAYERS, d)

    h = x.reshape(t, d)
    h = _ffn(h, ada4, norm_g, ffn_wgu, ffn_wd, [(0, 0, 0)], seq, tm_ffn, tf)
    for layer in range(depth):
        j = layer // 2
        g_row = norm_g[layer, 1][None, :]
        if layer % 2 == 0:
            u = _norm_linear(h, ada4, g_row, pool_w_in, j, layer, 1, seq, tm, tn)
            a = _pool_core(u, pool_w_grp, pool_scale, j, batch, seq)
            h = _linear_residual(a, pool_w_out, j, h, ada4, layer, 1, seq, tm, tn)
        else:
            w_small = _dsa_small_weights(dsa_w_in[j, :, DSA_MAIN_W:])
            main, kk, wi = _dsa_proj(h, ada4, g_row, jnp.swapaxes(dsa_w_in, 1, 2), j, w_small,
                                     tabs, layer, 1, seq, tm, tn)
            h = _dsa_attn(main, kk, wi, h, ada4, dsa_w_out, j, layer, 1, batch, seq, tq, tk, top_k)
        if layer == depth - 1:
            h = _ffn(h, ada4, norm_g, ffn_wgu, ffn_wd, [(layer, 1, 2)], seq, tm_ffn, tf,
                     final_g=final_g)
        else:
            h = _ffn(h, ada4, norm_g, ffn_wgu, ffn_wd, [(layer, 1, 2), (layer + 1, 0, 0)],
                     seq, tm_ffn, tf)
    return h.reshape(batch, seq, d)
```
